```python
import math
import jax, jax.numpy as jnp
from jax import lax
import numpy as np

D_MODEL = 2048
BATCH = 2
SEQ = 4096
DEPTH = 2

CHUNK = 64
EPS = 1e-6
CONV_W = 4
GDN_HEADS = D_MODEL // 256
GDN_DK = 128
GDN_DV = 128
GDN_QK = GDN_HEADS * GDN_DK
GDN_VW = GDN_HEADS * GDN_DV
GDN_CONV_CH = 2 * GDN_QK + GDN_VW
HG_HEADS = D_MODEL // 256
HG_DK = 128
HG_DV = 128
HG_FW = HG_HEADS * HG_DK
HG_VW = HG_HEADS * HG_DV
LRU_WIDTH = D_MODEL
RG_BLOCKS = D_MODEL // 256
RG_BLOCK = LRU_WIDTH // RG_BLOCKS
RG_C = 8.0
D_FF = ((8 * D_MODEL // 3) + 255) // 256 * 256
N_EXPERTS = 8
TOP_K = 2
D_EXPERT = 7 * D_MODEL // 2
N_EVEN = (DEPTH + 1) // 2
N_ODD = DEPTH // 2
MIX_SPLITS = (GDN_QK, GDN_QK, GDN_VW, GDN_VW, GDN_HEADS, GDN_HEADS, HG_FW, HG_FW, HG_VW, HG_VW)
MIX_IN = sum(MIX_SPLITS)
MIX_SPLIT_IDX = tuple(int(s) for s in np.cumsum(MIX_SPLITS)[:-1])
MIX_OUT = GDN_VW + HG_VW

kernel_name = "hybrid_gdn_hgrn2_rglru_moe_block"

F32 = jnp.float32


def _rmsnorm(x, w):
    xf = x.astype(F32)
    y = xf * lax.rsqrt(jnp.mean(xf * xf, axis=-1, keepdims=True) + EPS)
    return (y * w.astype(F32)).astype(x.dtype)


def _gated_head_rmsnorm(o, w, z):
    o = o * lax.rsqrt(jnp.mean(o * o, axis=-1, keepdims=True) + EPS) * w.astype(F32)
    o = o * jax.nn.silu(z.astype(F32).reshape(o.shape))
    return o.reshape(o.shape[0], o.shape[1], -1)


def _l2norm(t):
    return t * lax.rsqrt(jnp.sum(t * t, axis=-1, keepdims=True) + EPS)


def _causal_dwconv(x, w):
    ch = x.shape[-1]
    return lax.conv_general_dilated(
        x, w[:, None, :].astype(x.dtype), window_strides=(1,),
        padding=[(w.shape[0] - 1, 0)], dimension_numbers=('NWC', 'WIO', 'NWC'),
        feature_group_count=ch)


def _to_chunks(t):
    b, s, h, d = t.shape
    return t.reshape(b, s // CHUNK, CHUNK, h, d).transpose(0, 3, 1, 2, 4)


def _from_chunks(t):
    b, h, n, c, d = t.shape
    return t.transpose(0, 2, 3, 1, 4).reshape(b, n * c, h, d)


def _gated_delta_rule(q, k, v, g, beta):
    bsz, s, h, dk = q.shape
    dv = v.shape[-1]
    q = _to_chunks(_l2norm(q) * (dk ** -0.5))
    k = _to_chunks(_l2norm(k))
    v = _to_chunks(v)
    g = _to_chunks(g[..., None])[..., 0]
    beta = _to_chunks(beta[..., None])[..., 0]
    gc = jnp.cumsum(g, axis=-1)
    incl = jnp.tril(jnp.ones((CHUNK, CHUNK), dtype=bool))
    strict = jnp.tril(jnp.ones((CHUNK, CHUNK), dtype=bool), -1)
    diff = gc[..., :, None] - gc[..., None, :]
    decay = jnp.where(incl, jnp.exp(jnp.where(incl, diff, 0.0)), 0.0)
    kk = jnp.einsum('bhncd,bhnsd->bhncs', k, k)
    lmat = jnp.where(strict, beta[..., None] * kk * decay, 0.0)
    amat = lmat + jnp.eye(CHUNK, dtype=F32)
    rhs = jnp.concatenate([v * beta[..., None], k * (beta * jnp.exp(gc))[..., None]], axis=-1)
    sol = lax.linalg.triangular_solve(amat, rhs, left_side=True, lower=True, unit_diagonal=True)
    u, w = sol[..., :dv], sol[..., dv:]
    qk = jnp.einsum('bhncd,bhnsd->bhncs', q, k) * decay
    q_dec = q * jnp.exp(gc)[..., None]
    k_dec = k * jnp.exp(gc[..., -1:] - gc)[..., None]
    g_last = jnp.exp(gc[..., -1])

    def step(state, inp):
        u_n, w_n, qk_n, qd_n, kd_n, gl_n = inp
        v_new = u_n - jnp.einsum('bhcd,bhde->bhce', w_n, state)
        o_n = jnp.einsum('bhcd,bhde->bhce', qd_n, state) + jnp.einsum('bhcs,bhse->bhce', qk_n, v_new)
        state = state * gl_n[..., None, None] + jnp.einsum('bhcd,bhce->bhde', kd_n, v_new)
        return state, o_n

    xs = (jnp.moveaxis(u, 2, 0), jnp.moveaxis(w, 2, 0), jnp.moveaxis(qk, 2, 0),
          jnp.moveaxis(q_dec, 2, 0), jnp.moveaxis(k_dec, 2, 0), jnp.moveaxis(g_last, 2, 0))
    _, o = lax.scan(step, jnp.zeros((bsz, h, dk, dv), F32), xs)
    return _from_chunks(jnp.moveaxis(o, 0, 2))


def _hgrn2(q, f_logit, i, lb):
    bsz, s, h, dk = q.shape
    dv = i.shape[-1]
    lb = lb.reshape(h, dk)
    forget = lb + (1.0 - lb) * jax.nn.sigmoid(f_logit)
    logf = _to_chunks(jnp.log(forget))
    k = _to_chunks(1.0 - forget)
    q = _to_chunks(q * (dk ** -0.5))
    v = _to_chunks(i)
    b = jnp.cumsum(logf, axis=3)
    b_ref = b[..., CHUNK // 2 - 1:CHUNK // 2, :]
    q_t = q * jnp.exp(b - b_ref)
    k_t = k * jnp.exp(b_ref - b)
    incl = jnp.tril(jnp.ones((CHUNK, CHUNK), dtype=bool))
    att = jnp.where(incl, jnp.einsum('bhncd,bhnsd->bhncs', q_t, k_t), 0.0)
    o_intra = jnp.einsum('bhncs,bhnse->bhnce', att, v)
    b_last = b[..., -1, :]
    d_state = jnp.einsum('bhncd,bhnce->bhnde', k * jnp.exp(b_last[..., None, :] - b), v)

    def step(state, inp):
        ds_n, bl_n = inp
        return state * jnp.exp(bl_n)[..., None] + ds_n, state

    _, s_prev = lax.scan(step, jnp.zeros((bsz, h, dk, dv), F32),
                         (jnp.moveaxis(d_state, 2, 0), jnp.moveaxis(b_last, 2, 0)))
    s_prev = jnp.moveaxis(s_prev, 0, 2)
    o_inter = jnp.einsum('bhncd,bhnde->bhnce', q * jnp.exp(b), s_prev)
    return _from_chunks(o_intra + o_inter)


def _ab_mixer(h, w_in, conv_w, a_log, dt_bias, gdn_nw, lb, hg_nw, w_out):
    bsz, s, _ = h.shape
    proj = h @ w_in
    a_q, a_k, a_v, a_z, a_b, a_a, b_q, b_f, b_i, b_g = jnp.split(proj, MIX_SPLIT_IDX, axis=-1)
    qkv = jax.nn.silu(_causal_dwconv(jnp.concatenate([a_q, a_k, a_v], axis=-1), conv_w)).astype(F32)
    q, k, v = jnp.split(qkv, [GDN_QK, 2 * GDN_QK], axis=-1)
    q = q.reshape(bsz, s, GDN_HEADS, GDN_DK)
    k = k.reshape(bsz, s, GDN_HEADS, GDN_DK)
    v = v.reshape(bsz, s, GDN_HEADS, GDN_DV)
    beta = jax.nn.sigmoid(a_b.astype(F32))
    g = -jnp.exp(a_log.astype(F32)) * jax.nn.softplus(a_a.astype(F32) + dt_bias.astype(F32))
    o_a = _gated_head_rmsnorm(_gated_delta_rule(q, k, v, g, beta), gdn_nw, a_z)
    hq = jax.nn.silu(b_q.astype(F32)).reshape(bsz, s, HG_HEADS, HG_DK)
    hf = b_f.astype(F32).reshape(bsz, s, HG_HEADS, HG_DK)
    hi = b_i.astype(F32).reshape(bsz, s, HG_HEADS, HG_DV)
    o_b = _gated_head_rmsnorm(_hgrn2(hq, hf, hi, lb), hg_nw, b_g)
    o = jnp.concatenate([o_a, o_b], axis=-1).astype(h.dtype)
    return o @ w_out


def _lin_combine(e1, e2):
    a1, b1 = e1
    a2, b2 = e2
    return a1 * a2, a2 * b1 + b2


def _rglru_block(h, w_in, conv_w, conv_b, wa, ba, wx, bx, lam, w_out):
    bsz, s, _ = h.shape
    y_br, x_br = jnp.split(h @ w_in, 2, axis=-1)
    y_br = jax.nn.gelu(y_br, approximate=True)
    xc = (_causal_dwconv(x_br, conv_w) + conv_b).astype(F32)
    xb = xc.reshape(bsz, s, RG_BLOCKS, RG_BLOCK)
    r = jax.nn.sigmoid(jnp.einsum('bsnc,ncd->bsnd', xb, wa.astype(F32)).reshape(bsz, s, LRU_WIDTH) + ba.astype(F32))
    gi = jax.nn.sigmoid(jnp.einsum('bsnc,ncd->bsnd', xb, wx.astype(F32)).reshape(bsz, s, LRU_WIDTH) + bx.astype(F32))
    log_a = -RG_C * r * jax.nn.softplus(-lam.astype(F32))
    a = jnp.exp(log_a)
    b = jnp.sqrt(jnp.maximum(-jnp.expm1(2.0 * log_a), 0.0)) * (gi * xc)
    _, hs = lax.associative_scan(_lin_combine, (a, b), axis=1)
    return (y_br * hs.astype(h.dtype)) @ w_out


def _swiglu(h, w_gu, w_d):
    g, u = jnp.split(h @ w_gu, 2, axis=-1)
    return (jax.nn.silu(g) * u) @ w_d


def _moe(h, router_w, gate_up_w, down_w):
    bsz, s, d = h.shape
    t = h.reshape(-1, d)
    logits = (t @ router_w).astype(F32)
    top_v, top_i = lax.top_k(logits, TOP_K)
    gates = jax.nn.softmax(top_v, axis=-1)
    comb = jnp.sum(jax.nn.one_hot(top_i, N_EXPERTS, dtype=F32) * gates[..., None], axis=1)
    out = jnp.zeros(t.shape, F32)
    for e in range(N_EXPERTS):
        out = out + comb[:, e:e + 1] * _swiglu(t, gate_up_w[e], down_w[e]).astype(F32)
    return out.astype(h.dtype).reshape(bsz, s, d)


def setup_inputs(seed: int = 0) -> dict:
    key = jax.random.key(seed)
    ks = jax.random.split(key, 26)

    def nrm(k, shape, scale):
        return jax.random.normal(k, shape, F32) * scale

    x = nrm(ks[0], (BATCH, SEQ, D_MODEL), 1.0)
    norm_w = 1.0 + nrm(ks[1], (DEPTH, 2, D_MODEL), 0.02)
    final_norm_w = 1.0 + nrm(ks[2], (D_MODEL,), 0.02)
    mix_in_w = nrm(ks[3], (N_EVEN, D_MODEL, MIX_IN), D_MODEL ** -0.5)
    gdn_conv_w = nrm(ks[4], (N_EVEN, CONV_W, GDN_CONV_CH), CONV_W ** -0.5)
    gdn_a_log = jnp.log(jax.random.uniform(ks[5], (N_EVEN, GDN_HEADS), F32, 1.0, 16.0))
    dt = jnp.exp(jax.random.uniform(ks[6], (N_EVEN, GDN_HEADS), F32, math.log(1e-3), math.log(1e-1)))
    gdn_dt_bias = dt + jnp.log(-jnp.expm1(-dt))
    gdn_norm_w = 1.0 + nrm(ks[7], (N_EVEN, GDN_DV), 0.02)
    hgrn_lb_logits = nrm(ks[8], (N_EVEN + 1, HG_FW), 0.1)
    hgrn_norm_w = 1.0 + nrm(ks[9], (N_EVEN, HG_DV), 0.02)
    mix_out_w = nrm(ks[10], (N_EVEN, MIX_OUT, D_MODEL), MIX_OUT ** -0.5)
    ffn_gate_up_w = nrm(ks[11], (N_EVEN, D_MODEL, 2 * D_FF), D_MODEL ** -0.5)
    ffn_down_w = nrm(ks[12], (N_EVEN, D_FF, D_MODEL), D_FF ** -0.5)
    rg_in_w = nrm(ks[13], (N_ODD, D_MODEL, 2 * LRU_WIDTH), D_MODEL ** -0.5)
    rg_conv_w = nrm(ks[14], (N_ODD, CONV_W, LRU_WIDTH), CONV_W ** -0.5)
    rg_conv_b = nrm(ks[15], (N_ODD, LRU_WIDTH), 0.01)
    rg_gate_a_w = nrm(ks[16], (N_ODD, RG_BLOCKS, RG_BLOCK, RG_BLOCK), RG_BLOCK ** -0.5)
    rg_gate_a_b = nrm(ks[17], (N_ODD, LRU_WIDTH), 0.01)
    rg_gate_x_w = nrm(ks[18], (N_ODD, RG_BLOCKS, RG_BLOCK, RG_BLOCK), RG_BLOCK ** -0.5)
    rg_gate_x_b = nrm(ks[19], (N_ODD, LRU_WIDTH), 0.01)
    a_c = jax.random.uniform(ks[20], (N_ODD, LRU_WIDTH), F32, 0.9, 0.999)
    a0 = a_c ** (1.0 / RG_C)
    rg_lambda = jnp.log(a0) - jnp.log1p(-a0)
    rg_out_w = nrm(ks[21], (N_ODD, LRU_WIDTH, D_MODEL), LRU_WIDTH ** -0.5)
    moe_router_w = nrm(ks[22], (N_ODD, D_MODEL, N_EXPERTS), D_MODEL ** -0.5)
    moe_gate_up_w = nrm(ks[23], (N_ODD, N_EXPERTS, D_MODEL, 2 * D_EXPERT), D_MODEL ** -0.5)
    moe_down_w = nrm(ks[24], (N_ODD, N_EXPERTS, D_EXPERT, D_MODEL), D_EXPERT ** -0.5)
    return {
        "x": x, "norm_w": norm_w, "final_norm_w": final_norm_w,
        "mix_in_w": mix_in_w, "gdn_conv_w": gdn_conv_w, "gdn_a_log": gdn_a_log,
        "gdn_dt_bias": gdn_dt_bias, "gdn_norm_w": gdn_norm_w, "hgrn_lb_logits": hgrn_lb_logits,
        "hgrn_norm_w": hgrn_norm_w, "mix_out_w": mix_out_w,
        "ffn_gate_up_w": ffn_gate_up_w, "ffn_down_w": ffn_down_w,
        "rg_in_w": rg_in_w, "rg_conv_w": rg_conv_w, "rg_conv_b": rg_conv_b,
        "rg_gate_a_w": rg_gate_a_w, "rg_gate_a_b": rg_gate_a_b,
        "rg_gate_x_w": rg_gate_x_w, "rg_gate_x_b": rg_gate_x_b,
        "rg_lambda": rg_lambda, "rg_out_w": rg_out_w,
        "moe_router_w": moe_router_w, "moe_gate_up_w": moe_gate_up_w, "moe_down_w": moe_down_w,
    }


def reference(x, norm_w, final_norm_w, mix_in_w, gdn_conv_w, gdn_a_log, gdn_dt_bias, gdn_norm_w,
              hgrn_lb_logits, hgrn_norm_w, mix_out_w, ffn_gate_up_w, ffn_down_w,
              rg_in_w, rg_conv_w, rg_conv_b, rg_gate_a_w, rg_gate_a_b, rg_gate_x_w, rg_gate_x_b,
              rg_lambda, rg_out_w, moe_router_w, moe_gate_up_w, moe_down_w):
    hgrn_lb = jnp.cumsum(jax.nn.softmax(hgrn_lb_logits.astype(F32), axis=0), axis=0)
    for layer in range(DEPTH):
        j = layer // 2
        h = _rmsnorm(x, norm_w[layer, 0])
        if layer % 2 == 0:
            x = x + _ab_mixer(h, mix_in_w[j], gdn_conv_w[j], gdn_a_log[j], gdn_dt_bias[j],
                              gdn_norm_w[j], hgrn_lb[j], hgrn_norm_w[j], mix_out_w[j])
            h = _rmsnorm(x, norm_w[layer, 1])
            x = x + _swiglu(h, ffn_gate_up_w[j], ffn_down_w[j])
        else:
            x = x + _rglru_block(h, rg_in_w[j], rg_conv_w[j], rg_conv_b[j], rg_gate_a_w[j],
                                 rg_gate_a_b[j], rg_gate_x_w[j], rg_gate_x_b[j], rg_lambda[j], rg_out_w[j])
            h = _rmsnorm(x, norm_w[layer, 1])
            x = x + _moe(h, moe_router_w[j], moe_gate_up_w[j], moe_down_w[j])
    return _rmsnorm(x, final_norm_w)
```

```python
import functools

import jax
import jax.numpy as jnp
from jax import lax
from jax.experimental import pallas as pl
from jax.experimental.pallas import tpu as pltpu

F32 = jnp.float32
BF16 = jnp.bfloat16
HIGHEST = lax.Precision.HIGHEST

EPS = 1e-6
CHUNK = 64
CONV_W = 4
HEAD_DIM = 128
RG_BLOCK = 256
RG_C = 8.0
TOP_K = 2
LANES = 128
SUBLANES = 8
VMEM_LIMIT = 56 * 1024 * 1024

_NT = (((1,), (1,)), ((), ()))
_TN = (((0,), (0,)), ((), ()))


def _cparams(sem):
    return pltpu.CompilerParams(dimension_semantics=sem, vmem_limit_bytes=VMEM_LIMIT)


def _silu(x):
    return x * jax.nn.sigmoid(x)


def _softplus(x):
    return jnp.maximum(x, 0.0) + jnp.log1p(jnp.exp(-jnp.abs(x)))


def _bdot(a, b):
    return jnp.dot(a.astype(BF16), b.astype(BF16), preferred_element_type=F32)


def _bdot_g(a, b, dims):
    return lax.dot_general(a.astype(BF16), b.astype(BF16), dims, preferred_element_type=F32)


def _rmsnorm_kernel(x_ref, w_ref, o_ref):
    x = x_ref[...]
    y = x * lax.rsqrt(jnp.mean(x * x, axis=-1, keepdims=True) + EPS)
    o_ref[...] = (y * w_ref[...]).astype(o_ref.dtype)


def _rmsnorm(x, w, out_dtype, tm=512):
    m, d = x.shape
    return pl.pallas_call(
        _rmsnorm_kernel,
        grid=(m // tm,),
        in_specs=[pl.BlockSpec((tm, d), lambda i: (i, 0)),
                  pl.BlockSpec((1, d), lambda i: (0, 0))],
        out_specs=pl.BlockSpec((tm, d), lambda i: (i, 0)),
        out_shape=jax.ShapeDtypeStruct((m, d), out_dtype),
        compiler_params=_cparams(("parallel",)),
        name="rmsnorm",
    )(x, w.reshape(1, d).astype(F32))


def _mm_kernel(a_ref, w_ref, o_ref):
    o_ref[...] = jnp.dot(a_ref[...], w_ref[...], preferred_element_type=F32).astype(o_ref.dtype)


def _mm(a, w, out_dtype, tm, tn):
    m, k = a.shape
    n = w.shape[1]
    return pl.pallas_call(
        _mm_kernel,
        grid=(n // tn, m // tm),
        in_specs=[pl.BlockSpec((tm, k), lambda j, i: (i, 0)),
                  pl.BlockSpec((k, tn), lambda j, i: (0, j))],
        out_specs=pl.BlockSpec((tm, tn), lambda j, i: (i, j)),
        out_shape=jax.ShapeDtypeStruct((m, n), out_dtype),
        compiler_params=_cparams(("parallel", "parallel")),
        name="mm",
    )(a, w)


def _mm_swiglu_kernel(a_ref, wg_ref, wu_ref, o_ref):
    a = a_ref[...]
    g = jnp.dot(a, wg_ref[...], preferred_element_type=F32)
    u = jnp.dot(a, wu_ref[...], preferred_element_type=F32)
    o_ref[...] = (_silu(g) * u).astype(o_ref.dtype)


def _mm_swiglu(a, w_gu, tm, tn):
    m, k = a.shape
    f = w_gu.shape[1] // 2
    nb = f // tn
    return pl.pallas_call(
        _mm_swiglu_kernel,
        grid=(nb, m // tm),
        in_specs=[pl.BlockSpec((tm, k), lambda j, i: (i, 0)),
                  pl.BlockSpec((k, tn), lambda j, i: (0, j)),
                  pl.BlockSpec((k, tn), lambda j, i: (0, j + nb))],
        out_specs=pl.BlockSpec((tm, tn), lambda j, i: (i, j)),
        out_shape=jax.ShapeDtypeStruct((m, f), BF16),
        compiler_params=_cparams(("parallel", "parallel")),
        name="mm_swiglu",
    )(a, w_gu, w_gu)


def _mm_resid_kernel(*refs, n_pairs):
    r_ref, o_ref = refs[2 * n_pairs], refs[2 * n_pairs + 1]
    acc = r_ref[...]
    for p in range(n_pairs):
        acc = acc + jnp.dot(refs[2 * p][...], refs[2 * p + 1][...], preferred_element_type=F32)
    o_ref[...] = acc


def _mm_resid(pairs, resid, tm, tn):
    m, n = resid.shape
    in_specs, args = [], []
    for a, w in pairs:
        k = a.shape[1]
        in_specs += [pl.BlockSpec((tm, k), lambda j, i: (i, 0)),
                     pl.BlockSpec((k, tn), lambda j, i: (0, j))]
        args += [a, w]
    in_specs.append(pl.BlockSpec((tm, tn), lambda j, i: (i, j)))
    args.append(resid)
    return pl.pallas_call(
        functools.partial(_mm_resid_kernel, n_pairs=len(pairs)),
        grid=(n // tn, m // tm),
        in_specs=in_specs,
        out_specs=pl.BlockSpec((tm, tn), lambda j, i: (i, j)),
        out_shape=jax.ShapeDtypeStruct((m, n), F32),
        compiler_params=_cparams(("parallel", "parallel")),
        name="mm_resid",
    )(*args)


def _causal_conv(buf_ref, x, cw, t_len):
    buf_ref[pl.ds(SUBLANES, t_len), :] = x
    acc = cw[CONV_W - 1:CONV_W, :] * x
    for j in range(CONV_W - 1):
        off = SUBLANES - (CONV_W - 1) + j
        acc = acc + cw[j:j + 1, :] * buf_ref[pl.ds(off, t_len), :]
    buf_ref[pl.ds(0, SUBLANES), :] = x[t_len - SUBLANES:, :]
    return acc


def _tri_masks():
    row = lax.broadcasted_iota(jnp.int32, (CHUNK, CHUNK), 0)
    col = lax.broadcasted_iota(jnp.int32, (CHUNK, CHUNK), 1)
    return row >= col, row > col, row == col


def _gated_head_norm(o, nw, z):
    o = o * lax.rsqrt(jnp.mean(o * o, axis=-1, keepdims=True) + EPS) * nw
    return o * _silu(z)


def _gdn_kernel(q_ref, k_ref, v_ref, z_ref, ba_ref, cwq_ref, cwk_ref, cwv_ref, alog_ref, dtb_ref,
                nw_ref, o_ref, qbuf, kbuf, vbuf, qs, ks, vs, gs, bs, s_ref, *, heads, n_heads, t_len):
    hg = pl.program_id(1)
    t = pl.program_id(2)

    @pl.when(t == 0)
    def _():
        zeros = jnp.zeros((SUBLANES, heads * HEAD_DIM), F32)
        qbuf[pl.ds(0, SUBLANES), :] = zeros
        kbuf[pl.ds(0, SUBLANES), :] = zeros
        vbuf[pl.ds(0, SUBLANES), :] = zeros
        s_ref[...] = jnp.zeros_like(s_ref)

    qc = _silu(_causal_conv(qbuf, q_ref[0], cwq_ref[...], t_len))
    kc = _silu(_causal_conv(kbuf, k_ref[0], cwk_ref[...], t_len))
    vs[...] = _silu(_causal_conv(vbuf, v_ref[0], cwv_ref[...], t_len))
    for j in range(heads):
        sl = slice(j * HEAD_DIM, (j + 1) * HEAD_DIM)
        qh = qc[:, sl]
        kh = kc[:, sl]
        qs[:, sl] = qh * (lax.rsqrt(jnp.sum(qh * qh, axis=-1, keepdims=True) + EPS) * (HEAD_DIM ** -0.5))
        ks[:, sl] = kh * lax.rsqrt(jnp.sum(kh * kh, axis=-1, keepdims=True) + EPS)

    ba = ba_ref[0]
    gs[...] = -jnp.exp(alog_ref[...]) * _softplus(ba + dtb_ref[...])
    bs[...] = jax.nn.sigmoid(ba)

    incl, strict, eye = _tri_masks()
    tri = incl.astype(F32)
    eye_f = eye.astype(F32)
    ones = jnp.ones((CHUNK, LANES), F32)
    lane = lax.broadcasted_iota(jnp.int32, (CHUNK, LANES), 1)
    nw = nw_ref[...]

    def chunk_body(c, carry):
        r0 = pl.multiple_of(c * CHUNK, CHUNK)
        rows = pl.ds(r0, CHUNK)
        gc_all = jnp.dot(tri, gs[rows, :], precision=HIGHEST, preferred_element_type=F32)
        beta_all = bs[rows, :]
        for j in range(heads):
            h = hg * heads + j
            sl = slice(j * HEAD_DIM, (j + 1) * HEAD_DIM)
            gc_h = jnp.where(lane == n_heads + h, gc_all, 0.0)
            gc = jnp.sum(gc_h, axis=-1, keepdims=True)
            beta = jnp.sum(jnp.where(lane == h, beta_all, 0.0), axis=-1, keepdims=True)
            gc_row = lax.dot_general(ones, gc_h, _NT, precision=HIGHEST,
                                     preferred_element_type=F32)
            decay = jnp.where(incl, jnp.exp(jnp.where(incl, gc - gc_row, 0.0)), 0.0)
            egc = jnp.exp(gc)
            g_last = gc[CHUNK - 1:CHUNK, :]
            q = qs[rows, sl]
            k = ks[rows, sl]
            v = vs[rows, sl]
            kk = _bdot_g(k, k, _NT)
            x = jnp.where(strict, -(beta * kk * decay), 0.0)
            p = eye_f + x
            for _ in range(5):
                x = _bdot(x, x)
                p = p + _bdot(p, x)
            rhs = jnp.concatenate([v * beta, k * (beta * egc)], axis=-1)
            sol = _bdot(p, rhs)
            u = sol[:, :HEAD_DIM]
            w = sol[:, HEAD_DIM:]
            qk = _bdot_g(q, k, _NT) * decay
            state = s_ref[j]
            v_new = u - _bdot(w, state)
            o = _bdot(q * egc, state) + _bdot(qk, v_new)
            k_dec = k * jnp.exp(g_last - gc)
            s_ref[j] = state * jnp.exp(g_last) + _bdot_g(k_dec, v_new, _TN)
            o = _gated_head_norm(o, nw, z_ref[0, rows, sl])
            o_ref[0, rows, sl] = o.astype(o_ref.dtype)
        return carry

    lax.fori_loop(0, t_len // CHUNK, chunk_body, 0)


def _gdn(proj, ba, conv_w, a_log, dt_bias, norm_w, *, col0, heads_per_step=2, t_len=256):
    bsz, s, _ = proj.shape
    n_heads = a_log.shape[0]
    gw = heads_per_step * HEAD_DIM
    n_hg = n_heads // heads_per_step
    width = n_heads * HEAD_DIM
    cb = col0 // gw
    nb = width // gw

    def col_spec(group):
        return pl.BlockSpec((1, t_len, gw), lambda b, g, t: (b, t, cb + group * nb + g))

    def cw_spec(group):
        return pl.BlockSpec((CONV_W, gw), lambda b, g, t: (0, group * nb + g))

    row = jnp.zeros((1, LANES), F32)
    alog_row = row.at[0, n_heads:2 * n_heads].set(a_log.astype(F32))
    dtb_row = row.at[0, n_heads:2 * n_heads].set(dt_bias.astype(F32))
    small = pl.BlockSpec((1, LANES), lambda b, g, t: (0, 0))
    kern = functools.partial(_gdn_kernel, heads=heads_per_step, n_heads=n_heads, t_len=t_len)
    return pl.pallas_call(
        kern,
        grid=(bsz, n_hg, s // t_len),
        in_specs=[col_spec(0), col_spec(1), col_spec(2), col_spec(3),
                  pl.BlockSpec((1, t_len, LANES), lambda b, g, t: (b, t, 0)),
                  cw_spec(0), cw_spec(1), cw_spec(2), small, small,
                  pl.BlockSpec((1, HEAD_DIM), lambda b, g, t: (0, 0))],
        out_specs=pl.BlockSpec((1, t_len, gw), lambda b, g, t: (b, t, g)),
        out_shape=jax.ShapeDtypeStruct((bsz, s, width), BF16),
        scratch_shapes=[pltpu.VMEM((SUBLANES + t_len, gw), F32)] * 3
        + [pltpu.VMEM((t_len, gw), F32)] * 3
        + [pltpu.VMEM((t_len, LANES), F32)] * 2
        + [pltpu.VMEM((heads_per_step, HEAD_DIM, HEAD_DIM), F32)],
        compiler_params=_cparams(("parallel", "parallel", "arbitrary")),
        name="gdn",
    )(proj, proj, proj, proj, ba, conv_w, conv_w, conv_w, alog_row, dtb_row,
      norm_w.reshape(1, HEAD_DIM).astype(F32))


def _hgrn_kernel(q_ref, f_ref, i_ref, g_ref, lb_ref, nw_ref, o_ref, qs, ks, ls, st_ref, *, heads, t_len):
    t = pl.program_id(2)

    @pl.when(t == 0)
    def _():
        st_ref[...] = jnp.zeros_like(st_ref)

    lb = lb_ref[...]
    forget = lb + (1.0 - lb) * jax.nn.sigmoid(f_ref[0])
    ls[...] = jnp.log(forget)
    ks[...] = 1.0 - forget
    qs[...] = _silu(q_ref[0]) * (HEAD_DIM ** -0.5)

    incl, _, _ = _tri_masks()
    tri = incl.astype(F32)
    nw = nw_ref[...]
    mid = CHUNK // 2 - 1

    def chunk_body(c, carry):
        r0 = pl.multiple_of(c * CHUNK, CHUNK)
        rows = pl.ds(r0, CHUNK)
        b_all = jnp.dot(tri, ls[rows, :], precision=HIGHEST, preferred_element_type=F32)
        for j in range(heads):
            sl = slice(j * HEAD_DIM, (j + 1) * HEAD_DIM)
            b = b_all[:, sl]
            b_mid = b[mid:mid + 1, :]
            b_last = b[CHUNK - 1:CHUNK, :]
            q = qs[rows, sl]
            k = ks[rows, sl]
            v = i_ref[0, rows, sl]
            att = jnp.where(incl, _bdot_g(q * jnp.exp(b - b_mid), k * jnp.exp(b_mid - b), _NT), 0.0)
            st = st_ref[j]
            o = _bdot(att, v) + _bdot_g(q * jnp.exp(b), st, _NT)
            st_ref[j] = st * jnp.exp(b_last) + _bdot_g(v, k * jnp.exp(b_last - b), _TN)
            o = _gated_head_norm(o, nw, g_ref[0, rows, sl])
            o_ref[0, rows, sl] = o.astype(o_ref.dtype)
        return carry

    lax.fori_loop(0, t_len // CHUNK, chunk_body, 0)


def _hgrn(proj, lb, norm_w, *, col0, n_heads, heads_per_step=2, t_len=256):
    bsz, s, _ = proj.shape
    gw = heads_per_step * HEAD_DIM
    n_hg = n_heads // heads_per_step
    width = n_heads * HEAD_DIM
    cb = col0 // gw
    nb = width // gw

    def col_spec(group):
        return pl.BlockSpec((1, t_len, gw), lambda b, g, t: (b, t, cb + group * nb + g))

    kern = functools.partial(_hgrn_kernel, heads=heads_per_step, t_len=t_len)
    return pl.pallas_call(
        kern,
        grid=(bsz, n_hg, s // t_len),
        in_specs=[col_spec(0), col_spec(1), col_spec(2), col_spec(3),
                  pl.BlockSpec((1, gw), lambda b, g, t: (0, g)),
                  pl.BlockSpec((1, HEAD_DIM), lambda b, g, t: (0, 0))],
        out_specs=pl.BlockSpec((1, t_len, gw), lambda b, g, t: (b, t, g)),
        out_shape=jax.ShapeDtypeStruct((bsz, s, width), BF16),
        scratch_shapes=[pltpu.VMEM((t_len, gw), F32)] * 3
        + [pltpu.VMEM((heads_per_step, HEAD_DIM, HEAD_DIM), F32)],
        compiler_params=_cparams(("parallel", "parallel", "arbitrary")),
        name="hgrn2",
    )(proj, proj, proj, proj, lb.reshape(1, width).astype(F32), norm_w.reshape(1, HEAD_DIM).astype(F32))


def _rglru_kernel(y_ref, x_ref, cw_ref, cb_ref, wa_ref, ba_ref, wx_ref, bx_ref, lam_ref, o_ref,
                  xbuf, a_s, b_s, h_s, hcar, *, blocks, t_len):
    t = pl.program_id(2)

    @pl.when(t == 0)
    def _():
        xbuf[pl.ds(0, SUBLANES), :] = jnp.zeros((SUBLANES, blocks * RG_BLOCK), F32)
        hcar[...] = jnp.zeros_like(hcar)

    xc = _causal_conv(xbuf, x_ref[0], cw_ref[...], t_len) + cb_ref[...]
    for n in range(blocks):
        sl = slice(n * RG_BLOCK, (n + 1) * RG_BLOCK)
        xb = xc[:, sl]
        r = jax.nn.sigmoid(_bdot(xb, wa_ref[n]) + ba_ref[:, sl])
        gi = jax.nn.sigmoid(_bdot(xb, wx_ref[n]) + bx_ref[:, sl])
        log_a = (-RG_C) * r * _softplus(-lam_ref[:, sl])
        a = jnp.exp(log_a)
        a_s[:, sl] = a
        one_minus_a2 = -jnp.tanh(log_a) * (a * a + 1.0)
        b_s[:, sl] = jnp.sqrt(jnp.maximum(one_minus_a2, 0.0)) * (gi * xb)

    def row_body(i, h):
        h = a_s[pl.ds(i, 1), :] * h + b_s[pl.ds(i, 1), :]
        h_s[pl.ds(i, 1), :] = h
        return h

    hcar[...] = lax.fori_loop(0, t_len, row_body, hcar[...])
    o_ref[0] = (jax.nn.gelu(y_ref[0], approximate=True) * h_s[...]).astype(o_ref.dtype)


def _rglru(yx, conv_w, conv_b, wa, ba, wx, bx, lam, *, blocks_per_step=2, t_len=256):
    bsz, s, w2 = yx.shape
    width = w2 // 2
    gw = blocks_per_step * RG_BLOCK
    n_g = width // gw
    vec = lambda a: a.reshape(1, width).astype(F32)
    vspec = pl.BlockSpec((1, gw), lambda b, g, t: (0, g))
    wspec = pl.BlockSpec((blocks_per_step, RG_BLOCK, RG_BLOCK), lambda b, g, t: (g, 0, 0))
    kern = functools.partial(_rglru_kernel, blocks=blocks_per_step, t_len=t_len)
    return pl.pallas_call(
        kern,
        grid=(bsz, n_g, s // t_len),
        in_specs=[pl.BlockSpec((1, t_len, gw), lambda b, g, t: (b, t, g)),
                  pl.BlockSpec((1, t_len, gw), lambda b, g, t: (b, t, n_g + g)),
                  pl.BlockSpec((CONV_W, gw), lambda b, g, t: (0, g)),
                  vspec, wspec, vspec, wspec, vspec, vspec],
        out_specs=pl.BlockSpec((1, t_len, gw), lambda b, g, t: (b, t, g)),
        out_shape=jax.ShapeDtypeStruct((bsz, s, width), BF16),
        scratch_shapes=[pltpu.VMEM((SUBLANES + t_len, gw), F32)]
        + [pltpu.VMEM((t_len, gw), F32)] * 3
        + [pltpu.VMEM((1, gw), F32)],
        compiler_params=_cparams(("parallel", "parallel", "arbitrary")),
        name="rglru",
    )(yx, yx, conv_w.astype(F32), vec(conv_b), wa.astype(BF16), vec(ba), wx.astype(BF16), vec(bx), vec(lam))


def _norm_router_kernel(x_ref, nw_ref, w_ref, h_ref, o_ref, *, n_experts):
    x = x_ref[...]
    y = x * lax.rsqrt(jnp.mean(x * x, axis=-1, keepdims=True) + EPS) * nw_ref[...]
    h_ref[...] = y.astype(h_ref.dtype)
    logits = jnp.dot(y, w_ref[...], precision=HIGHEST, preferred_element_type=F32)
    lane = lax.broadcasted_iota(jnp.int32, logits.shape, 1).astype(F32)
    neg = jnp.float32(-jnp.inf)
    l1 = jnp.where(lane < n_experts, logits, neg)
    m1 = jnp.max(l1, axis=-1, keepdims=True)
    i1 = jnp.min(jnp.where(l1 == m1, lane, float(LANES)), axis=-1, keepdims=True)
    l2 = jnp.where(lane == i1, neg, l1)
    m2 = jnp.max(l2, axis=-1, keepdims=True)
    i2 = jnp.min(jnp.where(l2 == m2, lane, float(LANES)), axis=-1, keepdims=True)
    e2 = jnp.exp(m2 - m1)
    g1 = 1.0 / (1.0 + e2)
    g2 = e2 / (1.0 + e2)
    out = jnp.where(lane == 0, i1, 0.0)
    out = jnp.where(lane == 1, i2, out)
    out = jnp.where(lane == 2, g1, out)
    out = jnp.where(lane == 3, g2, out)
    o_ref[...] = out


def _norm_router(x, nw, router_w, tm=512):
    m, d = x.shape
    n_experts = router_w.shape[1]
    w = jnp.zeros((d, LANES), F32).at[:, :n_experts].set(router_w.astype(F32))
    return pl.pallas_call(
        functools.partial(_norm_router_kernel, n_experts=n_experts),
        grid=(m // tm,),
        in_specs=[pl.BlockSpec((tm, d), lambda i: (i, 0)),
                  pl.BlockSpec((1, d), lambda i: (0, 0)),
                  pl.BlockSpec((d, LANES), lambda i: (0, 0))],
        out_specs=[pl.BlockSpec((tm, d), lambda i: (i, 0)),
                   pl.BlockSpec((tm, LANES), lambda i: (i, 0))],
        out_shape=[jax.ShapeDtypeStruct((m, d), BF16), jax.ShapeDtypeStruct((m, LANES), F32)],
        compiler_params=_cparams(("parallel",)),
        name="norm_router",
    )(x, nw.reshape(1, d).astype(F32), w)


def _gmm_swiglu_kernel(te_ref, a_ref, wg_ref, wu_ref, o_ref):
    a = a_ref[...]
    g = jnp.dot(a, wg_ref[0], preferred_element_type=F32)
    u = jnp.dot(a, wu_ref[0], preferred_element_type=F32)
    o_ref[...] = (_silu(g) * u).astype(o_ref.dtype)


def _gmm_swiglu(tile_expert, a, w_gu, tm, tn):
    m, k = a.shape
    f = w_gu.shape[2] // 2
    nb = f // tn
    grid_spec = pltpu.PrefetchScalarGridSpec(
        num_scalar_prefetch=1,
        grid=(nb, m // tm),
        in_specs=[pl.BlockSpec((tm, k), lambda j, i, te: (i, 0)),
                  pl.BlockSpec((1, k, tn), lambda j, i, te: (te[i], 0, j)),
                  pl.BlockSpec((1, k, tn), lambda j, i, te: (te[i], 0, j + nb))],
        out_specs=pl.BlockSpec((tm, tn), lambda j, i, te: (i, j)),
    )
    return pl.pallas_call(
        _gmm_swiglu_kernel,
        grid_spec=grid_spec,
        out_shape=jax.ShapeDtypeStruct((m, f), BF16),
        compiler_params=_cparams(("parallel", "arbitrary")),
        name="gmm_swiglu",
    )(tile_expert, a, w_gu, w_gu)


def _gmm_down_kernel(te_ref, a_ref, w_ref, g_ref, o_ref):
    y = jnp.dot(a_ref[...], w_ref[0], preferred_element_type=F32)
    o_ref[...] = y * g_ref[...]


def _gmm_down(tile_expert, a, w_d, row_gate, tm, tn):
    m, k = a.shape
    n = w_d.shape[2]
    grid_spec = pltpu.PrefetchScalarGridSpec(
        num_scalar_prefetch=1,
        grid=(n // tn, m // tm),
        in_specs=[pl.BlockSpec((tm, k), lambda j, i, te: (i, 0)),
                  pl.BlockSpec((1, k, tn), lambda j, i, te: (te[i], 0, j)),
                  pl.BlockSpec((tm, 1), lambda j, i, te: (i, 0))],
        out_specs=pl.BlockSpec((tm, tn), lambda j, i, te: (i, j)),
    )
    return pl.pallas_call(
        _gmm_down_kernel,
        grid_spec=grid_spec,
        out_shape=jax.ShapeDtypeStruct((m, n), F32),
        compiler_params=_cparams(("parallel", "arbitrary")),
        name="gmm_down",
    )(tile_expert, a, w_d, row_gate)


def _combine_norm_kernel(x_ref, y0_ref, y1_ref, w_ref, o_ref):
    x = x_ref[...] + (y0_ref[...] + y1_ref[...])
    y = x * lax.rsqrt(jnp.mean(x * x, axis=-1, keepdims=True) + EPS)
    o_ref[...] = y * w_ref[...]


def _combine_norm(x, y0, y1, w, tm=512):
    m, d = x.shape
    spec = pl.BlockSpec((tm, d), lambda i: (i, 0))
    return pl.pallas_call(
        _combine_norm_kernel,
        grid=(m // tm,),
        in_specs=[spec, spec, spec, pl.BlockSpec((1, d), lambda i: (0, 0))],
        out_specs=spec,
        out_shape=jax.ShapeDtypeStruct((m, d), F32),
        compiler_params=_cparams(("parallel",)),
        name="combine_norm",
    )(x, y0, y1, w.reshape(1, d).astype(F32))


def _moe_routing(route, n_experts, tm):
    m = route.shape[0]
    ids = route[:, :TOP_K].astype(jnp.int32)
    gates = route[:, TOP_K:2 * TOP_K]
    flat_e = ids.reshape(-1)
    onehot = (flat_e[:, None] == jnp.arange(n_experts)[None, :]).astype(jnp.int32)
    rank = jnp.cumsum(onehot, axis=0) - onehot
    counts = jnp.sum(onehot, axis=0)
    padded = (counts + tm - 1) // tm * tm
    ends = jnp.cumsum(padded)
    starts = ends - padded
    pos = jnp.sum(onehot * (starts[None, :] + rank), axis=1)
    rows = m * TOP_K + n_experts * tm
    src_token = jnp.zeros((rows,), jnp.int32).at[pos].set(jnp.arange(m * TOP_K, dtype=jnp.int32) // TOP_K)
    row_gate = jnp.zeros((rows,), F32).at[pos].set(gates.reshape(-1))
    tile_start = jnp.arange(rows // tm, dtype=jnp.int32) * tm
    tile_expert = jnp.minimum(jnp.sum((tile_start[:, None] >= ends[None, :]).astype(jnp.int32), axis=1),
                              n_experts - 1).astype(jnp.int32)
    return src_token, row_gate.reshape(rows, 1), tile_expert, pos.reshape(m, TOP_K)


def kernel(x, norm_w, final_norm_w, mix_in_w, gdn_conv_w, gdn_a_log, gdn_dt_bias, gdn_norm_w, hgrn_lb_logits, hgrn_norm_w, mix_out_w, ffn_gate_up_w, ffn_down_w, rg_in_w, rg_conv_w, rg_conv_b, rg_gate_a_w, rg_gate_a_b, rg_gate_x_w, rg_gate_x_b, rg_lambda, rg_out_w, moe_router_w, moe_gate_up_w, moe_down_w):
    bsz, s, d = x.shape
    m = bsz * s
    n_heads = gdn_a_log.shape[1]
    hw = n_heads * HEAD_DIM
    n_experts = moe_router_w.shape[2]
    xr = x.reshape(m, d)

    hgrn_lb = jnp.cumsum(jax.nn.softmax(hgrn_lb_logits.astype(F32), axis=0), axis=0)
    w_in = mix_in_w[0]
    w_main = jnp.concatenate([w_in[:, :4 * hw], w_in[:, 4 * hw + 2 * n_heads:]], axis=1).astype(BF16)
    w_ba = jnp.zeros((d, LANES), BF16).at[:, :2 * n_heads].set(
        w_in[:, 4 * hw:4 * hw + 2 * n_heads].astype(BF16))
    h = _rmsnorm(xr, norm_w[0, 0], BF16)
    proj = _mm(h, w_main, F32, 1024, 1024).reshape(bsz, s, 8 * hw)
    ba = _mm(h, w_ba, F32, 1024, LANES).reshape(bsz, s, LANES)
    o_a = _gdn(proj, ba, gdn_conv_w[0].astype(F32), gdn_a_log[0], gdn_dt_bias[0], gdn_norm_w[0], col0=0)
    o_b = _hgrn(proj, hgrn_lb[0], hgrn_norm_w[0], col0=4 * hw, n_heads=n_heads)
    w_out = mix_out_w[0].astype(BF16)
    xr = _mm_resid([(o_a.reshape(m, hw), w_out[:hw]), (o_b.reshape(m, hw), w_out[hw:])], xr, 1024, 1024)
    h = _rmsnorm(xr, norm_w[0, 1], BF16)
    act = _mm_swiglu(h, ffn_gate_up_w[0].astype(BF16), 1024, 512)
    xr = _mm_resid([(act, ffn_down_w[0].astype(BF16))], xr, 512, 1024)

    h = _rmsnorm(xr, norm_w[1, 0], BF16)
    yx = _mm(h, rg_in_w[0].astype(BF16), F32, 1024, 1024).reshape(bsz, s, -1)
    rec = _rglru(yx, rg_conv_w[0], rg_conv_b[0], rg_gate_a_w[0], rg_gate_a_b[0],
                 rg_gate_x_w[0], rg_gate_x_b[0], rg_lambda[0])
    xr = _mm_resid([(rec.reshape(m, d), rg_out_w[0].astype(BF16))], xr, 1024, 1024)
    h, route = _norm_router(xr, norm_w[1, 1], moe_router_w[0])
    tm = 256
    src_token, row_gate, tile_expert, pos = _moe_routing(route, n_experts, tm)
    hs = jnp.take(h, src_token, axis=0)
    act = _gmm_swiglu(tile_expert, hs, moe_gate_up_w[0].astype(BF16), tm, 512)
    ys = _gmm_down(tile_expert, act, moe_down_w[0].astype(BF16), row_gate, tm, 512)
    y0 = jnp.take(ys, pos[:, 0], axis=0)
    y1 = jnp.take(ys, pos[:, 1], axis=0)
    out = _combine_norm(xr, y0, y1, final_norm_w)
    return out.reshape(bsz, s, d)
```

```python
import functools

import jax
import jax.numpy as jnp
from jax import lax
from jax.experimental import pallas as pl
from jax.experimental.pallas import tpu as pltpu

F32 = jnp.float32
BF16 = jnp.bfloat16
HIGHEST = lax.Precision.HIGHEST

EPS = 1e-6
CHUNK = 64
CONV_W = 4
HEAD_DIM = 128
RG_BLOCK = 256
RG_C = 8.0
TOP_K = 2
LANES = 128
SUBLANES = 8
VMEM_LIMIT = 56 * 1024 * 1024

_NT = (((1,), (1,)), ((), ()))
_TN = (((0,), (0,)), ((), ()))


def _cparams(sem):
    return pltpu.CompilerParams(dimension_semantics=sem, vmem_limit_bytes=VMEM_LIMIT)


def _silu(x):
    return x * jax.nn.sigmoid(x)


def _softplus(x):
    return jnp.maximum(x, 0.0) + jnp.log1p(jnp.exp(-jnp.abs(x)))


def _bdot(a, b):
    return jnp.dot(a.astype(BF16), b.astype(BF16), preferred_element_type=F32)


def _bdot_g(a, b, dims):
    return lax.dot_general(a.astype(BF16), b.astype(BF16), dims, preferred_element_type=F32)


def _rmsnorm_kernel(x_ref, w_ref, o_ref):
    x = x_ref[...]
    y = x * lax.rsqrt(jnp.mean(x * x, axis=-1, keepdims=True) + EPS)
    o_ref[...] = (y * w_ref[...]).astype(o_ref.dtype)


def _rmsnorm(x, w, out_dtype, tm=512):
    m, d = x.shape
    return pl.pallas_call(
        _rmsnorm_kernel,
        grid=(m // tm,),
        in_specs=[pl.BlockSpec((tm, d), lambda i: (i, 0)),
                  pl.BlockSpec((1, d), lambda i: (0, 0))],
        out_specs=pl.BlockSpec((tm, d), lambda i: (i, 0)),
        out_shape=jax.ShapeDtypeStruct((m, d), out_dtype),
        compiler_params=_cparams(("parallel",)),
        name="rmsnorm",
    )(x, w.reshape(1, d).astype(F32))


def _mm_kernel(a_ref, w_ref, o_ref, w_s):
    @pl.when(pl.program_id(1) == 0)
    def _():
        w_s[...] = w_ref[...].astype(BF16)

    o_ref[...] = jnp.dot(a_ref[...], w_s[...], preferred_element_type=F32).astype(o_ref.dtype)


def _mm(a, w, out_dtype, tm, tn, n_cols=None):
    m, k = a.shape
    n = w.shape[1] if n_cols is None else n_cols
    return pl.pallas_call(
        _mm_kernel,
        grid=(n // tn, m // tm),
        in_specs=[pl.BlockSpec((tm, k), lambda j, i: (i, 0)),
                  pl.BlockSpec((k, tn), lambda j, i: (0, j))],
        out_specs=pl.BlockSpec((tm, tn), lambda j, i: (i, j)),
        out_shape=jax.ShapeDtypeStruct((m, n), out_dtype),
        scratch_shapes=[pltpu.VMEM((k, tn), BF16)],
        compiler_params=_cparams(("arbitrary", "arbitrary")),
        name="mm",
    )(a, w)


def _mm_swiglu_kernel(a_ref, wg_ref, wu_ref, o_ref, wg_s, wu_s):
    @pl.when(pl.program_id(1) == 0)
    def _():
        wg_s[...] = wg_ref[...].astype(BF16)
        wu_s[...] = wu_ref[...].astype(BF16)

    a = a_ref[...]
    g = jnp.dot(a, wg_s[...], preferred_element_type=F32)
    u = jnp.dot(a, wu_s[...], preferred_element_type=F32)
    o_ref[...] = (_silu(g) * u).astype(o_ref.dtype)


def _mm_swiglu(a, w_gu, tm, tn):
    m, k = a.shape
    f = w_gu.shape[1] // 2
    nb = f // tn
    return pl.pallas_call(
        _mm_swiglu_kernel,
        grid=(nb, m // tm),
        in_specs=[pl.BlockSpec((tm, k), lambda j, i: (i, 0)),
                  pl.BlockSpec((k, tn), lambda j, i: (0, j)),
                  pl.BlockSpec((k, tn), lambda j, i: (0, j + nb))],
        out_specs=pl.BlockSpec((tm, tn), lambda j, i: (i, j)),
        out_shape=jax.ShapeDtypeStruct((m, f), BF16),
        scratch_shapes=[pltpu.VMEM((k, tn), BF16)] * 2,
        compiler_params=_cparams(("arbitrary", "arbitrary")),
        name="mm_swiglu",
    )(a, w_gu, w_gu)


def _mm_resid_kernel(*refs, n_pairs):
    r_ref, o_ref = refs[2 * n_pairs], refs[2 * n_pairs + 1]
    w_s = refs[2 * n_pairs + 2:]

    @pl.when(pl.program_id(1) == 0)
    def _():
        for p in range(n_pairs):
            w_s[p][...] = refs[2 * p + 1][...].astype(BF16)

    acc = r_ref[...]
    for p in range(n_pairs):
        acc = acc + jnp.dot(refs[2 * p][...], w_s[p][...], preferred_element_type=F32)
    o_ref[...] = acc


def _mm_resid(a_list, w, resid, tm, tn):
    m, n = resid.shape
    in_specs, args, scratch = [], [], []
    for p, a in enumerate(a_list):
        k = a.shape[1]
        in_specs += [pl.BlockSpec((tm, k), lambda j, i: (i, 0)),
                     pl.BlockSpec((k, tn), lambda j, i, p=p: (p, j))]
        args += [a, w]
        scratch.append(pltpu.VMEM((k, tn), BF16))
    in_specs.append(pl.BlockSpec((tm, tn), lambda j, i: (i, j)))
    args.append(resid)
    return pl.pallas_call(
        functools.partial(_mm_resid_kernel, n_pairs=len(a_list)),
        grid=(n // tn, m // tm),
        in_specs=in_specs,
        out_specs=pl.BlockSpec((tm, tn), lambda j, i: (i, j)),
        out_shape=jax.ShapeDtypeStruct((m, n), F32),
        scratch_shapes=scratch,
        compiler_params=_cparams(("arbitrary", "arbitrary")),
        name="mm_resid",
    )(*args)


def _causal_conv(buf_ref, x, cw, t_len):
    buf_ref[pl.ds(SUBLANES, t_len), :] = x
    acc = cw[CONV_W - 1:CONV_W, :] * x
    for j in range(CONV_W - 1):
        off = SUBLANES - (CONV_W - 1) + j
        acc = acc + cw[j:j + 1, :] * buf_ref[pl.ds(off, t_len), :]
    buf_ref[pl.ds(0, SUBLANES), :] = x[t_len - SUBLANES:, :]
    return acc


def _tri_masks():
    row = lax.broadcasted_iota(jnp.int32, (CHUNK, CHUNK), 0)
    col = lax.broadcasted_iota(jnp.int32, (CHUNK, CHUNK), 1)
    return row >= col, row > col, row == col


def _gated_head_norm(o, nw, z):
    o = o * lax.rsqrt(jnp.mean(o * o, axis=-1, keepdims=True) + EPS) * nw
    return o * _silu(z)


def _gdn_kernel(q_ref, k_ref, v_ref, z_ref, ba_ref, cwq_ref, cwk_ref, cwv_ref, alog_ref, dtb_ref,
                nw_ref, o_ref, qbuf, kbuf, vbuf, qs, ks, vs, gcs, gct, bs, u_s, w_s, qk_s, qd_s, kd_s, s_ref,
                *, n_heads, t_len):
    t = pl.program_id(1)
    n_chunks = t_len // CHUNK
    width = n_heads * HEAD_DIM

    @pl.when(t == 0)
    def _():
        zeros = jnp.zeros((SUBLANES, width), F32)
        qbuf[pl.ds(0, SUBLANES), :] = zeros
        kbuf[pl.ds(0, SUBLANES), :] = zeros
        vbuf[pl.ds(0, SUBLANES), :] = zeros
        s_ref[...] = jnp.zeros_like(s_ref)

    qc = _silu(_causal_conv(qbuf, q_ref[0], cwq_ref[...], t_len))
    kc = _silu(_causal_conv(kbuf, k_ref[0], cwk_ref[...], t_len))
    vs[...] = _silu(_causal_conv(vbuf, v_ref[0], cwv_ref[...], t_len))
    for h in range(n_heads):
        sl = slice(h * HEAD_DIM, (h + 1) * HEAD_DIM)
        qh = qc[:, sl]
        kh = kc[:, sl]
        qs[:, sl] = qh * (lax.rsqrt(jnp.sum(qh * qh, axis=-1, keepdims=True) + EPS) * (HEAD_DIM ** -0.5))
        ks[:, sl] = kh * lax.rsqrt(jnp.sum(kh * kh, axis=-1, keepdims=True) + EPS)

    ba = ba_ref[0]
    g_log = -jnp.exp(alog_ref[...]) * _softplus(ba + dtb_ref[...])
    bs[...] = jax.nn.sigmoid(ba)
    row = lax.broadcasted_iota(jnp.int32, (t_len, t_len), 0)
    col = lax.broadcasted_iota(jnp.int32, (t_len, t_len), 1)
    block_tri = ((row // CHUNK == col // CHUNK) & (row >= col)).astype(F32)
    gc_blk = jnp.dot(block_tri, g_log, precision=HIGHEST, preferred_element_type=F32)
    gcs[...] = gc_blk
    for c in range(n_chunks):
        gct[c] = gc_blk[c * CHUNK:(c + 1) * CHUNK, :].T

    incl, strict, eye = _tri_masks()
    eye_f = eye.astype(F32)
    nw = nw_ref[...]

    heads = range(n_heads)
    hsl = [slice(h * HEAD_DIM, (h + 1) * HEAD_DIM) for h in heads]
    qsl = [slice(h * HEAD_DIM, h * HEAD_DIM + CHUNK) for h in heads]
    chunks_per_iter = 2
    assert n_chunks % chunks_per_iter == 0

    def pass_a(it, carry):
        items = []
        for cc in range(chunks_per_iter):
            c = it * chunks_per_iter + cc
            rows = pl.ds(pl.multiple_of(c * CHUNK, CHUNK), CHUNK)
            gc_all = gcs[rows, :]
            beta_all = bs[rows, :]
            gct_c = gct[c]
            for h in heads:
                gc = gc_all[:, n_heads + h:n_heads + h + 1]
                beta = beta_all[:, h:h + 1]
                gc_row = gct_c[n_heads + h:n_heads + h + 1, :]
                decay = jnp.where(incl, jnp.exp(jnp.where(incl, gc - gc_row, 0.0)), 0.0)
                items.append(dict(rows=rows, h=h, gc=gc, beta=beta, decay=decay,
                                  q=qs[rows, hsl[h]], k=ks[rows, hsl[h]]))
        for it_ in items:
            it_["kb"] = it_["k"].astype(BF16)
            it_["kk"] = lax.dot_general(it_["kb"], it_["kb"], _NT, preferred_element_type=F32)
        for it_ in items:
            it_["x"] = jnp.where(strict, -(it_["beta"] * it_["kk"] * it_["decay"]), 0.0)
            it_["p"] = eye_f + it_["x"]
        for _ in range(5):
            for it_ in items:
                xb = it_["x"].astype(BF16)
                it_["x"] = jnp.dot(xb, xb, preferred_element_type=F32)
            for it_ in items:
                it_["p"] = it_["p"] + _bdot(it_["p"], it_["x"])
        for it_ in items:
            egc = jnp.exp(it_["gc"])
            it_["egc"] = egc
            v = vs[it_["rows"], hsl[it_["h"]]]
            rhs = jnp.concatenate([v * it_["beta"], it_["k"] * (it_["beta"] * egc)], axis=-1)
            it_["sol"] = _bdot(it_["p"], rhs)
        for it_ in items:
            it_["qk"] = lax.dot_general(it_["q"].astype(BF16), it_["kb"], _NT,
                                        preferred_element_type=F32) * it_["decay"]
        for it_ in items:
            rows, h, gc = it_["rows"], it_["h"], it_["gc"]
            g_last = gc[CHUNK - 1:CHUNK, :]
            u_s[rows, hsl[h]] = it_["sol"][:, :HEAD_DIM]
            w_s[rows, hsl[h]] = it_["sol"][:, HEAD_DIM:].astype(BF16)
            qk_s[rows, qsl[h]] = it_["qk"].astype(BF16)
            qd_s[rows, hsl[h]] = (it_["q"] * it_["egc"]).astype(BF16)
            kd_s[rows, hsl[h]] = (it_["k"] * jnp.exp(g_last - gc)).astype(BF16)
        return carry

    lax.fori_loop(0, n_chunks // chunks_per_iter, pass_a, 0)

    def pass_b(c, carry):
        rows = pl.ds(pl.multiple_of(c * CHUNK, CHUNK), CHUNK)
        gc_all = gcs[rows, :]
        states = [s_ref[h] for h in heads]
        sbs = [s.astype(BF16) for s in states]
        v_new = [u_s[rows, hsl[h]] - jnp.dot(w_s[rows, hsl[h]], sbs[h], preferred_element_type=F32)
                 for h in heads]
        vbs = [v.astype(BF16) for v in v_new]
        for h in heads:
            g_last = gc_all[CHUNK - 1:CHUNK, n_heads + h:n_heads + h + 1]
            s_ref[h] = states[h] * jnp.exp(g_last) + lax.dot_general(kd_s[rows, hsl[h]], vbs[h], _TN,
                                                                      preferred_element_type=F32)
        outs = [jnp.dot(qd_s[rows, hsl[h]], sbs[h], preferred_element_type=F32)
                + jnp.dot(qk_s[rows, qsl[h]], vbs[h], preferred_element_type=F32) for h in heads]
        for h in heads:
            o = _gated_head_norm(outs[h], nw, z_ref[0, rows, hsl[h]])
            o_ref[0, rows, hsl[h]] = o.astype(o_ref.dtype)
        return carry

    lax.fori_loop(0, n_chunks, pass_b, 0)


def _gdn(proj, ba, conv_w, a_log, dt_bias, norm_w, *, col0, t_len=256):
    bsz, s, _ = proj.shape
    n_heads = a_log.shape[0]
    width = n_heads * HEAD_DIM
    cb = col0 // width

    def col_spec(group):
        return pl.BlockSpec((1, t_len, width), lambda b, t: (b, t, cb + group))

    def cw_spec(group):
        return pl.BlockSpec((CONV_W, width), lambda b, t: (0, group))

    row = jnp.zeros((1, LANES), F32)
    alog_row = row.at[0, n_heads:2 * n_heads].set(a_log.astype(F32))
    dtb_row = row.at[0, n_heads:2 * n_heads].set(dt_bias.astype(F32))
    small = pl.BlockSpec((1, LANES), lambda b, t: (0, 0))
    kern = functools.partial(_gdn_kernel, n_heads=n_heads, t_len=t_len)
    return pl.pallas_call(
        kern,
        grid=(bsz, s // t_len),
        in_specs=[col_spec(0), col_spec(1), col_spec(2), col_spec(3),
                  pl.BlockSpec((1, t_len, LANES), lambda b, t: (b, t, 0)),
                  cw_spec(0), cw_spec(1), cw_spec(2), small, small,
                  pl.BlockSpec((1, HEAD_DIM), lambda b, t: (0, 0))],
        out_specs=pl.BlockSpec((1, t_len, width), lambda b, t: (b, t, 0)),
        out_shape=jax.ShapeDtypeStruct((bsz, s, width), BF16),
        scratch_shapes=[pltpu.VMEM((SUBLANES + t_len, width), F32)] * 3
        + [pltpu.VMEM((t_len, width), F32)] * 3
        + [pltpu.VMEM((t_len, LANES), F32),
           pltpu.VMEM((t_len // CHUNK, LANES, CHUNK), F32),
           pltpu.VMEM((t_len, LANES), F32),
           pltpu.VMEM((t_len, width), F32)]
        + [pltpu.VMEM((t_len, width), BF16)] * 4
        + [pltpu.VMEM((n_heads, HEAD_DIM, HEAD_DIM), F32)],
        compiler_params=_cparams(("parallel", "arbitrary")),
        name="gdn",
    )(proj, proj, proj, proj, ba, conv_w, conv_w, conv_w, alog_row, dtb_row,
      norm_w.reshape(1, HEAD_DIM).astype(F32))


def _hgrn_kernel(q_ref, f_ref, i_ref, g_ref, lb_ref, nw_ref, o_ref, qs, ks, ls, st_ref, *, heads, t_len):
    t = pl.program_id(1)

    @pl.when(t == 0)
    def _():
        st_ref[...] = jnp.zeros_like(st_ref)

    lb = lb_ref[...]
    forget = lb + (1.0 - lb) * jax.nn.sigmoid(f_ref[0])
    ks[...] = 1.0 - forget
    qs[...] = _silu(q_ref[0]) * (HEAD_DIM ** -0.5)
    row = lax.broadcasted_iota(jnp.int32, (t_len, t_len), 0)
    col = lax.broadcasted_iota(jnp.int32, (t_len, t_len), 1)
    block_tri = ((row // CHUNK == col // CHUNK) & (row >= col)).astype(F32)
    ls[...] = jnp.dot(block_tri, jnp.log(forget), precision=HIGHEST, preferred_element_type=F32)

    incl, _, _ = _tri_masks()
    nw = nw_ref[...]
    mid = CHUNK // 2 - 1

    hsl = [slice(j * HEAD_DIM, (j + 1) * HEAD_DIM) for j in range(heads)]

    def chunk_body(c, carry):
        rows = pl.ds(pl.multiple_of(c * CHUNK, CHUNK), CHUNK)
        items = []
        for j in range(heads):
            b = ls[rows, hsl[j]]
            items.append(dict(b=b, b_mid=b[mid:mid + 1, :], b_last=b[CHUNK - 1:CHUNK, :],
                              q=qs[rows, hsl[j]], k=ks[rows, hsl[j]],
                              vb=i_ref[0, rows, hsl[j]].astype(BF16)))
        for it in items:
            it["att"] = jnp.where(incl, _bdot_g(it["q"] * jnp.exp(it["b"] - it["b_mid"]),
                                                it["k"] * jnp.exp(it["b_mid"] - it["b"]), _NT), 0.0)
        for j, it in enumerate(items):
            st = st_ref[j]
            it["o"] = _bdot_g(it["q"] * jnp.exp(it["b"]), st, _NT)
            st_ref[j] = st * jnp.exp(it["b_last"]) + lax.dot_general(
                it["vb"], (it["k"] * jnp.exp(it["b_last"] - it["b"])).astype(BF16), _TN,
                preferred_element_type=F32)
        for it in items:
            it["o"] = it["o"] + jnp.dot(it["att"].astype(BF16), it["vb"], preferred_element_type=F32)
        for j, it in enumerate(items):
            o = _gated_head_norm(it["o"], nw, g_ref[0, rows, hsl[j]])
            o_ref[0, rows, hsl[j]] = o.astype(o_ref.dtype)
        return carry

    lax.fori_loop(0, t_len // CHUNK, chunk_body, 0)


def _hgrn(proj, lb, norm_w, *, col0, n_heads, t_len=256):
    bsz, s, _ = proj.shape
    width = n_heads * HEAD_DIM
    cb = col0 // width

    def col_spec(group):
        return pl.BlockSpec((1, t_len, width), lambda b, t: (b, t, cb + group))

    kern = functools.partial(_hgrn_kernel, heads=n_heads, t_len=t_len)
    return pl.pallas_call(
        kern,
        grid=(bsz, s // t_len),
        in_specs=[col_spec(0), col_spec(1), col_spec(2), col_spec(3),
                  pl.BlockSpec((1, width), lambda b, t: (0, 0)),
                  pl.BlockSpec((1, HEAD_DIM), lambda b, t: (0, 0))],
        out_specs=pl.BlockSpec((1, t_len, width), lambda b, t: (b, t, 0)),
        out_shape=jax.ShapeDtypeStruct((bsz, s, width), BF16),
        scratch_shapes=[pltpu.VMEM((t_len, width), F32)] * 3
        + [pltpu.VMEM((n_heads, HEAD_DIM, HEAD_DIM), F32)],
        compiler_params=_cparams(("parallel", "arbitrary")),
        name="hgrn2",
    )(proj, proj, proj, proj, lb.reshape(1, width).astype(F32), norm_w.reshape(1, HEAD_DIM).astype(F32))


def _rglru_kernel(y_ref, x_ref, cw_ref, cb_ref, wa_ref, ba_ref, wx_ref, bx_ref, lam_ref, o_ref,
                  xbuf, a_s, b_s, h_s, hcar, *, blocks, t_len):
    t = pl.program_id(2)

    @pl.when(t == 0)
    def _():
        xbuf[pl.ds(0, SUBLANES), :] = jnp.zeros((SUBLANES, blocks * RG_BLOCK), F32)
        hcar[...] = jnp.zeros_like(hcar)

    xc = _causal_conv(xbuf, x_ref[0], cw_ref[...], t_len) + cb_ref[...]
    for n in range(blocks):
        sl = slice(n * RG_BLOCK, (n + 1) * RG_BLOCK)
        xb = xc[:, sl]
        r = jax.nn.sigmoid(_bdot(xb, wa_ref[n]) + ba_ref[:, sl])
        gi = jax.nn.sigmoid(_bdot(xb, wx_ref[n]) + bx_ref[:, sl])
        log_a = (-RG_C) * r * _softplus(-lam_ref[:, sl])
        a = jnp.exp(log_a)
        a_s[:, sl] = a
        one_minus_a2 = -jnp.tanh(log_a) * (a * a + 1.0)
        b_s[:, sl] = jnp.sqrt(jnp.maximum(one_minus_a2, 0.0)) * (gi * xb)

    def row_body(i, h):
        h = a_s[pl.ds(i, 1), :] * h + b_s[pl.ds(i, 1), :]
        h_s[pl.ds(i, 1), :] = h
        return h

    hcar[...] = lax.fori_loop(0, t_len, row_body, hcar[...])
    o_ref[0] = (jax.nn.gelu(y_ref[0], approximate=True) * h_s[...]).astype(o_ref.dtype)


def _rglru(yx, conv_w, conv_b, wa, ba, wx, bx, lam, *, blocks_per_step=2, t_len=256):
    bsz, s, w2 = yx.shape
    width = w2 // 2
    gw = blocks_per_step * RG_BLOCK
    n_g = width // gw
    vec = lambda a: a.reshape(1, width).astype(F32)
    vspec = pl.BlockSpec((1, gw), lambda b, g, t: (0, g))
    wspec = pl.BlockSpec((blocks_per_step, RG_BLOCK, RG_BLOCK), lambda b, g, t: (g, 0, 0))
    kern = functools.partial(_rglru_kernel, blocks=blocks_per_step, t_len=t_len)
    return pl.pallas_call(
        kern,
        grid=(bsz, n_g, s // t_len),
        in_specs=[pl.BlockSpec((1, t_len, gw), lambda b, g, t: (b, t, g)),
                  pl.BlockSpec((1, t_len, gw), lambda b, g, t: (b, t, n_g + g)),
                  pl.BlockSpec((CONV_W, gw), lambda b, g, t: (0, g)),
                  vspec, wspec, vspec, wspec, vspec, vspec],
        out_specs=pl.BlockSpec((1, t_len, gw), lambda b, g, t: (b, t, g)),
        out_shape=jax.ShapeDtypeStruct((bsz, s, width), BF16),
        scratch_shapes=[pltpu.VMEM((SUBLANES + t_len, gw), F32)]
        + [pltpu.VMEM((t_len, gw), F32)] * 3
        + [pltpu.VMEM((1, gw), F32)],
        compiler_params=_cparams(("parallel", "parallel", "arbitrary")),
        name="rglru",
    )(yx, yx, conv_w.astype(F32), vec(conv_b), wa.astype(BF16), vec(ba), wx.astype(BF16), vec(bx), vec(lam))


def _norm_router_kernel(x_ref, nw_ref, w_ref, o_ref, *, n_experts):
    x = x_ref[...]
    y = x * lax.rsqrt(jnp.mean(x * x, axis=-1, keepdims=True) + EPS) * nw_ref[...]
    logits = jnp.dot(y, w_ref[...], precision=HIGHEST, preferred_element_type=F32)
    lane = lax.broadcasted_iota(jnp.int32, logits.shape, 1).astype(F32)
    neg = jnp.float32(-jnp.inf)
    l1 = jnp.where(lane < n_experts, logits, neg)
    m1 = jnp.max(l1, axis=-1, keepdims=True)
    i1 = jnp.min(jnp.where(l1 == m1, lane, float(LANES)), axis=-1, keepdims=True)
    l2 = jnp.where(lane == i1, neg, l1)
    m2 = jnp.max(l2, axis=-1, keepdims=True)
    i2 = jnp.min(jnp.where(l2 == m2, lane, float(LANES)), axis=-1, keepdims=True)
    e2 = jnp.exp(m2 - m1)
    g1 = 1.0 / (1.0 + e2)
    g2 = e2 / (1.0 + e2)
    out = jnp.where(lane == 0, i1, 0.0)
    out = jnp.where(lane == 1, i2, out)
    out = jnp.where(lane == 2, g1, out)
    out = jnp.where(lane == 3, g2, out)
    o_ref[...] = out


def _norm_router(x, nw, router_w, tm=512):
    m, d = x.shape
    n_experts = router_w.shape[1]
    w = jnp.zeros((d, LANES), F32).at[:, :n_experts].set(router_w.astype(F32))
    return pl.pallas_call(
        functools.partial(_norm_router_kernel, n_experts=n_experts),
        grid=(m // tm,),
        in_specs=[pl.BlockSpec((tm, d), lambda i: (i, 0)),
                  pl.BlockSpec((1, d), lambda i: (0, 0)),
                  pl.BlockSpec((d, LANES), lambda i: (0, 0))],
        out_specs=pl.BlockSpec((tm, LANES), lambda i: (i, 0)),
        out_shape=jax.ShapeDtypeStruct((m, LANES), F32),
        compiler_params=_cparams(("parallel",)),
        name="norm_router",
    )(x, nw.reshape(1, d).astype(F32), w)


def _start_row_gather(idx_ref, base, src_hbm, dst, sem, n_rows):
    def body(r, carry):
        row = idx_ref[base + r]
        pltpu.make_async_copy(src_hbm.at[pl.ds(row, 1), :], dst.at[pl.ds(r, 1), :], sem).start()
        return carry
    lax.fori_loop(0, n_rows, body, 0, unroll=8)


def _wait_row_gather(src_hbm, dst, sem, n_rows):
    def body(r, carry):
        pltpu.make_async_copy(src_hbm.at[pl.ds(0, 1), :], dst.at[pl.ds(r, 1), :], sem).wait()
        return carry
    lax.fori_loop(0, n_rows, body, 0, unroll=8)


def _gather_norm_kernel(idx_ref, nv_ref, x_hbm, nw_ref, o_ref, buf, sem, *, tm):
    i = pl.program_id(0)
    slot = i % 2

    @pl.when(i == 0)
    def _():
        _start_row_gather(idx_ref, 0, x_hbm, buf.at[0], sem.at[0], tm)

    @pl.when(i + 1 < nv_ref[0])
    def _():
        _start_row_gather(idx_ref, (i + 1) * tm, x_hbm, buf.at[1 - slot], sem.at[1 - slot], tm)

    @pl.when(i < nv_ref[0])
    def _():
        _wait_row_gather(x_hbm, buf.at[slot], sem.at[slot], tm)
        x = buf[slot]
        y = x * lax.rsqrt(jnp.mean(x * x, axis=-1, keepdims=True) + EPS) * nw_ref[...]
        o_ref[...] = y.astype(o_ref.dtype)

    @pl.when(i >= nv_ref[0])
    def _():
        o_ref[...] = jnp.zeros_like(o_ref)


def _gather_norm(src_token, n_valid, x, nw, tm):
    rows = src_token.shape[0]
    d = x.shape[1]
    grid_spec = pltpu.PrefetchScalarGridSpec(
        num_scalar_prefetch=2,
        grid=(rows // tm,),
        in_specs=[pl.BlockSpec(memory_space=pl.ANY),
                  pl.BlockSpec((1, d), lambda i, idx, nv: (0, 0))],
        out_specs=pl.BlockSpec((tm, d), lambda i, idx, nv: (i, 0)),
        scratch_shapes=[pltpu.VMEM((2, tm, d), F32), pltpu.SemaphoreType.DMA((2,))],
    )
    return pl.pallas_call(
        functools.partial(_gather_norm_kernel, tm=tm),
        grid_spec=grid_spec,
        out_shape=jax.ShapeDtypeStruct((rows, d), BF16),
        compiler_params=_cparams(("arbitrary",)),
        name="gather_norm",
    )(src_token, n_valid, x, nw.reshape(1, d).astype(F32))


def _expert_changed(te_ref, i):
    return jnp.logical_or(i == 0, te_ref[i] != te_ref[jnp.maximum(i - 1, 0)])


def _gmm_swiglu_kernel(te_ref, nv_ref, a_ref, wg_ref, wu_ref, o_ref, wg_s, wu_s):
    i = pl.program_id(1)

    @pl.when(_expert_changed(te_ref, i))
    def _():
        wg_s[...] = wg_ref[0].astype(BF16)
        wu_s[...] = wu_ref[0].astype(BF16)

    @pl.when(i < nv_ref[0])
    def _():
        a = a_ref[...]
        g = jnp.dot(a, wg_s[...], preferred_element_type=F32)
        u = jnp.dot(a, wu_s[...], preferred_element_type=F32)
        o_ref[...] = (_silu(g) * u).astype(o_ref.dtype)

    @pl.when(i >= nv_ref[0])
    def _():
        o_ref[...] = jnp.zeros_like(o_ref)


def _gmm_swiglu(tile_expert, n_valid, a, w_gu, tm, tn):
    m, k = a.shape
    f = w_gu.shape[2] // 2
    nb = f // tn
    last = lambda i, nv: jnp.minimum(i, nv[0] - 1)
    grid_spec = pltpu.PrefetchScalarGridSpec(
        num_scalar_prefetch=2,
        grid=(nb, m // tm),
        in_specs=[pl.BlockSpec((tm, k), lambda j, i, te, nv: (last(i, nv), 0)),
                  pl.BlockSpec((1, k, tn), lambda j, i, te, nv: (te[i], 0, j)),
                  pl.BlockSpec((1, k, tn), lambda j, i, te, nv: (te[i], 0, j + nb))],
        out_specs=pl.BlockSpec((tm, tn), lambda j, i, te, nv: (i, j)),
        scratch_shapes=[pltpu.VMEM((k, tn), BF16)] * 2,
    )
    return pl.pallas_call(
        _gmm_swiglu_kernel,
        grid_spec=grid_spec,
        out_shape=jax.ShapeDtypeStruct((m, f), BF16),
        compiler_params=_cparams(("arbitrary", "arbitrary")),
        name="gmm_swiglu",
    )(tile_expert, n_valid, a, w_gu, w_gu)


def _gmm_down_kernel(te_ref, nv_ref, a_ref, w_ref, g_ref, o_ref, w_s):
    i = pl.program_id(1)

    @pl.when(_expert_changed(te_ref, i))
    def _():
        w_s[...] = w_ref[0].astype(BF16)

    @pl.when(i < nv_ref[0])
    def _():
        y = jnp.dot(a_ref[...], w_s[...], preferred_element_type=F32)
        o_ref[...] = y * g_ref[...]

    @pl.when(i >= nv_ref[0])
    def _():
        o_ref[...] = jnp.zeros_like(o_ref)


def _gmm_down(tile_expert, n_valid, a, w_d, row_gate, tm, tn):
    m, k = a.shape
    n = w_d.shape[2]
    last = lambda i, nv: jnp.minimum(i, nv[0] - 1)
    grid_spec = pltpu.PrefetchScalarGridSpec(
        num_scalar_prefetch=2,
        grid=(n // tn, m // tm),
        in_specs=[pl.BlockSpec((tm, k), lambda j, i, te, nv: (last(i, nv), 0)),
                  pl.BlockSpec((1, k, tn), lambda j, i, te, nv: (te[i], 0, j)),
                  pl.BlockSpec((tm, 1), lambda j, i, te, nv: (i, 0))],
        out_specs=pl.BlockSpec((tm, tn), lambda j, i, te, nv: (i, j)),
        scratch_shapes=[pltpu.VMEM((k, tn), BF16)],
    )
    return pl.pallas_call(
        _gmm_down_kernel,
        grid_spec=grid_spec,
        out_shape=jax.ShapeDtypeStruct((m, n), F32),
        compiler_params=_cparams(("arbitrary", "arbitrary")),
        name="gmm_down",
    )(tile_expert, n_valid, a, w_d, row_gate)


def _combine_norm_kernel(pos_ref, x_ref, ys_hbm, w_ref, o_ref, buf, sem, *, tm):
    i = pl.program_id(0)
    n = pl.num_programs(0)
    slot = i % 2

    def start(tile, s):
        for kk in range(TOP_K):
            _start_row_gather(pos_ref, kk * (n * tm) + tile * tm, ys_hbm, buf.at[s, kk], sem.at[s], tm)

    @pl.when(i == 0)
    def _():
        start(0, 0)

    @pl.when(i + 1 < n)
    def _():
        start(i + 1, 1 - slot)

    for kk in range(TOP_K):
        _wait_row_gather(ys_hbm, buf.at[slot, kk], sem.at[slot], tm)
    x = x_ref[...] + (buf[slot, 0] + buf[slot, 1])
    y = x * lax.rsqrt(jnp.mean(x * x, axis=-1, keepdims=True) + EPS)
    o_ref[...] = y * w_ref[...]


def _combine_norm(pos, x, ys, w, tm=256):
    m, d = x.shape
    grid_spec = pltpu.PrefetchScalarGridSpec(
        num_scalar_prefetch=1,
        grid=(m // tm,),
        in_specs=[pl.BlockSpec((tm, d), lambda i, p: (i, 0)),
                  pl.BlockSpec(memory_space=pl.ANY),
                  pl.BlockSpec((1, d), lambda i, p: (0, 0))],
        out_specs=pl.BlockSpec((tm, d), lambda i, p: (i, 0)),
        scratch_shapes=[pltpu.VMEM((2, TOP_K, tm, d), F32), pltpu.SemaphoreType.DMA((2,))],
    )
    return pl.pallas_call(
        functools.partial(_combine_norm_kernel, tm=tm),
        grid_spec=grid_spec,
        out_shape=jax.ShapeDtypeStruct((m, d), F32),
        compiler_params=_cparams(("arbitrary",)),
        name="combine_norm",
    )(pos, x, ys, w.reshape(1, d).astype(F32))


def _moe_routing(route, n_experts, tm):
    m = route.shape[0]
    ids = route[:, :TOP_K].astype(jnp.int32)
    gates = route[:, TOP_K:2 * TOP_K]
    flat_e = ids.reshape(-1)
    onehot = (flat_e[:, None] == jnp.arange(n_experts)[None, :]).astype(jnp.int32)
    rank = jnp.cumsum(onehot, axis=0) - onehot
    counts = jnp.sum(onehot, axis=0)
    padded = (counts + tm - 1) // tm * tm
    ends = jnp.cumsum(padded)
    starts = ends - padded
    pos = jnp.sum(onehot * (starts[None, :] + rank), axis=1)
    rows = m * TOP_K + n_experts * tm
    n_tiles = rows // tm
    src_token = jnp.zeros((rows,), jnp.int32).at[pos].set(jnp.arange(m * TOP_K, dtype=jnp.int32) // TOP_K)
    row_gate = jnp.zeros((rows,), F32).at[pos].set(gates.reshape(-1))
    n_valid = (ends[n_experts - 1] // tm).astype(jnp.int32)
    tile_start = jnp.minimum(jnp.arange(n_tiles, dtype=jnp.int32), n_valid - 1) * tm
    tile_expert = jnp.sum((tile_start[:, None] >= ends[None, :]).astype(jnp.int32), axis=1).astype(jnp.int32)
    pos_kmajor = pos.reshape(m, TOP_K).T.reshape(-1).astype(jnp.int32)
    return src_token, row_gate.reshape(rows, 1), tile_expert, n_valid.reshape(1), pos_kmajor


def kernel(x, norm_w, final_norm_w, mix_in_w, gdn_conv_w, gdn_a_log, gdn_dt_bias, gdn_norm_w, hgrn_lb_logits, hgrn_norm_w, mix_out_w, ffn_gate_up_w, ffn_down_w, rg_in_w, rg_conv_w, rg_conv_b, rg_gate_a_w, rg_gate_a_b, rg_gate_x_w, rg_gate_x_b, rg_lambda, rg_out_w, moe_router_w, moe_gate_up_w, moe_down_w):
    bsz, s, d = x.shape
    m = bsz * s
    n_heads = gdn_a_log.shape[1]
    hw = n_heads * HEAD_DIM
    n_experts = moe_router_w.shape[2]
    xr = x.reshape(m, d)

    hgrn_lb = jnp.cumsum(jax.nn.softmax(hgrn_lb_logits.astype(F32), axis=0), axis=0)
    w_in = mix_in_w[0]
    w_b = w_in[:, 4 * hw + 2 * n_heads:]
    w_ba = jnp.zeros((d, LANES), F32).at[:, :2 * n_heads].set(w_in[:, 4 * hw:4 * hw + 2 * n_heads])
    h = _rmsnorm(xr, norm_w[0, 0], BF16)
    proj_a = _mm(h, w_in, F32, 1024, 1024, n_cols=4 * hw).reshape(bsz, s, 4 * hw)
    proj_b = _mm(h, w_b, F32, 1024, 1024).reshape(bsz, s, 4 * hw)
    ba = _mm(h, w_ba, F32, 1024, LANES).reshape(bsz, s, LANES)
    o_a = _gdn(proj_a, ba, gdn_conv_w[0].astype(F32), gdn_a_log[0], gdn_dt_bias[0], gdn_norm_w[0], col0=0)
    o_b = _hgrn(proj_b, hgrn_lb[0], hgrn_norm_w[0], col0=0, n_heads=n_heads)
    xr = _mm_resid([o_a.reshape(m, hw), o_b.reshape(m, hw)], mix_out_w[0], xr, 1024, 1024)
    h = _rmsnorm(xr, norm_w[0, 1], BF16)
    act = _mm_swiglu(h, ffn_gate_up_w[0], 1024, 512)
    xr = _mm_resid([act], ffn_down_w[0], xr, 512, 512)

    h = _rmsnorm(xr, norm_w[1, 0], BF16)
    yx = _mm(h, rg_in_w[0], F32, 1024, 1024).reshape(bsz, s, -1)
    rec = _rglru(yx, rg_conv_w[0], rg_conv_b[0], rg_gate_a_w[0], rg_gate_a_b[0],
                 rg_gate_x_w[0], rg_gate_x_b[0], rg_lambda[0])
    xr = _mm_resid([rec.reshape(m, d)], rg_out_w[0], xr, 1024, 1024)
    route = _norm_router(xr, norm_w[1, 1], moe_router_w[0])
    tm = 256
    src_token, row_gate, tile_expert, n_valid, pos = _moe_routing(route, n_experts, tm)
    hs = _gather_norm(src_token, n_valid, xr, norm_w[1, 1], tm)
    act = _gmm_swiglu(tile_expert, n_valid, hs, moe_gate_up_w[0], tm, 512)
    ys = _gmm_down(tile_expert, n_valid, act, moe_down_w[0], row_gate, tm, 512)
    out = _combine_norm(pos, xr, ys, final_norm_w)
    return out.reshape(bsz, s, d)
```

```python
import functools

import jax
import jax.numpy as jnp
from jax import lax
from jax.experimental import pallas as pl
from jax.experimental.pallas import tpu as pltpu

F32 = jnp.float32
BF16 = jnp.bfloat16
HIGHEST = lax.Precision.HIGHEST

EPS = 1e-6
CHUNK = 64
CONV_W = 4
HEAD_DIM = 128
RG_BLOCK = 256
RG_C = 8.0
TOP_K = 2
LANES = 128
SUBLANES = 8
VMEM_LIMIT = 56 * 1024 * 1024
VMEM_LIMIT_MAX = 60 * 1024 * 1024
MOE_CHUNK = 1024
MOE_SUB = 256

_NT = (((1,), (1,)), ((), ()))
_TN = (((0,), (0,)), ((), ()))


def _cparams(sem):
    return pltpu.CompilerParams(dimension_semantics=sem, vmem_limit_bytes=VMEM_LIMIT)


def _silu(x):
    return x * jax.nn.sigmoid(x)


def _softplus(x):
    return jnp.maximum(x, 0.0) + jnp.log1p(jnp.exp(-jnp.abs(x)))


def _bdot(a, b):
    return jnp.dot(a.astype(BF16), b.astype(BF16), preferred_element_type=F32)


def _bdot_g(a, b, dims):
    return lax.dot_general(a.astype(BF16), b.astype(BF16), dims, preferred_element_type=F32)


def _rmsnorm_kernel(x_ref, w_ref, o_ref):
    x = x_ref[...]
    y = x * lax.rsqrt(jnp.mean(x * x, axis=-1, keepdims=True) + EPS)
    o_ref[...] = (y * w_ref[...]).astype(o_ref.dtype)


def _rmsnorm(x, w, out_dtype, tm=512):
    m, d = x.shape
    return pl.pallas_call(
        _rmsnorm_kernel,
        grid=(m // tm,),
        in_specs=[pl.BlockSpec((tm, d), lambda i: (i, 0)),
                  pl.BlockSpec((1, d), lambda i: (0, 0))],
        out_specs=pl.BlockSpec((tm, d), lambda i: (i, 0)),
        out_shape=jax.ShapeDtypeStruct((m, d), out_dtype),
        compiler_params=_cparams(("parallel",)),
        name="rmsnorm",
    )(x, w.reshape(1, d).astype(F32))


def _mm_kernel(a_ref, w_ref, o_ref, w_s):
    @pl.when(pl.program_id(1) == 0)
    def _():
        w_s[...] = w_ref[...].astype(BF16)

    o_ref[...] = jnp.dot(a_ref[...], w_s[...], preferred_element_type=F32).astype(o_ref.dtype)


def _mm(a, w, out_dtype, tm, tn, n_cols=None):
    m, k = a.shape
    n = w.shape[1] if n_cols is None else n_cols
    return pl.pallas_call(
        _mm_kernel,
        grid=(n // tn, m // tm),
        in_specs=[pl.BlockSpec((tm, k), lambda j, i: (i, 0)),
                  pl.BlockSpec((k, tn), lambda j, i: (0, j))],
        out_specs=pl.BlockSpec((tm, tn), lambda j, i: (i, j)),
        out_shape=jax.ShapeDtypeStruct((m, n), out_dtype),
        scratch_shapes=[pltpu.VMEM((k, tn), BF16)],
        compiler_params=_cparams(("arbitrary", "arbitrary")),
        name="mm",
    )(a, w)


def _mm_swiglu_kernel(a_ref, wg_ref, wu_ref, o_ref, wg_s, wu_s):
    @pl.when(pl.program_id(1) == 0)
    def _():
        wg_s[...] = wg_ref[...].astype(BF16)
        wu_s[...] = wu_ref[...].astype(BF16)

    a = a_ref[...]
    g = jnp.dot(a, wg_s[...], preferred_element_type=F32)
    u = jnp.dot(a, wu_s[...], preferred_element_type=F32)
    o_ref[...] = (_silu(g) * u).astype(o_ref.dtype)


def _mm_swiglu(a, w_gu, tm, tn):
    m, k = a.shape
    f = w_gu.shape[1] // 2
    nb = f // tn
    return pl.pallas_call(
        _mm_swiglu_kernel,
        grid=(nb, m // tm),
        in_specs=[pl.BlockSpec((tm, k), lambda j, i: (i, 0)),
                  pl.BlockSpec((k, tn), lambda j, i: (0, j)),
                  pl.BlockSpec((k, tn), lambda j, i: (0, j + nb))],
        out_specs=pl.BlockSpec((tm, tn), lambda j, i: (i, j)),
        out_shape=jax.ShapeDtypeStruct((m, f), BF16),
        scratch_shapes=[pltpu.VMEM((k, tn), BF16)] * 2,
        compiler_params=_cparams(("arbitrary", "arbitrary")),
        name="mm_swiglu",
    )(a, w_gu, w_gu)


def _mm_resid_kernel(*refs, n_pairs):
    r_ref, o_ref = refs[2 * n_pairs], refs[2 * n_pairs + 1]
    w_s = refs[2 * n_pairs + 2:]

    @pl.when(pl.program_id(1) == 0)
    def _():
        for p in range(n_pairs):
            w_s[p][...] = refs[2 * p + 1][...].astype(BF16)

    acc = r_ref[...]
    for p in range(n_pairs):
        acc = acc + jnp.dot(refs[2 * p][...], w_s[p][...], preferred_element_type=F32)
    o_ref[...] = acc


def _mm_resid(a_list, w, resid, tm, tn):
    m, n = resid.shape
    in_specs, args, scratch = [], [], []
    for p, a in enumerate(a_list):
        k = a.shape[1]
        in_specs += [pl.BlockSpec((tm, k), lambda j, i: (i, 0)),
                     pl.BlockSpec((k, tn), lambda j, i, p=p: (p, j))]
        args += [a, w]
        scratch.append(pltpu.VMEM((k, tn), BF16))
    in_specs.append(pl.BlockSpec((tm, tn), lambda j, i: (i, j)))
    args.append(resid)
    return pl.pallas_call(
        functools.partial(_mm_resid_kernel, n_pairs=len(a_list)),
        grid=(n // tn, m // tm),
        in_specs=in_specs,
        out_specs=pl.BlockSpec((tm, tn), lambda j, i: (i, j)),
        out_shape=jax.ShapeDtypeStruct((m, n), F32),
        scratch_shapes=scratch,
        compiler_params=_cparams(("arbitrary", "arbitrary")),
        name="mm_resid",
    )(*args)


def _causal_conv(buf_ref, x, cw, t_len):
    buf_ref[pl.ds(SUBLANES, t_len), :] = x
    acc = cw[CONV_W - 1:CONV_W, :] * x
    for j in range(CONV_W - 1):
        off = SUBLANES - (CONV_W - 1) + j
        acc = acc + cw[j:j + 1, :] * buf_ref[pl.ds(off, t_len), :]
    buf_ref[pl.ds(0, SUBLANES), :] = x[t_len - SUBLANES:, :]
    return acc


def _tri_masks():
    row = lax.broadcasted_iota(jnp.int32, (CHUNK, CHUNK), 0)
    col = lax.broadcasted_iota(jnp.int32, (CHUNK, CHUNK), 1)
    return row >= col, row > col, row == col


def _gated_head_norm(o, nw, z):
    o = o * lax.rsqrt(jnp.mean(o * o, axis=-1, keepdims=True) + EPS) * nw
    return o * _silu(z)


def _gdn_kernel(q_ref, k_ref, v_ref, z_ref, ba_ref, cwq_ref, cwk_ref, cwv_ref, alog_ref, dtb_ref,
                nw_ref, o_ref, qbuf, kbuf, vbuf, qs, ks, vs, gcs, gct, bs, u_s, w_s, qk_s, qd_s, kd_s, s_ref,
                *, n_heads, t_len):
    t = pl.program_id(1)
    n_chunks = t_len // CHUNK
    width = n_heads * HEAD_DIM

    @pl.when(t == 0)
    def _():
        zeros = jnp.zeros((SUBLANES, width), F32)
        qbuf[pl.ds(0, SUBLANES), :] = zeros
        kbuf[pl.ds(0, SUBLANES), :] = zeros
        vbuf[pl.ds(0, SUBLANES), :] = zeros
        s_ref[...] = jnp.zeros_like(s_ref)

    qc = _silu(_causal_conv(qbuf, q_ref[0], cwq_ref[...], t_len))
    kc = _silu(_causal_conv(kbuf, k_ref[0], cwk_ref[...], t_len))
    vs[...] = _silu(_causal_conv(vbuf, v_ref[0], cwv_ref[...], t_len))
    for h in range(n_heads):
        sl = slice(h * HEAD_DIM, (h + 1) * HEAD_DIM)
        qh = qc[:, sl]
        kh = kc[:, sl]
        qs[:, sl] = qh * (lax.rsqrt(jnp.sum(qh * qh, axis=-1, keepdims=True) + EPS) * (HEAD_DIM ** -0.5))
        ks[:, sl] = kh * lax.rsqrt(jnp.sum(kh * kh, axis=-1, keepdims=True) + EPS)

    ba = ba_ref[0]
    g_log = -jnp.exp(alog_ref[...]) * _softplus(ba + dtb_ref[...])
    bs[...] = jax.nn.sigmoid(ba)
    row = lax.broadcasted_iota(jnp.int32, (t_len, t_len), 0)
    col = lax.broadcasted_iota(jnp.int32, (t_len, t_len), 1)
    block_tri = ((row // CHUNK == col // CHUNK) & (row >= col)).astype(F32)
    gc_blk = jnp.dot(block_tri, g_log, precision=HIGHEST, preferred_element_type=F32)
    gcs[...] = gc_blk
    for c in range(n_chunks):
        gct[c] = gc_blk[c * CHUNK:(c + 1) * CHUNK, :].T

    incl, strict, eye = _tri_masks()
    eye_f = eye.astype(F32)
    nw = nw_ref[...]

    heads = range(n_heads)
    hsl = [slice(h * HEAD_DIM, (h + 1) * HEAD_DIM) for h in heads]
    qsl = [slice(h * HEAD_DIM, h * HEAD_DIM + CHUNK) for h in heads]
    chunks_per_iter = 2
    assert n_chunks % chunks_per_iter == 0

    def pass_a(it, carry):
        items = []
        for cc in range(chunks_per_iter):
            c = it * chunks_per_iter + cc
            rows = pl.ds(pl.multiple_of(c * CHUNK, CHUNK), CHUNK)
            gc_all = gcs[rows, :]
            beta_all = bs[rows, :]
            gct_c = gct[c]
            for h in heads:
                gc = gc_all[:, n_heads + h:n_heads + h + 1]
                beta = beta_all[:, h:h + 1]
                gc_row = gct_c[n_heads + h:n_heads + h + 1, :]
                decay = jnp.where(incl, jnp.exp(jnp.where(incl, gc - gc_row, 0.0)), 0.0)
                items.append(dict(rows=rows, h=h, gc=gc, beta=beta, decay=decay,
                                  q=qs[rows, hsl[h]], k=ks[rows, hsl[h]]))
        for it_ in items:
            it_["kb"] = it_["k"].astype(BF16)
            it_["kk"] = lax.dot_general(it_["kb"], it_["kb"], _NT, preferred_element_type=F32)
        for it_ in items:
            it_["x"] = jnp.where(strict, -(it_["beta"] * it_["kk"] * it_["decay"]), 0.0)
            it_["p"] = eye_f + it_["x"]
        for _ in range(5):
            for it_ in items:
                xb = it_["x"].astype(BF16)
                it_["x"] = jnp.dot(xb, xb, preferred_element_type=F32)
            for it_ in items:
                it_["p"] = it_["p"] + _bdot(it_["p"], it_["x"])
        for it_ in items:
            egc = jnp.exp(it_["gc"])
            it_["egc"] = egc
            v = vs[it_["rows"], hsl[it_["h"]]]
            rhs = jnp.concatenate([v * it_["beta"], it_["k"] * (it_["beta"] * egc)], axis=-1)
            it_["sol"] = _bdot(it_["p"], rhs)
        for it_ in items:
            it_["qk"] = lax.dot_general(it_["q"].astype(BF16), it_["kb"], _NT,
                                        preferred_element_type=F32) * it_["decay"]
        for it_ in items:
            rows, h, gc = it_["rows"], it_["h"], it_["gc"]
            g_last = gc[CHUNK - 1:CHUNK, :]
            u_s[rows, hsl[h]] = it_["sol"][:, :HEAD_DIM]
            w_s[rows, hsl[h]] = it_["sol"][:, HEAD_DIM:].astype(BF16)
            qk_s[rows, qsl[h]] = it_["qk"].astype(BF16)
            qd_s[rows, hsl[h]] = (it_["q"] * it_["egc"]).astype(BF16)
            kd_s[rows, hsl[h]] = (it_["k"] * jnp.exp(g_last - gc)).astype(BF16)
        return carry

    lax.fori_loop(0, n_chunks // chunks_per_iter, pass_a, 0)

    def pass_b(c, carry):
        rows = pl.ds(pl.multiple_of(c * CHUNK, CHUNK), CHUNK)
        gc_all = gcs[rows, :]
        states = [s_ref[h] for h in heads]
        sbs = [s.astype(BF16) for s in states]
        v_new = [u_s[rows, hsl[h]] - jnp.dot(w_s[rows, hsl[h]], sbs[h], preferred_element_type=F32)
                 for h in heads]
        vbs = [v.astype(BF16) for v in v_new]
        for h in heads:
            g_last = gc_all[CHUNK - 1:CHUNK, n_heads + h:n_heads + h + 1]
            s_ref[h] = states[h] * jnp.exp(g_last) + lax.dot_general(kd_s[rows, hsl[h]], vbs[h], _TN,
                                                                      preferred_element_type=F32)
        outs = [jnp.dot(qd_s[rows, hsl[h]], sbs[h], preferred_element_type=F32)
                + jnp.dot(qk_s[rows, qsl[h]], vbs[h], preferred_element_type=F32) for h in heads]
        for h in heads:
            o = _gated_head_norm(outs[h], nw, z_ref[0, rows, hsl[h]])
            o_ref[0, rows, hsl[h]] = o.astype(o_ref.dtype)
        return carry

    lax.fori_loop(0, n_chunks, pass_b, 0)


def _gdn(proj, ba, conv_w, a_log, dt_bias, norm_w, *, col0, t_len=256):
    bsz, s, _ = proj.shape
    n_heads = a_log.shape[0]
    width = n_heads * HEAD_DIM
    cb = col0 // width

    def col_spec(group):
        return pl.BlockSpec((1, t_len, width), lambda b, t: (b, t, cb + group))

    def cw_spec(group):
        return pl.BlockSpec((CONV_W, width), lambda b, t: (0, group))

    row = jnp.zeros((1, LANES), F32)
    alog_row = row.at[0, n_heads:2 * n_heads].set(a_log.astype(F32))
    dtb_row = row.at[0, n_heads:2 * n_heads].set(dt_bias.astype(F32))
    small = pl.BlockSpec((1, LANES), lambda b, t: (0, 0))
    kern = functools.partial(_gdn_kernel, n_heads=n_heads, t_len=t_len)
    return pl.pallas_call(
        kern,
        grid=(bsz, s // t_len),
        in_specs=[col_spec(0), col_spec(1), col_spec(2), col_spec(3),
                  pl.BlockSpec((1, t_len, LANES), lambda b, t: (b, t, 0)),
                  cw_spec(0), cw_spec(1), cw_spec(2), small, small,
                  pl.BlockSpec((1, HEAD_DIM), lambda b, t: (0, 0))],
        out_specs=pl.BlockSpec((1, t_len, width), lambda b, t: (b, t, 0)),
        out_shape=jax.ShapeDtypeStruct((bsz, s, width), BF16),
        scratch_shapes=[pltpu.VMEM((SUBLANES + t_len, width), F32)] * 3
        + [pltpu.VMEM((t_len, width), F32)] * 3
        + [pltpu.VMEM((t_len, LANES), F32),
           pltpu.VMEM((t_len // CHUNK, LANES, CHUNK), F32),
           pltpu.VMEM((t_len, LANES), F32),
           pltpu.VMEM((t_len, width), F32)]
        + [pltpu.VMEM((t_len, width), BF16)] * 4
        + [pltpu.VMEM((n_heads, HEAD_DIM, HEAD_DIM), F32)],
        compiler_params=_cparams(("parallel", "arbitrary")),
        name="gdn",
    )(proj, proj, proj, proj, ba, conv_w, conv_w, conv_w, alog_row, dtb_row,
      norm_w.reshape(1, HEAD_DIM).astype(F32))


def _hgrn_kernel(q_ref, f_ref, i_ref, g_ref, lb_ref, nw_ref, o_ref, qs, ks, ls, st_ref, *, heads, t_len):
    t = pl.program_id(1)

    @pl.when(t == 0)
    def _():
        st_ref[...] = jnp.zeros_like(st_ref)

    lb = lb_ref[...]
    forget = lb + (1.0 - lb) * jax.nn.sigmoid(f_ref[0])
    ks[...] = 1.0 - forget
    qs[...] = _silu(q_ref[0]) * (HEAD_DIM ** -0.5)
    row = lax.broadcasted_iota(jnp.int32, (t_len, t_len), 0)
    col = lax.broadcasted_iota(jnp.int32, (t_len, t_len), 1)
    block_tri = ((row // CHUNK == col // CHUNK) & (row >= col)).astype(F32)
    ls[...] = jnp.dot(block_tri, jnp.log(forget), precision=HIGHEST, preferred_element_type=F32)

    incl, _, _ = _tri_masks()
    nw = nw_ref[...]
    mid = CHUNK // 2 - 1

    hsl = [slice(j * HEAD_DIM, (j + 1) * HEAD_DIM) for j in range(heads)]

    def chunk_body(c, carry):
        rows = pl.ds(pl.multiple_of(c * CHUNK, CHUNK), CHUNK)
        items = []
        for j in range(heads):
            b = ls[rows, hsl[j]]
            items.append(dict(b=b, b_mid=b[mid:mid + 1, :], b_last=b[CHUNK - 1:CHUNK, :],
                              q=qs[rows, hsl[j]], k=ks[rows, hsl[j]],
                              vb=i_ref[0, rows, hsl[j]].astype(BF16)))
        for it in items:
            it["att"] = jnp.where(incl, _bdot_g(it["q"] * jnp.exp(it["b"] - it["b_mid"]),
                                                it["k"] * jnp.exp(it["b_mid"] - it["b"]), _NT), 0.0)
        for j, it in enumerate(items):
            st = st_ref[j]
            it["o"] = _bdot_g(it["q"] * jnp.exp(it["b"]), st, _NT)
            st_ref[j] = st * jnp.exp(it["b_last"]) + lax.dot_general(
                it["vb"], (it["k"] * jnp.exp(it["b_last"] - it["b"])).astype(BF16), _TN,
                preferred_element_type=F32)
        for it in items:
            it["o"] = it["o"] + jnp.dot(it["att"].astype(BF16), it["vb"], preferred_element_type=F32)
        for j, it in enumerate(items):
            o = _gated_head_norm(it["o"], nw, g_ref[0, rows, hsl[j]])
            o_ref[0, rows, hsl[j]] = o.astype(o_ref.dtype)
        return carry

    lax.fori_loop(0, t_len // CHUNK, chunk_body, 0)


def _hgrn(proj, lb, norm_w, *, col0, n_heads, t_len=256):
    bsz, s, _ = proj.shape
    width = n_heads * HEAD_DIM
    cb = col0 // width

    def col_spec(group):
        return pl.BlockSpec((1, t_len, width), lambda b, t: (b, t, cb + group))

    kern = functools.partial(_hgrn_kernel, heads=n_heads, t_len=t_len)
    return pl.pallas_call(
        kern,
        grid=(bsz, s // t_len),
        in_specs=[col_spec(0), col_spec(1), col_spec(2), col_spec(3),
                  pl.BlockSpec((1, width), lambda b, t: (0, 0)),
                  pl.BlockSpec((1, HEAD_DIM), lambda b, t: (0, 0))],
        out_specs=pl.BlockSpec((1, t_len, width), lambda b, t: (b, t, 0)),
        out_shape=jax.ShapeDtypeStruct((bsz, s, width), BF16),
        scratch_shapes=[pltpu.VMEM((t_len, width), F32)] * 3
        + [pltpu.VMEM((n_heads, HEAD_DIM, HEAD_DIM), F32)],
        compiler_params=_cparams(("parallel", "arbitrary")),
        name="hgrn2",
    )(proj, proj, proj, proj, lb.reshape(1, width).astype(F32), norm_w.reshape(1, HEAD_DIM).astype(F32))


def _rglru_kernel(y_ref, x_ref, cw_ref, cb_ref, wa_ref, ba_ref, wx_ref, bx_ref, lam_ref, o_ref,
                  xbuf, a_s, b_s, h_s, hcar, *, blocks, t_len):
    t = pl.program_id(2)

    @pl.when(t == 0)
    def _():
        xbuf[pl.ds(0, SUBLANES), :] = jnp.zeros((SUBLANES, blocks * RG_BLOCK), F32)
        hcar[...] = jnp.zeros_like(hcar)

    xc = _causal_conv(xbuf, x_ref[0], cw_ref[...], t_len) + cb_ref[...]
    for n in range(blocks):
        sl = slice(n * RG_BLOCK, (n + 1) * RG_BLOCK)
        xb = xc[:, sl]
        r = jax.nn.sigmoid(_bdot(xb, wa_ref[n]) + ba_ref[:, sl])
        gi = jax.nn.sigmoid(_bdot(xb, wx_ref[n]) + bx_ref[:, sl])
        log_a = (-RG_C) * r * _softplus(-lam_ref[:, sl])
        a = jnp.exp(log_a)
        a_s[:, sl] = a
        one_minus_a2 = -jnp.tanh(log_a) * (a * a + 1.0)
        b_s[:, sl] = jnp.sqrt(jnp.maximum(one_minus_a2, 0.0)) * (gi * xb)

    def row_body(i, h):
        h = a_s[pl.ds(i, 1), :] * h + b_s[pl.ds(i, 1), :]
        h_s[pl.ds(i, 1), :] = h
        return h

    hcar[...] = lax.fori_loop(0, t_len, row_body, hcar[...])
    o_ref[0] = (jax.nn.gelu(y_ref[0], approximate=True) * h_s[...]).astype(o_ref.dtype)


def _rglru(yx, conv_w, conv_b, wa, ba, wx, bx, lam, *, blocks_per_step=8, t_len=256):
    bsz, s, w2 = yx.shape
    width = w2 // 2
    gw = blocks_per_step * RG_BLOCK
    n_g = width // gw
    vec = lambda a: a.reshape(1, width).astype(F32)
    vspec = pl.BlockSpec((1, gw), lambda b, g, t: (0, g))
    wspec = pl.BlockSpec((blocks_per_step, RG_BLOCK, RG_BLOCK), lambda b, g, t: (g, 0, 0))
    kern = functools.partial(_rglru_kernel, blocks=blocks_per_step, t_len=t_len)
    return pl.pallas_call(
        kern,
        grid=(bsz, n_g, s // t_len),
        in_specs=[pl.BlockSpec((1, t_len, gw), lambda b, g, t: (b, t, g)),
                  pl.BlockSpec((1, t_len, gw), lambda b, g, t: (b, t, n_g + g)),
                  pl.BlockSpec((CONV_W, gw), lambda b, g, t: (0, g)),
                  vspec, wspec, vspec, wspec, vspec, vspec],
        out_specs=pl.BlockSpec((1, t_len, gw), lambda b, g, t: (b, t, g)),
        out_shape=jax.ShapeDtypeStruct((bsz, s, width), BF16),
        scratch_shapes=[pltpu.VMEM((SUBLANES + t_len, gw), F32)]
        + [pltpu.VMEM((t_len, gw), F32)] * 3
        + [pltpu.VMEM((1, gw), F32)],
        compiler_params=_cparams(("parallel", "parallel", "arbitrary")),
        name="rglru",
    )(yx, yx, conv_w.astype(F32), vec(conv_b), wa.astype(BF16), vec(ba), wx.astype(BF16), vec(bx), vec(lam))


def _norm_router_kernel(x_ref, nw_ref, w_ref, o_ref, *, n_experts):
    x = x_ref[...]
    y = x * lax.rsqrt(jnp.mean(x * x, axis=-1, keepdims=True) + EPS) * nw_ref[...]
    logits = jnp.dot(y, w_ref[...], precision=HIGHEST, preferred_element_type=F32)
    lane = lax.broadcasted_iota(jnp.int32, logits.shape, 1).astype(F32)
    neg = jnp.float32(-jnp.inf)
    l1 = jnp.where(lane < n_experts, logits, neg)
    m1 = jnp.max(l1, axis=-1, keepdims=True)
    i1 = jnp.min(jnp.where(l1 == m1, lane, float(LANES)), axis=-1, keepdims=True)
    l2 = jnp.where(lane == i1, neg, l1)
    m2 = jnp.max(l2, axis=-1, keepdims=True)
    i2 = jnp.min(jnp.where(l2 == m2, lane, float(LANES)), axis=-1, keepdims=True)
    e2 = jnp.exp(m2 - m1)
    g1 = 1.0 / (1.0 + e2)
    g2 = e2 / (1.0 + e2)
    out = jnp.where(lane == 0, i1, 0.0)
    out = jnp.where(lane == 1, i2, out)
    out = jnp.where(lane == 2, g1, out)
    out = jnp.where(lane == 3, g2, out)
    o_ref[...] = out


def _norm_router(x, nw, router_w, tm=512):
    m, d = x.shape
    n_experts = router_w.shape[1]
    w = jnp.zeros((d, LANES), F32).at[:, :n_experts].set(router_w.astype(F32))
    return pl.pallas_call(
        functools.partial(_norm_router_kernel, n_experts=n_experts),
        grid=(m // tm,),
        in_specs=[pl.BlockSpec((tm, d), lambda i: (i, 0)),
                  pl.BlockSpec((1, d), lambda i: (0, 0)),
                  pl.BlockSpec((d, LANES), lambda i: (0, 0))],
        out_specs=pl.BlockSpec((tm, LANES), lambda i: (i, 0)),
        out_shape=jax.ShapeDtypeStruct((m, LANES), F32),
        compiler_params=_cparams(("parallel",)),
        name="norm_router",
    )(x, nw.reshape(1, d).astype(F32), w)


def _start_row_gather(idx_ref, base, src_hbm, dst, sem, n_rows):
    def body(g, carry):
        for u in range(SUBLANES):
            r = g * SUBLANES + u
            row = idx_ref[base + r]
            pltpu.make_async_copy(src_hbm.at[pl.ds(row, 1), :], dst.at[pl.ds(r, 1), :], sem).start()
        return carry
    lax.fori_loop(0, n_rows // SUBLANES, body, 0)


def _wait_row_gather(src_hbm, dst, sem, n_rows):
    def body(g, carry):
        for u in range(SUBLANES):
            r = g * SUBLANES + u
            pltpu.make_async_copy(src_hbm.at[pl.ds(0, 1), :], dst.at[pl.ds(r, 1), :], sem).wait()
        return carry
    lax.fori_loop(0, n_rows // SUBLANES, body, 0)


def _gather_norm_kernel(idx_ref, nq_ref, x_hbm, nw_ref, o_ref, buf, sem):
    c = pl.program_id(0)
    slot = c % 2

    def start(chunk, s):
        _start_row_gather(idx_ref, chunk * MOE_CHUNK, x_hbm, buf.at[s], sem.at[s], nq_ref[chunk] * MOE_SUB)

    @pl.when(c == 0)
    def _():
        start(0, 0)

    @pl.when(c + 1 < pl.num_programs(0))
    def _():
        start(c + 1, 1 - slot)

    _wait_row_gather(x_hbm, buf.at[slot], sem.at[slot], nq_ref[c] * MOE_SUB)
    for sb in range(MOE_CHUNK // MOE_SUB):
        rows = pl.ds(sb * MOE_SUB, MOE_SUB)

        @pl.when(sb < nq_ref[c])
        def _():
            x = buf[slot, rows, :]
            y = x * lax.rsqrt(jnp.mean(x * x, axis=-1, keepdims=True) + EPS) * nw_ref[...]
            o_ref[rows, :] = y.astype(o_ref.dtype)

        @pl.when(sb >= nq_ref[c])
        def _():
            o_ref[rows, :] = jnp.zeros((MOE_SUB, o_ref.shape[1]), o_ref.dtype)


def _gather_norm(src_token, chunk_nq, x, nw):
    rows = src_token.shape[0]
    d = x.shape[1]
    grid_spec = pltpu.PrefetchScalarGridSpec(
        num_scalar_prefetch=2,
        grid=(rows // MOE_CHUNK,),
        in_specs=[pl.BlockSpec(memory_space=pl.ANY),
                  pl.BlockSpec((1, d), lambda c, idx, nq: (0, 0))],
        out_specs=pl.BlockSpec((MOE_CHUNK, d), lambda c, idx, nq: (c, 0)),
        scratch_shapes=[pltpu.VMEM((2, MOE_CHUNK, d), F32), pltpu.SemaphoreType.DMA((2,))],
    )
    return pl.pallas_call(
        _gather_norm_kernel,
        grid_spec=grid_spec,
        out_shape=jax.ShapeDtypeStruct((rows, d), BF16),
        compiler_params=_cparams(("arbitrary",)),
        name="gather_norm",
    )(src_token, chunk_nq, x, nw.reshape(1, d).astype(F32))


def _expert_changed(te_ref, i):
    return jnp.logical_or(i == 0, te_ref[i] != te_ref[jnp.maximum(i - 1, 0)])


def _for_valid_rows(nq, o_ref, fn):
    n_sub = MOE_CHUNK // MOE_SUB
    for v in range(n_sub + 1):
        @pl.when(nq == v)
        def _(v=v):
            rows = v * MOE_SUB
            if rows:
                fn(rows)
            if rows < MOE_CHUNK:
                o_ref[pl.ds(rows, MOE_CHUNK - rows), :] = jnp.zeros((MOE_CHUNK - rows, o_ref.shape[1]),
                                                                    o_ref.dtype)


def _gmm_swiglu_kernel(te_ref, nq_ref, nv_ref, a_ref, wg_ref, wu_ref, o_ref, wg_s, wu_s):
    c = pl.program_id(1)
    nq = nq_ref[c]

    @pl.when(jnp.logical_and(nq > 0, _expert_changed(te_ref, c)))
    def _():
        wg_s[...] = wg_ref[0].astype(BF16)
        wu_s[...] = wu_ref[0].astype(BF16)

    def compute(rows):
        a = a_ref[pl.ds(0, rows), :]
        g = jnp.dot(a, wg_s[...], preferred_element_type=F32)
        u = jnp.dot(a, wu_s[...], preferred_element_type=F32)
        o_ref[pl.ds(0, rows), :] = (_silu(g) * u).astype(o_ref.dtype)

    _for_valid_rows(nq, o_ref, compute)


def _gmm_swiglu(chunk_expert, chunk_nq, n_valid, a, w_gu, tn):
    m, k = a.shape
    f = w_gu.shape[2] // 2
    nb = f // tn
    last = lambda c, nv: jnp.minimum(c, nv[0] - 1)
    grid_spec = pltpu.PrefetchScalarGridSpec(
        num_scalar_prefetch=3,
        grid=(nb, m // MOE_CHUNK),
        in_specs=[pl.BlockSpec((MOE_CHUNK, k), lambda j, c, te, nq, nv: (last(c, nv), 0)),
                  pl.BlockSpec((1, k, tn), lambda j, c, te, nq, nv: (te[c], 0, j)),
                  pl.BlockSpec((1, k, tn), lambda j, c, te, nq, nv: (te[c], 0, j + nb))],
        out_specs=pl.BlockSpec((MOE_CHUNK, tn), lambda j, c, te, nq, nv: (c, j)),
        scratch_shapes=[pltpu.VMEM((k, tn), BF16)] * 2,
    )
    return pl.pallas_call(
        _gmm_swiglu_kernel,
        grid_spec=grid_spec,
        out_shape=jax.ShapeDtypeStruct((m, f), BF16),
        compiler_params=_cparams(("arbitrary", "arbitrary")),
        name="gmm_swiglu",
    )(chunk_expert, chunk_nq, n_valid, a, w_gu, w_gu)


def _gmm_down_kernel(te_ref, nq_ref, nv_ref, a_ref, w_ref, o_ref):
    c = pl.program_id(0)

    def compute(rows):
        w = w_ref[0].astype(BF16)
        o_ref[pl.ds(0, rows), :] = jnp.dot(a_ref[pl.ds(0, rows), :], w, preferred_element_type=F32)

    _for_valid_rows(nq_ref[c], o_ref, compute)


def _gmm_down(chunk_expert, chunk_nq, n_valid, a, w_d, tn):
    m, k = a.shape
    n = w_d.shape[2]
    nj = n // tn
    last = lambda c, nv: jnp.minimum(c, nv[0] - 1)
    col = lambda j, c, nv: jnp.where(c < nv[0], j, nj - 1)
    grid_spec = pltpu.PrefetchScalarGridSpec(
        num_scalar_prefetch=3,
        grid=(m // MOE_CHUNK, nj),
        in_specs=[pl.BlockSpec((MOE_CHUNK, k), lambda c, j, te, nq, nv: (last(c, nv), 0)),
                  pl.BlockSpec((1, k, tn), lambda c, j, te, nq, nv: (te[c], 0, col(j, c, nv)))],
        out_specs=pl.BlockSpec((MOE_CHUNK, tn), lambda c, j, te, nq, nv: (c, j)),
    )
    return pl.pallas_call(
        _gmm_down_kernel,
        grid_spec=grid_spec,
        out_shape=jax.ShapeDtypeStruct((m, n), F32),
        compiler_params=pltpu.CompilerParams(dimension_semantics=("arbitrary", "arbitrary"),
                                             vmem_limit_bytes=VMEM_LIMIT_MAX),
        name="gmm_down",
    )(chunk_expert, chunk_nq, n_valid, a, w_d)


def _combine_norm_kernel(pos_ref, x_ref, r_ref, ys_hbm, w_ref, o_ref, buf, sem, *, tm):
    i = pl.program_id(0)
    n = pl.num_programs(0)
    slot = i % 2

    def start(tile, s):
        for kk in range(TOP_K):
            _start_row_gather(pos_ref, kk * (n * tm) + tile * tm, ys_hbm, buf.at[s, kk], sem.at[s], tm)

    @pl.when(i == 0)
    def _():
        start(0, 0)

    @pl.when(i + 1 < n)
    def _():
        start(i + 1, 1 - slot)

    for kk in range(TOP_K):
        _wait_row_gather(ys_hbm, buf.at[slot, kk], sem.at[slot], tm)
    r = r_ref[...]
    moe = r[:, TOP_K:TOP_K + 1] * buf[slot, 0]
    for kk in range(1, TOP_K):
        moe = moe + r[:, TOP_K + kk:TOP_K + kk + 1] * buf[slot, kk]
    x = x_ref[...] + moe
    y = x * lax.rsqrt(jnp.mean(x * x, axis=-1, keepdims=True) + EPS)
    o_ref[...] = y * w_ref[...]


def _combine_norm(pos, x, route, ys, w, tm=256):
    m, d = x.shape
    grid_spec = pltpu.PrefetchScalarGridSpec(
        num_scalar_prefetch=1,
        grid=(m // tm,),
        in_specs=[pl.BlockSpec((tm, d), lambda i, p: (i, 0)),
                  pl.BlockSpec((tm, LANES), lambda i, p: (i, 0)),
                  pl.BlockSpec(memory_space=pl.ANY),
                  pl.BlockSpec((1, d), lambda i, p: (0, 0))],
        out_specs=pl.BlockSpec((tm, d), lambda i, p: (i, 0)),
        scratch_shapes=[pltpu.VMEM((2, TOP_K, tm, d), F32), pltpu.SemaphoreType.DMA((2,))],
    )
    return pl.pallas_call(
        functools.partial(_combine_norm_kernel, tm=tm),
        grid_spec=grid_spec,
        out_shape=jax.ShapeDtypeStruct((m, d), F32),
        compiler_params=_cparams(("arbitrary",)),
        name="combine_norm",
    )(pos, x, route, ys, w.reshape(1, d).astype(F32))


def _moe_routing(route, n_experts):
    m = route.shape[0]
    n_sub = MOE_CHUNK // MOE_SUB
    ids = route[:, :TOP_K].astype(jnp.int32)
    flat_e = ids.reshape(-1)
    onehot = (flat_e[:, None] == jnp.arange(n_experts)[None, :]).astype(jnp.int32)
    rank = jnp.sum((jnp.cumsum(onehot, axis=0) - onehot) * onehot, axis=1)
    counts = jnp.sum(onehot, axis=0)
    subs = (counts + MOE_SUB - 1) // MOE_SUB
    n_chunks = (subs + n_sub - 1) // n_sub
    chunk_end = jnp.cumsum(n_chunks)
    chunk_start = chunk_end - n_chunks
    max_chunks = (m * TOP_K // MOE_SUB + n_experts + n_sub - 1) // n_sub + n_experts
    rows = max_chunks * MOE_CHUNK
    pos = (jnp.sum(onehot * chunk_start[None, :], axis=1) + rank // MOE_CHUNK) * MOE_CHUNK + rank % MOE_CHUNK
    src_token = jnp.zeros((rows,), jnp.int32).at[pos].set(jnp.arange(m * TOP_K, dtype=jnp.int32) // TOP_K)
    n_valid = chunk_end[n_experts - 1].astype(jnp.int32)
    cidx = jnp.minimum(jnp.arange(max_chunks, dtype=jnp.int32), n_valid - 1)
    chunk_expert = jnp.sum((cidx[:, None] >= chunk_end[None, :]).astype(jnp.int32), axis=1).astype(jnp.int32)
    sel = (chunk_expert[:, None] == jnp.arange(n_experts)[None, :]).astype(jnp.int32)
    within = cidx - jnp.sum(sel * chunk_start[None, :], axis=1)
    chunk_nq = jnp.clip(jnp.sum(sel * subs[None, :], axis=1) - n_sub * within, 0, n_sub)
    chunk_nq = jnp.where(jnp.arange(max_chunks) < n_valid, chunk_nq, 0).astype(jnp.int32)
    pos_kmajor = pos.reshape(m, TOP_K).T.reshape(-1).astype(jnp.int32)
    return src_token, chunk_expert, chunk_nq, n_valid.reshape(1), pos_kmajor


def kernel(x, norm_w, final_norm_w, mix_in_w, gdn_conv_w, gdn_a_log, gdn_dt_bias, gdn_norm_w, hgrn_lb_logits, hgrn_norm_w, mix_out_w, ffn_gate_up_w, ffn_down_w, rg_in_w, rg_conv_w, rg_conv_b, rg_gate_a_w, rg_gate_a_b, rg_gate_x_w, rg_gate_x_b, rg_lambda, rg_out_w, moe_router_w, moe_gate_up_w, moe_down_w):
    bsz, s, d = x.shape
    m = bsz * s
    n_heads = gdn_a_log.shape[1]
    hw = n_heads * HEAD_DIM
    n_experts = moe_router_w.shape[2]
    xr = x.reshape(m, d)

    hgrn_lb = jnp.cumsum(jax.nn.softmax(hgrn_lb_logits.astype(F32), axis=0), axis=0)
    w_in = mix_in_w[0]
    w_b = w_in[:, 4 * hw + 2 * n_heads:]
    w_ba = jnp.zeros((d, LANES), F32).at[:, :2 * n_heads].set(w_in[:, 4 * hw:4 * hw + 2 * n_heads])
    h = _rmsnorm(xr, norm_w[0, 0], BF16)
    proj_a = _mm(h, w_in, F32, 1024, 1024, n_cols=4 * hw).reshape(bsz, s, 4 * hw)
    proj_b = _mm(h, w_b, F32, 1024, 1024).reshape(bsz, s, 4 * hw)
    ba = _mm(h, w_ba, F32, 1024, LANES).reshape(bsz, s, LANES)
    o_a = _gdn(proj_a, ba, gdn_conv_w[0].astype(F32), gdn_a_log[0], gdn_dt_bias[0], gdn_norm_w[0], col0=0)
    o_b = _hgrn(proj_b, hgrn_lb[0], hgrn_norm_w[0], col0=0, n_heads=n_heads)
    xr = _mm_resid([o_a.reshape(m, hw), o_b.reshape(m, hw)], mix_out_w[0], xr, 1024, 1024)
    h = _rmsnorm(xr, norm_w[0, 1], BF16)
    act = _mm_swiglu(h, ffn_gate_up_w[0], 1024, 512)
    xr = _mm_resid([act], ffn_down_w[0], xr, 512, 512)

    h = _rmsnorm(xr, norm_w[1, 0], BF16)
    yx = _mm(h, rg_in_w[0], F32, 1024, 1024).reshape(bsz, s, -1)
    rec = _rglru(yx, rg_conv_w[0], rg_conv_b[0], rg_gate_a_w[0], rg_gate_a_b[0],
                 rg_gate_x_w[0], rg_gate_x_b[0], rg_lambda[0])
    xr = _mm_resid([rec.reshape(m, d)], rg_out_w[0], xr, 1024, 1024)
    route = _norm_router(xr, norm_w[1, 1], moe_router_w[0])
    src_token, chunk_expert, chunk_nq, n_valid, pos = _moe_routing(route, n_experts)
    hs = _gather_norm(src_token, chunk_nq, xr, norm_w[1, 1])
    act = _gmm_swiglu(chunk_expert, chunk_nq, n_valid, hs, moe_gate_up_w[0], 512)
    ys = _gmm_down(chunk_expert, chunk_nq, n_valid, act, moe_down_w[0], 256)
    out = _combine_norm(pos, xr, route, ys, final_norm_w)
    return out.reshape(bsz, s, d)
```

```python
import functools

import jax
import jax.numpy as jnp
from jax import lax
from jax.experimental import pallas as pl
from jax.experimental.pallas import tpu as pltpu

F32 = jnp.float32
BF16 = jnp.bfloat16
HIGHEST = lax.Precision.HIGHEST

EPS = 1e-6
CHUNK = 64
CONV_W = 4
HEAD_DIM = 128
RG_BLOCK = 256
RG_C = 8.0
TOP_K = 2
LANES = 128
SUBLANES = 8
VMEM_LIMIT = 56 * 1024 * 1024
VMEM_LIMIT_MAX = 60 * 1024 * 1024
MOE_CHUNK = 1024
MOE_SUB = 256

_NT = (((1,), (1,)), ((), ()))
_TN = (((0,), (0,)), ((), ()))


def _cparams(sem):
    return pltpu.CompilerParams(dimension_semantics=sem, vmem_limit_bytes=VMEM_LIMIT)


def _silu(x):
    return x * jax.nn.sigmoid(x)


def _softplus(x):
    return jnp.maximum(x, 0.0) + jnp.log1p(jnp.exp(-jnp.abs(x)))


def _bdot(a, b):
    return jnp.dot(a.astype(BF16), b.astype(BF16), preferred_element_type=F32)


def _bdot_g(a, b, dims):
    return lax.dot_general(a.astype(BF16), b.astype(BF16), dims, preferred_element_type=F32)


def _rmsnorm_kernel(x_ref, w_ref, o_ref):
    x = x_ref[...]
    y = x * lax.rsqrt(jnp.mean(x * x, axis=-1, keepdims=True) + EPS)
    o_ref[...] = (y * w_ref[...]).astype(o_ref.dtype)


def _rmsnorm(x, w, out_dtype, tm=512):
    m, d = x.shape
    return pl.pallas_call(
        _rmsnorm_kernel,
        grid=(m // tm,),
        in_specs=[pl.BlockSpec((tm, d), lambda i: (i, 0)),
                  pl.BlockSpec((1, d), lambda i: (0, 0))],
        out_specs=pl.BlockSpec((tm, d), lambda i: (i, 0)),
        out_shape=jax.ShapeDtypeStruct((m, d), out_dtype),
        compiler_params=_cparams(("parallel",)),
        name="rmsnorm",
    )(x, w.reshape(1, d).astype(F32))


def _bf16_weight(w_ref, w_s, fresh):
    if not fresh:
        return w_s[...]
    w = w_ref[...].astype(BF16)
    w_s[...] = w
    return w


def _first_row_tile_or_not(body):
    first = pl.program_id(1) == 0
    pl.when(first)(lambda: body(True))
    pl.when(jnp.logical_not(first))(lambda: body(False))


def _mm_kernel(a_ref, w_ref, o_ref, w_s):
    def body(fresh):
        w = _bf16_weight(w_ref, w_s, fresh)
        o_ref[...] = jnp.dot(a_ref[...], w, preferred_element_type=F32).astype(o_ref.dtype)

    _first_row_tile_or_not(body)


def _mm(a, w, out_dtype, tm, tn, n_cols=None):
    m, k = a.shape
    n = w.shape[1] if n_cols is None else n_cols
    return pl.pallas_call(
        _mm_kernel,
        grid=(n // tn, m // tm),
        in_specs=[pl.BlockSpec((tm, k), lambda j, i: (i, 0)),
                  pl.BlockSpec((k, tn), lambda j, i: (0, j))],
        out_specs=pl.BlockSpec((tm, tn), lambda j, i: (i, j)),
        out_shape=jax.ShapeDtypeStruct((m, n), out_dtype),
        scratch_shapes=[pltpu.VMEM((k, tn), BF16)],
        compiler_params=_cparams(("arbitrary", "arbitrary")),
        name="mm",
    )(a, w)


def _mm_swiglu_kernel(a_ref, wg_ref, wu_ref, o_ref, wg_s, wu_s):
    def body(fresh):
        a = a_ref[...]
        g = jnp.dot(a, _bf16_weight(wg_ref, wg_s, fresh), preferred_element_type=F32)
        u = jnp.dot(a, _bf16_weight(wu_ref, wu_s, fresh), preferred_element_type=F32)
        o_ref[...] = (_silu(g) * u).astype(o_ref.dtype)

    _first_row_tile_or_not(body)


def _mm_swiglu(a, w_gu, tm, tn):
    m, k = a.shape
    f = w_gu.shape[1] // 2
    nb = f // tn
    return pl.pallas_call(
        _mm_swiglu_kernel,
        grid=(nb, m // tm),
        in_specs=[pl.BlockSpec((tm, k), lambda j, i: (i, 0)),
                  pl.BlockSpec((k, tn), lambda j, i: (0, j)),
                  pl.BlockSpec((k, tn), lambda j, i: (0, j + nb))],
        out_specs=pl.BlockSpec((tm, tn), lambda j, i: (i, j)),
        out_shape=jax.ShapeDtypeStruct((m, f), BF16),
        scratch_shapes=[pltpu.VMEM((k, tn), BF16)] * 2,
        compiler_params=_cparams(("arbitrary", "arbitrary")),
        name="mm_swiglu",
    )(a, w_gu, w_gu)


def _mm_resid_kernel(*refs, n_pairs):
    r_ref, o_ref = refs[2 * n_pairs], refs[2 * n_pairs + 1]
    w_s = refs[2 * n_pairs + 2:]

    def body(fresh):
        acc = r_ref[...]
        for p in range(n_pairs):
            w = _bf16_weight(refs[2 * p + 1], w_s[p], fresh)
            acc = acc + jnp.dot(refs[2 * p][...], w, preferred_element_type=F32)
        o_ref[...] = acc

    _first_row_tile_or_not(body)


def _mm_resid(a_list, w, resid, tm, tn):
    m, n = resid.shape
    in_specs, args, scratch = [], [], []
    for p, a in enumerate(a_list):
        k = a.shape[1]
        in_specs += [pl.BlockSpec((tm, k), lambda j, i: (i, 0)),
                     pl.BlockSpec((k, tn), lambda j, i, p=p: (p, j))]
        args += [a, w]
        scratch.append(pltpu.VMEM((k, tn), BF16))
    in_specs.append(pl.BlockSpec((tm, tn), lambda j, i: (i, j)))
    args.append(resid)
    return pl.pallas_call(
        functools.partial(_mm_resid_kernel, n_pairs=len(a_list)),
        grid=(n // tn, m // tm),
        in_specs=in_specs,
        out_specs=pl.BlockSpec((tm, tn), lambda j, i: (i, j)),
        out_shape=jax.ShapeDtypeStruct((m, n), F32),
        scratch_shapes=scratch,
        compiler_params=_cparams(("arbitrary", "arbitrary")),
        name="mm_resid",
    )(*args)


def _causal_conv(buf_ref, x, cw, t_len):
    buf_ref[pl.ds(SUBLANES, t_len), :] = x
    acc = cw[CONV_W - 1:CONV_W, :] * x
    for j in range(CONV_W - 1):
        off = SUBLANES - (CONV_W - 1) + j
        acc = acc + cw[j:j + 1, :] * buf_ref[pl.ds(off, t_len), :]
    buf_ref[pl.ds(0, SUBLANES), :] = x[t_len - SUBLANES:, :]
    return acc


def _tri_masks():
    row = lax.broadcasted_iota(jnp.int32, (CHUNK, CHUNK), 0)
    col = lax.broadcasted_iota(jnp.int32, (CHUNK, CHUNK), 1)
    return row >= col, row > col, row == col


def _gated_head_norm(o, nw, z):
    o = o * lax.rsqrt(jnp.mean(o * o, axis=-1, keepdims=True) + EPS) * nw
    return o * _silu(z)


def _gdn_kernel(q_ref, k_ref, v_ref, z_ref, ba_ref, cwq_ref, cwk_ref, cwv_ref, alog_ref, dtb_ref,
                nw_ref, o_ref, qbuf, kbuf, vbuf, qs, ks, vs, gcs, gct, bs, u_s, w_s, qk_s, qd_s, kd_s, s_ref,
                *, n_heads, t_len):
    t = pl.program_id(1)
    n_chunks = t_len // CHUNK
    width = n_heads * HEAD_DIM

    @pl.when(t == 0)
    def _():
        zeros = jnp.zeros((SUBLANES, width), F32)
        qbuf[pl.ds(0, SUBLANES), :] = zeros
        kbuf[pl.ds(0, SUBLANES), :] = zeros
        vbuf[pl.ds(0, SUBLANES), :] = zeros
        s_ref[...] = jnp.zeros_like(s_ref)

    qc = _silu(_causal_conv(qbuf, q_ref[0], cwq_ref[...], t_len))
    kc = _silu(_causal_conv(kbuf, k_ref[0], cwk_ref[...], t_len))
    vs[...] = _silu(_causal_conv(vbuf, v_ref[0], cwv_ref[...], t_len))
    for h in range(n_heads):
        sl = slice(h * HEAD_DIM, (h + 1) * HEAD_DIM)
        qh = qc[:, sl]
        kh = kc[:, sl]
        qs[:, sl] = qh * (lax.rsqrt(jnp.sum(qh * qh, axis=-1, keepdims=True) + EPS) * (HEAD_DIM ** -0.5))
        ks[:, sl] = kh * lax.rsqrt(jnp.sum(kh * kh, axis=-1, keepdims=True) + EPS)

    ba = ba_ref[0]
    g_log = -jnp.exp(alog_ref[...]) * _softplus(ba + dtb_ref[...])
    bs[...] = jax.nn.sigmoid(ba)
    row = lax.broadcasted_iota(jnp.int32, (t_len, t_len), 0)
    col = lax.broadcasted_iota(jnp.int32, (t_len, t_len), 1)
    block_tri = ((row // CHUNK == col // CHUNK) & (row >= col)).astype(F32)
    gc_blk = jnp.dot(block_tri, g_log, precision=HIGHEST, preferred_element_type=F32)
    gcs[...] = gc_blk
    for c in range(n_chunks):
        gct[c] = gc_blk[c * CHUNK:(c + 1) * CHUNK, :].T

    incl, strict, eye = _tri_masks()
    eye_f = eye.astype(F32)
    nw = nw_ref[...]

    heads = range(n_heads)
    hsl = [slice(h * HEAD_DIM, (h + 1) * HEAD_DIM) for h in heads]
    qsl = [slice(h * HEAD_DIM, h * HEAD_DIM + CHUNK) for h in heads]
    chunks_per_iter = 2
    assert n_chunks % chunks_per_iter == 0

    def pass_a(it, carry):
        items = []
        for cc in range(chunks_per_iter):
            c = it * chunks_per_iter + cc
            rows = pl.ds(pl.multiple_of(c * CHUNK, CHUNK), CHUNK)
            gc_all = gcs[rows, :]
            beta_all = bs[rows, :]
            gct_c = gct[c]
            for h in heads:
                gc = gc_all[:, n_heads + h:n_heads + h + 1]
                beta = beta_all[:, h:h + 1]
                gc_row = gct_c[n_heads + h:n_heads + h + 1, :]
                decay = jnp.where(incl, jnp.exp(jnp.where(incl, gc - gc_row, 0.0)), 0.0)
                items.append(dict(rows=rows, h=h, gc=gc, beta=beta, decay=decay,
                                  q=qs[rows, hsl[h]], k=ks[rows, hsl[h]]))
        for it_ in items:
            it_["kb"] = it_["k"].astype(BF16)
            it_["kk"] = lax.dot_general(it_["kb"], it_["kb"], _NT, preferred_element_type=F32)
        for it_ in items:
            it_["x"] = jnp.where(strict, -(it_["beta"] * it_["kk"] * it_["decay"]), 0.0)
            it_["p"] = eye_f + it_["x"]
        for _ in range(5):
            for it_ in items:
                xb = it_["x"].astype(BF16)
                it_["x"] = jnp.dot(xb, xb, preferred_element_type=F32)
            for it_ in items:
                it_["p"] = it_["p"] + _bdot(it_["p"], it_["x"])
        for it_ in items:
            egc = jnp.exp(it_["gc"])
            it_["egc"] = egc
            v = vs[it_["rows"], hsl[it_["h"]]]
            rhs = jnp.concatenate([v * it_["beta"], it_["k"] * (it_["beta"] * egc)], axis=-1)
            it_["sol"] = _bdot(it_["p"], rhs)
        for it_ in items:
            it_["qk"] = lax.dot_general(it_["q"].astype(BF16), it_["kb"], _NT,
                                        preferred_element_type=F32) * it_["decay"]
        for it_ in items:
            rows, h, gc = it_["rows"], it_["h"], it_["gc"]
            g_last = gc[CHUNK - 1:CHUNK, :]
            u_s[rows, hsl[h]] = it_["sol"][:, :HEAD_DIM]
            w_s[rows, hsl[h]] = it_["sol"][:, HEAD_DIM:].astype(BF16)
            qk_s[rows, qsl[h]] = it_["qk"].astype(BF16)
            qd_s[rows, hsl[h]] = (it_["q"] * it_["egc"]).astype(BF16)
            kd_s[rows, hsl[h]] = (it_["k"] * jnp.exp(g_last - gc)).astype(BF16)
        return carry

    lax.fori_loop(0, n_chunks // chunks_per_iter, pass_a, 0)

    def pass_b(c, carry):
        rows = pl.ds(pl.multiple_of(c * CHUNK, CHUNK), CHUNK)
        gc_all = gcs[rows, :]
        states = [s_ref[h] for h in heads]
        sbs = [s.astype(BF16) for s in states]
        v_new = [u_s[rows, hsl[h]] - jnp.dot(w_s[rows, hsl[h]], sbs[h], preferred_element_type=F32)
                 for h in heads]
        vbs = [v.astype(BF16) for v in v_new]
        for h in heads:
            g_last = gc_all[CHUNK - 1:CHUNK, n_heads + h:n_heads + h + 1]
            s_ref[h] = states[h] * jnp.exp(g_last) + lax.dot_general(kd_s[rows, hsl[h]], vbs[h], _TN,
                                                                      preferred_element_type=F32)
        outs = [jnp.dot(qd_s[rows, hsl[h]], sbs[h], preferred_element_type=F32)
                + jnp.dot(qk_s[rows, qsl[h]], vbs[h], preferred_element_type=F32) for h in heads]
        for h in heads:
            o = _gated_head_norm(outs[h], nw, z_ref[0, rows, hsl[h]])
            o_ref[0, rows, hsl[h]] = o.astype(o_ref.dtype)
        return carry

    lax.fori_loop(0, n_chunks, pass_b, 0)


def _gdn(proj, ba, conv_w, a_log, dt_bias, norm_w, *, col0, t_len=256):
    bsz, s, _ = proj.shape
    n_heads = a_log.shape[0]
    width = n_heads * HEAD_DIM
    cb = col0 // width

    def col_spec(group):
        return pl.BlockSpec((1, t_len, width), lambda b, t: (b, t, cb + group))

    def cw_spec(group):
        return pl.BlockSpec((CONV_W, width), lambda b, t: (0, group))

    row = jnp.zeros((1, LANES), F32)
    alog_row = row.at[0, n_heads:2 * n_heads].set(a_log.astype(F32))
    dtb_row = row.at[0, n_heads:2 * n_heads].set(dt_bias.astype(F32))
    small = pl.BlockSpec((1, LANES), lambda b, t: (0, 0))
    kern = functools.partial(_gdn_kernel, n_heads=n_heads, t_len=t_len)
    return pl.pallas_call(
        kern,
        grid=(bsz, s // t_len),
        in_specs=[col_spec(0), col_spec(1), col_spec(2), col_spec(3),
                  pl.BlockSpec((1, t_len, LANES), lambda b, t: (b, t, 0)),
                  cw_spec(0), cw_spec(1), cw_spec(2), small, small,
                  pl.BlockSpec((1, HEAD_DIM), lambda b, t: (0, 0))],
        out_specs=pl.BlockSpec((1, t_len, width), lambda b, t: (b, t, 0)),
        out_shape=jax.ShapeDtypeStruct((bsz, s, width), BF16),
        scratch_shapes=[pltpu.VMEM((SUBLANES + t_len, width), F32)] * 3
        + [pltpu.VMEM((t_len, width), F32)] * 3
        + [pltpu.VMEM((t_len, LANES), F32),
           pltpu.VMEM((t_len // CHUNK, LANES, CHUNK), F32),
           pltpu.VMEM((t_len, LANES), F32),
           pltpu.VMEM((t_len, width), F32)]
        + [pltpu.VMEM((t_len, width), BF16)] * 4
        + [pltpu.VMEM((n_heads, HEAD_DIM, HEAD_DIM), F32)],
        compiler_params=_cparams(("parallel", "arbitrary")),
        name="gdn",
    )(proj, proj, proj, proj, ba, conv_w, conv_w, conv_w, alog_row, dtb_row,
      norm_w.reshape(1, HEAD_DIM).astype(F32))


def _hgrn_kernel(q_ref, f_ref, i_ref, g_ref, lb_ref, nw_ref, o_ref, qs, ks, ls, st_ref, *, heads, t_len):
    t = pl.program_id(1)

    @pl.when(t == 0)
    def _():
        st_ref[...] = jnp.zeros_like(st_ref)

    lb = lb_ref[...]
    forget = lb + (1.0 - lb) * jax.nn.sigmoid(f_ref[0])
    ks[...] = 1.0 - forget
    qs[...] = _silu(q_ref[0]) * (HEAD_DIM ** -0.5)
    row = lax.broadcasted_iota(jnp.int32, (t_len, t_len), 0)
    col = lax.broadcasted_iota(jnp.int32, (t_len, t_len), 1)
    block_tri = ((row // CHUNK == col // CHUNK) & (row >= col)).astype(F32)
    ls[...] = jnp.dot(block_tri, jnp.log(forget), precision=HIGHEST, preferred_element_type=F32)

    incl, _, _ = _tri_masks()
    nw = nw_ref[...]
    mid = CHUNK // 2 - 1

    hsl = [slice(j * HEAD_DIM, (j + 1) * HEAD_DIM) for j in range(heads)]

    def chunk_body(c, carry):
        rows = pl.ds(pl.multiple_of(c * CHUNK, CHUNK), CHUNK)
        items = []
        for j in range(heads):
            b = ls[rows, hsl[j]]
            items.append(dict(b=b, b_mid=b[mid:mid + 1, :], b_last=b[CHUNK - 1:CHUNK, :],
                              q=qs[rows, hsl[j]], k=ks[rows, hsl[j]],
                              vb=i_ref[0, rows, hsl[j]].astype(BF16)))
        for it in items:
            it["att"] = jnp.where(incl, _bdot_g(it["q"] * jnp.exp(it["b"] - it["b_mid"]),
                                                it["k"] * jnp.exp(it["b_mid"] - it["b"]), _NT), 0.0)
        for j, it in enumerate(items):
            st = st_ref[j]
            it["o"] = _bdot_g(it["q"] * jnp.exp(it["b"]), st, _NT)
            st_ref[j] = st * jnp.exp(it["b_last"]) + lax.dot_general(
                it["vb"], (it["k"] * jnp.exp(it["b_last"] - it["b"])).astype(BF16), _TN,
                preferred_element_type=F32)
        for it in items:
            it["o"] = it["o"] + jnp.dot(it["att"].astype(BF16), it["vb"], preferred_element_type=F32)
        for j, it in enumerate(items):
            o = _gated_head_norm(it["o"], nw, g_ref[0, rows, hsl[j]])
            o_ref[0, rows, hsl[j]] = o.astype(o_ref.dtype)
        return carry

    lax.fori_loop(0, t_len // CHUNK, chunk_body, 0)


def _hgrn(proj, lb, norm_w, *, col0, n_heads, t_len=256):
    bsz, s, _ = proj.shape
    width = n_heads * HEAD_DIM
    cb = col0 // width

    def col_spec(group):
        return pl.BlockSpec((1, t_len, width), lambda b, t: (b, t, cb + group))

    kern = functools.partial(_hgrn_kernel, heads=n_heads, t_len=t_len)
    return pl.pallas_call(
        kern,
        grid=(bsz, s // t_len),
        in_specs=[col_spec(0), col_spec(1), col_spec(2), col_spec(3),
                  pl.BlockSpec((1, width), lambda b, t: (0, 0)),
                  pl.BlockSpec((1, HEAD_DIM), lambda b, t: (0, 0))],
        out_specs=pl.BlockSpec((1, t_len, width), lambda b, t: (b, t, 0)),
        out_shape=jax.ShapeDtypeStruct((bsz, s, width), BF16),
        scratch_shapes=[pltpu.VMEM((t_len, width), F32)] * 3
        + [pltpu.VMEM((n_heads, HEAD_DIM, HEAD_DIM), F32)],
        compiler_params=_cparams(("parallel", "arbitrary")),
        name="hgrn2",
    )(proj, proj, proj, proj, lb.reshape(1, width).astype(F32), norm_w.reshape(1, HEAD_DIM).astype(F32))


def _rglru_kernel(y_ref, x_ref, cw_ref, cb_ref, wa_ref, ba_ref, wx_ref, bx_ref, lam_ref, o_ref,
                  xbuf, a_s, b_s, h_s, hcar, *, blocks, t_len):
    t = pl.program_id(2)

    @pl.when(t == 0)
    def _():
        xbuf[pl.ds(0, SUBLANES), :] = jnp.zeros((SUBLANES, blocks * RG_BLOCK), F32)
        hcar[...] = jnp.zeros_like(hcar)

    xc = _causal_conv(xbuf, x_ref[0], cw_ref[...], t_len) + cb_ref[...]
    for n in range(blocks):
        sl = slice(n * RG_BLOCK, (n + 1) * RG_BLOCK)
        xb = xc[:, sl]
        r = jax.nn.sigmoid(_bdot(xb, wa_ref[n]) + ba_ref[:, sl])
        gi = jax.nn.sigmoid(_bdot(xb, wx_ref[n]) + bx_ref[:, sl])
        log_a = (-RG_C) * r * _softplus(-lam_ref[:, sl])
        a = jnp.exp(log_a)
        a_s[:, sl] = a
        one_minus_a2 = -jnp.tanh(log_a) * (a * a + 1.0)
        b_s[:, sl] = jnp.sqrt(jnp.maximum(one_minus_a2, 0.0)) * (gi * xb)

    def row_body(i, h):
        h = a_s[pl.ds(i, 1), :] * h + b_s[pl.ds(i, 1), :]
        h_s[pl.ds(i, 1), :] = h
        return h

    hcar[...] = lax.fori_loop(0, t_len, row_body, hcar[...])
    o_ref[0] = (jax.nn.gelu(y_ref[0], approximate=True) * h_s[...]).astype(o_ref.dtype)


def _rglru(yx, conv_w, conv_b, wa, ba, wx, bx, lam, *, blocks_per_step=8, t_len=256):
    bsz, s, w2 = yx.shape
    width = w2 // 2
    gw = blocks_per_step * RG_BLOCK
    n_g = width // gw
    vec = lambda a: a.reshape(1, width).astype(F32)
    vspec = pl.BlockSpec((1, gw), lambda b, g, t: (0, g))
    wspec = pl.BlockSpec((blocks_per_step, RG_BLOCK, RG_BLOCK), lambda b, g, t: (g, 0, 0))
    kern = functools.partial(_rglru_kernel, blocks=blocks_per_step, t_len=t_len)
    return pl.pallas_call(
        kern,
        grid=(bsz, n_g, s // t_len),
        in_specs=[pl.BlockSpec((1, t_len, gw), lambda b, g, t: (b, t, g)),
                  pl.BlockSpec((1, t_len, gw), lambda b, g, t: (b, t, n_g + g)),
                  pl.BlockSpec((CONV_W, gw), lambda b, g, t: (0, g)),
                  vspec, wspec, vspec, wspec, vspec, vspec],
        out_specs=pl.BlockSpec((1, t_len, gw), lambda b, g, t: (b, t, g)),
        out_shape=jax.ShapeDtypeStruct((bsz, s, width), BF16),
        scratch_shapes=[pltpu.VMEM((SUBLANES + t_len, gw), F32)]
        + [pltpu.VMEM((t_len, gw), F32)] * 3
        + [pltpu.VMEM((1, gw), F32)],
        compiler_params=_cparams(("parallel", "parallel", "arbitrary")),
        name="rglru",
    )(yx, yx, conv_w.astype(F32), vec(conv_b), wa.astype(BF16), vec(ba), wx.astype(BF16), vec(bx), vec(lam))


def _norm_router_kernel(x_ref, nw_ref, w_ref, o_ref, *, n_experts):
    x = x_ref[...]
    y = x * lax.rsqrt(jnp.mean(x * x, axis=-1, keepdims=True) + EPS) * nw_ref[...]
    logits = jnp.dot(y, w_ref[...], precision=HIGHEST, preferred_element_type=F32)
    lane = lax.broadcasted_iota(jnp.int32, logits.shape, 1).astype(F32)
    neg = jnp.float32(-jnp.inf)
    l1 = jnp.where(lane < n_experts, logits, neg)
    m1 = jnp.max(l1, axis=-1, keepdims=True)
    i1 = jnp.min(jnp.where(l1 == m1, lane, float(LANES)), axis=-1, keepdims=True)
    l2 = jnp.where(lane == i1, neg, l1)
    m2 = jnp.max(l2, axis=-1, keepdims=True)
    i2 = jnp.min(jnp.where(l2 == m2, lane, float(LANES)), axis=-1, keepdims=True)
    e2 = jnp.exp(m2 - m1)
    g1 = 1.0 / (1.0 + e2)
    g2 = e2 / (1.0 + e2)
    out = jnp.where(lane == 0, i1, 0.0)
    out = jnp.where(lane == 1, i2, out)
    out = jnp.where(lane == 2, g1, out)
    out = jnp.where(lane == 3, g2, out)
    o_ref[...] = out


def _norm_router(x, nw, router_w, tm=512):
    m, d = x.shape
    n_experts = router_w.shape[1]
    w = jnp.zeros((d, LANES), F32).at[:, :n_experts].set(router_w.astype(F32))
    return pl.pallas_call(
        functools.partial(_norm_router_kernel, n_experts=n_experts),
        grid=(m // tm,),
        in_specs=[pl.BlockSpec((tm, d), lambda i: (i, 0)),
                  pl.BlockSpec((1, d), lambda i: (0, 0)),
                  pl.BlockSpec((d, LANES), lambda i: (0, 0))],
        out_specs=pl.BlockSpec((tm, LANES), lambda i: (i, 0)),
        out_shape=jax.ShapeDtypeStruct((m, LANES), F32),
        compiler_params=_cparams(("parallel",)),
        name="norm_router",
    )(x, nw.reshape(1, d).astype(F32), w)


def _start_row_gather(idx_ref, base, src_hbm, dst, sem, n_rows):
    def body(g, carry):
        for u in range(SUBLANES):
            r = g * SUBLANES + u
            row = idx_ref[base + r]
            pltpu.make_async_copy(src_hbm.at[pl.ds(row, 1), :], dst.at[pl.ds(r, 1), :], sem).start(
                priority=u % 2)
        return carry
    lax.fori_loop(0, n_rows // SUBLANES, body, 0)


def _wait_row_gather(src_hbm, dst, sem, n_rows):
    pltpu.make_async_copy(src_hbm.at[pl.ds(0, n_rows), :], dst.at[pl.ds(0, n_rows), :], sem).wait()


def _gather_norm_kernel(idx_ref, nq_ref, x_hbm, nw_ref, o_ref, buf, sem):
    c = pl.program_id(0)
    slot = c % 2

    def start(chunk, s):
        _start_row_gather(idx_ref, chunk * MOE_CHUNK, x_hbm, buf.at[s], sem.at[s], nq_ref[chunk] * MOE_SUB)

    @pl.when(c == 0)
    def _():
        start(0, 0)

    @pl.when(c + 1 < pl.num_programs(0))
    def _():
        start(c + 1, 1 - slot)

    for sb in range(MOE_CHUNK // MOE_SUB):
        @pl.when(sb < nq_ref[c])
        def _():
            _wait_row_gather(x_hbm, buf.at[slot], sem.at[slot], MOE_SUB)

    for sb in range(MOE_CHUNK // MOE_SUB):
        rows = pl.ds(sb * MOE_SUB, MOE_SUB)

        @pl.when(sb < nq_ref[c])
        def _():
            x = buf[slot, rows, :]
            y = x * lax.rsqrt(jnp.mean(x * x, axis=-1, keepdims=True) + EPS) * nw_ref[...]
            o_ref[rows, :] = y.astype(o_ref.dtype)

        @pl.when(sb >= nq_ref[c])
        def _():
            o_ref[rows, :] = jnp.zeros((MOE_SUB, o_ref.shape[1]), o_ref.dtype)


def _gather_norm(src_token, chunk_nq, x, nw):
    rows = src_token.shape[0]
    d = x.shape[1]
    grid_spec = pltpu.PrefetchScalarGridSpec(
        num_scalar_prefetch=2,
        grid=(rows // MOE_CHUNK,),
        in_specs=[pl.BlockSpec(memory_space=pl.ANY),
                  pl.BlockSpec((1, d), lambda c, idx, nq: (0, 0))],
        out_specs=pl.BlockSpec((MOE_CHUNK, d), lambda c, idx, nq: (c, 0)),
        scratch_shapes=[pltpu.VMEM((2, MOE_CHUNK, d), F32), pltpu.SemaphoreType.DMA((2,))],
    )
    return pl.pallas_call(
        _gather_norm_kernel,
        grid_spec=grid_spec,
        out_shape=jax.ShapeDtypeStruct((rows, d), BF16),
        compiler_params=_cparams(("arbitrary",)),
        name="gather_norm",
    )(src_token, chunk_nq, x, nw.reshape(1, d).astype(F32))


def _expert_changed(te_ref, i):
    return jnp.logical_or(i == 0, te_ref[i] != te_ref[jnp.maximum(i - 1, 0)])


def _for_valid_rows(nq, o_ref, fn, pred=True):
    n_sub = MOE_CHUNK // MOE_SUB
    for v in range(n_sub + 1):
        @pl.when(jnp.logical_and(nq == v, pred))
        def _(v=v):
            rows = v * MOE_SUB
            if rows:
                fn(rows)
            if rows < MOE_CHUNK:
                o_ref[pl.ds(rows, MOE_CHUNK - rows), :] = jnp.zeros((MOE_CHUNK - rows, o_ref.shape[1]),
                                                                    o_ref.dtype)


def _gmm_swiglu_kernel(te_ref, nq_ref, nv_ref, a_ref, wg_ref, wu_ref, o_ref, wg_s, wu_s):
    c = pl.program_id(1)
    nq = nq_ref[c]
    changed = _expert_changed(te_ref, c)

    def compute(rows, fresh):
        a = a_ref[pl.ds(0, rows), :]
        if fresh:
            wg = wg_ref[0].astype(BF16)
            wg_s[...] = wg
            g = jnp.dot(a, wg, preferred_element_type=F32)
            wu = wu_ref[0].astype(BF16)
            wu_s[...] = wu
            u = jnp.dot(a, wu, preferred_element_type=F32)
        else:
            g = jnp.dot(a, wg_s[...], preferred_element_type=F32)
            u = jnp.dot(a, wu_s[...], preferred_element_type=F32)
        o_ref[pl.ds(0, rows), :] = (_silu(g) * u).astype(o_ref.dtype)

    _for_valid_rows(nq, o_ref, lambda rows: compute(rows, False), jnp.logical_not(changed))
    _for_valid_rows(nq, o_ref, lambda rows: compute(rows, True), changed)


def _gmm_swiglu(chunk_expert, chunk_nq, n_valid, a, w_gu, tn):
    m, k = a.shape
    f = w_gu.shape[2] // 2
    nb = f // tn
    last = lambda c, nv: jnp.minimum(c, nv[0] - 1)
    grid_spec = pltpu.PrefetchScalarGridSpec(
        num_scalar_prefetch=3,
        grid=(nb, m // MOE_CHUNK),
        in_specs=[pl.BlockSpec((MOE_CHUNK, k), lambda j, c, te, nq, nv: (last(c, nv), 0)),
                  pl.BlockSpec((1, k, tn), lambda j, c, te, nq, nv: (te[c], 0, j)),
                  pl.BlockSpec((1, k, tn), lambda j, c, te, nq, nv: (te[c], 0, j + nb))],
        out_specs=pl.BlockSpec((MOE_CHUNK, tn), lambda j, c, te, nq, nv: (c, j)),
        scratch_shapes=[pltpu.VMEM((k, tn), BF16)] * 2,
    )
    return pl.pallas_call(
        _gmm_swiglu_kernel,
        grid_spec=grid_spec,
        out_shape=jax.ShapeDtypeStruct((m, f), BF16),
        compiler_params=_cparams(("arbitrary", "arbitrary")),
        name="gmm_swiglu",
    )(chunk_expert, chunk_nq, n_valid, a, w_gu, w_gu)


def _gmm_down_kernel(te_ref, nq_ref, nv_ref, a_ref, w_ref, o_ref):
    c = pl.program_id(0)

    def compute(rows):
        w = w_ref[0].astype(BF16)
        o_ref[pl.ds(0, rows), :] = jnp.dot(a_ref[pl.ds(0, rows), :], w, preferred_element_type=F32)

    _for_valid_rows(nq_ref[c], o_ref, compute)


def _gmm_down(chunk_expert, chunk_nq, n_valid, a, w_d, tn):
    m, k = a.shape
    n = w_d.shape[2]
    nj = n // tn
    last = lambda c, nv: jnp.minimum(c, nv[0] - 1)
    col = lambda j, c, nv: jnp.where(c < nv[0], j, nj - 1)
    grid_spec = pltpu.PrefetchScalarGridSpec(
        num_scalar_prefetch=3,
        grid=(m // MOE_CHUNK, nj),
        in_specs=[pl.BlockSpec((MOE_CHUNK, k), lambda c, j, te, nq, nv: (last(c, nv), 0)),
                  pl.BlockSpec((1, k, tn), lambda c, j, te, nq, nv: (te[c], 0, col(j, c, nv)))],
        out_specs=pl.BlockSpec((MOE_CHUNK, tn), lambda c, j, te, nq, nv: (c, j)),
    )
    return pl.pallas_call(
        _gmm_down_kernel,
        grid_spec=grid_spec,
        out_shape=jax.ShapeDtypeStruct((m, n), F32),
        compiler_params=pltpu.CompilerParams(dimension_semantics=("arbitrary", "arbitrary"),
                                             vmem_limit_bytes=VMEM_LIMIT_MAX),
        name="gmm_down",
    )(chunk_expert, chunk_nq, n_valid, a, w_d)


def _combine_norm_kernel(pos_ref, x_ref, r_ref, ys_hbm, w_ref, o_ref, buf, sem, *, tm):
    i = pl.program_id(0)
    n = pl.num_programs(0)
    slot = i % 2

    def start(tile, s):
        for kk in range(TOP_K):
            _start_row_gather(pos_ref, kk * (n * tm) + tile * tm, ys_hbm, buf.at[s, kk], sem.at[s], tm)

    @pl.when(i == 0)
    def _():
        start(0, 0)

    @pl.when(i + 1 < n)
    def _():
        start(i + 1, 1 - slot)

    for kk in range(TOP_K):
        _wait_row_gather(ys_hbm, buf.at[slot, kk], sem.at[slot], tm)
    r = r_ref[...]
    moe = r[:, TOP_K:TOP_K + 1] * buf[slot, 0]
    for kk in range(1, TOP_K):
        moe = moe + r[:, TOP_K + kk:TOP_K + kk + 1] * buf[slot, kk]
    x = x_ref[...] + moe
    y = x * lax.rsqrt(jnp.mean(x * x, axis=-1, keepdims=True) + EPS)
    o_ref[...] = y * w_ref[...]


def _combine_norm(pos, x, route, ys, w, tm=256):
    m, d = x.shape
    grid_spec = pltpu.PrefetchScalarGridSpec(
        num_scalar_prefetch=1,
        grid=(m // tm,),
        in_specs=[pl.BlockSpec((tm, d), lambda i, p: (i, 0)),
                  pl.BlockSpec((tm, LANES), lambda i, p: (i, 0)),
                  pl.BlockSpec(memory_space=pl.ANY),
                  pl.BlockSpec((1, d), lambda i, p: (0, 0))],
        out_specs=pl.BlockSpec((tm, d), lambda i, p: (i, 0)),
        scratch_shapes=[pltpu.VMEM((2, TOP_K, tm, d), F32), pltpu.SemaphoreType.DMA((2,))],
    )
    return pl.pallas_call(
        functools.partial(_combine_norm_kernel, tm=tm),
        grid_spec=grid_spec,
        out_shape=jax.ShapeDtypeStruct((m, d), F32),
        compiler_params=_cparams(("arbitrary",)),
        name="combine_norm",
    )(pos, x, route, ys, w.reshape(1, d).astype(F32))


def _moe_routing(route, n_experts):
    m = route.shape[0]
    n_sub = MOE_CHUNK // MOE_SUB
    ids = route[:, :TOP_K].astype(jnp.int32)
    flat_e = ids.reshape(-1)
    onehot = (flat_e[:, None] == jnp.arange(n_experts)[None, :]).astype(jnp.int32)
    rank = jnp.sum((jnp.cumsum(onehot, axis=0) - onehot) * onehot, axis=1)
    counts = jnp.sum(onehot, axis=0)
    subs = (counts + MOE_SUB - 1) // MOE_SUB
    n_chunks = (subs + n_sub - 1) // n_sub
    chunk_end = jnp.cumsum(n_chunks)
    chunk_start = chunk_end - n_chunks
    max_chunks = (m * TOP_K // MOE_SUB + n_experts + n_sub - 1) // n_sub + n_experts
    rows = max_chunks * MOE_CHUNK
    first_subs = jnp.where(subs % n_sub == 0, n_sub, subs % n_sub)
    first_rows = jnp.sum(onehot * first_subs[None, :], axis=1) * MOE_SUB
    rest = rank - first_rows
    chunk_off = jnp.where(rest < 0, 0, 1 + rest // MOE_CHUNK)
    row_off = jnp.where(rest < 0, rank, rest % MOE_CHUNK)
    pos = (jnp.sum(onehot * chunk_start[None, :], axis=1) + chunk_off) * MOE_CHUNK + row_off
    src_token = jnp.zeros((rows,), jnp.int32).at[pos].set(jnp.arange(m * TOP_K, dtype=jnp.int32) // TOP_K)
    n_valid = chunk_end[n_experts - 1].astype(jnp.int32)
    cidx = jnp.minimum(jnp.arange(max_chunks, dtype=jnp.int32), n_valid - 1)
    chunk_expert = jnp.sum((cidx[:, None] >= chunk_end[None, :]).astype(jnp.int32), axis=1).astype(jnp.int32)
    sel = (chunk_expert[:, None] == jnp.arange(n_experts)[None, :]).astype(jnp.int32)
    within = cidx - jnp.sum(sel * chunk_start[None, :], axis=1)
    chunk_nq = jnp.where(within == 0, jnp.sum(sel * first_subs[None, :], axis=1), n_sub)
    chunk_nq = jnp.where(jnp.arange(max_chunks) < n_valid, chunk_nq, 0).astype(jnp.int32)
    pos_kmajor = pos.reshape(m, TOP_K).T.reshape(-1).astype(jnp.int32)
    return src_token, chunk_expert, chunk_nq, n_valid.reshape(1), pos_kmajor


def kernel(x, norm_w, final_norm_w, mix_in_w, gdn_conv_w, gdn_a_log, gdn_dt_bias, gdn_norm_w, hgrn_lb_logits, hgrn_norm_w, mix_out_w, ffn_gate_up_w, ffn_down_w, rg_in_w, rg_conv_w, rg_conv_b, rg_gate_a_w, rg_gate_a_b, rg_gate_x_w, rg_gate_x_b, rg_lambda, rg_out_w, moe_router_w, moe_gate_up_w, moe_down_w):
    bsz, s, d = x.shape
    m = bsz * s
    n_heads = gdn_a_log.shape[1]
    hw = n_heads * HEAD_DIM
    n_experts = moe_router_w.shape[2]
    xr = x.reshape(m, d)

    hgrn_lb = jnp.cumsum(jax.nn.softmax(hgrn_lb_logits.astype(F32), axis=0), axis=0)
    w_in = mix_in_w[0]
    w_b = w_in[:, 4 * hw + 2 * n_heads:]
    w_ba = jnp.zeros((d, LANES), F32).at[:, :2 * n_heads].set(w_in[:, 4 * hw:4 * hw + 2 * n_heads])
    h = _rmsnorm(xr, norm_w[0, 0], BF16)
    proj_a = _mm(h, w_in, F32, 1024, 1024, n_cols=4 * hw).reshape(bsz, s, 4 * hw)
    proj_b = _mm(h, w_b, F32, 1024, 1024).reshape(bsz, s, 4 * hw)
    ba = _mm(h, w_ba, F32, 1024, LANES).reshape(bsz, s, LANES)
    o_a = _gdn(proj_a, ba, gdn_conv_w[0].astype(F32), gdn_a_log[0], gdn_dt_bias[0], gdn_norm_w[0], col0=0)
    o_b = _hgrn(proj_b, hgrn_lb[0], hgrn_norm_w[0], col0=0, n_heads=n_heads)
    xr = _mm_resid([o_a.reshape(m, hw), o_b.reshape(m, hw)], mix_out_w[0], xr, 1024, 1024)
    h = _rmsnorm(xr, norm_w[0, 1], BF16)
    act = _mm_swiglu(h, ffn_gate_up_w[0], 1024, 512)
    xr = _mm_resid([act], ffn_down_w[0], xr, 512, 512)

    h = _rmsnorm(xr, norm_w[1, 0], BF16)
    yx = _mm(h, rg_in_w[0], F32, 1024, 1024).reshape(bsz, s, -1)
    rec = _rglru(yx, rg_conv_w[0], rg_conv_b[0], rg_gate_a_w[0], rg_gate_a_b[0],
                 rg_gate_x_w[0], rg_gate_x_b[0], rg_lambda[0])
    xr = _mm_resid([rec.reshape(m, d)], rg_out_w[0], xr, 1024, 1024)
    route = _norm_router(xr, norm_w[1, 1], moe_router_w[0])
    src_token, chunk_expert, chunk_nq, n_valid, pos = _moe_routing(route, n_experts)
    hs = _gather_norm(src_token, chunk_nq, xr, norm_w[1, 1])
    act = _gmm_swiglu(chunk_expert, chunk_nq, n_valid, hs, moe_gate_up_w[0], 512)
    ys = _gmm_down(chunk_expert, chunk_nq, n_valid, act, moe_down_w[0], 256)
    out = _combine_norm(pos, xr, route, ys, final_norm_w)
    return out.reshape(bsz, s, d)
```

```python
import functools

import jax
import jax.numpy as jnp
from jax import lax
from jax.experimental import pallas as pl
from jax.experimental.pallas import tpu as pltpu

F32 = jnp.float32
BF16 = jnp.bfloat16
HIGHEST = lax.Precision.HIGHEST

EPS = 1e-6
CHUNK = 64
CONV_W = 4
HEAD_DIM = 128
RG_BLOCK = 256
RG_C = 8.0
TOP_K = 2
LANES = 128
SUBLANES = 8
VMEM_LIMIT = 56 * 1024 * 1024
VMEM_LIMIT_MAX = 60 * 1024 * 1024
MOE_CHUNK = 1024
MOE_SUB = 128

_NT = (((1,), (1,)), ((), ()))
_TN = (((0,), (0,)), ((), ()))


def _cparams(sem):
    return pltpu.CompilerParams(dimension_semantics=sem, vmem_limit_bytes=VMEM_LIMIT)


def _sigmoid(x):
    return jax.nn.sigmoid(x)


def _silu(x):
    return x * _sigmoid(x)


def _softplus(x):
    return jnp.maximum(x, 0.0) + jnp.log1p(jnp.exp(-jnp.abs(x)))


def _bdot(a, b):
    return jnp.dot(a.astype(BF16), b.astype(BF16), preferred_element_type=F32)


def _bdot_g(a, b, dims):
    return lax.dot_general(a.astype(BF16), b.astype(BF16), dims, preferred_element_type=F32)


def _rmsnorm_kernel(x_ref, w_ref, o_ref):
    x = x_ref[...]
    y = x * lax.rsqrt(jnp.mean(x * x, axis=-1, keepdims=True) + EPS)
    o_ref[...] = (y * w_ref[...]).astype(o_ref.dtype)


def _rmsnorm(x, w, out_dtype, tm=512):
    m, d = x.shape
    return pl.pallas_call(
        _rmsnorm_kernel,
        grid=(m // tm,),
        in_specs=[pl.BlockSpec((tm, d), lambda i: (i, 0)),
                  pl.BlockSpec((1, d), lambda i: (0, 0))],
        out_specs=pl.BlockSpec((tm, d), lambda i: (i, 0)),
        out_shape=jax.ShapeDtypeStruct((m, d), out_dtype),
        compiler_params=_cparams(("parallel",)),
        name="rmsnorm",
    )(x, w.reshape(1, d).astype(F32))


def _bf16_weight(w_ref, w_s, fresh):
    if not fresh:
        return w_s[...]
    w = w_ref[...].astype(BF16)
    w_s[...] = w
    return w


def _first_row_tile_or_not(body):
    first = pl.program_id(1) == 0
    pl.when(first)(lambda: body(True))
    pl.when(jnp.logical_not(first))(lambda: body(False))


def _mm_kernel(a_ref, w_ref, o_ref, w_s):
    def body(fresh):
        w = _bf16_weight(w_ref, w_s, fresh)
        o_ref[...] = jnp.dot(a_ref[...], w, preferred_element_type=F32).astype(o_ref.dtype)

    _first_row_tile_or_not(body)


def _mm_shifted_kernel(a_ref, w_ref, we_ref, o_ref, w_s, *, shift):
    def body(fresh):
        if fresh:
            wide = jnp.concatenate([w_ref[...], we_ref[...]], axis=1)
            w = wide[:, shift:shift + w_ref.shape[1]].astype(BF16)
            w_s[...] = w
        else:
            w = w_s[...]
        o_ref[...] = jnp.dot(a_ref[...], w, preferred_element_type=F32).astype(o_ref.dtype)

    _first_row_tile_or_not(body)


def _mm_shifted(a, w, out_dtype, tm, tn, col0, n_cols):
    m, k = a.shape
    base, shift = col0 // LANES * LANES, col0 % LANES
    assert base % tn == 0 and 0 < shift and n_cols % tn == 0
    edge = lambda j, i: (0, (base + (j + 1) * tn) // LANES)
    return pl.pallas_call(
        functools.partial(_mm_shifted_kernel, shift=shift),
        grid=(n_cols // tn, m // tm),
        in_specs=[pl.BlockSpec((tm, k), lambda j, i: (i, 0)),
                  pl.BlockSpec((k, tn), lambda j, i: (0, base // tn + j)),
                  pl.BlockSpec((k, LANES), edge)],
        out_specs=pl.BlockSpec((tm, tn), lambda j, i: (i, j)),
        out_shape=jax.ShapeDtypeStruct((m, n_cols), out_dtype),
        scratch_shapes=[pltpu.VMEM((k, tn), BF16)],
        compiler_params=_cparams(("arbitrary", "arbitrary")),
        name="mm_shifted",
    )(a, w, w)


def _mm(a, w, out_dtype, tm, tn, n_cols=None, col0=0):
    m, k = a.shape
    n = w.shape[1] if n_cols is None else n_cols
    cb = col0 // tn
    return pl.pallas_call(
        _mm_kernel,
        grid=(n // tn, m // tm),
        in_specs=[pl.BlockSpec((tm, k), lambda j, i: (i, 0)),
                  pl.BlockSpec((k, tn), lambda j, i: (0, cb + j))],
        out_specs=pl.BlockSpec((tm, tn), lambda j, i: (i, j)),
        out_shape=jax.ShapeDtypeStruct((m, n), out_dtype),
        scratch_shapes=[pltpu.VMEM((k, tn), BF16)],
        compiler_params=_cparams(("arbitrary", "arbitrary")),
        name="mm",
    )(a, w)


def _mm_swiglu_kernel(a_ref, wg_ref, wu_ref, o_ref, wg_s, wu_s):
    def body(fresh):
        a = a_ref[...]
        g = jnp.dot(a, _bf16_weight(wg_ref, wg_s, fresh), preferred_element_type=F32)
        u = jnp.dot(a, _bf16_weight(wu_ref, wu_s, fresh), preferred_element_type=F32)
        o_ref[...] = (_silu(g) * u).astype(o_ref.dtype)

    _first_row_tile_or_not(body)


def _mm_swiglu(a, w_gu, tm, tn):
    m, k = a.shape
    f = w_gu.shape[1] // 2
    nb = f // tn
    return pl.pallas_call(
        _mm_swiglu_kernel,
        grid=(nb, m // tm),
        in_specs=[pl.BlockSpec((tm, k), lambda j, i: (i, 0)),
                  pl.BlockSpec((k, tn), lambda j, i: (0, j)),
                  pl.BlockSpec((k, tn), lambda j, i: (0, j + nb))],
        out_specs=pl.BlockSpec((tm, tn), lambda j, i: (i, j)),
        out_shape=jax.ShapeDtypeStruct((m, f), BF16),
        scratch_shapes=[pltpu.VMEM((k, tn), BF16)] * 2,
        compiler_params=_cparams(("arbitrary", "arbitrary")),
        name="mm_swiglu",
    )(a, w_gu, w_gu)


def _mm_resid_kernel(*refs, n_pairs):
    r_ref, o_ref = refs[2 * n_pairs], refs[2 * n_pairs + 1]
    w_s = refs[2 * n_pairs + 2:]

    def body(fresh):
        acc = r_ref[...]
        for p in range(n_pairs):
            w = _bf16_weight(refs[2 * p + 1], w_s[p], fresh)
            acc = acc + jnp.dot(refs[2 * p][...], w, preferred_element_type=F32)
        o_ref[...] = acc

    _first_row_tile_or_not(body)


def _mm_resid(a_list, w, resid, tm, tn):
    m, n = resid.shape
    in_specs, args, scratch = [], [], []
    for p, a in enumerate(a_list):
        k = a.shape[1]
        in_specs += [pl.BlockSpec((tm, k), lambda j, i: (i, 0)),
                     pl.BlockSpec((k, tn), lambda j, i, p=p: (p, j))]
        args += [a, w]
        scratch.append(pltpu.VMEM((k, tn), BF16))
    in_specs.append(pl.BlockSpec((tm, tn), lambda j, i: (i, j)))
    args.append(resid)
    return pl.pallas_call(
        functools.partial(_mm_resid_kernel, n_pairs=len(a_list)),
        grid=(n // tn, m // tm),
        in_specs=in_specs,
        out_specs=pl.BlockSpec((tm, tn), lambda j, i: (i, j)),
        out_shape=jax.ShapeDtypeStruct((m, n), F32),
        scratch_shapes=scratch,
        compiler_params=_cparams(("arbitrary", "arbitrary")),
        name="mm_resid",
    )(*args)


def _causal_conv(buf_ref, x, cw, t_len):
    buf_ref[pl.ds(SUBLANES, t_len), :] = x
    acc = cw[CONV_W - 1:CONV_W, :] * x
    for j in range(CONV_W - 1):
        off = SUBLANES - (CONV_W - 1) + j
        acc = acc + cw[j:j + 1, :] * buf_ref[pl.ds(off, t_len), :]
    buf_ref[pl.ds(0, SUBLANES), :] = x[t_len - SUBLANES:, :]
    return acc


def _tri_masks():
    row = lax.broadcasted_iota(jnp.int32, (CHUNK, CHUNK), 0)
    col = lax.broadcasted_iota(jnp.int32, (CHUNK, CHUNK), 1)
    return row >= col, row > col, row == col


def _gated_head_norm(o, nw, z):
    o = o * lax.rsqrt(jnp.mean(o * o, axis=-1, keepdims=True) + EPS) * nw
    return o * _silu(z)


def _gdn_kernel(q_ref, k_ref, v_ref, z_ref, ba_ref, cwq_ref, cwk_ref, cwv_ref, alog_ref, dtb_ref,
                nw_ref, o_ref, qbuf, kbuf, vbuf, qs, ks, vs, gcs, gct, bs, u_s, w_s, qk_s, qd_s, kd_s, s_ref,
                *, n_heads, t_len):
    t = pl.program_id(1)
    n_chunks = t_len // CHUNK
    width = n_heads * HEAD_DIM

    @pl.when(t == 0)
    def _():
        zeros = jnp.zeros((SUBLANES, width), F32)
        qbuf[pl.ds(0, SUBLANES), :] = zeros
        kbuf[pl.ds(0, SUBLANES), :] = zeros
        vbuf[pl.ds(0, SUBLANES), :] = zeros
        s_ref[...] = jnp.zeros_like(s_ref)

    qc = _silu(_causal_conv(qbuf, q_ref[0], cwq_ref[...], t_len))
    kc = _silu(_causal_conv(kbuf, k_ref[0], cwk_ref[...], t_len))
    vs[...] = _silu(_causal_conv(vbuf, v_ref[0], cwv_ref[...], t_len))
    for h in range(n_heads):
        sl = slice(h * HEAD_DIM, (h + 1) * HEAD_DIM)
        qh = qc[:, sl]
        kh = kc[:, sl]
        qs[:, sl] = qh * (lax.rsqrt(jnp.sum(qh * qh, axis=-1, keepdims=True) + EPS) * (HEAD_DIM ** -0.5))
        ks[:, sl] = kh * lax.rsqrt(jnp.sum(kh * kh, axis=-1, keepdims=True) + EPS)

    ba = ba_ref[0]
    g_log = -jnp.exp(alog_ref[...]) * _softplus(ba + dtb_ref[...])
    bs[...] = _sigmoid(ba)
    row = lax.broadcasted_iota(jnp.int32, (t_len, t_len), 0)
    col = lax.broadcasted_iota(jnp.int32, (t_len, t_len), 1)
    block_tri = ((row // CHUNK == col // CHUNK) & (row >= col)).astype(F32)
    gc_blk = jnp.dot(block_tri, g_log, precision=HIGHEST, preferred_element_type=F32)
    gcs[...] = gc_blk
    for c in range(n_chunks):
        gct[c] = gc_blk[c * CHUNK:(c + 1) * CHUNK, :].T

    incl, strict, eye = _tri_masks()
    eye_f = eye.astype(F32)
    nw = nw_ref[...]

    heads = range(n_heads)
    hsl = [slice(h * HEAD_DIM, (h + 1) * HEAD_DIM) for h in heads]
    qsl = [slice(h * HEAD_DIM, h * HEAD_DIM + CHUNK) for h in heads]
    chunks_per_iter = 2
    assert n_chunks % chunks_per_iter == 0

    def pass_a(it, carry):
        items = []
        for cc in range(chunks_per_iter):
            c = it * chunks_per_iter + cc
            rows = pl.ds(pl.multiple_of(c * CHUNK, CHUNK), CHUNK)
            gc_all = gcs[rows, :]
            beta_all = bs[rows, :]
            gct_c = gct[c]
            for h in heads:
                gc = gc_all[:, n_heads + h:n_heads + h + 1]
                beta = beta_all[:, h:h + 1]
                gc_row = gct_c[n_heads + h:n_heads + h + 1, :]
                decay = jnp.where(incl, jnp.exp(jnp.where(incl, gc - gc_row, 0.0)), 0.0)
                items.append(dict(rows=rows, h=h, gc=gc, beta=beta, decay=decay,
                                  q=qs[rows, hsl[h]], k=ks[rows, hsl[h]]))
        for it_ in items:
            it_["kb"] = it_["k"].astype(BF16)
            it_["kk"] = lax.dot_general(it_["kb"], it_["kb"], _NT, preferred_element_type=F32)
        for it_ in items:
            it_["x"] = jnp.where(strict, -(it_["beta"] * it_["kk"] * it_["decay"]), 0.0)
            it_["p"] = eye_f + it_["x"]
        for _ in range(5):
            for it_ in items:
                xb = it_["x"].astype(BF16)
                it_["x"] = jnp.dot(xb, xb, preferred_element_type=F32)
            for it_ in items:
                it_["p"] = it_["p"] + _bdot(it_["p"], it_["x"])
        for it_ in items:
            egc = jnp.exp(it_["gc"])
            it_["egc"] = egc
            v = vs[it_["rows"], hsl[it_["h"]]]
            rhs = jnp.concatenate([v * it_["beta"], it_["k"] * (it_["beta"] * egc)], axis=-1)
            it_["sol"] = _bdot(it_["p"], rhs)
        for it_ in items:
            it_["qk"] = lax.dot_general(it_["q"].astype(BF16), it_["kb"], _NT,
                                        preferred_element_type=F32) * it_["decay"]
        for it_ in items:
            rows, h, gc = it_["rows"], it_["h"], it_["gc"]
            g_last = gc[CHUNK - 1:CHUNK, :]
            u_s[rows, hsl[h]] = it_["sol"][:, :HEAD_DIM]
            w_s[rows, hsl[h]] = it_["sol"][:, HEAD_DIM:].astype(BF16)
            qk_s[rows, qsl[h]] = it_["qk"].astype(BF16)
            qd_s[rows, hsl[h]] = (it_["q"] * it_["egc"]).astype(BF16)
            kd_s[rows, hsl[h]] = (it_["k"] * jnp.exp(g_last - gc)).astype(BF16)
        return carry

    lax.fori_loop(0, n_chunks // chunks_per_iter, pass_a, 0)

    def pass_b(c, carry):
        rows = pl.ds(pl.multiple_of(c * CHUNK, CHUNK), CHUNK)
        gc_all = gcs[rows, :]
        states = [s_ref[h] for h in heads]
        sbs = [s.astype(BF16) for s in states]
        v_new = [u_s[rows, hsl[h]] - jnp.dot(w_s[rows, hsl[h]], sbs[h], preferred_element_type=F32)
                 for h in heads]
        vbs = [v.astype(BF16) for v in v_new]
        for h in heads:
            g_last = gc_all[CHUNK - 1:CHUNK, n_heads + h:n_heads + h + 1]
            s_ref[h] = states[h] * jnp.exp(g_last) + lax.dot_general(kd_s[rows, hsl[h]], vbs[h], _TN,
                                                                      preferred_element_type=F32)
        outs = [jnp.dot(qd_s[rows, hsl[h]], sbs[h], preferred_element_type=F32)
                + jnp.dot(qk_s[rows, qsl[h]], vbs[h], preferred_element_type=F32) for h in heads]
        for h in heads:
            o = _gated_head_norm(outs[h], nw, z_ref[0, rows, hsl[h]])
            o_ref[0, rows, hsl[h]] = o.astype(o_ref.dtype)
        return carry

    lax.fori_loop(0, n_chunks, pass_b, 0)


def _gdn(proj, ba, conv_w, a_log, dt_bias, norm_w, *, col0, t_len=256):
    bsz, s, _ = proj.shape
    n_heads = a_log.shape[0]
    width = n_heads * HEAD_DIM
    cb = col0 // width

    def col_spec(group):
        return pl.BlockSpec((1, t_len, width), lambda b, t: (b, t, cb + group))

    def cw_spec(group):
        return pl.BlockSpec((CONV_W, width), lambda b, t: (0, group))

    row = jnp.zeros((1, LANES), F32)
    alog_row = row.at[0, n_heads:2 * n_heads].set(a_log.astype(F32))
    dtb_row = row.at[0, n_heads:2 * n_heads].set(dt_bias.astype(F32))
    small = pl.BlockSpec((1, LANES), lambda b, t: (0, 0))
    kern = functools.partial(_gdn_kernel, n_heads=n_heads, t_len=t_len)
    return pl.pallas_call(
        kern,
        grid=(bsz, s // t_len),
        in_specs=[col_spec(0), col_spec(1), col_spec(2), col_spec(3),
                  pl.BlockSpec((1, t_len, LANES), lambda b, t: (b, t, 0)),
                  cw_spec(0), cw_spec(1), cw_spec(2), small, small,
                  pl.BlockSpec((1, HEAD_DIM), lambda b, t: (0, 0))],
        out_specs=pl.BlockSpec((1, t_len, width), lambda b, t: (b, t, 0)),
        out_shape=jax.ShapeDtypeStruct((bsz, s, width), BF16),
        scratch_shapes=[pltpu.VMEM((SUBLANES + t_len, width), F32)] * 3
        + [pltpu.VMEM((t_len, width), F32)] * 3
        + [pltpu.VMEM((t_len, LANES), F32),
           pltpu.VMEM((t_len // CHUNK, LANES, CHUNK), F32),
           pltpu.VMEM((t_len, LANES), F32),
           pltpu.VMEM((t_len, width), F32)]
        + [pltpu.VMEM((t_len, width), BF16)] * 4
        + [pltpu.VMEM((n_heads, HEAD_DIM, HEAD_DIM), F32)],
        compiler_params=_cparams(("parallel", "arbitrary")),
        name="gdn",
    )(proj, proj, proj, proj, ba, conv_w, conv_w, conv_w, alog_row, dtb_row,
      norm_w.reshape(1, HEAD_DIM).astype(F32))


def _hgrn_kernel(q_ref, f_ref, i_ref, g_ref, lb_ref, nw_ref, o_ref, qs, ks, ls, st_ref, *, heads, t_len):
    t = pl.program_id(1)

    @pl.when(t == 0)
    def _():
        st_ref[...] = jnp.zeros_like(st_ref)

    lb = lb_ref[...]
    forget = lb + (1.0 - lb) * _sigmoid(f_ref[0])
    ks[...] = 1.0 - forget
    qs[...] = _silu(q_ref[0]) * (HEAD_DIM ** -0.5)
    row = lax.broadcasted_iota(jnp.int32, (t_len, t_len), 0)
    col = lax.broadcasted_iota(jnp.int32, (t_len, t_len), 1)
    block_tri = ((row // CHUNK == col // CHUNK) & (row >= col)).astype(F32)
    ls[...] = jnp.dot(block_tri, jnp.log(forget), precision=HIGHEST, preferred_element_type=F32)

    incl, _, _ = _tri_masks()
    nw = nw_ref[...]
    mid = CHUNK // 2 - 1

    hsl = [slice(j * HEAD_DIM, (j + 1) * HEAD_DIM) for j in range(heads)]

    def chunk_body(c, carry):
        rows = pl.ds(pl.multiple_of(c * CHUNK, CHUNK), CHUNK)
        items = []
        for j in range(heads):
            b = ls[rows, hsl[j]]
            items.append(dict(b=b, b_mid=b[mid:mid + 1, :], b_last=b[CHUNK - 1:CHUNK, :],
                              q=qs[rows, hsl[j]], k=ks[rows, hsl[j]],
                              vb=i_ref[0, rows, hsl[j]].astype(BF16)))
        for it in items:
            it["att"] = jnp.where(incl, _bdot_g(it["q"] * jnp.exp(it["b"] - it["b_mid"]),
                                                it["k"] * jnp.exp(it["b_mid"] - it["b"]), _NT), 0.0)
        for j, it in enumerate(items):
            st = st_ref[j]
            it["o"] = _bdot_g(it["q"] * jnp.exp(it["b"]), st, _NT)
            st_ref[j] = st * jnp.exp(it["b_last"]) + lax.dot_general(
                it["vb"], (it["k"] * jnp.exp(it["b_last"] - it["b"])).astype(BF16), _TN,
                preferred_element_type=F32)
        for it in items:
            it["o"] = it["o"] + jnp.dot(it["att"].astype(BF16), it["vb"], preferred_element_type=F32)
        for j, it in enumerate(items):
            o = _gated_head_norm(it["o"], nw, g_ref[0, rows, hsl[j]])
            o_ref[0, rows, hsl[j]] = o.astype(o_ref.dtype)
        return carry

    lax.fori_loop(0, t_len // CHUNK, chunk_body, 0)


def _hgrn(proj, lb, norm_w, *, col0, n_heads, t_len=256):
    bsz, s, _ = proj.shape
    width = n_heads * HEAD_DIM
    cb = col0 // width

    def col_spec(group):
        return pl.BlockSpec((1, t_len, width), lambda b, t: (b, t, cb + group))

    kern = functools.partial(_hgrn_kernel, heads=n_heads, t_len=t_len)
    return pl.pallas_call(
        kern,
        grid=(bsz, s // t_len),
        in_specs=[col_spec(0), col_spec(1), col_spec(2), col_spec(3),
                  pl.BlockSpec((1, width), lambda b, t: (0, 0)),
                  pl.BlockSpec((1, HEAD_DIM), lambda b, t: (0, 0))],
        out_specs=pl.BlockSpec((1, t_len, width), lambda b, t: (b, t, 0)),
        out_shape=jax.ShapeDtypeStruct((bsz, s, width), BF16),
        scratch_shapes=[pltpu.VMEM((t_len, width), F32)] * 3
        + [pltpu.VMEM((n_heads, HEAD_DIM, HEAD_DIM), F32)],
        compiler_params=_cparams(("parallel", "arbitrary")),
        name="hgrn2",
    )(proj, proj, proj, proj, lb.reshape(1, width).astype(F32), norm_w.reshape(1, HEAD_DIM).astype(F32))


def _rglru_kernel(y_ref, x_ref, cw_ref, cb_ref, wa_ref, ba_ref, wx_ref, bx_ref, lam_ref, o_ref,
                  xbuf, a_s, b_s, h_s, hcar, *, blocks, t_len):
    t = pl.program_id(2)

    @pl.when(t == 0)
    def _():
        xbuf[pl.ds(0, SUBLANES), :] = jnp.zeros((SUBLANES, blocks * RG_BLOCK), F32)
        hcar[...] = jnp.zeros_like(hcar)

    xc = _causal_conv(xbuf, x_ref[0], cw_ref[...], t_len) + cb_ref[...]
    for n in range(blocks):
        sl = slice(n * RG_BLOCK, (n + 1) * RG_BLOCK)
        xb = xc[:, sl]
        r = _sigmoid(_bdot(xb, wa_ref[n]) + ba_ref[:, sl])
        gi = _sigmoid(_bdot(xb, wx_ref[n]) + bx_ref[:, sl])
        log_a = (-RG_C) * r * _softplus(-lam_ref[:, sl])
        a = jnp.exp(log_a)
        a_s[:, sl] = a
        one_minus_a2 = -jnp.tanh(log_a) * (a * a + 1.0)
        b_s[:, sl] = jnp.sqrt(jnp.maximum(one_minus_a2, 0.0)) * (gi * xb)

    def row_body(i, h):
        h = a_s[pl.ds(i, 1), :] * h + b_s[pl.ds(i, 1), :]
        h_s[pl.ds(i, 1), :] = h
        return h

    hcar[...] = lax.fori_loop(0, t_len, row_body, hcar[...])
    o_ref[0] = (jax.nn.gelu(y_ref[0], approximate=True) * h_s[...]).astype(o_ref.dtype)


def _rglru(yx, conv_w, conv_b, wa, ba, wx, bx, lam, *, blocks_per_step=8, t_len=256):
    bsz, s, w2 = yx.shape
    width = w2 // 2
    gw = blocks_per_step * RG_BLOCK
    n_g = width // gw
    vec = lambda a: a.reshape(1, width).astype(F32)
    vspec = pl.BlockSpec((1, gw), lambda b, g, t: (0, g))
    wspec = pl.BlockSpec((blocks_per_step, RG_BLOCK, RG_BLOCK), lambda b, g, t: (g, 0, 0))
    kern = functools.partial(_rglru_kernel, blocks=blocks_per_step, t_len=t_len)
    return pl.pallas_call(
        kern,
        grid=(bsz, n_g, s // t_len),
        in_specs=[pl.BlockSpec((1, t_len, gw), lambda b, g, t: (b, t, g)),
                  pl.BlockSpec((1, t_len, gw), lambda b, g, t: (b, t, n_g + g)),
                  pl.BlockSpec((CONV_W, gw), lambda b, g, t: (0, g)),
                  vspec, wspec, vspec, wspec, vspec, vspec],
        out_specs=pl.BlockSpec((1, t_len, gw), lambda b, g, t: (b, t, g)),
        out_shape=jax.ShapeDtypeStruct((bsz, s, width), BF16),
        scratch_shapes=[pltpu.VMEM((SUBLANES + t_len, gw), F32)]
        + [pltpu.VMEM((t_len, gw), F32)] * 3
        + [pltpu.VMEM((1, gw), F32)],
        compiler_params=_cparams(("parallel", "parallel", "arbitrary")),
        name="rglru",
    )(yx, yx, conv_w.astype(F32), vec(conv_b), wa.astype(BF16), vec(ba), wx.astype(BF16), vec(bx), vec(lam))


def _norm_router_kernel(x_ref, nw_ref, w_ref, o_ref, *, n_experts):
    x = x_ref[...]
    y = x * lax.rsqrt(jnp.mean(x * x, axis=-1, keepdims=True) + EPS) * nw_ref[...]
    w = w_ref[...]
    y_hi = y.astype(BF16)
    y_lo = (y - y_hi.astype(F32)).astype(BF16)
    w_hi = w.astype(BF16)
    w_lo = (w - w_hi.astype(F32)).astype(BF16)
    logits = (jnp.dot(y_hi, w_hi, preferred_element_type=F32)
              + (jnp.dot(y_lo, w_hi, preferred_element_type=F32)
                 + jnp.dot(y_hi, w_lo, preferred_element_type=F32)))
    lane = lax.broadcasted_iota(jnp.int32, logits.shape, 1).astype(F32)
    neg = jnp.float32(-jnp.inf)
    l1 = jnp.where(lane < n_experts, logits, neg)
    m1 = jnp.max(l1, axis=-1, keepdims=True)
    i1 = jnp.min(jnp.where(l1 == m1, lane, float(LANES)), axis=-1, keepdims=True)
    l2 = jnp.where(lane == i1, neg, l1)
    m2 = jnp.max(l2, axis=-1, keepdims=True)
    i2 = jnp.min(jnp.where(l2 == m2, lane, float(LANES)), axis=-1, keepdims=True)
    e2 = jnp.exp(m2 - m1)
    g1 = 1.0 / (1.0 + e2)
    g2 = e2 / (1.0 + e2)
    out = jnp.where(lane == 0, i1, 0.0)
    out = jnp.where(lane == 1, i2, out)
    out = jnp.where(lane == 2, g1, out)
    out = jnp.where(lane == 3, g2, out)
    o_ref[...] = out


def _norm_router(x, nw, router_w, tm=512):
    m, d = x.shape
    n_experts = router_w.shape[1]
    w = jnp.zeros((d, LANES), F32).at[:, :n_experts].set(router_w.astype(F32))
    return pl.pallas_call(
        functools.partial(_norm_router_kernel, n_experts=n_experts),
        grid=(m // tm,),
        in_specs=[pl.BlockSpec((tm, d), lambda i: (i, 0)),
                  pl.BlockSpec((1, d), lambda i: (0, 0)),
                  pl.BlockSpec((d, LANES), lambda i: (0, 0))],
        out_specs=pl.BlockSpec((tm, LANES), lambda i: (i, 0)),
        out_shape=jax.ShapeDtypeStruct((m, LANES), F32),
        compiler_params=_cparams(("parallel",)),
        name="norm_router",
    )(x, nw.reshape(1, d).astype(F32), w)


def _start_row_gather(idx_ref, base, src3, dst3, sem, n_rows):
    def body(g, carry):
        for u in range(SUBLANES):
            row = idx_ref[base + g * SUBLANES + u]
            src = src3.at[lax.shift_right_logical(row, 3), pl.ds(row & (SUBLANES - 1), 1), :]
            pltpu.make_async_copy(src, dst3.at[g, pl.ds(u, 1), :], sem).start(priority=u % 2)
        return carry
    lax.fori_loop(0, n_rows // SUBLANES, body, 0)


def _wait_row_gather(src3, dst3, sem, n_rows):
    n = n_rows // SUBLANES
    pltpu.make_async_copy(src3.at[pl.ds(0, n)], dst3.at[pl.ds(0, n)], sem).wait()


def _gather_norm_kernel(idx_ref, nq_ref, x_hbm, nw_ref, o_ref, buf, sem):
    c = pl.program_id(0)
    slot = c % 2

    def start(chunk, s):
        _start_row_gather(idx_ref, chunk * MOE_CHUNK, x_hbm, buf.at[s], sem.at[s], nq_ref[chunk] * MOE_SUB)

    @pl.when(c == 0)
    def _():
        start(0, 0)

    @pl.when(c + 1 < pl.num_programs(0))
    def _():
        start(c + 1, 1 - slot)

    for sb in range(MOE_CHUNK // MOE_SUB):
        @pl.when(sb < nq_ref[c])
        def _():
            _wait_row_gather(x_hbm, buf.at[slot], sem.at[slot], MOE_SUB)

    for sb in range(MOE_CHUNK // MOE_SUB):
        rows = pl.ds(sb * MOE_SUB, MOE_SUB)

        @pl.when(sb < nq_ref[c])
        def _(sb=sb, rows=rows):
            tiles = pl.ds(sb * (MOE_SUB // SUBLANES), MOE_SUB // SUBLANES)
            x = buf[slot, tiles].reshape(MOE_SUB, o_ref.shape[1])
            y = x * lax.rsqrt(jnp.mean(x * x, axis=-1, keepdims=True) + EPS) * nw_ref[...]
            o_ref[rows, :] = y.astype(o_ref.dtype)

        @pl.when(sb >= nq_ref[c])
        def _():
            o_ref[rows, :] = jnp.zeros((MOE_SUB, o_ref.shape[1]), o_ref.dtype)


def _gather_norm(src_token, chunk_nq, x, nw):
    rows = src_token.shape[0]
    d = x.shape[1]
    grid_spec = pltpu.PrefetchScalarGridSpec(
        num_scalar_prefetch=2,
        grid=(rows // MOE_CHUNK,),
        in_specs=[pl.BlockSpec(memory_space=pl.ANY),
                  pl.BlockSpec((1, d), lambda c, idx, nq: (0, 0))],
        out_specs=pl.BlockSpec((MOE_CHUNK, d), lambda c, idx, nq: (c, 0)),
        scratch_shapes=[pltpu.VMEM((2, MOE_CHUNK // SUBLANES, SUBLANES, d), F32),
                        pltpu.SemaphoreType.DMA((2,))],
    )
    return pl.pallas_call(
        _gather_norm_kernel,
        grid_spec=grid_spec,
        out_shape=jax.ShapeDtypeStruct((rows, d), BF16),
        compiler_params=_cparams(("arbitrary",)),
        name="gather_norm",
    )(src_token, chunk_nq, x.reshape(-1, SUBLANES, d), nw.reshape(1, d).astype(F32))


def _expert_changed(te_ref, i):
    return jnp.logical_or(i == 0, te_ref[i] != te_ref[jnp.maximum(i - 1, 0)])


def _for_valid_rows(nq, o_ref, fn, pred=True):
    n_sub = MOE_CHUNK // MOE_SUB
    for v in range(n_sub + 1):
        @pl.when(jnp.logical_and(nq == v, pred))
        def _(v=v):
            rows = v * MOE_SUB
            if rows:
                fn(rows)
            if rows < MOE_CHUNK:
                o_ref[pl.ds(rows, MOE_CHUNK - rows), :] = jnp.zeros((MOE_CHUNK - rows, o_ref.shape[1]),
                                                                    o_ref.dtype)


def _gmm_swiglu_kernel(te_ref, nq_ref, nv_ref, a_ref, wg_ref, wu_ref, o_ref, wg_s, wu_s):
    c = pl.program_id(1)
    nq = nq_ref[c]
    changed = _expert_changed(te_ref, c)

    def compute(rows, fresh):
        a = a_ref[pl.ds(0, rows), :]
        if fresh:
            wg = wg_ref[0].astype(BF16)
            wg_s[...] = wg
            g = jnp.dot(a, wg, preferred_element_type=F32)
            wu = wu_ref[0].astype(BF16)
            wu_s[...] = wu
            u = jnp.dot(a, wu, preferred_element_type=F32)
        else:
            g = jnp.dot(a, wg_s[...], preferred_element_type=F32)
            u = jnp.dot(a, wu_s[...], preferred_element_type=F32)
        o_ref[pl.ds(0, rows), :] = (_silu(g) * u).astype(o_ref.dtype)

    _for_valid_rows(nq, o_ref, lambda rows: compute(rows, False), jnp.logical_not(changed))
    _for_valid_rows(nq, o_ref, lambda rows: compute(rows, True), changed)


def _gmm_swiglu(chunk_expert, chunk_nq, n_valid, a, w_gu, tn):
    m, k = a.shape
    f = w_gu.shape[2] // 2
    nb = f // tn
    last = lambda c, nv: jnp.minimum(c, nv[0] - 1)
    grid_spec = pltpu.PrefetchScalarGridSpec(
        num_scalar_prefetch=3,
        grid=(nb, m // MOE_CHUNK),
        in_specs=[pl.BlockSpec((MOE_CHUNK, k), lambda j, c, te, nq, nv: (last(c, nv), 0)),
                  pl.BlockSpec((1, k, tn), lambda j, c, te, nq, nv: (te[c], 0, j)),
                  pl.BlockSpec((1, k, tn), lambda j, c, te, nq, nv: (te[c], 0, j + nb))],
        out_specs=pl.BlockSpec((MOE_CHUNK, tn), lambda j, c, te, nq, nv: (c, j)),
        scratch_shapes=[pltpu.VMEM((k, tn), BF16)] * 2,
    )
    return pl.pallas_call(
        _gmm_swiglu_kernel,
        grid_spec=grid_spec,
        out_shape=jax.ShapeDtypeStruct((m, f), BF16),
        compiler_params=_cparams(("arbitrary", "arbitrary")),
        name="gmm_swiglu",
    )(chunk_expert, chunk_nq, n_valid, a, w_gu, w_gu)


def _gmm_down_kernel(te_ref, nq_ref, nv_ref, a_ref, w_ref, o_ref):
    c = pl.program_id(0)

    def compute(rows):
        w = w_ref[0].astype(BF16)
        o_ref[pl.ds(0, rows), :] = jnp.dot(a_ref[pl.ds(0, rows), :], w, preferred_element_type=F32)

    _for_valid_rows(nq_ref[c], o_ref, compute)


def _gmm_down(chunk_expert, chunk_nq, n_valid, a, w_d, tn):
    m, k = a.shape
    n = w_d.shape[2]
    nj = n // tn
    last = lambda c, nv: jnp.minimum(c, nv[0] - 1)
    col = lambda j, c, nv: jnp.where(c < nv[0], j, nj - 1)
    grid_spec = pltpu.PrefetchScalarGridSpec(
        num_scalar_prefetch=3,
        grid=(m // MOE_CHUNK, nj),
        in_specs=[pl.BlockSpec((MOE_CHUNK, k), lambda c, j, te, nq, nv: (last(c, nv), 0)),
                  pl.BlockSpec((1, k, tn), lambda c, j, te, nq, nv: (te[c], 0, col(j, c, nv)))],
        out_specs=pl.BlockSpec((MOE_CHUNK, tn), lambda c, j, te, nq, nv: (c, j)),
    )
    return pl.pallas_call(
        _gmm_down_kernel,
        grid_spec=grid_spec,
        out_shape=jax.ShapeDtypeStruct((m, n), F32),
        compiler_params=pltpu.CompilerParams(dimension_semantics=("arbitrary", "arbitrary"),
                                             vmem_limit_bytes=VMEM_LIMIT_MAX),
        name="gmm_down",
    )(chunk_expert, chunk_nq, n_valid, a, w_d)


def _combine_norm_kernel(pos_ref, x_ref, r_ref, ys_hbm, w_ref, o_ref, buf, sem, *, tm):
    i = pl.program_id(0)
    n = pl.num_programs(0)
    slot = i % 2

    def start(tile, s):
        for kk in range(TOP_K):
            _start_row_gather(pos_ref, kk * (n * tm) + tile * tm, ys_hbm, buf.at[s, kk], sem.at[s], tm)

    @pl.when(i == 0)
    def _():
        start(0, 0)

    @pl.when(i + 1 < n)
    def _():
        start(i + 1, 1 - slot)

    for kk in range(TOP_K):
        _wait_row_gather(ys_hbm, buf.at[slot, kk], sem.at[slot], tm)
    r = r_ref[...]
    d = x_ref.shape[1]
    moe = r[:, TOP_K:TOP_K + 1] * buf[slot, 0].reshape(tm, d)
    for kk in range(1, TOP_K):
        moe = moe + r[:, TOP_K + kk:TOP_K + kk + 1] * buf[slot, kk].reshape(tm, d)
    x = x_ref[...] + moe
    y = x * lax.rsqrt(jnp.mean(x * x, axis=-1, keepdims=True) + EPS)
    o_ref[...] = y * w_ref[...]


def _combine_norm(pos, x, route, ys, w, tm=256):
    m, d = x.shape
    grid_spec = pltpu.PrefetchScalarGridSpec(
        num_scalar_prefetch=1,
        grid=(m // tm,),
        in_specs=[pl.BlockSpec((tm, d), lambda i, p: (i, 0)),
                  pl.BlockSpec((tm, LANES), lambda i, p: (i, 0)),
                  pl.BlockSpec(memory_space=pl.ANY),
                  pl.BlockSpec((1, d), lambda i, p: (0, 0))],
        out_specs=pl.BlockSpec((tm, d), lambda i, p: (i, 0)),
        scratch_shapes=[pltpu.VMEM((2, TOP_K, tm // SUBLANES, SUBLANES, d), F32),
                        pltpu.SemaphoreType.DMA((2,))],
    )
    return pl.pallas_call(
        functools.partial(_combine_norm_kernel, tm=tm),
        grid_spec=grid_spec,
        out_shape=jax.ShapeDtypeStruct((m, d), F32),
        compiler_params=_cparams(("arbitrary",)),
        name="combine_norm",
    )(pos, x, route, ys.reshape(-1, SUBLANES, d), w.reshape(1, d).astype(F32))


def _moe_routing(route, n_experts):
    m = route.shape[0]
    n_sub = MOE_CHUNK // MOE_SUB
    ids = route[:, :TOP_K].astype(jnp.int32)
    flat_e = ids.reshape(-1)
    onehot = (flat_e[:, None] == jnp.arange(n_experts)[None, :]).astype(jnp.int32)
    rank = jnp.sum((jnp.cumsum(onehot, axis=0) - onehot) * onehot, axis=1)
    counts = jnp.sum(onehot, axis=0)
    subs = (counts + MOE_SUB - 1) // MOE_SUB
    n_chunks = (subs + n_sub - 1) // n_sub
    chunk_end = jnp.cumsum(n_chunks)
    chunk_start = chunk_end - n_chunks
    max_chunks = (m * TOP_K // MOE_SUB + n_experts + n_sub - 1) // n_sub + n_experts
    rows = max_chunks * MOE_CHUNK
    first_subs = jnp.where(subs % n_sub == 0, n_sub, subs % n_sub)
    first_rows = jnp.sum(onehot * first_subs[None, :], axis=1) * MOE_SUB
    rest = rank - first_rows
    chunk_off = jnp.where(rest < 0, 0, 1 + rest // MOE_CHUNK)
    row_off = jnp.where(rest < 0, rank, rest % MOE_CHUNK)
    pos = (jnp.sum(onehot * chunk_start[None, :], axis=1) + chunk_off) * MOE_CHUNK + row_off
    src_token = jnp.zeros((rows,), jnp.int32).at[pos].set(jnp.arange(m * TOP_K, dtype=jnp.int32) // TOP_K)
    n_valid = chunk_end[n_experts - 1].astype(jnp.int32)
    cidx = jnp.minimum(jnp.arange(max_chunks, dtype=jnp.int32), n_valid - 1)
    chunk_expert = jnp.sum((cidx[:, None] >= chunk_end[None, :]).astype(jnp.int32), axis=1).astype(jnp.int32)
    sel = (chunk_expert[:, None] == jnp.arange(n_experts)[None, :]).astype(jnp.int32)
    within = cidx - jnp.sum(sel * chunk_start[None, :], axis=1)
    chunk_nq = jnp.where(within == 0, jnp.sum(sel * first_subs[None, :], axis=1), n_sub)
    chunk_nq = jnp.where(jnp.arange(max_chunks) < n_valid, chunk_nq, 0).astype(jnp.int32)
    pos_kmajor = pos.reshape(m, TOP_K).T.reshape(-1).astype(jnp.int32)
    return src_token, chunk_expert, chunk_nq, n_valid.reshape(1), pos_kmajor


def kernel(x, norm_w, final_norm_w, mix_in_w, gdn_conv_w, gdn_a_log, gdn_dt_bias, gdn_norm_w, hgrn_lb_logits, hgrn_norm_w, mix_out_w, ffn_gate_up_w, ffn_down_w, rg_in_w, rg_conv_w, rg_conv_b, rg_gate_a_w, rg_gate_a_b, rg_gate_x_w, rg_gate_x_b, rg_lambda, rg_out_w, moe_router_w, moe_gate_up_w, moe_down_w):
    bsz, s, d = x.shape
    m = bsz * s
    n_heads = gdn_a_log.shape[1]
    hw = n_heads * HEAD_DIM
    n_experts = moe_router_w.shape[2]
    xr = x.reshape(m, d)

    hgrn_lb = jnp.cumsum(jax.nn.softmax(hgrn_lb_logits.astype(F32), axis=0), axis=0)
    w_in = mix_in_w[0]
    h = _rmsnorm(xr, norm_w[0, 0], BF16)
    proj_a = _mm(h, w_in, F32, 1024, 1024, n_cols=4 * hw).reshape(bsz, s, 4 * hw)
    proj_b = _mm_shifted(h, w_in, F32, 1024, 1024, col0=4 * hw + 2 * n_heads,
                         n_cols=4 * hw).reshape(bsz, s, 4 * hw)
    ba = _mm(h, w_in, F32, 1024, LANES, n_cols=LANES, col0=4 * hw).reshape(bsz, s, LANES)
    o_a = _gdn(proj_a, ba, gdn_conv_w[0].astype(F32), gdn_a_log[0], gdn_dt_bias[0], gdn_norm_w[0], col0=0)
    o_b = _hgrn(proj_b, hgrn_lb[0], hgrn_norm_w[0], col0=0, n_heads=n_heads)
    xr = _mm_resid([o_a.reshape(m, hw), o_b.reshape(m, hw)], mix_out_w[0], xr, 1024, 1024)
    h = _rmsnorm(xr, norm_w[0, 1], BF16)
    act = _mm_swiglu(h, ffn_gate_up_w[0], 1024, 512)
    xr = _mm_resid([act], ffn_down_w[0], xr, 512, 512)

    h = _rmsnorm(xr, norm_w[1, 0], BF16)
    yx = _mm(h, rg_in_w[0], F32, 1024, 1024).reshape(bsz, s, -1)
    rec = _rglru(yx, rg_conv_w[0], rg_conv_b[0], rg_gate_a_w[0], rg_gate_a_b[0],
                 rg_gate_x_w[0], rg_gate_x_b[0], rg_lambda[0])
    xr = _mm_resid([rec.reshape(m, d)], rg_out_w[0], xr, 1024, 1024)
    route = _norm_router(xr, norm_w[1, 1], moe_router_w[0])
    src_token, chunk_expert, chunk_nq, n_valid, pos = _moe_routing(route, n_experts)
    hs = _gather_norm(src_token, chunk_nq, xr, norm_w[1, 1])
    act = _gmm_swiglu(chunk_expert, chunk_nq, n_valid, hs, moe_gate_up_w[0], 512)
    ys = _gmm_down(chunk_expert, chunk_nq, n_valid, act, moe_down_w[0], 256)
    out = _combine_norm(pos, xr, route, ys, final_norm_w)
    return out.reshape(bsz, s, d)
```

```python
import functools

import jax
import jax.numpy as jnp
from jax import lax
from jax.experimental import pallas as pl
from jax.experimental.pallas import tpu as pltpu

F32 = jnp.float32
BF16 = jnp.bfloat16
HIGHEST = lax.Precision.HIGHEST

EPS = 1e-6
CHUNK = 64
CONV_W = 4
HEAD_DIM = 128
RG_BLOCK = 256
RG_C = 8.0
TOP_K = 2
LANES = 128
SUBLANES = 8
VMEM_LIMIT = 56 * 1024 * 1024
VMEM_LIMIT_MAX = 60 * 1024 * 1024
MOE_CHUNK = 1024
MOE_SUB = 128

_NT = (((1,), (1,)), ((), ()))
_TN = (((0,), (0,)), ((), ()))


def _cparams(sem):
    return pltpu.CompilerParams(dimension_semantics=sem, vmem_limit_bytes=VMEM_LIMIT)


def _sigmoid(x):
    return jax.nn.sigmoid(x)


def _silu(x):
    return x * _sigmoid(x)


def _softplus(x):
    return jnp.maximum(x, 0.0) + jnp.log1p(jnp.exp(-jnp.abs(x)))


def _bdot(a, b):
    return jnp.dot(a.astype(BF16), b.astype(BF16), preferred_element_type=F32)


def _bdot_g(a, b, dims):
    return lax.dot_general(a.astype(BF16), b.astype(BF16), dims, preferred_element_type=F32)


def _rmsnorm_kernel(x_ref, w_ref, o_ref):
    x = x_ref[...]
    y = x * lax.rsqrt(jnp.mean(x * x, axis=-1, keepdims=True) + EPS)
    o_ref[...] = (y * w_ref[...]).astype(o_ref.dtype)


def _rmsnorm(x, w, out_dtype, tm=512):
    m, d = x.shape
    return pl.pallas_call(
        _rmsnorm_kernel,
        grid=(m // tm,),
        in_specs=[pl.BlockSpec((tm, d), lambda i: (i, 0)),
                  pl.BlockSpec((1, d), lambda i: (0, 0))],
        out_specs=pl.BlockSpec((tm, d), lambda i: (i, 0)),
        out_shape=jax.ShapeDtypeStruct((m, d), out_dtype),
        compiler_params=_cparams(("parallel",)),
        name="rmsnorm",
    )(x, w.reshape(1, d).astype(F32))


def _bf16_weight(w_ref, w_s, fresh):
    if not fresh:
        return w_s[...]
    w = w_ref[...].astype(BF16)
    w_s[...] = w
    return w


def _first_row_tile_or_not(body):
    first = pl.program_id(1) == 0
    pl.when(first)(lambda: body(True))
    pl.when(jnp.logical_not(first))(lambda: body(False))


def _mm_kernel(a_ref, w_ref, o_ref, w_s):
    def body(fresh):
        w = _bf16_weight(w_ref, w_s, fresh)
        o_ref[...] = jnp.dot(a_ref[...], w, preferred_element_type=F32).astype(o_ref.dtype)

    _first_row_tile_or_not(body)


def _mm_nt_kernel(*refs, shift):
    a_ref, w_ref = refs[0], refs[1]
    o_ref, w_s = refs[-2], refs[-1]

    def body(fresh):
        if fresh:
            w = w_ref[...]
            if shift:
                w = jnp.concatenate([w, refs[2][...]], axis=0)[shift:shift + w_ref.shape[0], :]
            w = w.astype(BF16)
            w_s[...] = w
        else:
            w = w_s[...]
        o_ref[...] = lax.dot_general(a_ref[...], w, _NT, preferred_element_type=F32).astype(o_ref.dtype)

    _first_row_tile_or_not(body)


def _mm_nt(a, wt, out_dtype, tm, tn, row0, n_rows):
    m, k = a.shape
    base, shift = row0 // tn * tn, row0 % tn
    assert shift % SUBLANES == 0 and shift <= LANES and n_rows % tn == 0
    in_specs = [pl.BlockSpec((tm, k), lambda j, i: (i, 0)),
                pl.BlockSpec((tn, k), lambda j, i: (base // tn + j, 0))]
    args = [a, wt]
    if shift:
        in_specs.append(pl.BlockSpec((LANES, k), lambda j, i: ((base + (j + 1) * tn) // LANES, 0)))
        args.append(wt)
    return pl.pallas_call(
        functools.partial(_mm_nt_kernel, shift=shift),
        grid=(n_rows // tn, m // tm),
        in_specs=in_specs,
        out_specs=pl.BlockSpec((tm, tn), lambda j, i: (i, j)),
        out_shape=jax.ShapeDtypeStruct((m, n_rows), out_dtype),
        scratch_shapes=[pltpu.VMEM((tn, k), BF16)],
        compiler_params=_cparams(("arbitrary", "arbitrary")),
        name="mm_nt",
    )(*args)


def _mm(a, w, out_dtype, tm, tn, n_cols=None, col0=0):
    m, k = a.shape
    n = w.shape[1] if n_cols is None else n_cols
    cb = col0 // tn
    return pl.pallas_call(
        _mm_kernel,
        grid=(n // tn, m // tm),
        in_specs=[pl.BlockSpec((tm, k), lambda j, i: (i, 0)),
                  pl.BlockSpec((k, tn), lambda j, i: (0, cb + j))],
        out_specs=pl.BlockSpec((tm, tn), lambda j, i: (i, j)),
        out_shape=jax.ShapeDtypeStruct((m, n), out_dtype),
        scratch_shapes=[pltpu.VMEM((k, tn), BF16)],
        compiler_params=_cparams(("arbitrary", "arbitrary")),
        name="mm",
    )(a, w)


def _mm_swiglu_kernel(a_ref, wg_ref, wu_ref, o_ref, wg_s, wu_s):
    def body(fresh):
        a = a_ref[...]
        g = jnp.dot(a, _bf16_weight(wg_ref, wg_s, fresh), preferred_element_type=F32)
        u = jnp.dot(a, _bf16_weight(wu_ref, wu_s, fresh), preferred_element_type=F32)
        o_ref[...] = (_silu(g) * u).astype(o_ref.dtype)

    _first_row_tile_or_not(body)


def _mm_swiglu(a, w_gu, tm, tn):
    m, k = a.shape
    f = w_gu.shape[1] // 2
    nb = f // tn
    return pl.pallas_call(
        _mm_swiglu_kernel,
        grid=(nb, m // tm),
        in_specs=[pl.BlockSpec((tm, k), lambda j, i: (i, 0)),
                  pl.BlockSpec((k, tn), lambda j, i: (0, j)),
                  pl.BlockSpec((k, tn), lambda j, i: (0, j + nb))],
        out_specs=pl.BlockSpec((tm, tn), lambda j, i: (i, j)),
        out_shape=jax.ShapeDtypeStruct((m, f), BF16),
        scratch_shapes=[pltpu.VMEM((k, tn), BF16)] * 2,
        compiler_params=_cparams(("arbitrary", "arbitrary")),
        name="mm_swiglu",
    )(a, w_gu, w_gu)


def _mm_resid_kernel(*refs, n_pairs):
    r_ref, o_ref = refs[2 * n_pairs], refs[2 * n_pairs + 1]
    w_s = refs[2 * n_pairs + 2:]

    def body(fresh):
        acc = r_ref[...]
        for p in range(n_pairs):
            w = _bf16_weight(refs[2 * p + 1], w_s[p], fresh)
            acc = acc + jnp.dot(refs[2 * p][...], w, preferred_element_type=F32)
        o_ref[...] = acc

    _first_row_tile_or_not(body)


def _mm_resid(a_list, w, resid, tm, tn):
    m, n = resid.shape
    in_specs, args, scratch = [], [], []
    for p, a in enumerate(a_list):
        k = a.shape[1]
        in_specs += [pl.BlockSpec((tm, k), lambda j, i: (i, 0)),
                     pl.BlockSpec((k, tn), lambda j, i, p=p: (p, j))]
        args += [a, w]
        scratch.append(pltpu.VMEM((k, tn), BF16))
    in_specs.append(pl.BlockSpec((tm, tn), lambda j, i: (i, j)))
    args.append(resid)
    return pl.pallas_call(
        functools.partial(_mm_resid_kernel, n_pairs=len(a_list)),
        grid=(n // tn, m // tm),
        in_specs=in_specs,
        out_specs=pl.BlockSpec((tm, tn), lambda j, i: (i, j)),
        out_shape=jax.ShapeDtypeStruct((m, n), F32),
        scratch_shapes=scratch,
        compiler_params=_cparams(("arbitrary", "arbitrary")),
        name="mm_resid",
    )(*args)


def _causal_conv(buf_ref, x, cw, t_len):
    buf_ref[pl.ds(SUBLANES, t_len), :] = x
    acc = cw[CONV_W - 1:CONV_W, :] * x
    for j in range(CONV_W - 1):
        off = SUBLANES - (CONV_W - 1) + j
        acc = acc + cw[j:j + 1, :] * buf_ref[pl.ds(off, t_len), :]
    buf_ref[pl.ds(0, SUBLANES), :] = x[t_len - SUBLANES:, :]
    return acc


def _tri_masks():
    row = lax.broadcasted_iota(jnp.int32, (CHUNK, CHUNK), 0)
    col = lax.broadcasted_iota(jnp.int32, (CHUNK, CHUNK), 1)
    return row >= col, row > col, row == col


def _gated_head_norm(o, nw, z):
    o = o * lax.rsqrt(jnp.mean(o * o, axis=-1, keepdims=True) + EPS) * nw
    return o * _silu(z)


def _gdn_kernel(q_ref, k_ref, v_ref, z_ref, ba_ref, cwq_ref, cwk_ref, cwv_ref, alog_ref, dtb_ref,
                nw_ref, o_ref, qbuf, kbuf, vbuf, qs, ks, vs, gcs, gct, bs, u_s, w_s, qk_s, qd_s, kd_s, s_ref,
                *, n_heads, t_len):
    t = pl.program_id(1)
    n_chunks = t_len // CHUNK
    width = n_heads * HEAD_DIM

    @pl.when(t == 0)
    def _():
        zeros = jnp.zeros((SUBLANES, width), F32)
        qbuf[pl.ds(0, SUBLANES), :] = zeros
        kbuf[pl.ds(0, SUBLANES), :] = zeros
        vbuf[pl.ds(0, SUBLANES), :] = zeros
        s_ref[...] = jnp.zeros_like(s_ref)

    qc = _silu(_causal_conv(qbuf, q_ref[0], cwq_ref[...], t_len))
    kc = _silu(_causal_conv(kbuf, k_ref[0], cwk_ref[...], t_len))
    vs[...] = _silu(_causal_conv(vbuf, v_ref[0], cwv_ref[...], t_len))
    for h in range(n_heads):
        sl = slice(h * HEAD_DIM, (h + 1) * HEAD_DIM)
        qh = qc[:, sl]
        kh = kc[:, sl]
        qs[:, sl] = qh * (lax.rsqrt(jnp.sum(qh * qh, axis=-1, keepdims=True) + EPS) * (HEAD_DIM ** -0.5))
        ks[:, sl] = kh * lax.rsqrt(jnp.sum(kh * kh, axis=-1, keepdims=True) + EPS)

    ba = ba_ref[0]
    g_log = -jnp.exp(alog_ref[...]) * _softplus(ba + dtb_ref[...])
    bs[...] = _sigmoid(ba)
    row = lax.broadcasted_iota(jnp.int32, (t_len, t_len), 0)
    col = lax.broadcasted_iota(jnp.int32, (t_len, t_len), 1)
    block_tri = ((row // CHUNK == col // CHUNK) & (row >= col)).astype(F32)
    gc_blk = jnp.dot(block_tri, g_log, precision=HIGHEST, preferred_element_type=F32)
    gcs[...] = gc_blk
    for c in range(n_chunks):
        gct[c] = gc_blk[c * CHUNK:(c + 1) * CHUNK, :].T

    incl, strict, eye = _tri_masks()
    eye_f = eye.astype(F32)
    nw = nw_ref[...]

    heads = range(n_heads)
    hsl = [slice(h * HEAD_DIM, (h + 1) * HEAD_DIM) for h in heads]
    qsl = [slice(h * HEAD_DIM, h * HEAD_DIM + CHUNK) for h in heads]
    chunks_per_iter = 2
    assert n_chunks % chunks_per_iter == 0

    def pass_a(it, carry):
        items = []
        for cc in range(chunks_per_iter):
            c = it * chunks_per_iter + cc
            rows = pl.ds(pl.multiple_of(c * CHUNK, CHUNK), CHUNK)
            gc_all = gcs[rows, :]
            beta_all = bs[rows, :]
            gct_c = gct[c]
            for h in heads:
                gc = gc_all[:, n_heads + h:n_heads + h + 1]
                beta = beta_all[:, h:h + 1]
                gc_row = gct_c[n_heads + h:n_heads + h + 1, :]
                decay = jnp.where(incl, jnp.exp(jnp.where(incl, gc - gc_row, 0.0)), 0.0)
                items.append(dict(rows=rows, h=h, gc=gc, beta=beta, decay=decay,
                                  q=qs[rows, hsl[h]], k=ks[rows, hsl[h]]))
        for it_ in items:
            it_["kb"] = it_["k"].astype(BF16)
            it_["kk"] = lax.dot_general(it_["kb"], it_["kb"], _NT, preferred_element_type=F32)
        for it_ in items:
            it_["x"] = jnp.where(strict, -(it_["beta"] * it_["kk"] * it_["decay"]), 0.0)
            it_["p"] = eye_f + it_["x"]
        for _ in range(5):
            for it_ in items:
                xb = it_["x"].astype(BF16)
                it_["x"] = jnp.dot(xb, xb, preferred_element_type=F32)
            for it_ in items:
                it_["p"] = it_["p"] + _bdot(it_["p"], it_["x"])
        for it_ in items:
            egc = jnp.exp(it_["gc"])
            it_["egc"] = egc
            v = vs[it_["rows"], hsl[it_["h"]]]
            rhs = jnp.concatenate([v * it_["beta"], it_["k"] * (it_["beta"] * egc)], axis=-1)
            it_["sol"] = _bdot(it_["p"], rhs)
        for it_ in items:
            it_["qk"] = lax.dot_general(it_["q"].astype(BF16), it_["kb"], _NT,
                                        preferred_element_type=F32) * it_["decay"]
        for it_ in items:
            rows, h, gc = it_["rows"], it_["h"], it_["gc"]
            g_last = gc[CHUNK - 1:CHUNK, :]
            u_s[rows, hsl[h]] = it_["sol"][:, :HEAD_DIM]
            w_s[rows, hsl[h]] = it_["sol"][:, HEAD_DIM:].astype(BF16)
            qk_s[rows, qsl[h]] = it_["qk"].astype(BF16)
            qd_s[rows, hsl[h]] = (it_["q"] * it_["egc"]).astype(BF16)
            kd_s[rows, hsl[h]] = (it_["k"] * jnp.exp(g_last - gc)).astype(BF16)
        return carry

    lax.fori_loop(0, n_chunks // chunks_per_iter, pass_a, 0)

    def pass_b(c, carry):
        rows = pl.ds(pl.multiple_of(c * CHUNK, CHUNK), CHUNK)
        gc_all = gcs[rows, :]
        states = [s_ref[h] for h in heads]
        sbs = [s.astype(BF16) for s in states]
        v_new = [u_s[rows, hsl[h]] - jnp.dot(w_s[rows, hsl[h]], sbs[h], preferred_element_type=F32)
                 for h in heads]
        vbs = [v.astype(BF16) for v in v_new]
        for h in heads:
            g_last = gc_all[CHUNK - 1:CHUNK, n_heads + h:n_heads + h + 1]
            s_ref[h] = states[h] * jnp.exp(g_last) + lax.dot_general(kd_s[rows, hsl[h]], vbs[h], _TN,
                                                                      preferred_element_type=F32)
        outs = [jnp.dot(qd_s[rows, hsl[h]], sbs[h], preferred_element_type=F32)
                + jnp.dot(qk_s[rows, qsl[h]], vbs[h], preferred_element_type=F32) for h in heads]
        for h in heads:
            o = _gated_head_norm(outs[h], nw, z_ref[0, rows, hsl[h]])
            o_ref[0, rows, hsl[h]] = o.astype(o_ref.dtype)
        return carry

    lax.fori_loop(0, n_chunks, pass_b, 0)


def _gdn(proj, ba, conv_w, a_log, dt_bias, norm_w, *, col0, t_len=256):
    bsz, s, _ = proj.shape
    n_heads = a_log.shape[0]
    width = n_heads * HEAD_DIM
    cb = col0 // width

    def col_spec(group):
        return pl.BlockSpec((1, t_len, width), lambda b, t: (b, t, cb + group))

    def cw_spec(group):
        return pl.BlockSpec((CONV_W, width), lambda b, t: (0, group))

    row = jnp.zeros((1, LANES), F32)
    alog_row = row.at[0, n_heads:2 * n_heads].set(a_log.astype(F32))
    dtb_row = row.at[0, n_heads:2 * n_heads].set(dt_bias.astype(F32))
    small = pl.BlockSpec((1, LANES), lambda b, t: (0, 0))
    kern = functools.partial(_gdn_kernel, n_heads=n_heads, t_len=t_len)
    return pl.pallas_call(
        kern,
        grid=(bsz, s // t_len),
        in_specs=[col_spec(0), col_spec(1), col_spec(2), col_spec(3),
                  pl.BlockSpec((1, t_len, LANES), lambda b, t: (b, t, 0)),
                  cw_spec(0), cw_spec(1), cw_spec(2), small, small,
                  pl.BlockSpec((1, HEAD_DIM), lambda b, t: (0, 0))],
        out_specs=pl.BlockSpec((1, t_len, width), lambda b, t: (b, t, 0)),
        out_shape=jax.ShapeDtypeStruct((bsz, s, width), BF16),
        scratch_shapes=[pltpu.VMEM((SUBLANES + t_len, width), F32)] * 3
        + [pltpu.VMEM((t_len, width), F32)] * 3
        + [pltpu.VMEM((t_len, LANES), F32),
           pltpu.VMEM((t_len // CHUNK, LANES, CHUNK), F32),
           pltpu.VMEM((t_len, LANES), F32),
           pltpu.VMEM((t_len, width), F32)]
        + [pltpu.VMEM((t_len, width), BF16)] * 4
        + [pltpu.VMEM((n_heads, HEAD_DIM, HEAD_DIM), F32)],
        compiler_params=_cparams(("parallel", "arbitrary")),
        name="gdn",
    )(proj, proj, proj, proj, ba, conv_w, conv_w, conv_w, alog_row, dtb_row,
      norm_w.reshape(1, HEAD_DIM).astype(F32))


def _hgrn_kernel(q_ref, f_ref, i_ref, g_ref, lb_ref, nw_ref, o_ref, qs, ks, ls, st_ref, *, heads, t_len):
    t = pl.program_id(1)

    @pl.when(t == 0)
    def _():
        st_ref[...] = jnp.zeros_like(st_ref)

    lb = lb_ref[...]
    forget = lb + (1.0 - lb) * _sigmoid(f_ref[0])
    ks[...] = 1.0 - forget
    qs[...] = _silu(q_ref[0]) * (HEAD_DIM ** -0.5)
    row = lax.broadcasted_iota(jnp.int32, (t_len, t_len), 0)
    col = lax.broadcasted_iota(jnp.int32, (t_len, t_len), 1)
    block_tri = ((row // CHUNK == col // CHUNK) & (row >= col)).astype(F32)
    ls[...] = jnp.dot(block_tri, jnp.log(forget), precision=HIGHEST, preferred_element_type=F32)

    incl, _, _ = _tri_masks()
    nw = nw_ref[...]
    mid = CHUNK // 2 - 1

    hsl = [slice(j * HEAD_DIM, (j + 1) * HEAD_DIM) for j in range(heads)]

    def chunk_body(c, carry):
        rows = pl.ds(pl.multiple_of(c * CHUNK, CHUNK), CHUNK)
        items = []
        for j in range(heads):
            b = ls[rows, hsl[j]]
            items.append(dict(b=b, b_mid=b[mid:mid + 1, :], b_last=b[CHUNK - 1:CHUNK, :],
                              q=qs[rows, hsl[j]], k=ks[rows, hsl[j]],
                              vb=i_ref[0, rows, hsl[j]].astype(BF16)))
        for it in items:
            it["att"] = jnp.where(incl, _bdot_g(it["q"] * jnp.exp(it["b"] - it["b_mid"]),
                                                it["k"] * jnp.exp(it["b_mid"] - it["b"]), _NT), 0.0)
        for j, it in enumerate(items):
            st = st_ref[j]
            it["o"] = _bdot_g(it["q"] * jnp.exp(it["b"]), st, _NT)
            st_ref[j] = st * jnp.exp(it["b_last"]) + lax.dot_general(
                it["vb"], (it["k"] * jnp.exp(it["b_last"] - it["b"])).astype(BF16), _TN,
                preferred_element_type=F32)
        for it in items:
            it["o"] = it["o"] + jnp.dot(it["att"].astype(BF16), it["vb"], preferred_element_type=F32)
        for j, it in enumerate(items):
            o = _gated_head_norm(it["o"], nw, g_ref[0, rows, hsl[j]])
            o_ref[0, rows, hsl[j]] = o.astype(o_ref.dtype)
        return carry

    lax.fori_loop(0, t_len // CHUNK, chunk_body, 0)


def _hgrn(proj, lb, norm_w, *, col0, n_heads, t_len=256):
    bsz, s, _ = proj.shape
    width = n_heads * HEAD_DIM
    cb = col0 // width

    def col_spec(group):
        return pl.BlockSpec((1, t_len, width), lambda b, t: (b, t, cb + group))

    kern = functools.partial(_hgrn_kernel, heads=n_heads, t_len=t_len)
    return pl.pallas_call(
        kern,
        grid=(bsz, s // t_len),
        in_specs=[col_spec(0), col_spec(1), col_spec(2), col_spec(3),
                  pl.BlockSpec((1, width), lambda b, t: (0, 0)),
                  pl.BlockSpec((1, HEAD_DIM), lambda b, t: (0, 0))],
        out_specs=pl.BlockSpec((1, t_len, width), lambda b, t: (b, t, 0)),
        out_shape=jax.ShapeDtypeStruct((bsz, s, width), BF16),
        scratch_shapes=[pltpu.VMEM((t_len, width), F32)] * 3
        + [pltpu.VMEM((n_heads, HEAD_DIM, HEAD_DIM), F32)],
        compiler_params=_cparams(("parallel", "arbitrary")),
        name="hgrn2",
    )(proj, proj, proj, proj, lb.reshape(1, width).astype(F32), norm_w.reshape(1, HEAD_DIM).astype(F32))


def _rglru_kernel(y_ref, x_ref, cw_ref, cb_ref, wa_ref, ba_ref, wx_ref, bx_ref, lam_ref, o_ref,
                  xbuf, a_s, b_s, h_s, hcar, *, blocks, t_len):
    t = pl.program_id(2)

    @pl.when(t == 0)
    def _():
        xbuf[pl.ds(0, SUBLANES), :] = jnp.zeros((SUBLANES, blocks * RG_BLOCK), F32)
        hcar[...] = jnp.zeros_like(hcar)

    xc = _causal_conv(xbuf, x_ref[0], cw_ref[...], t_len) + cb_ref[...]
    for n in range(blocks):
        sl = slice(n * RG_BLOCK, (n + 1) * RG_BLOCK)
        xb = xc[:, sl]
        r = _sigmoid(_bdot(xb, wa_ref[n]) + ba_ref[:, sl])
        gi = _sigmoid(_bdot(xb, wx_ref[n]) + bx_ref[:, sl])
        log_a = (-RG_C) * r * _softplus(-lam_ref[:, sl])
        a = jnp.exp(log_a)
        a_s[:, sl] = a
        one_minus_a2 = -jnp.tanh(log_a) * (a * a + 1.0)
        b_s[:, sl] = jnp.sqrt(jnp.maximum(one_minus_a2, 0.0)) * (gi * xb)

    def row_body(i, h):
        h = a_s[pl.ds(i, 1), :] * h + b_s[pl.ds(i, 1), :]
        h_s[pl.ds(i, 1), :] = h
        return h

    hcar[...] = lax.fori_loop(0, t_len, row_body, hcar[...])
    o_ref[0] = (jax.nn.gelu(y_ref[0], approximate=True) * h_s[...]).astype(o_ref.dtype)


def _rglru(yx, conv_w, conv_b, wa, ba, wx, bx, lam, *, blocks_per_step=8, t_len=256):
    bsz, s, w2 = yx.shape
    width = w2 // 2
    gw = blocks_per_step * RG_BLOCK
    n_g = width // gw
    vec = lambda a: a.reshape(1, width).astype(F32)
    vspec = pl.BlockSpec((1, gw), lambda b, g, t: (0, g))
    wspec = pl.BlockSpec((blocks_per_step, RG_BLOCK, RG_BLOCK), lambda b, g, t: (g, 0, 0))
    kern = functools.partial(_rglru_kernel, blocks=blocks_per_step, t_len=t_len)
    return pl.pallas_call(
        kern,
        grid=(bsz, n_g, s // t_len),
        in_specs=[pl.BlockSpec((1, t_len, gw), lambda b, g, t: (b, t, g)),
                  pl.BlockSpec((1, t_len, gw), lambda b, g, t: (b, t, n_g + g)),
                  pl.BlockSpec((CONV_W, gw), lambda b, g, t: (0, g)),
                  vspec, wspec, vspec, wspec, vspec, vspec],
        out_specs=pl.BlockSpec((1, t_len, gw), lambda b, g, t: (b, t, g)),
        out_shape=jax.ShapeDtypeStruct((bsz, s, width), BF16),
        scratch_shapes=[pltpu.VMEM((SUBLANES + t_len, gw), F32)]
        + [pltpu.VMEM((t_len, gw), F32)] * 3
        + [pltpu.VMEM((1, gw), F32)],
        compiler_params=_cparams(("parallel", "parallel", "arbitrary")),
        name="rglru",
    )(yx, yx, conv_w.astype(F32), vec(conv_b), wa.astype(BF16), vec(ba), wx.astype(BF16), vec(bx), vec(lam))


def _norm_router_kernel(x_ref, nw_ref, w_ref, o_ref, *, n_experts):
    x = x_ref[...]
    y = x * lax.rsqrt(jnp.mean(x * x, axis=-1, keepdims=True) + EPS) * nw_ref[...]
    w = w_ref[...]
    y_hi = y.astype(BF16)
    y_lo = (y - y_hi.astype(F32)).astype(BF16)
    w_hi = w.astype(BF16)
    w_lo = (w - w_hi.astype(F32)).astype(BF16)
    logits = (jnp.dot(y_hi, w_hi, preferred_element_type=F32)
              + (jnp.dot(y_lo, w_hi, preferred_element_type=F32)
                 + jnp.dot(y_hi, w_lo, preferred_element_type=F32)))
    lane = lax.broadcasted_iota(jnp.int32, logits.shape, 1).astype(F32)
    neg = jnp.float32(-jnp.inf)
    l1 = jnp.where(lane < n_experts, logits, neg)
    m1 = jnp.max(l1, axis=-1, keepdims=True)
    i1 = jnp.min(jnp.where(l1 == m1, lane, float(LANES)), axis=-1, keepdims=True)
    l2 = jnp.where(lane == i1, neg, l1)
    m2 = jnp.max(l2, axis=-1, keepdims=True)
    i2 = jnp.min(jnp.where(l2 == m2, lane, float(LANES)), axis=-1, keepdims=True)
    e2 = jnp.exp(m2 - m1)
    g1 = 1.0 / (1.0 + e2)
    g2 = e2 / (1.0 + e2)
    out = jnp.where(lane == 0, i1, 0.0)
    out = jnp.where(lane == 1, i2, out)
    out = jnp.where(lane == 2, g1, out)
    out = jnp.where(lane == 3, g2, out)
    o_ref[...] = out


def _norm_router(x, nw, router_w, tm=512):
    m, d = x.shape
    n_experts = router_w.shape[1]
    w = jnp.zeros((d, LANES), F32).at[:, :n_experts].set(router_w.astype(F32))
    return pl.pallas_call(
        functools.partial(_norm_router_kernel, n_experts=n_experts),
        grid=(m // tm,),
        in_specs=[pl.BlockSpec((tm, d), lambda i: (i, 0)),
                  pl.BlockSpec((1, d), lambda i: (0, 0)),
                  pl.BlockSpec((d, LANES), lambda i: (0, 0))],
        out_specs=pl.BlockSpec((tm, LANES), lambda i: (i, 0)),
        out_shape=jax.ShapeDtypeStruct((m, LANES), F32),
        compiler_params=_cparams(("parallel",)),
        name="norm_router",
    )(x, nw.reshape(1, d).astype(F32), w)


def _start_row_gather(idx_ref, base, src3, dst3, sem, n_rows):
    def body(g, carry):
        for u in range(SUBLANES):
            row = idx_ref[base + g * SUBLANES + u]
            src = src3.at[lax.shift_right_logical(row, 3), pl.ds(row & (SUBLANES - 1), 1), :]
            pltpu.make_async_copy(src, dst3.at[g, pl.ds(u, 1), :], sem).start(priority=u % 2)
        return carry
    lax.fori_loop(0, n_rows // SUBLANES, body, 0)


def _wait_row_gather(src3, dst3, sem, n_rows):
    n = n_rows // SUBLANES
    pltpu.make_async_copy(src3.at[pl.ds(0, n)], dst3.at[pl.ds(0, n)], sem).wait()


def _gather_norm_kernel(idx_ref, nq_ref, x_hbm, nw_ref, o_ref, buf, sem):
    c = pl.program_id(0)
    slot = c % 2

    def start(chunk, s):
        _start_row_gather(idx_ref, chunk * MOE_CHUNK, x_hbm, buf.at[s], sem.at[s], nq_ref[chunk] * MOE_SUB)

    @pl.when(c == 0)
    def _():
        start(0, 0)

    @pl.when(c + 1 < pl.num_programs(0))
    def _():
        start(c + 1, 1 - slot)

    for sb in range(MOE_CHUNK // MOE_SUB):
        @pl.when(sb < nq_ref[c])
        def _():
            _wait_row_gather(x_hbm, buf.at[slot], sem.at[slot], MOE_SUB)

    for sb in range(MOE_CHUNK // MOE_SUB):
        rows = pl.ds(sb * MOE_SUB, MOE_SUB)

        @pl.when(sb < nq_ref[c])
        def _(sb=sb, rows=rows):
            tiles = pl.ds(sb * (MOE_SUB // SUBLANES), MOE_SUB // SUBLANES)
            x = buf[slot, tiles].reshape(MOE_SUB, o_ref.shape[1])
            y = x * lax.rsqrt(jnp.mean(x * x, axis=-1, keepdims=True) + EPS) * nw_ref[...]
            o_ref[rows, :] = y.astype(o_ref.dtype)

        @pl.when(sb >= nq_ref[c])
        def _():
            o_ref[rows, :] = jnp.zeros((MOE_SUB, o_ref.shape[1]), o_ref.dtype)


def _gather_norm(src_token, chunk_nq, x, nw):
    rows = src_token.shape[0]
    d = x.shape[1]
    grid_spec = pltpu.PrefetchScalarGridSpec(
        num_scalar_prefetch=2,
        grid=(rows // MOE_CHUNK,),
        in_specs=[pl.BlockSpec(memory_space=pl.ANY),
                  pl.BlockSpec((1, d), lambda c, idx, nq: (0, 0))],
        out_specs=pl.BlockSpec((MOE_CHUNK, d), lambda c, idx, nq: (c, 0)),
        scratch_shapes=[pltpu.VMEM((2, MOE_CHUNK // SUBLANES, SUBLANES, d), F32),
                        pltpu.SemaphoreType.DMA((2,))],
    )
    return pl.pallas_call(
        _gather_norm_kernel,
        grid_spec=grid_spec,
        out_shape=jax.ShapeDtypeStruct((rows, d), BF16),
        compiler_params=_cparams(("arbitrary",)),
        name="gather_norm",
    )(src_token, chunk_nq, x.reshape(-1, SUBLANES, d), nw.reshape(1, d).astype(F32))


def _expert_changed(te_ref, i):
    return jnp.logical_or(i == 0, te_ref[i] != te_ref[jnp.maximum(i - 1, 0)])


def _for_valid_rows(nq, o_ref, fn, pred=True):
    n_sub = MOE_CHUNK // MOE_SUB
    for v in range(n_sub + 1):
        @pl.when(jnp.logical_and(nq == v, pred))
        def _(v=v):
            rows = v * MOE_SUB
            if rows:
                fn(rows)
            if rows < MOE_CHUNK:
                o_ref[pl.ds(rows, MOE_CHUNK - rows), :] = jnp.zeros((MOE_CHUNK - rows, o_ref.shape[1]),
                                                                    o_ref.dtype)


def _gmm_swiglu_kernel(te_ref, nq_ref, nv_ref, a_ref, wg_ref, wu_ref, o_ref, wg_s, wu_s):
    c = pl.program_id(1)
    nq = nq_ref[c]
    changed = _expert_changed(te_ref, c)

    def compute(rows, fresh):
        a = a_ref[pl.ds(0, rows), :]
        if fresh:
            wg = wg_ref[0].astype(BF16)
            wg_s[...] = wg
            g = jnp.dot(a, wg, preferred_element_type=F32)
            wu = wu_ref[0].astype(BF16)
            wu_s[...] = wu
            u = jnp.dot(a, wu, preferred_element_type=F32)
        else:
            g = jnp.dot(a, wg_s[...], preferred_element_type=F32)
            u = jnp.dot(a, wu_s[...], preferred_element_type=F32)
        o_ref[pl.ds(0, rows), :] = (_silu(g) * u).astype(o_ref.dtype)

    _for_valid_rows(nq, o_ref, lambda rows: compute(rows, False), jnp.logical_not(changed))
    _for_valid_rows(nq, o_ref, lambda rows: compute(rows, True), changed)


def _gmm_swiglu(chunk_expert, chunk_nq, n_valid, a, w_gu, tn):
    m, k = a.shape
    f = w_gu.shape[2] // 2
    nb = f // tn
    last = lambda c, nv: jnp.minimum(c, nv[0] - 1)
    grid_spec = pltpu.PrefetchScalarGridSpec(
        num_scalar_prefetch=3,
        grid=(nb, m // MOE_CHUNK),
        in_specs=[pl.BlockSpec((MOE_CHUNK, k), lambda j, c, te, nq, nv: (last(c, nv), 0)),
                  pl.BlockSpec((1, k, tn), lambda j, c, te, nq, nv: (te[c], 0, j)),
                  pl.BlockSpec((1, k, tn), lambda j, c, te, nq, nv: (te[c], 0, j + nb))],
        out_specs=pl.BlockSpec((MOE_CHUNK, tn), lambda j, c, te, nq, nv: (c, j)),
        scratch_shapes=[pltpu.VMEM((k, tn), BF16)] * 2,
    )
    return pl.pallas_call(
        _gmm_swiglu_kernel,
        grid_spec=grid_spec,
        out_shape=jax.ShapeDtypeStruct((m, f), BF16),
        compiler_params=_cparams(("arbitrary", "arbitrary")),
        name="gmm_swiglu",
    )(chunk_expert, chunk_nq, n_valid, a, w_gu, w_gu)


def _gmm_down_kernel(te_ref, nq_ref, nv_ref, a_ref, w_ref, o_ref):
    c = pl.program_id(0)

    def compute(rows):
        w = w_ref[0].astype(BF16)
        o_ref[pl.ds(0, rows), :] = jnp.dot(a_ref[pl.ds(0, rows), :], w, preferred_element_type=F32)

    _for_valid_rows(nq_ref[c], o_ref, compute)


def _gmm_down(chunk_expert, chunk_nq, n_valid, a, w_d, tn):
    m, k = a.shape
    n = w_d.shape[2]
    nj = n // tn
    last = lambda c, nv: jnp.minimum(c, nv[0] - 1)
    col = lambda j, c, nv: jnp.where(c < nv[0], j, nj - 1)
    grid_spec = pltpu.PrefetchScalarGridSpec(
        num_scalar_prefetch=3,
        grid=(m // MOE_CHUNK, nj),
        in_specs=[pl.BlockSpec((MOE_CHUNK, k), lambda c, j, te, nq, nv: (last(c, nv), 0)),
                  pl.BlockSpec((1, k, tn), lambda c, j, te, nq, nv: (te[c], 0, col(j, c, nv)))],
        out_specs=pl.BlockSpec((MOE_CHUNK, tn), lambda c, j, te, nq, nv: (c, j)),
    )
    return pl.pallas_call(
        _gmm_down_kernel,
        grid_spec=grid_spec,
        out_shape=jax.ShapeDtypeStruct((m, n), F32),
        compiler_params=pltpu.CompilerParams(dimension_semantics=("arbitrary", "arbitrary"),
                                             vmem_limit_bytes=VMEM_LIMIT_MAX),
        name="gmm_down",
    )(chunk_expert, chunk_nq, n_valid, a, w_d)


def _combine_norm_kernel(pos_ref, x_ref, r_ref, ys_hbm, w_ref, o_ref, buf, sem, *, tm):
    i = pl.program_id(0)
    n = pl.num_programs(0)
    slot = i % 2

    def start(tile, s):
        for kk in range(TOP_K):
            _start_row_gather(pos_ref, kk * (n * tm) + tile * tm, ys_hbm, buf.at[s, kk], sem.at[s], tm)

    @pl.when(i == 0)
    def _():
        start(0, 0)

    @pl.when(i + 1 < n)
    def _():
        start(i + 1, 1 - slot)

    for kk in range(TOP_K):
        _wait_row_gather(ys_hbm, buf.at[slot, kk], sem.at[slot], tm)
    r = r_ref[...]
    d = x_ref.shape[1]
    moe = r[:, TOP_K:TOP_K + 1] * buf[slot, 0].reshape(tm, d)
    for kk in range(1, TOP_K):
        moe = moe + r[:, TOP_K + kk:TOP_K + kk + 1] * buf[slot, kk].reshape(tm, d)
    x = x_ref[...] + moe
    y = x * lax.rsqrt(jnp.mean(x * x, axis=-1, keepdims=True) + EPS)
    o_ref[...] = y * w_ref[...]


def _combine_norm(pos, x, route, ys, w, tm=256):
    m, d = x.shape
    grid_spec = pltpu.PrefetchScalarGridSpec(
        num_scalar_prefetch=1,
        grid=(m // tm,),
        in_specs=[pl.BlockSpec((tm, d), lambda i, p: (i, 0)),
                  pl.BlockSpec((tm, LANES), lambda i, p: (i, 0)),
                  pl.BlockSpec(memory_space=pl.ANY),
                  pl.BlockSpec((1, d), lambda i, p: (0, 0))],
        out_specs=pl.BlockSpec((tm, d), lambda i, p: (i, 0)),
        scratch_shapes=[pltpu.VMEM((2, TOP_K, tm // SUBLANES, SUBLANES, d), F32),
                        pltpu.SemaphoreType.DMA((2,))],
    )
    return pl.pallas_call(
        functools.partial(_combine_norm_kernel, tm=tm),
        grid_spec=grid_spec,
        out_shape=jax.ShapeDtypeStruct((m, d), F32),
        compiler_params=_cparams(("arbitrary",)),
        name="combine_norm",
    )(pos, x, route, ys.reshape(-1, SUBLANES, d), w.reshape(1, d).astype(F32))


def _moe_routing(route, n_experts):
    m = route.shape[0]
    n_sub = MOE_CHUNK // MOE_SUB
    ids = route[:, :TOP_K].astype(jnp.int32)
    flat_e = ids.reshape(-1)
    onehot = (flat_e[:, None] == jnp.arange(n_experts)[None, :]).astype(jnp.int32)
    rank = jnp.sum((jnp.cumsum(onehot, axis=0) - onehot) * onehot, axis=1)
    counts = jnp.sum(onehot, axis=0)
    subs = (counts + MOE_SUB - 1) // MOE_SUB
    n_chunks = (subs + n_sub - 1) // n_sub
    chunk_end = jnp.cumsum(n_chunks)
    chunk_start = chunk_end - n_chunks
    max_chunks = (m * TOP_K // MOE_SUB + n_experts + n_sub - 1) // n_sub + n_experts
    rows = max_chunks * MOE_CHUNK
    first_subs = jnp.where(subs % n_sub == 0, n_sub, subs % n_sub)
    first_rows = jnp.sum(onehot * first_subs[None, :], axis=1) * MOE_SUB
    rest = rank - first_rows
    chunk_off = jnp.where(rest < 0, 0, 1 + rest // MOE_CHUNK)
    row_off = jnp.where(rest < 0, rank, rest % MOE_CHUNK)
    pos = (jnp.sum(onehot * chunk_start[None, :], axis=1) + chunk_off) * MOE_CHUNK + row_off
    src_token = jnp.zeros((rows,), jnp.int32).at[pos].set(jnp.arange(m * TOP_K, dtype=jnp.int32) // TOP_K)
    n_valid = chunk_end[n_experts - 1].astype(jnp.int32)
    cidx = jnp.minimum(jnp.arange(max_chunks, dtype=jnp.int32), n_valid - 1)
    chunk_expert = jnp.sum((cidx[:, None] >= chunk_end[None, :]).astype(jnp.int32), axis=1).astype(jnp.int32)
    sel = (chunk_expert[:, None] == jnp.arange(n_experts)[None, :]).astype(jnp.int32)
    within = cidx - jnp.sum(sel * chunk_start[None, :], axis=1)
    chunk_nq = jnp.where(within == 0, jnp.sum(sel * first_subs[None, :], axis=1), n_sub)
    chunk_nq = jnp.where(jnp.arange(max_chunks) < n_valid, chunk_nq, 0).astype(jnp.int32)
    pos_kmajor = pos.reshape(m, TOP_K).T.reshape(-1).astype(jnp.int32)
    return src_token, chunk_expert, chunk_nq, n_valid.reshape(1), pos_kmajor


def kernel(x, norm_w, final_norm_w, mix_in_w, gdn_conv_w, gdn_a_log, gdn_dt_bias, gdn_norm_w, hgrn_lb_logits, hgrn_norm_w, mix_out_w, ffn_gate_up_w, ffn_down_w, rg_in_w, rg_conv_w, rg_conv_b, rg_gate_a_w, rg_gate_a_b, rg_gate_x_w, rg_gate_x_b, rg_lambda, rg_out_w, moe_router_w, moe_gate_up_w, moe_down_w):
    bsz, s, d = x.shape
    m = bsz * s
    n_heads = gdn_a_log.shape[1]
    hw = n_heads * HEAD_DIM
    n_experts = moe_router_w.shape[2]
    xr = x.reshape(m, d)

    hgrn_lb = jnp.cumsum(jax.nn.softmax(hgrn_lb_logits.astype(F32), axis=0), axis=0)
    w_in_t = jnp.swapaxes(mix_in_w[0], 0, 1)
    h = _rmsnorm(xr, norm_w[0, 0], BF16)
    proj_a = _mm_nt(h, w_in_t, F32, 1024, 1024, 0, 4 * hw).reshape(bsz, s, 4 * hw)
    proj_b = _mm_nt(h, w_in_t, F32, 1024, 1024, 4 * hw + 2 * n_heads, 4 * hw).reshape(bsz, s, 4 * hw)
    ba = _mm_nt(h, w_in_t, F32, 1024, LANES, 4 * hw, LANES).reshape(bsz, s, LANES)
    o_a = _gdn(proj_a, ba, gdn_conv_w[0].astype(F32), gdn_a_log[0], gdn_dt_bias[0], gdn_norm_w[0], col0=0)
    o_b = _hgrn(proj_b, hgrn_lb[0], hgrn_norm_w[0], col0=0, n_heads=n_heads)
    xr = _mm_resid([o_a.reshape(m, hw), o_b.reshape(m, hw)], mix_out_w[0], xr, 1024, 1024)
    h = _rmsnorm(xr, norm_w[0, 1], BF16)
    act = _mm_swiglu(h, ffn_gate_up_w[0], 1024, 512)
    xr = _mm_resid([act], ffn_down_w[0], xr, 512, 512)

    h = _rmsnorm(xr, norm_w[1, 0], BF16)
    yx = _mm(h, rg_in_w[0], F32, 1024, 1024).reshape(bsz, s, -1)
    rec = _rglru(yx, rg_conv_w[0], rg_conv_b[0], rg_gate_a_w[0], rg_gate_a_b[0],
                 rg_gate_x_w[0], rg_gate_x_b[0], rg_lambda[0])
    xr = _mm_resid([rec.reshape(m, d)], rg_out_w[0], xr, 1024, 1024)
    route = _norm_router(xr, norm_w[1, 1], moe_router_w[0])
    src_token, chunk_expert, chunk_nq, n_valid, pos = _moe_routing(route, n_experts)
    hs = _gather_norm(src_token, chunk_nq, xr, norm_w[1, 1])
    act = _gmm_swiglu(chunk_expert, chunk_nq, n_valid, hs, moe_gate_up_w[0], 512)
    ys = _gmm_down(chunk_expert, chunk_nq, n_valid, act, moe_down_w[0], 256)
    out = _combine_norm(pos, xr, route, ys, final_norm_w)
    return out.reshape(bsz, s, d)
```

```python
import functools

import jax
import jax.numpy as jnp
from jax import lax
from jax.experimental import pallas as pl
from jax.experimental.pallas import tpu as pltpu

F32 = jnp.float32
BF16 = jnp.bfloat16
HIGHEST = lax.Precision.HIGHEST

EPS = 1e-6
CHUNK = 64
CONV_W = 4
HEAD_DIM = 128
RG_BLOCK = 256
RG_C = 8.0
TOP_K = 2
LANES = 128
SUBLANES = 8
VMEM_LIMIT = 56 * 1024 * 1024
VMEM_LIMIT_MAX = 60 * 1024 * 1024
MOE_CHUNK = 2048
MOE_UNIT = 1024
MOE_SUB = 256

_NT = (((1,), (1,)), ((), ()))
_TN = (((0,), (0,)), ((), ()))


def _cparams(sem):
    return pltpu.CompilerParams(dimension_semantics=sem, vmem_limit_bytes=VMEM_LIMIT)


def _sigmoid(x):
    return jax.nn.sigmoid(x)


def _silu(x):
    return x * _sigmoid(x)


def _softplus(x):
    return jnp.maximum(x, 0.0) + jnp.log1p(jnp.exp(-jnp.abs(x)))


def _bdot(a, b):
    return jnp.dot(a.astype(BF16), b.astype(BF16), preferred_element_type=F32)


def _bdot_g(a, b, dims):
    return lax.dot_general(a.astype(BF16), b.astype(BF16), dims, preferred_element_type=F32)


def _rmsnorm_kernel(x_ref, w_ref, o_ref):
    x = x_ref[...]
    y = x * lax.rsqrt(jnp.mean(x * x, axis=-1, keepdims=True) + EPS)
    o_ref[...] = (y * w_ref[...]).astype(o_ref.dtype)


def _rmsnorm(x, w, out_dtype, tm=512):
    m, d = x.shape
    return pl.pallas_call(
        _rmsnorm_kernel,
        grid=(m // tm,),
        in_specs=[pl.BlockSpec((tm, d), lambda i: (i, 0)),
                  pl.BlockSpec((1, d), lambda i: (0, 0))],
        out_specs=pl.BlockSpec((tm, d), lambda i: (i, 0)),
        out_shape=jax.ShapeDtypeStruct((m, d), out_dtype),
        compiler_params=_cparams(("parallel",)),
        name="rmsnorm",
    )(x, w.reshape(1, d).astype(F32))


def _bf16_weight(w_ref, w_s, fresh):
    if not fresh:
        return w_s[...]
    w = w_ref[...].astype(BF16)
    w_s[...] = w
    return w


def _first_row_tile_or_not(body):
    first = pl.program_id(1) == 0
    pl.when(first)(lambda: body(True))
    pl.when(jnp.logical_not(first))(lambda: body(False))


def _mm_kernel(a_ref, w_ref, o_ref, w_s):
    def body(fresh):
        w = _bf16_weight(w_ref, w_s, fresh)
        o_ref[...] = jnp.dot(a_ref[...], w, preferred_element_type=F32).astype(o_ref.dtype)

    _first_row_tile_or_not(body)


def _mm_nt_kernel(*refs, shift):
    a_ref, w_ref = refs[0], refs[1]
    o_ref, w_s = refs[-2], refs[-1]

    def body(fresh):
        if fresh:
            w = w_ref[...]
            if shift:
                w = jnp.concatenate([w, refs[2][...]], axis=0)[shift:shift + w_ref.shape[0], :]
            w = w.astype(BF16)
            w_s[...] = w
        else:
            w = w_s[...]
        o_ref[...] = lax.dot_general(a_ref[...], w, _NT, preferred_element_type=F32).astype(o_ref.dtype)

    _first_row_tile_or_not(body)


def _mm_nt(a, wt, out_dtype, tm, tn, row0, n_rows):
    m, k = a.shape
    base, shift = row0 // tn * tn, row0 % tn
    assert shift % SUBLANES == 0 and shift <= LANES and n_rows % tn == 0
    in_specs = [pl.BlockSpec((tm, k), lambda j, i: (i, 0)),
                pl.BlockSpec((tn, k), lambda j, i: (base // tn + j, 0))]
    args = [a, wt]
    if shift:
        in_specs.append(pl.BlockSpec((LANES, k), lambda j, i: ((base + (j + 1) * tn) // LANES, 0)))
        args.append(wt)
    return pl.pallas_call(
        functools.partial(_mm_nt_kernel, shift=shift),
        grid=(n_rows // tn, m // tm),
        in_specs=in_specs,
        out_specs=pl.BlockSpec((tm, tn), lambda j, i: (i, j)),
        out_shape=jax.ShapeDtypeStruct((m, n_rows), out_dtype),
        scratch_shapes=[pltpu.VMEM((tn, k), BF16)],
        compiler_params=_cparams(("arbitrary", "arbitrary")),
        name="mm_nt",
    )(*args)


def _mm(a, w, out_dtype, tm, tn, n_cols=None, col0=0):
    m, k = a.shape
    n = w.shape[1] if n_cols is None else n_cols
    cb = col0 // tn
    return pl.pallas_call(
        _mm_kernel,
        grid=(n // tn, m // tm),
        in_specs=[pl.BlockSpec((tm, k), lambda j, i: (i, 0)),
                  pl.BlockSpec((k, tn), lambda j, i: (0, cb + j))],
        out_specs=pl.BlockSpec((tm, tn), lambda j, i: (i, j)),
        out_shape=jax.ShapeDtypeStruct((m, n), out_dtype),
        scratch_shapes=[pltpu.VMEM((k, tn), BF16)],
        compiler_params=_cparams(("arbitrary", "arbitrary")),
        name="mm",
    )(a, w)


def _mm_swiglu_kernel(a_ref, wg_ref, wu_ref, o_ref, wg_s, wu_s):
    def body(fresh):
        a = a_ref[...]
        g = jnp.dot(a, _bf16_weight(wg_ref, wg_s, fresh), preferred_element_type=F32)
        u = jnp.dot(a, _bf16_weight(wu_ref, wu_s, fresh), preferred_element_type=F32)
        o_ref[...] = (_silu(g) * u).astype(o_ref.dtype)

    _first_row_tile_or_not(body)


def _mm_swiglu(a, w_gu, tm, tn):
    m, k = a.shape
    f = w_gu.shape[1] // 2
    nb = f // tn
    return pl.pallas_call(
        _mm_swiglu_kernel,
        grid=(nb, m // tm),
        in_specs=[pl.BlockSpec((tm, k), lambda j, i: (i, 0)),
                  pl.BlockSpec((k, tn), lambda j, i: (0, j)),
                  pl.BlockSpec((k, tn), lambda j, i: (0, j + nb))],
        out_specs=pl.BlockSpec((tm, tn), lambda j, i: (i, j)),
        out_shape=jax.ShapeDtypeStruct((m, f), BF16),
        scratch_shapes=[pltpu.VMEM((k, tn), BF16)] * 2,
        compiler_params=_cparams(("arbitrary", "arbitrary")),
        name="mm_swiglu",
    )(a, w_gu, w_gu)


def _mm_resid_kernel(*refs, n_pairs):
    r_ref, o_ref = refs[2 * n_pairs], refs[2 * n_pairs + 1]
    w_s = refs[2 * n_pairs + 2:]

    def body(fresh):
        acc = r_ref[...]
        for p in range(n_pairs):
            w = _bf16_weight(refs[2 * p + 1], w_s[p], fresh)
            acc = acc + jnp.dot(refs[2 * p][...], w, preferred_element_type=F32)
        o_ref[...] = acc

    _first_row_tile_or_not(body)


def _mm_resid(a_list, w, resid, tm, tn):
    m, n = resid.shape
    in_specs, args, scratch = [], [], []
    for p, a in enumerate(a_list):
        k = a.shape[1]
        in_specs += [pl.BlockSpec((tm, k), lambda j, i: (i, 0)),
                     pl.BlockSpec((k, tn), lambda j, i, p=p: (p, j))]
        args += [a, w]
        scratch.append(pltpu.VMEM((k, tn), BF16))
    in_specs.append(pl.BlockSpec((tm, tn), lambda j, i: (i, j)))
    args.append(resid)
    return pl.pallas_call(
        functools.partial(_mm_resid_kernel, n_pairs=len(a_list)),
        grid=(n // tn, m // tm),
        in_specs=in_specs,
        out_specs=pl.BlockSpec((tm, tn), lambda j, i: (i, j)),
        out_shape=jax.ShapeDtypeStruct((m, n), F32),
        scratch_shapes=scratch,
        compiler_params=_cparams(("arbitrary", "arbitrary")),
        name="mm_resid",
    )(*args)


def _mm_resid_rows_kernel(a_ref, w_ref, r_ref, o_ref):
    w = w_ref[...].astype(BF16)
    o_ref[...] = r_ref[...] + jnp.dot(a_ref[...], w, preferred_element_type=F32)


def _mm_resid_rows(a, w, resid, tm, tn):
    m, k = a.shape
    n = resid.shape[1]
    return pl.pallas_call(
        _mm_resid_rows_kernel,
        grid=(m // tm, n // tn),
        in_specs=[pl.BlockSpec((tm, k), lambda i, j: (i, 0)),
                  pl.BlockSpec((k, tn), lambda i, j: (0, j)),
                  pl.BlockSpec((tm, tn), lambda i, j: (i, j))],
        out_specs=pl.BlockSpec((tm, tn), lambda i, j: (i, j)),
        out_shape=jax.ShapeDtypeStruct((m, n), F32),
        compiler_params=_cparams(("parallel", "parallel")),
        name="mm_resid_rows",
    )(a, w, resid)


def _causal_conv(buf_ref, x, cw, t_len):
    buf_ref[pl.ds(SUBLANES, t_len), :] = x
    acc = cw[CONV_W - 1:CONV_W, :] * x
    for j in range(CONV_W - 1):
        off = SUBLANES - (CONV_W - 1) + j
        acc = acc + cw[j:j + 1, :] * buf_ref[pl.ds(off, t_len), :]
    buf_ref[pl.ds(0, SUBLANES), :] = x[t_len - SUBLANES:, :]
    return acc


def _tri_masks():
    row = lax.broadcasted_iota(jnp.int32, (CHUNK, CHUNK), 0)
    col = lax.broadcasted_iota(jnp.int32, (CHUNK, CHUNK), 1)
    return row >= col, row > col, row == col


def _gated_head_norm(o, nw, z):
    o = o * lax.rsqrt(jnp.mean(o * o, axis=-1, keepdims=True) + EPS) * nw
    return o * _silu(z)


def _gdn_kernel(q_ref, k_ref, v_ref, z_ref, ba_ref, cwq_ref, cwk_ref, cwv_ref, alog_ref, dtb_ref,
                nw_ref, o_ref, qbuf, kbuf, vbuf, qs, ks, vs, gcs, gct, bs, u_s, w_s, qk_s, qd_s, kd_s, s_ref,
                *, n_heads, t_len):
    t = pl.program_id(1)
    n_chunks = t_len // CHUNK
    width = n_heads * HEAD_DIM

    @pl.when(t == 0)
    def _():
        zeros = jnp.zeros((SUBLANES, width), F32)
        qbuf[pl.ds(0, SUBLANES), :] = zeros
        kbuf[pl.ds(0, SUBLANES), :] = zeros
        vbuf[pl.ds(0, SUBLANES), :] = zeros
        s_ref[...] = jnp.zeros_like(s_ref)

    qc = _silu(_causal_conv(qbuf, q_ref[0], cwq_ref[...], t_len))
    kc = _silu(_causal_conv(kbuf, k_ref[0], cwk_ref[...], t_len))
    vs[...] = _silu(_causal_conv(vbuf, v_ref[0], cwv_ref[...], t_len))
    for h in range(n_heads):
        sl = slice(h * HEAD_DIM, (h + 1) * HEAD_DIM)
        qh = qc[:, sl]
        kh = kc[:, sl]
        qs[:, sl] = qh * (lax.rsqrt(jnp.sum(qh * qh, axis=-1, keepdims=True) + EPS) * (HEAD_DIM ** -0.5))
        ks[:, sl] = kh * lax.rsqrt(jnp.sum(kh * kh, axis=-1, keepdims=True) + EPS)

    ba = ba_ref[0]
    g_log = -jnp.exp(alog_ref[...]) * _softplus(ba + dtb_ref[...])
    bs[...] = _sigmoid(ba)
    row = lax.broadcasted_iota(jnp.int32, (t_len, t_len), 0)
    col = lax.broadcasted_iota(jnp.int32, (t_len, t_len), 1)
    block_tri = ((row // CHUNK == col // CHUNK) & (row >= col)).astype(F32)
    gc_blk = jnp.dot(block_tri, g_log, precision=HIGHEST, preferred_element_type=F32)
    gcs[...] = gc_blk
    for c in range(n_chunks):
        gct[c] = gc_blk[c * CHUNK:(c + 1) * CHUNK, :].T

    incl, strict, eye = _tri_masks()
    eye_f = eye.astype(F32)
    nw = nw_ref[...]

    heads = range(n_heads)
    hsl = [slice(h * HEAD_DIM, (h + 1) * HEAD_DIM) for h in heads]
    qsl = [slice(h * HEAD_DIM, h * HEAD_DIM + CHUNK) for h in heads]
    chunks_per_iter = 2
    assert n_chunks % chunks_per_iter == 0

    def pass_a(it, carry):
        items = []
        for cc in range(chunks_per_iter):
            c = it * chunks_per_iter + cc
            rows = pl.ds(pl.multiple_of(c * CHUNK, CHUNK), CHUNK)
            gc_all = gcs[rows, :]
            beta_all = bs[rows, :]
            gct_c = gct[c]
            for h in heads:
                gc = gc_all[:, n_heads + h:n_heads + h + 1]
                beta = beta_all[:, h:h + 1]
                gc_row = gct_c[n_heads + h:n_heads + h + 1, :]
                decay = jnp.where(incl, jnp.exp(jnp.where(incl, gc - gc_row, 0.0)), 0.0)
                items.append(dict(rows=rows, h=h, gc=gc, beta=beta, decay=decay,
                                  q=qs[rows, hsl[h]], k=ks[rows, hsl[h]]))
        for it_ in items:
            it_["kb"] = it_["k"].astype(BF16)
            it_["kk"] = lax.dot_general(it_["kb"], it_["kb"], _NT, preferred_element_type=F32)
        for it_ in items:
            it_["x"] = jnp.where(strict, -(it_["beta"] * it_["kk"] * it_["decay"]), 0.0)
            it_["p"] = eye_f + it_["x"]
        for _ in range(5):
            for it_ in items:
                xb = it_["x"].astype(BF16)
                it_["x"] = jnp.dot(xb, xb, preferred_element_type=F32)
            for it_ in items:
                it_["p"] = it_["p"] + _bdot(it_["p"], it_["x"])
        for it_ in items:
            egc = jnp.exp(it_["gc"])
            it_["egc"] = egc
            v = vs[it_["rows"], hsl[it_["h"]]]
            rhs = jnp.concatenate([v * it_["beta"], it_["k"] * (it_["beta"] * egc)], axis=-1)
            it_["sol"] = _bdot(it_["p"], rhs)
        for it_ in items:
            it_["qk"] = lax.dot_general(it_["q"].astype(BF16), it_["kb"], _NT,
                                        preferred_element_type=F32) * it_["decay"]
        for it_ in items:
            rows, h, gc = it_["rows"], it_["h"], it_["gc"]
            g_last = gc[CHUNK - 1:CHUNK, :]
            u_s[rows, hsl[h]] = it_["sol"][:, :HEAD_DIM]
            w_s[rows, hsl[h]] = it_["sol"][:, HEAD_DIM:].astype(BF16)
            qk_s[rows, qsl[h]] = it_["qk"].astype(BF16)
            qd_s[rows, hsl[h]] = (it_["q"] * it_["egc"]).astype(BF16)
            kd_s[rows, hsl[h]] = (it_["k"] * jnp.exp(g_last - gc)).astype(BF16)
        return carry

    lax.fori_loop(0, n_chunks // chunks_per_iter, pass_a, 0)

    def pass_b(c, carry):
        rows = pl.ds(pl.multiple_of(c * CHUNK, CHUNK), CHUNK)
        gc_all = gcs[rows, :]
        states = [s_ref[h] for h in heads]
        sbs = [s.astype(BF16) for s in states]
        v_new = [u_s[rows, hsl[h]] - jnp.dot(w_s[rows, hsl[h]], sbs[h], preferred_element_type=F32)
                 for h in heads]
        vbs = [v.astype(BF16) for v in v_new]
        for h in heads:
            g_last = gc_all[CHUNK - 1:CHUNK, n_heads + h:n_heads + h + 1]
            s_ref[h] = states[h] * jnp.exp(g_last) + lax.dot_general(kd_s[rows, hsl[h]], vbs[h], _TN,
                                                                      preferred_element_type=F32)
        outs = [jnp.dot(qd_s[rows, hsl[h]], sbs[h], preferred_element_type=F32)
                + jnp.dot(qk_s[rows, qsl[h]], vbs[h], preferred_element_type=F32) for h in heads]
        for h in heads:
            o = _gated_head_norm(outs[h], nw, z_ref[0, rows, hsl[h]])
            o_ref[0, rows, hsl[h]] = o.astype(o_ref.dtype)
        return carry

    lax.fori_loop(0, n_chunks, pass_b, 0)


def _gdn(proj, ba, conv_w, a_log, dt_bias, norm_w, *, col0, t_len=256):
    bsz, s, _ = proj.shape
    n_heads = a_log.shape[0]
    width = n_heads * HEAD_DIM
    cb = col0 // width

    def col_spec(group):
        return pl.BlockSpec((1, t_len, width), lambda b, t: (b, t, cb + group))

    def cw_spec(group):
        return pl.BlockSpec((CONV_W, width), lambda b, t: (0, group))

    row = jnp.zeros((1, LANES), F32)
    alog_row = row.at[0, n_heads:2 * n_heads].set(a_log.astype(F32))
    dtb_row = row.at[0, n_heads:2 * n_heads].set(dt_bias.astype(F32))
    small = pl.BlockSpec((1, LANES), lambda b, t: (0, 0))
    kern = functools.partial(_gdn_kernel, n_heads=n_heads, t_len=t_len)
    return pl.pallas_call(
        kern,
        grid=(bsz, s // t_len),
        in_specs=[col_spec(0), col_spec(1), col_spec(2), col_spec(3),
                  pl.BlockSpec((1, t_len, LANES), lambda b, t: (b, t, 0)),
                  cw_spec(0), cw_spec(1), cw_spec(2), small, small,
                  pl.BlockSpec((1, HEAD_DIM), lambda b, t: (0, 0))],
        out_specs=pl.BlockSpec((1, t_len, width), lambda b, t: (b, t, 0)),
        out_shape=jax.ShapeDtypeStruct((bsz, s, width), BF16),
        scratch_shapes=[pltpu.VMEM((SUBLANES + t_len, width), F32)] * 3
        + [pltpu.VMEM((t_len, width), F32)] * 3
        + [pltpu.VMEM((t_len, LANES), F32),
           pltpu.VMEM((t_len // CHUNK, LANES, CHUNK), F32),
           pltpu.VMEM((t_len, LANES), F32),
           pltpu.VMEM((t_len, width), F32)]
        + [pltpu.VMEM((t_len, width), BF16)] * 4
        + [pltpu.VMEM((n_heads, HEAD_DIM, HEAD_DIM), F32)],
        compiler_params=_cparams(("parallel", "arbitrary")),
        name="gdn",
    )(proj, proj, proj, proj, ba, conv_w, conv_w, conv_w, alog_row, dtb_row,
      norm_w.reshape(1, HEAD_DIM).astype(F32))


def _hgrn_kernel(q_ref, f_ref, i_ref, g_ref, lb_ref, nw_ref, o_ref, qs, ks, ls, st_ref, *, heads, t_len):
    t = pl.program_id(1)

    @pl.when(t == 0)
    def _():
        st_ref[...] = jnp.zeros_like(st_ref)

    lb = lb_ref[...]
    forget = lb + (1.0 - lb) * _sigmoid(f_ref[0])
    ks[...] = 1.0 - forget
    qs[...] = _silu(q_ref[0]) * (HEAD_DIM ** -0.5)
    row = lax.broadcasted_iota(jnp.int32, (t_len, t_len), 0)
    col = lax.broadcasted_iota(jnp.int32, (t_len, t_len), 1)
    block_tri = ((row // CHUNK == col // CHUNK) & (row >= col)).astype(F32)
    ls[...] = jnp.dot(block_tri, jnp.log(forget), precision=HIGHEST, preferred_element_type=F32)

    incl, _, _ = _tri_masks()
    nw = nw_ref[...]
    mid = CHUNK // 2 - 1

    hsl = [slice(j * HEAD_DIM, (j + 1) * HEAD_DIM) for j in range(heads)]

    def chunk_body(c, carry):
        rows = pl.ds(pl.multiple_of(c * CHUNK, CHUNK), CHUNK)
        items = []
        for j in range(heads):
            b = ls[rows, hsl[j]]
            items.append(dict(b=b, b_mid=b[mid:mid + 1, :], b_last=b[CHUNK - 1:CHUNK, :],
                              q=qs[rows, hsl[j]], k=ks[rows, hsl[j]],
                              vb=i_ref[0, rows, hsl[j]].astype(BF16)))
        for it in items:
            it["att"] = jnp.where(incl, _bdot_g(it["q"] * jnp.exp(it["b"] - it["b_mid"]),
                                                it["k"] * jnp.exp(it["b_mid"] - it["b"]), _NT), 0.0)
        for j, it in enumerate(items):
            st = st_ref[j]
            it["o"] = _bdot_g(it["q"] * jnp.exp(it["b"]), st, _NT)
            st_ref[j] = st * jnp.exp(it["b_last"]) + lax.dot_general(
                it["vb"], (it["k"] * jnp.exp(it["b_last"] - it["b"])).astype(BF16), _TN,
                preferred_element_type=F32)
        for it in items:
            it["o"] = it["o"] + jnp.dot(it["att"].astype(BF16), it["vb"], preferred_element_type=F32)
        for j, it in enumerate(items):
            o = _gated_head_norm(it["o"], nw, g_ref[0, rows, hsl[j]])
            o_ref[0, rows, hsl[j]] = o.astype(o_ref.dtype)
        return carry

    lax.fori_loop(0, t_len // CHUNK, chunk_body, 0)


def _hgrn(proj, lb, norm_w, *, col0, n_heads, t_len=256):
    bsz, s, _ = proj.shape
    width = n_heads * HEAD_DIM
    cb = col0 // width

    def col_spec(group):
        return pl.BlockSpec((1, t_len, width), lambda b, t: (b, t, cb + group))

    kern = functools.partial(_hgrn_kernel, heads=n_heads, t_len=t_len)
    return pl.pallas_call(
        kern,
        grid=(bsz, s // t_len),
        in_specs=[col_spec(0), col_spec(1), col_spec(2), col_spec(3),
                  pl.BlockSpec((1, width), lambda b, t: (0, 0)),
                  pl.BlockSpec((1, HEAD_DIM), lambda b, t: (0, 0))],
        out_specs=pl.BlockSpec((1, t_len, width), lambda b, t: (b, t, 0)),
        out_shape=jax.ShapeDtypeStruct((bsz, s, width), BF16),
        scratch_shapes=[pltpu.VMEM((t_len, width), F32)] * 3
        + [pltpu.VMEM((n_heads, HEAD_DIM, HEAD_DIM), F32)],
        compiler_params=_cparams(("parallel", "arbitrary")),
        name="hgrn2",
    )(proj, proj, proj, proj, lb.reshape(1, width).astype(F32), norm_w.reshape(1, HEAD_DIM).astype(F32))


def _rglru_kernel(y_ref, x_ref, cw_ref, cb_ref, wa_ref, ba_ref, wx_ref, bx_ref, lam_ref, o_ref,
                  xbuf, a_s, b_s, h_s, hcar, *, blocks, t_len):
    t = pl.program_id(2)

    @pl.when(t == 0)
    def _():
        xbuf[pl.ds(0, SUBLANES), :] = jnp.zeros((SUBLANES, blocks * RG_BLOCK), F32)
        hcar[...] = jnp.zeros_like(hcar)

    xc = _causal_conv(xbuf, x_ref[0], cw_ref[...], t_len) + cb_ref[...]
    for n in range(blocks):
        sl = slice(n * RG_BLOCK, (n + 1) * RG_BLOCK)
        xb = xc[:, sl]
        r = _sigmoid(_bdot(xb, wa_ref[n]) + ba_ref[:, sl])
        gi = _sigmoid(_bdot(xb, wx_ref[n]) + bx_ref[:, sl])
        log_a = (-RG_C) * r * _softplus(-lam_ref[:, sl])
        a = jnp.exp(log_a)
        a_s[:, sl] = a
        one_minus_a2 = -jnp.tanh(log_a) * (a * a + 1.0)
        b_s[:, sl] = jnp.sqrt(jnp.maximum(one_minus_a2, 0.0)) * (gi * xb)

    def row_body(i, h):
        h = a_s[pl.ds(i, 1), :] * h + b_s[pl.ds(i, 1), :]
        h_s[pl.ds(i, 1), :] = h
        return h

    hcar[...] = lax.fori_loop(0, t_len, row_body, hcar[...])
    o_ref[0] = (jax.nn.gelu(y_ref[0], approximate=True) * h_s[...]).astype(o_ref.dtype)


def _rglru(yx, conv_w, conv_b, wa, ba, wx, bx, lam, *, blocks_per_step=8, t_len=256):
    bsz, s, w2 = yx.shape
    width = w2 // 2
    gw = blocks_per_step * RG_BLOCK
    n_g = width // gw
    vec = lambda a: a.reshape(1, width).astype(F32)
    vspec = pl.BlockSpec((1, gw), lambda b, g, t: (0, g))
    wspec = pl.BlockSpec((blocks_per_step, RG_BLOCK, RG_BLOCK), lambda b, g, t: (g, 0, 0))
    kern = functools.partial(_rglru_kernel, blocks=blocks_per_step, t_len=t_len)
    return pl.pallas_call(
        kern,
        grid=(bsz, n_g, s // t_len),
        in_specs=[pl.BlockSpec((1, t_len, gw), lambda b, g, t: (b, t, g)),
                  pl.BlockSpec((1, t_len, gw), lambda b, g, t: (b, t, n_g + g)),
                  pl.BlockSpec((CONV_W, gw), lambda b, g, t: (0, g)),
                  vspec, wspec, vspec, wspec, vspec, vspec],
        out_specs=pl.BlockSpec((1, t_len, gw), lambda b, g, t: (b, t, g)),
        out_shape=jax.ShapeDtypeStruct((bsz, s, width), BF16),
        scratch_shapes=[pltpu.VMEM((SUBLANES + t_len, gw), F32)]
        + [pltpu.VMEM((t_len, gw), F32)] * 3
        + [pltpu.VMEM((1, gw), F32)],
        compiler_params=_cparams(("parallel", "parallel", "arbitrary")),
        name="rglru",
    )(yx, yx, conv_w.astype(F32), vec(conv_b), wa.astype(BF16), vec(ba), wx.astype(BF16), vec(bx), vec(lam))


def _norm_router_kernel(x_ref, nw_ref, w_ref, o_ref, *, n_experts):
    x = x_ref[...]
    y = x * lax.rsqrt(jnp.mean(x * x, axis=-1, keepdims=True) + EPS) * nw_ref[...]
    w = w_ref[...]
    y_hi = y.astype(BF16)
    y_lo = (y - y_hi.astype(F32)).astype(BF16)
    w_hi = w.astype(BF16)
    w_lo = (w - w_hi.astype(F32)).astype(BF16)
    logits = (jnp.dot(y_hi, w_hi, preferred_element_type=F32)
              + (jnp.dot(y_lo, w_hi, preferred_element_type=F32)
                 + jnp.dot(y_hi, w_lo, preferred_element_type=F32)))
    lane = lax.broadcasted_iota(jnp.int32, logits.shape, 1).astype(F32)
    neg = jnp.float32(-jnp.inf)
    l1 = jnp.where(lane < n_experts, logits, neg)
    m1 = jnp.max(l1, axis=-1, keepdims=True)
    i1 = jnp.min(jnp.where(l1 == m1, lane, float(LANES)), axis=-1, keepdims=True)
    l2 = jnp.where(lane == i1, neg, l1)
    m2 = jnp.max(l2, axis=-1, keepdims=True)
    i2 = jnp.min(jnp.where(l2 == m2, lane, float(LANES)), axis=-1, keepdims=True)
    e2 = jnp.exp(m2 - m1)
    g1 = 1.0 / (1.0 + e2)
    g2 = e2 / (1.0 + e2)
    out = jnp.where(lane == 0, i1, 0.0)
    out = jnp.where(lane == 1, i2, out)
    out = jnp.where(lane == 2, g1, out)
    out = jnp.where(lane == 3, g2, out)
    o_ref[...] = out


def _norm_router(x, nw, router_w, tm=512):
    m, d = x.shape
    n_experts = router_w.shape[1]
    w = jnp.zeros((d, LANES), F32).at[:, :n_experts].set(router_w.astype(F32))
    return pl.pallas_call(
        functools.partial(_norm_router_kernel, n_experts=n_experts),
        grid=(m // tm,),
        in_specs=[pl.BlockSpec((tm, d), lambda i: (i, 0)),
                  pl.BlockSpec((1, d), lambda i: (0, 0)),
                  pl.BlockSpec((d, LANES), lambda i: (0, 0))],
        out_specs=pl.BlockSpec((tm, LANES), lambda i: (i, 0)),
        out_shape=jax.ShapeDtypeStruct((m, LANES), F32),
        compiler_params=_cparams(("parallel",)),
        name="norm_router",
    )(x, nw.reshape(1, d).astype(F32), w)


def _start_row_gather(idx_ref, base, src3, dst3, sem, n_rows):
    def body(g, carry):
        for u in range(SUBLANES):
            row = idx_ref[base + g * SUBLANES + u]
            src = src3.at[lax.shift_right_logical(row, 3), pl.ds(row & (SUBLANES - 1), 1), :]
            pltpu.make_async_copy(src, dst3.at[g, pl.ds(u, 1), :], sem).start(priority=u % 2)
        return carry
    lax.fori_loop(0, n_rows // SUBLANES, body, 0)


def _wait_row_gather(src3, dst3, sem, n_rows):
    n = n_rows // SUBLANES
    pltpu.make_async_copy(src3.at[pl.ds(0, n)], dst3.at[pl.ds(0, n)], sem).wait()


def _gather_norm_kernel(idx_ref, nq_ref, x_hbm, nw_ref, o_ref, buf, sem):
    c = pl.program_id(0)
    slot = c % 2

    def start(chunk, s):
        _start_row_gather(idx_ref, chunk * MOE_CHUNK, x_hbm, buf.at[s], sem.at[s], nq_ref[chunk] * MOE_SUB)

    @pl.when(c == 0)
    def _():
        start(0, 0)

    @pl.when(c + 1 < pl.num_programs(0))
    def _():
        start(c + 1, 1 - slot)

    for sb in range(MOE_CHUNK // MOE_SUB):
        @pl.when(sb < nq_ref[c])
        def _():
            _wait_row_gather(x_hbm, buf.at[slot], sem.at[slot], MOE_SUB)

    for sb in range(MOE_CHUNK // MOE_SUB):
        rows = pl.ds(sb * MOE_SUB, MOE_SUB)

        @pl.when(sb < nq_ref[c])
        def _(sb=sb, rows=rows):
            tiles = pl.ds(sb * (MOE_SUB // SUBLANES), MOE_SUB // SUBLANES)
            x = buf[slot, tiles].reshape(MOE_SUB, o_ref.shape[1])
            y = x * lax.rsqrt(jnp.mean(x * x, axis=-1, keepdims=True) + EPS) * nw_ref[...]
            o_ref[rows, :] = y.astype(o_ref.dtype)

        @pl.when(sb >= nq_ref[c])
        def _():
            o_ref[rows, :] = jnp.zeros((MOE_SUB, o_ref.shape[1]), o_ref.dtype)


def _gather_norm(src_token, chunk_nq, x, nw):
    rows = src_token.shape[0]
    d = x.shape[1]
    grid_spec = pltpu.PrefetchScalarGridSpec(
        num_scalar_prefetch=2,
        grid=(rows // MOE_CHUNK,),
        in_specs=[pl.BlockSpec(memory_space=pl.ANY),
                  pl.BlockSpec((1, d), lambda c, idx, nq: (0, 0))],
        out_specs=pl.BlockSpec((MOE_CHUNK, d), lambda c, idx, nq: (c, 0)),
        scratch_shapes=[pltpu.VMEM((2, MOE_CHUNK // SUBLANES, SUBLANES, d), F32),
                        pltpu.SemaphoreType.DMA((2,))],
    )
    return pl.pallas_call(
        _gather_norm_kernel,
        grid_spec=grid_spec,
        out_shape=jax.ShapeDtypeStruct((rows, d), BF16),
        compiler_params=_cparams(("arbitrary",)),
        name="gather_norm",
    )(src_token, chunk_nq, x.reshape(-1, SUBLANES, d), nw.reshape(1, d).astype(F32))


def _expert_changed(te_ref, i):
    return jnp.logical_or(i == 0, te_ref[i] != te_ref[jnp.maximum(i - 1, 0)])


def _for_valid_rows(nq, o_ref, fn, pred=True):
    total = o_ref.shape[0]
    for v in range(total // MOE_SUB + 1):
        @pl.when(jnp.logical_and(nq == v, pred))
        def _(v=v):
            rows = v * MOE_SUB
            if rows:
                fn(rows)
            if rows < total:
                o_ref[pl.ds(rows, total - rows), :] = jnp.zeros((total - rows, o_ref.shape[1]), o_ref.dtype)


def _gmm_swiglu_kernel(te_ref, nq_ref, nv_ref, a_ref, wg_ref, wu_ref, o_ref, wg_s, wu_s):
    c = pl.program_id(1)
    nq = nq_ref[c]
    changed = _expert_changed(te_ref, c)

    def compute(rows, fresh):
        a = a_ref[pl.ds(0, rows), :]
        if fresh:
            wg = wg_ref[0].astype(BF16)
            wg_s[...] = wg
            g = jnp.dot(a, wg, preferred_element_type=F32)
            wu = wu_ref[0].astype(BF16)
            wu_s[...] = wu
            u = jnp.dot(a, wu, preferred_element_type=F32)
        else:
            g = jnp.dot(a, wg_s[...], preferred_element_type=F32)
            u = jnp.dot(a, wu_s[...], preferred_element_type=F32)
        o_ref[pl.ds(0, rows), :] = (_silu(g) * u).astype(o_ref.dtype)

    _for_valid_rows(nq, o_ref, lambda rows: compute(rows, False), jnp.logical_not(changed))
    _for_valid_rows(nq, o_ref, lambda rows: compute(rows, True), changed)


def _gmm_swiglu(chunk_expert, chunk_nq, n_valid, a, w_gu, tn):
    m, k = a.shape
    f = w_gu.shape[2] // 2
    nb = f // tn
    last = lambda c, nv: jnp.minimum(c, nv[0] - 1)
    grid_spec = pltpu.PrefetchScalarGridSpec(
        num_scalar_prefetch=3,
        grid=(nb, m // MOE_CHUNK),
        in_specs=[pl.BlockSpec((MOE_CHUNK, k), lambda j, c, te, nq, nv: (last(c, nv), 0)),
                  pl.BlockSpec((1, k, tn), lambda j, c, te, nq, nv: (te[c], 0, j)),
                  pl.BlockSpec((1, k, tn), lambda j, c, te, nq, nv: (te[c], 0, j + nb))],
        out_specs=pl.BlockSpec((MOE_CHUNK, tn), lambda j, c, te, nq, nv: (c, j)),
        scratch_shapes=[pltpu.VMEM((k, tn), BF16)] * 2,
    )
    return pl.pallas_call(
        _gmm_swiglu_kernel,
        grid_spec=grid_spec,
        out_shape=jax.ShapeDtypeStruct((m, f), BF16),
        compiler_params=pltpu.CompilerParams(dimension_semantics=("arbitrary", "arbitrary"),
                                             vmem_limit_bytes=VMEM_LIMIT_MAX),
        name="gmm_swiglu",
    )(chunk_expert, chunk_nq, n_valid, a, w_gu, w_gu)


def _gmm_down_kernel(te_ref, nq_ref, blk_ref, a_ref, w_ref, o_ref):
    u = pl.program_id(0)

    def compute(rows):
        w = w_ref[0].astype(BF16)
        o_ref[pl.ds(0, rows), :] = jnp.dot(a_ref[pl.ds(0, rows), :], w, preferred_element_type=F32)

    _for_valid_rows(nq_ref[u], o_ref, compute)


def _gmm_down(unit_expert, unit_nq, unit_blk, a, w_d, tn):
    m, k = a.shape
    n = w_d.shape[2]
    nj = n // tn
    col = lambda j, u, nq: jnp.where(nq[u] > 0, j, nj - 1)
    grid_spec = pltpu.PrefetchScalarGridSpec(
        num_scalar_prefetch=3,
        grid=(m // MOE_UNIT, nj),
        in_specs=[pl.BlockSpec((MOE_UNIT, k), lambda u, j, te, nq, blk: (blk[u], 0)),
                  pl.BlockSpec((1, k, tn), lambda u, j, te, nq, blk: (te[u], 0, col(j, u, nq)))],
        out_specs=pl.BlockSpec((MOE_UNIT, tn), lambda u, j, te, nq, blk: (u, j)),
    )
    return pl.pallas_call(
        _gmm_down_kernel,
        grid_spec=grid_spec,
        out_shape=jax.ShapeDtypeStruct((m, n), F32),
        compiler_params=pltpu.CompilerParams(dimension_semantics=("arbitrary", "arbitrary"),
                                             vmem_limit_bytes=VMEM_LIMIT_MAX),
        name="gmm_down",
    )(unit_expert, unit_nq, unit_blk, a, w_d)


def _combine_norm_kernel(pos_ref, x_ref, r_ref, ys_hbm, w_ref, o_ref, buf, sem, *, tm):
    i = pl.program_id(0)
    n = pl.num_programs(0)
    slot = i % 2

    def start(tile, s):
        for kk in range(TOP_K):
            _start_row_gather(pos_ref, kk * (n * tm) + tile * tm, ys_hbm, buf.at[s, kk], sem.at[s], tm)

    @pl.when(i == 0)
    def _():
        start(0, 0)

    @pl.when(i + 1 < n)
    def _():
        start(i + 1, 1 - slot)

    for kk in range(TOP_K):
        _wait_row_gather(ys_hbm, buf.at[slot, kk], sem.at[slot], tm)
    r = r_ref[...]
    d = x_ref.shape[1]
    moe = r[:, TOP_K:TOP_K + 1] * buf[slot, 0].reshape(tm, d)
    for kk in range(1, TOP_K):
        moe = moe + r[:, TOP_K + kk:TOP_K + kk + 1] * buf[slot, kk].reshape(tm, d)
    x = x_ref[...] + moe
    y = x * lax.rsqrt(jnp.mean(x * x, axis=-1, keepdims=True) + EPS)
    o_ref[...] = y * w_ref[...]


def _combine_norm(pos, x, route, ys, w, tm=256):
    m, d = x.shape
    grid_spec = pltpu.PrefetchScalarGridSpec(
        num_scalar_prefetch=1,
        grid=(m // tm,),
        in_specs=[pl.BlockSpec((tm, d), lambda i, p: (i, 0)),
                  pl.BlockSpec((tm, LANES), lambda i, p: (i, 0)),
                  pl.BlockSpec(memory_space=pl.ANY),
                  pl.BlockSpec((1, d), lambda i, p: (0, 0))],
        out_specs=pl.BlockSpec((tm, d), lambda i, p: (i, 0)),
        scratch_shapes=[pltpu.VMEM((2, TOP_K, tm // SUBLANES, SUBLANES, d), F32),
                        pltpu.SemaphoreType.DMA((2,))],
    )
    return pl.pallas_call(
        functools.partial(_combine_norm_kernel, tm=tm),
        grid_spec=grid_spec,
        out_shape=jax.ShapeDtypeStruct((m, d), F32),
        compiler_params=_cparams(("arbitrary",)),
        name="combine_norm",
    )(pos, x, route, ys.reshape(-1, SUBLANES, d), w.reshape(1, d).astype(F32))


def _moe_routing(route, n_experts):
    m = route.shape[0]
    n_sub = MOE_CHUNK // MOE_SUB
    ids = route[:, :TOP_K].astype(jnp.int32)
    flat_e = ids.reshape(-1)
    onehot = (flat_e[:, None] == jnp.arange(n_experts)[None, :]).astype(jnp.int32)
    rank = jnp.sum((jnp.cumsum(onehot, axis=0) - onehot) * onehot, axis=1)
    counts = jnp.sum(onehot, axis=0)
    subs = (counts + MOE_SUB - 1) // MOE_SUB
    n_chunks = (subs + n_sub - 1) // n_sub
    chunk_end = jnp.cumsum(n_chunks)
    chunk_start = chunk_end - n_chunks
    max_chunks = (m * TOP_K // MOE_SUB + n_experts + n_sub - 1) // n_sub + n_experts
    rows = max_chunks * MOE_CHUNK
    first_subs = jnp.where(subs % n_sub == 0, n_sub, subs % n_sub)
    first_rows = jnp.sum(onehot * first_subs[None, :], axis=1) * MOE_SUB
    rest = rank - first_rows
    chunk_off = jnp.where(rest < 0, 0, 1 + rest // MOE_CHUNK)
    row_off = jnp.where(rest < 0, rank, rest % MOE_CHUNK)
    pos = (jnp.sum(onehot * chunk_start[None, :], axis=1) + chunk_off) * MOE_CHUNK + row_off
    src_token = jnp.zeros((rows,), jnp.int32).at[pos].set(jnp.arange(m * TOP_K, dtype=jnp.int32) // TOP_K)
    n_valid = chunk_end[n_experts - 1].astype(jnp.int32)
    cidx = jnp.minimum(jnp.arange(max_chunks, dtype=jnp.int32), n_valid - 1)
    chunk_expert = jnp.sum((cidx[:, None] >= chunk_end[None, :]).astype(jnp.int32), axis=1).astype(jnp.int32)
    sel = (chunk_expert[:, None] == jnp.arange(n_experts)[None, :]).astype(jnp.int32)
    within = cidx - jnp.sum(sel * chunk_start[None, :], axis=1)
    chunk_nq = jnp.where(within == 0, jnp.sum(sel * first_subs[None, :], axis=1), n_sub)
    chunk_nq = jnp.where(jnp.arange(max_chunks) < n_valid, chunk_nq, 0).astype(jnp.int32)
    pos_kmajor = pos.reshape(m, TOP_K).T.reshape(-1).astype(jnp.int32)
    per = MOE_CHUNK // MOE_UNIT
    unit_sub = MOE_UNIT // MOE_SUB
    half = jnp.arange(max_chunks * per, dtype=jnp.int32) % per
    unit_nq = jnp.clip(jnp.repeat(chunk_nq, per) - unit_sub * half, 0, unit_sub).astype(jnp.int32)
    unit_expert = jnp.repeat(chunk_expert, per)
    uidx = jnp.arange(max_chunks * per, dtype=jnp.int32)
    unit_blk = lax.cummax(jnp.where(unit_nq > 0, uidx, 0), axis=0).astype(jnp.int32)
    return (src_token, chunk_expert, chunk_nq, n_valid.reshape(1), pos_kmajor,
            unit_expert, unit_nq, unit_blk)


def kernel(x, norm_w, final_norm_w, mix_in_w, gdn_conv_w, gdn_a_log, gdn_dt_bias, gdn_norm_w, hgrn_lb_logits, hgrn_norm_w, mix_out_w, ffn_gate_up_w, ffn_down_w, rg_in_w, rg_conv_w, rg_conv_b, rg_gate_a_w, rg_gate_a_b, rg_gate_x_w, rg_gate_x_b, rg_lambda, rg_out_w, moe_router_w, moe_gate_up_w, moe_down_w):
    bsz, s, d = x.shape
    m = bsz * s
    n_heads = gdn_a_log.shape[1]
    hw = n_heads * HEAD_DIM
    n_experts = moe_router_w.shape[2]
    xr = x.reshape(m, d)

    hgrn_lb = jnp.cumsum(jax.nn.softmax(hgrn_lb_logits.astype(F32), axis=0), axis=0)
    w_in_t = jnp.swapaxes(mix_in_w[0], 0, 1)
    h = _rmsnorm(xr, norm_w[0, 0], BF16)
    proj_a = _mm_nt(h, w_in_t, F32, 1024, 1024, 0, 4 * hw).reshape(bsz, s, 4 * hw)
    proj_b = _mm_nt(h, w_in_t, F32, 1024, 1024, 4 * hw + 2 * n_heads, 4 * hw).reshape(bsz, s, 4 * hw)
    ba = _mm_nt(h, w_in_t, F32, 1024, LANES, 4 * hw, LANES).reshape(bsz, s, LANES)
    o_a = _gdn(proj_a, ba, gdn_conv_w[0].astype(F32), gdn_a_log[0], gdn_dt_bias[0], gdn_norm_w[0], col0=0)
    o_b = _hgrn(proj_b, hgrn_lb[0], hgrn_norm_w[0], col0=0, n_heads=n_heads)
    xr = _mm_resid([o_a.reshape(m, hw), o_b.reshape(m, hw)], mix_out_w[0], xr, 1024, 1024)
    h = _rmsnorm(xr, norm_w[0, 1], BF16)
    act = _mm_swiglu(h, ffn_gate_up_w[0], 1024, 512)
    xr = _mm_resid_rows(act, ffn_down_w[0], xr, 1024, 256)

    h = _rmsnorm(xr, norm_w[1, 0], BF16)
    yx = _mm(h, rg_in_w[0], F32, 1024, 1024).reshape(bsz, s, -1)
    rec = _rglru(yx, rg_conv_w[0], rg_conv_b[0], rg_gate_a_w[0], rg_gate_a_b[0],
                 rg_gate_x_w[0], rg_gate_x_b[0], rg_lambda[0])
    xr = _mm_resid([rec.reshape(m, d)], rg_out_w[0], xr, 1024, 1024)
    route = _norm_router(xr, norm_w[1, 1], moe_router_w[0])
    (src_token, chunk_expert, chunk_nq, n_valid, pos,
     unit_expert, unit_nq, unit_blk) = _moe_routing(route, n_experts)
    hs = _gather_norm(src_token, chunk_nq, xr, norm_w[1, 1])
    act = _gmm_swiglu(chunk_expert, chunk_nq, n_valid, hs, moe_gate_up_w[0], 512)
    ys = _gmm_down(unit_expert, unit_nq, unit_blk, act, moe_down_w[0], 256)
    out = _combine_norm(pos, xr, route, ys, final_norm_w)
    return out.reshape(bsz, s, d)
```

```python
import functools

import jax
import jax.numpy as jnp
from jax import lax
from jax.experimental import pallas as pl
from jax.experimental.pallas import tpu as pltpu

F32 = jnp.float32
BF16 = jnp.bfloat16
HIGHEST = lax.Precision.HIGHEST

EPS = 1e-6
CHUNK = 64
CONV_W = 4
HEAD_DIM = 128
RG_BLOCK = 256
RG_C = 8.0
TOP_K = 2
LANES = 128
SUBLANES = 8
VMEM_LIMIT = 56 * 1024 * 1024
VMEM_LIMIT_MAX = 60 * 1024 * 1024
MOE_CHUNK = 1024
MOE_UNIT = 1024
MOE_SUB = 128

_NT = (((1,), (1,)), ((), ()))
_TN = (((0,), (0,)), ((), ()))


def _cparams(sem):
    return pltpu.CompilerParams(dimension_semantics=sem, vmem_limit_bytes=VMEM_LIMIT)


def _sigmoid(x):
    return jax.nn.sigmoid(x)


def _silu(x):
    return x * _sigmoid(x)


def _softplus(x):
    return jnp.maximum(x, 0.0) + jnp.log1p(jnp.exp(-jnp.abs(x)))


def _bdot(a, b):
    return jnp.dot(a.astype(BF16), b.astype(BF16), preferred_element_type=F32)


def _bdot_g(a, b, dims):
    return lax.dot_general(a.astype(BF16), b.astype(BF16), dims, preferred_element_type=F32)


def _rmsnorm_kernel(x_ref, w_ref, o_ref):
    x = x_ref[...]
    y = x * lax.rsqrt(jnp.mean(x * x, axis=-1, keepdims=True) + EPS)
    o_ref[...] = (y * w_ref[...]).astype(o_ref.dtype)


def _rmsnorm(x, w, out_dtype, tm=512):
    m, d = x.shape
    return pl.pallas_call(
        _rmsnorm_kernel,
        grid=(m // tm,),
        in_specs=[pl.BlockSpec((tm, d), lambda i: (i, 0)),
                  pl.BlockSpec((1, d), lambda i: (0, 0))],
        out_specs=pl.BlockSpec((tm, d), lambda i: (i, 0)),
        out_shape=jax.ShapeDtypeStruct((m, d), out_dtype),
        compiler_params=_cparams(("parallel",)),
        name="rmsnorm",
    )(x, w.reshape(1, d).astype(F32))


def _bf16_weight(w_ref, w_s, fresh):
    if not fresh:
        return w_s[...]
    w = w_ref[...].astype(BF16)
    w_s[...] = w
    return w


def _first_row_tile_or_not(body):
    first = pl.program_id(1) == 0
    pl.when(first)(lambda: body(True))
    pl.when(jnp.logical_not(first))(lambda: body(False))


def _mm_kernel(a_ref, w_ref, o_ref, w_s):
    def body(fresh):
        w = _bf16_weight(w_ref, w_s, fresh)
        o_ref[...] = jnp.dot(a_ref[...], w, preferred_element_type=F32).astype(o_ref.dtype)

    _first_row_tile_or_not(body)


def _mm_nt_kernel(*refs, shift):
    a_ref, w_ref = refs[0], refs[1]
    o_ref, w_s = refs[-2], refs[-1]

    def body(fresh):
        if fresh:
            w = w_ref[...]
            if shift:
                w = jnp.concatenate([w, refs[2][...]], axis=0)[shift:shift + w_ref.shape[0], :]
            w = w.astype(BF16)
            w_s[...] = w
        else:
            w = w_s[...]
        o_ref[...] = lax.dot_general(a_ref[...], w, _NT, preferred_element_type=F32).astype(o_ref.dtype)

    _first_row_tile_or_not(body)


def _mm_nt(a, wt, out_dtype, tm, tn, row0, n_rows):
    m, k = a.shape
    base, shift = row0 // tn * tn, row0 % tn
    assert shift % SUBLANES == 0 and shift <= LANES and n_rows % tn == 0
    in_specs = [pl.BlockSpec((tm, k), lambda j, i: (i, 0)),
                pl.BlockSpec((tn, k), lambda j, i: (base // tn + j, 0))]
    args = [a, wt]
    if shift:
        in_specs.append(pl.BlockSpec((LANES, k), lambda j, i: ((base + (j + 1) * tn) // LANES, 0)))
        args.append(wt)
    return pl.pallas_call(
        functools.partial(_mm_nt_kernel, shift=shift),
        grid=(n_rows // tn, m // tm),
        in_specs=in_specs,
        out_specs=pl.BlockSpec((tm, tn), lambda j, i: (i, j)),
        out_shape=jax.ShapeDtypeStruct((m, n_rows), out_dtype),
        scratch_shapes=[pltpu.VMEM((tn, k), BF16)],
        compiler_params=_cparams(("arbitrary", "arbitrary")),
        name="mm_nt",
    )(*args)


def _mm(a, w, out_dtype, tm, tn, n_cols=None, col0=0):
    m, k = a.shape
    n = w.shape[1] if n_cols is None else n_cols
    cb = col0 // tn
    return pl.pallas_call(
        _mm_kernel,
        grid=(n // tn, m // tm),
        in_specs=[pl.BlockSpec((tm, k), lambda j, i: (i, 0)),
                  pl.BlockSpec((k, tn), lambda j, i: (0, cb + j))],
        out_specs=pl.BlockSpec((tm, tn), lambda j, i: (i, j)),
        out_shape=jax.ShapeDtypeStruct((m, n), out_dtype),
        scratch_shapes=[pltpu.VMEM((k, tn), BF16)],
        compiler_params=_cparams(("arbitrary", "arbitrary")),
        name="mm",
    )(a, w)


def _mm_swiglu_kernel(a_ref, wg_ref, wu_ref, o_ref, wg_s, wu_s):
    def body(fresh):
        a = a_ref[...]
        g = jnp.dot(a, _bf16_weight(wg_ref, wg_s, fresh), preferred_element_type=F32)
        u = jnp.dot(a, _bf16_weight(wu_ref, wu_s, fresh), preferred_element_type=F32)
        o_ref[...] = (_silu(g) * u).astype(o_ref.dtype)

    _first_row_tile_or_not(body)


def _mm_swiglu(a, w_gu, tm, tn):
    m, k = a.shape
    f = w_gu.shape[1] // 2
    nb = f // tn
    return pl.pallas_call(
        _mm_swiglu_kernel,
        grid=(nb, m // tm),
        in_specs=[pl.BlockSpec((tm, k), lambda j, i: (i, 0)),
                  pl.BlockSpec((k, tn), lambda j, i: (0, j)),
                  pl.BlockSpec((k, tn), lambda j, i: (0, j + nb))],
        out_specs=pl.BlockSpec((tm, tn), lambda j, i: (i, j)),
        out_shape=jax.ShapeDtypeStruct((m, f), BF16),
        scratch_shapes=[pltpu.VMEM((k, tn), BF16)] * 2,
        compiler_params=_cparams(("arbitrary", "arbitrary")),
        name="mm_swiglu",
    )(a, w_gu, w_gu)


def _mm_resid_kernel(*refs, n_pairs):
    r_ref, o_ref = refs[2 * n_pairs], refs[2 * n_pairs + 1]
    w_s = refs[2 * n_pairs + 2:]

    def body(fresh):
        acc = r_ref[...]
        for p in range(n_pairs):
            w = _bf16_weight(refs[2 * p + 1], w_s[p], fresh)
            acc = acc + jnp.dot(refs[2 * p][...], w, preferred_element_type=F32)
        o_ref[...] = acc

    _first_row_tile_or_not(body)


def _mm_resid(a_list, w, resid, tm, tn):
    m, n = resid.shape
    in_specs, args, scratch = [], [], []
    for p, a in enumerate(a_list):
        k = a.shape[1]
        in_specs += [pl.BlockSpec((tm, k), lambda j, i: (i, 0)),
                     pl.BlockSpec((k, tn), lambda j, i, p=p: (p, j))]
        args += [a, w]
        scratch.append(pltpu.VMEM((k, tn), BF16))
    in_specs.append(pl.BlockSpec((tm, tn), lambda j, i: (i, j)))
    args.append(resid)
    return pl.pallas_call(
        functools.partial(_mm_resid_kernel, n_pairs=len(a_list)),
        grid=(n // tn, m // tm),
        in_specs=in_specs,
        out_specs=pl.BlockSpec((tm, tn), lambda j, i: (i, j)),
        out_shape=jax.ShapeDtypeStruct((m, n), F32),
        scratch_shapes=scratch,
        compiler_params=_cparams(("arbitrary", "arbitrary")),
        name="mm_resid",
    )(*args)


def _causal_conv(buf_ref, x, cw, t_len):
    buf_ref[pl.ds(SUBLANES, t_len), :] = x
    acc = cw[CONV_W - 1:CONV_W, :] * x
    for j in range(CONV_W - 1):
        off = SUBLANES - (CONV_W - 1) + j
        acc = acc + cw[j:j + 1, :] * buf_ref[pl.ds(off, t_len), :]
    buf_ref[pl.ds(0, SUBLANES), :] = x[t_len - SUBLANES:, :]
    return acc


def _tri_masks():
    row = lax.broadcasted_iota(jnp.int32, (CHUNK, CHUNK), 0)
    col = lax.broadcasted_iota(jnp.int32, (CHUNK, CHUNK), 1)
    return row >= col, row > col, row == col


def _gated_head_norm(o, nw, z):
    o = o * lax.rsqrt(jnp.mean(o * o, axis=-1, keepdims=True) + EPS) * nw
    return o * _silu(z)


def _gdn_kernel(q_ref, k_ref, v_ref, z_ref, ba_ref, cwq_ref, cwk_ref, cwv_ref, alog_ref, dtb_ref,
                nw_ref, o_ref, qbuf, kbuf, vbuf, qs, ks, vs, gcs, gct, bs, u_s, w_s, qk_s, qd_s, kd_s, s_ref,
                *, n_heads, t_len):
    t = pl.program_id(1)
    n_chunks = t_len // CHUNK
    width = n_heads * HEAD_DIM

    @pl.when(t == 0)
    def _():
        zeros = jnp.zeros((SUBLANES, width), F32)
        qbuf[pl.ds(0, SUBLANES), :] = zeros
        kbuf[pl.ds(0, SUBLANES), :] = zeros
        vbuf[pl.ds(0, SUBLANES), :] = zeros
        s_ref[...] = jnp.zeros_like(s_ref)

    qc = _silu(_causal_conv(qbuf, q_ref[0], cwq_ref[...], t_len))
    kc = _silu(_causal_conv(kbuf, k_ref[0], cwk_ref[...], t_len))
    vs[...] = _silu(_causal_conv(vbuf, v_ref[0], cwv_ref[...], t_len))
    for h in range(n_heads):
        sl = slice(h * HEAD_DIM, (h + 1) * HEAD_DIM)
        qh = qc[:, sl]
        kh = kc[:, sl]
        qs[:, sl] = qh * (lax.rsqrt(jnp.sum(qh * qh, axis=-1, keepdims=True) + EPS) * (HEAD_DIM ** -0.5))
        ks[:, sl] = kh * lax.rsqrt(jnp.sum(kh * kh, axis=-1, keepdims=True) + EPS)

    ba = ba_ref[0]
    g_log = -jnp.exp(alog_ref[...]) * _softplus(ba + dtb_ref[...])
    bs[...] = _sigmoid(ba)
    row = lax.broadcasted_iota(jnp.int32, (t_len, t_len), 0)
    col = lax.broadcasted_iota(jnp.int32, (t_len, t_len), 1)
    block_tri = ((row // CHUNK == col // CHUNK) & (row >= col)).astype(F32)
    gc_blk = jnp.dot(block_tri, g_log, precision=HIGHEST, preferred_element_type=F32)
    gcs[...] = gc_blk
    for c in range(n_chunks):
        gct[c] = gc_blk[c * CHUNK:(c + 1) * CHUNK, :].T

    incl, strict, eye = _tri_masks()
    eye_f = eye.astype(F32)
    nw = nw_ref[...]

    heads = range(n_heads)
    hsl = [slice(h * HEAD_DIM, (h + 1) * HEAD_DIM) for h in heads]
    qsl = [slice(h * HEAD_DIM, h * HEAD_DIM + CHUNK) for h in heads]
    chunks_per_iter = 2
    assert n_chunks % chunks_per_iter == 0

    def pass_a(it, carry):
        items = []
        for cc in range(chunks_per_iter):
            c = it * chunks_per_iter + cc
            rows = pl.ds(pl.multiple_of(c * CHUNK, CHUNK), CHUNK)
            gc_all = gcs[rows, :]
            beta_all = bs[rows, :]
            gct_c = gct[c]
            for h in heads:
                gc = gc_all[:, n_heads + h:n_heads + h + 1]
                beta = beta_all[:, h:h + 1]
                gc_row = gct_c[n_heads + h:n_heads + h + 1, :]
                decay = jnp.where(incl, jnp.exp(jnp.where(incl, gc - gc_row, 0.0)), 0.0)
                items.append(dict(rows=rows, h=h, gc=gc, beta=beta, decay=decay,
                                  q=qs[rows, hsl[h]], k=ks[rows, hsl[h]]))
        for it_ in items:
            it_["kb"] = it_["k"].astype(BF16)
            it_["kk"] = lax.dot_general(it_["kb"], it_["kb"], _NT, preferred_element_type=F32)
        for it_ in items:
            it_["x"] = jnp.where(strict, -(it_["beta"] * it_["kk"] * it_["decay"]), 0.0)
            it_["p"] = eye_f + it_["x"]
        for _ in range(5):
            for it_ in items:
                xb = it_["x"].astype(BF16)
                it_["x"] = jnp.dot(xb, xb, preferred_element_type=F32)
            for it_ in items:
                it_["p"] = it_["p"] + _bdot(it_["p"], it_["x"])
        for it_ in items:
            egc = jnp.exp(it_["gc"])
            it_["egc"] = egc
            v = vs[it_["rows"], hsl[it_["h"]]]
            rhs = jnp.concatenate([v * it_["beta"], it_["k"] * (it_["beta"] * egc)], axis=-1)
            it_["sol"] = _bdot(it_["p"], rhs)
        for it_ in items:
            it_["qk"] = lax.dot_general(it_["q"].astype(BF16), it_["kb"], _NT,
                                        preferred_element_type=F32) * it_["decay"]
        for it_ in items:
            rows, h, gc = it_["rows"], it_["h"], it_["gc"]
            g_last = gc[CHUNK - 1:CHUNK, :]
            u_s[rows, hsl[h]] = it_["sol"][:, :HEAD_DIM]
            w_s[rows, hsl[h]] = it_["sol"][:, HEAD_DIM:].astype(BF16)
            qk_s[rows, qsl[h]] = it_["qk"].astype(BF16)
            qd_s[rows, hsl[h]] = (it_["q"] * it_["egc"]).astype(BF16)
            kd_s[rows, hsl[h]] = (it_["k"] * jnp.exp(g_last - gc)).astype(BF16)
        return carry

    lax.fori_loop(0, n_chunks // chunks_per_iter, pass_a, 0)

    def pass_b(c, carry):
        rows = pl.ds(pl.multiple_of(c * CHUNK, CHUNK), CHUNK)
        gc_all = gcs[rows, :]
        states = [s_ref[h] for h in heads]
        sbs = [s.astype(BF16) for s in states]
        v_new = [u_s[rows, hsl[h]] - jnp.dot(w_s[rows, hsl[h]], sbs[h], preferred_element_type=F32)
                 for h in heads]
        vbs = [v.astype(BF16) for v in v_new]
        for h in heads:
            g_last = gc_all[CHUNK - 1:CHUNK, n_heads + h:n_heads + h + 1]
            s_ref[h] = states[h] * jnp.exp(g_last) + lax.dot_general(kd_s[rows, hsl[h]], vbs[h], _TN,
                                                                      preferred_element_type=F32)
        outs = [jnp.dot(qd_s[rows, hsl[h]], sbs[h], preferred_element_type=F32)
                + jnp.dot(qk_s[rows, qsl[h]], vbs[h], preferred_element_type=F32) for h in heads]
        for h in heads:
            o = _gated_head_norm(outs[h], nw, z_ref[0, rows, hsl[h]])
            o_ref[0, rows, hsl[h]] = o.astype(o_ref.dtype)
        return carry

    lax.fori_loop(0, n_chunks, pass_b, 0)


def _gdn(proj, ba, conv_w, a_log, dt_bias, norm_w, *, col0, t_len=256):
    bsz, s, _ = proj.shape
    n_heads = a_log.shape[0]
    width = n_heads * HEAD_DIM
    cb = col0 // width

    def col_spec(group):
        return pl.BlockSpec((1, t_len, width), lambda b, t: (b, t, cb + group))

    def cw_spec(group):
        return pl.BlockSpec((CONV_W, width), lambda b, t: (0, group))

    row = jnp.zeros((1, LANES), F32)
    alog_row = row.at[0, n_heads:2 * n_heads].set(a_log.astype(F32))
    dtb_row = row.at[0, n_heads:2 * n_heads].set(dt_bias.astype(F32))
    small = pl.BlockSpec((1, LANES), lambda b, t: (0, 0))
    kern = functools.partial(_gdn_kernel, n_heads=n_heads, t_len=t_len)
    return pl.pallas_call(
        kern,
        grid=(bsz, s // t_len),
        in_specs=[col_spec(0), col_spec(1), col_spec(2), col_spec(3),
                  pl.BlockSpec((1, t_len, LANES), lambda b, t: (b, t, 0)),
                  cw_spec(0), cw_spec(1), cw_spec(2), small, small,
                  pl.BlockSpec((1, HEAD_DIM), lambda b, t: (0, 0))],
        out_specs=pl.BlockSpec((1, t_len, width), lambda b, t: (b, t, 0)),
        out_shape=jax.ShapeDtypeStruct((bsz, s, width), BF16),
        scratch_shapes=[pltpu.VMEM((SUBLANES + t_len, width), F32)] * 3
        + [pltpu.VMEM((t_len, width), F32)] * 3
        + [pltpu.VMEM((t_len, LANES), F32),
           pltpu.VMEM((t_len // CHUNK, LANES, CHUNK), F32),
           pltpu.VMEM((t_len, LANES), F32),
           pltpu.VMEM((t_len, width), F32)]
        + [pltpu.VMEM((t_len, width), BF16)] * 4
        + [pltpu.VMEM((n_heads, HEAD_DIM, HEAD_DIM), F32)],
        compiler_params=_cparams(("parallel", "arbitrary")),
        name="gdn",
    )(proj, proj, proj, proj, ba, conv_w, conv_w, conv_w, alog_row, dtb_row,
      norm_w.reshape(1, HEAD_DIM).astype(F32))


def _hgrn_kernel(q_ref, f_ref, i_ref, g_ref, lb_ref, nw_ref, o_ref, qs, ks, ls, st_ref, *, heads, t_len):
    t = pl.program_id(1)

    @pl.when(t == 0)
    def _():
        st_ref[...] = jnp.zeros_like(st_ref)

    lb = lb_ref[...]
    forget = lb + (1.0 - lb) * _sigmoid(f_ref[0])
    ks[...] = 1.0 - forget
    qs[...] = _silu(q_ref[0]) * (HEAD_DIM ** -0.5)
    row = lax.broadcasted_iota(jnp.int32, (t_len, t_len), 0)
    col = lax.broadcasted_iota(jnp.int32, (t_len, t_len), 1)
    block_tri = ((row // CHUNK == col // CHUNK) & (row >= col)).astype(BF16)
    log_f = jnp.log(forget)
    hi = log_f.astype(BF16)
    rest = log_f - hi.astype(F32)
    mid = rest.astype(BF16)
    lo = (rest - mid.astype(F32)).astype(BF16)
    ls[...] = (jnp.dot(block_tri, hi, preferred_element_type=F32)
               + (jnp.dot(block_tri, mid, preferred_element_type=F32)
                  + jnp.dot(block_tri, lo, preferred_element_type=F32)))

    incl, _, _ = _tri_masks()
    nw = nw_ref[...]
    mid = CHUNK // 2 - 1

    hsl = [slice(j * HEAD_DIM, (j + 1) * HEAD_DIM) for j in range(heads)]

    def chunk_body(c, carry):
        rows = pl.ds(pl.multiple_of(c * CHUNK, CHUNK), CHUNK)
        items = []
        for j in range(heads):
            b = ls[rows, hsl[j]]
            items.append(dict(b=b, b_mid=b[mid:mid + 1, :], b_last=b[CHUNK - 1:CHUNK, :],
                              q=qs[rows, hsl[j]], k=ks[rows, hsl[j]],
                              vb=i_ref[0, rows, hsl[j]].astype(BF16)))
        for it in items:
            it["att"] = jnp.where(incl, _bdot_g(it["q"] * jnp.exp(it["b"] - it["b_mid"]),
                                                it["k"] * jnp.exp(it["b_mid"] - it["b"]), _NT), 0.0)
        for j, it in enumerate(items):
            st = st_ref[j]
            it["o"] = _bdot_g(it["q"] * jnp.exp(it["b"]), st, _NT)
            st_ref[j] = st * jnp.exp(it["b_last"]) + lax.dot_general(
                it["vb"], (it["k"] * jnp.exp(it["b_last"] - it["b"])).astype(BF16), _TN,
                preferred_element_type=F32)
        for it in items:
            it["o"] = it["o"] + jnp.dot(it["att"].astype(BF16), it["vb"], preferred_element_type=F32)
        for j, it in enumerate(items):
            o = _gated_head_norm(it["o"], nw, g_ref[0, rows, hsl[j]])
            o_ref[0, rows, hsl[j]] = o.astype(o_ref.dtype)
        return carry

    lax.fori_loop(0, t_len // CHUNK, chunk_body, 0)


def _hgrn(proj, lb, norm_w, *, col0, n_heads, t_len=256):
    bsz, s, _ = proj.shape
    width = n_heads * HEAD_DIM
    cb = col0 // width

    def col_spec(group):
        return pl.BlockSpec((1, t_len, width), lambda b, t: (b, t, cb + group))

    kern = functools.partial(_hgrn_kernel, heads=n_heads, t_len=t_len)
    return pl.pallas_call(
        kern,
        grid=(bsz, s // t_len),
        in_specs=[col_spec(0), col_spec(1), col_spec(2), col_spec(3),
                  pl.BlockSpec((1, width), lambda b, t: (0, 0)),
                  pl.BlockSpec((1, HEAD_DIM), lambda b, t: (0, 0))],
        out_specs=pl.BlockSpec((1, t_len, width), lambda b, t: (b, t, 0)),
        out_shape=jax.ShapeDtypeStruct((bsz, s, width), BF16),
        scratch_shapes=[pltpu.VMEM((t_len, width), F32)] * 3
        + [pltpu.VMEM((n_heads, HEAD_DIM, HEAD_DIM), F32)],
        compiler_params=_cparams(("parallel", "arbitrary")),
        name="hgrn2",
    )(proj, proj, proj, proj, lb.reshape(1, width).astype(F32), norm_w.reshape(1, HEAD_DIM).astype(F32))


def _rglru_kernel(y_ref, x_ref, cw_ref, cb_ref, wa_ref, ba_ref, wx_ref, bx_ref, lam_ref, o_ref,
                  xbuf, a_s, b_s, h_s, hcar, *, blocks, t_len):
    t = pl.program_id(2)

    @pl.when(t == 0)
    def _():
        xbuf[pl.ds(0, SUBLANES), :] = jnp.zeros((SUBLANES, blocks * RG_BLOCK), F32)
        hcar[...] = jnp.zeros_like(hcar)

    xc = _causal_conv(xbuf, x_ref[0], cw_ref[...], t_len) + cb_ref[...]
    for n in range(blocks):
        sl = slice(n * RG_BLOCK, (n + 1) * RG_BLOCK)
        xb = xc[:, sl]
        r = _sigmoid(_bdot(xb, wa_ref[n]) + ba_ref[:, sl])
        gi = _sigmoid(_bdot(xb, wx_ref[n]) + bx_ref[:, sl])
        log_a = (-RG_C) * r * _softplus(-lam_ref[:, sl])
        a = jnp.exp(log_a)
        a_s[:, sl] = a
        one_minus_a2 = -jnp.tanh(log_a) * (a * a + 1.0)
        b_s[:, sl] = jnp.sqrt(jnp.maximum(one_minus_a2, 0.0)) * (gi * xb)

    def row_body(i, h):
        h = a_s[pl.ds(i, 1), :] * h + b_s[pl.ds(i, 1), :]
        h_s[pl.ds(i, 1), :] = h
        return h

    hcar[...] = lax.fori_loop(0, t_len, row_body, hcar[...])
    o_ref[0] = (jax.nn.gelu(y_ref[0], approximate=True) * h_s[...]).astype(o_ref.dtype)


def _rglru(yx, conv_w, conv_b, wa, ba, wx, bx, lam, *, blocks_per_step=8, t_len=256):
    bsz, s, w2 = yx.shape
    width = w2 // 2
    gw = blocks_per_step * RG_BLOCK
    n_g = width // gw
    vec = lambda a: a.reshape(1, width).astype(F32)
    vspec = pl.BlockSpec((1, gw), lambda b, g, t: (0, g))
    wspec = pl.BlockSpec((blocks_per_step, RG_BLOCK, RG_BLOCK), lambda b, g, t: (g, 0, 0))
    kern = functools.partial(_rglru_kernel, blocks=blocks_per_step, t_len=t_len)
    return pl.pallas_call(
        kern,
        grid=(bsz, n_g, s // t_len),
        in_specs=[pl.BlockSpec((1, t_len, gw), lambda b, g, t: (b, t, g)),
                  pl.BlockSpec((1, t_len, gw), lambda b, g, t: (b, t, n_g + g)),
                  pl.BlockSpec((CONV_W, gw), lambda b, g, t: (0, g)),
                  vspec, wspec, vspec, wspec, vspec, vspec],
        out_specs=pl.BlockSpec((1, t_len, gw), lambda b, g, t: (b, t, g)),
        out_shape=jax.ShapeDtypeStruct((bsz, s, width), BF16),
        scratch_shapes=[pltpu.VMEM((SUBLANES + t_len, gw), F32)]
        + [pltpu.VMEM((t_len, gw), F32)] * 3
        + [pltpu.VMEM((1, gw), F32)],
        compiler_params=_cparams(("parallel", "parallel", "arbitrary")),
        name="rglru",
    )(yx, yx, conv_w.astype(F32), vec(conv_b), wa.astype(BF16), vec(ba), wx.astype(BF16), vec(bx), vec(lam))


def _norm_router_kernel(x_ref, nw_ref, w_ref, o_ref, *, n_experts):
    x = x_ref[...]
    y = x * lax.rsqrt(jnp.mean(x * x, axis=-1, keepdims=True) + EPS) * nw_ref[...]
    w = w_ref[...]
    y_hi = y.astype(BF16)
    y_lo = (y - y_hi.astype(F32)).astype(BF16)
    w_hi = w.astype(BF16)
    w_lo = (w - w_hi.astype(F32)).astype(BF16)
    logits = (jnp.dot(y_hi, w_hi, preferred_element_type=F32)
              + (jnp.dot(y_lo, w_hi, preferred_element_type=F32)
                 + jnp.dot(y_hi, w_lo, preferred_element_type=F32)))
    lane = lax.broadcasted_iota(jnp.int32, logits.shape, 1).astype(F32)
    neg = jnp.float32(-jnp.inf)
    l1 = jnp.where(lane < n_experts, logits, neg)
    m1 = jnp.max(l1, axis=-1, keepdims=True)
    i1 = jnp.min(jnp.where(l1 == m1, lane, float(LANES)), axis=-1, keepdims=True)
    l2 = jnp.where(lane == i1, neg, l1)
    m2 = jnp.max(l2, axis=-1, keepdims=True)
    i2 = jnp.min(jnp.where(l2 == m2, lane, float(LANES)), axis=-1, keepdims=True)
    e2 = jnp.exp(m2 - m1)
    g1 = 1.0 / (1.0 + e2)
    g2 = e2 / (1.0 + e2)
    out = jnp.where(lane == 0, i1, 0.0)
    out = jnp.where(lane == 1, i2, out)
    out = jnp.where(lane == 2, g1, out)
    out = jnp.where(lane == 3, g2, out)
    o_ref[...] = out


def _norm_router(x, nw, router_w, tm=512):
    m, d = x.shape
    n_experts = router_w.shape[1]
    w = jnp.zeros((d, LANES), F32).at[:, :n_experts].set(router_w.astype(F32))
    return pl.pallas_call(
        functools.partial(_norm_router_kernel, n_experts=n_experts),
        grid=(m // tm,),
        in_specs=[pl.BlockSpec((tm, d), lambda i: (i, 0)),
                  pl.BlockSpec((1, d), lambda i: (0, 0)),
                  pl.BlockSpec((d, LANES), lambda i: (0, 0))],
        out_specs=pl.BlockSpec((tm, LANES), lambda i: (i, 0)),
        out_shape=jax.ShapeDtypeStruct((m, LANES), F32),
        compiler_params=_cparams(("parallel",)),
        name="norm_router",
    )(x, nw.reshape(1, d).astype(F32), w)


def _start_row_gather(idx_ref, base, src3, dst3, sem, n_rows):
    def body(g, carry):
        for u in range(SUBLANES):
            row = idx_ref[base + g * SUBLANES + u]
            src = src3.at[lax.shift_right_logical(row, 3), pl.ds(row & (SUBLANES - 1), 1), :]
            pltpu.make_async_copy(src, dst3.at[g, pl.ds(u, 1), :], sem).start(priority=u % 2)
        return carry
    lax.fori_loop(0, n_rows // SUBLANES, body, 0)


def _wait_row_gather(src3, dst3, sem, n_rows):
    n = n_rows // SUBLANES
    pltpu.make_async_copy(src3.at[pl.ds(0, n)], dst3.at[pl.ds(0, n)], sem).wait()


def _gather_norm_kernel(idx_ref, nq_ref, x_hbm, nw_ref, o_ref, buf, sem):
    c = pl.program_id(0)
    slot = c % 2

    def start(chunk, s):
        _start_row_gather(idx_ref, chunk * MOE_CHUNK, x_hbm, buf.at[s], sem.at[s], nq_ref[chunk] * MOE_SUB)

    @pl.when(c == 0)
    def _():
        start(0, 0)

    @pl.when(c + 1 < pl.num_programs(0))
    def _():
        start(c + 1, 1 - slot)

    for sb in range(MOE_CHUNK // MOE_SUB):
        @pl.when(sb < nq_ref[c])
        def _():
            _wait_row_gather(x_hbm, buf.at[slot], sem.at[slot], MOE_SUB)

    for sb in range(MOE_CHUNK // MOE_SUB):
        rows = pl.ds(sb * MOE_SUB, MOE_SUB)

        @pl.when(sb < nq_ref[c])
        def _(sb=sb, rows=rows):
            tiles = pl.ds(sb * (MOE_SUB // SUBLANES), MOE_SUB // SUBLANES)
            x = buf[slot, tiles].reshape(MOE_SUB, o_ref.shape[1])
            y = x * lax.rsqrt(jnp.mean(x * x, axis=-1, keepdims=True) + EPS) * nw_ref[...]
            o_ref[rows, :] = y.astype(o_ref.dtype)

        @pl.when(sb >= nq_ref[c])
        def _():
            o_ref[rows, :] = jnp.zeros((MOE_SUB, o_ref.shape[1]), o_ref.dtype)


def _gather_norm(src_token, chunk_nq, x, nw):
    rows = src_token.shape[0]
    d = x.shape[1]
    grid_spec = pltpu.PrefetchScalarGridSpec(
        num_scalar_prefetch=2,
        grid=(rows // MOE_CHUNK,),
        in_specs=[pl.BlockSpec(memory_space=pl.ANY),
                  pl.BlockSpec((1, d), lambda c, idx, nq: (0, 0))],
        out_specs=pl.BlockSpec((MOE_CHUNK, d), lambda c, idx, nq: (c, 0)),
        scratch_shapes=[pltpu.VMEM((2, MOE_CHUNK // SUBLANES, SUBLANES, d), F32),
                        pltpu.SemaphoreType.DMA((2,))],
    )
    return pl.pallas_call(
        _gather_norm_kernel,
        grid_spec=grid_spec,
        out_shape=jax.ShapeDtypeStruct((rows, d), BF16),
        compiler_params=_cparams(("arbitrary",)),
        name="gather_norm",
    )(src_token, chunk_nq, x.reshape(-1, SUBLANES, d), nw.reshape(1, d).astype(F32))


def _for_valid_rows(nq, o_ref, fn, pred=True):
    total = o_ref.shape[0]
    for v in range(total // MOE_SUB + 1):
        @pl.when(jnp.logical_and(nq == v, pred))
        def _(v=v):
            rows = v * MOE_SUB
            if rows:
                fn(rows)
            if rows < total:
                o_ref[pl.ds(rows, total - rows), :] = jnp.zeros((total - rows, o_ref.shape[1]), o_ref.dtype)


def _gmm_swiglu_kernel(te_ref, nq_ref, nv_ref, a_ref, wg_ref, wu_ref, o_ref):
    c = pl.program_id(0)

    def compute(rows):
        a = a_ref[pl.ds(0, rows), :]
        g = jnp.dot(a, wg_ref[0].astype(BF16), preferred_element_type=F32)
        u = jnp.dot(a, wu_ref[0].astype(BF16), preferred_element_type=F32)
        o_ref[pl.ds(0, rows), :] = (_silu(g) * u).astype(o_ref.dtype)

    _for_valid_rows(nq_ref[c], o_ref, compute)


def _gmm_swiglu(chunk_expert, chunk_nq, n_valid, a, w_gu, tn):
    m, k = a.shape
    f = w_gu.shape[2] // 2
    nb = f // tn
    last = lambda c, nv: jnp.minimum(c, nv[0] - 1)
    col = lambda j, c, nv: jnp.where(c < nv[0], j, nb - 1)
    grid_spec = pltpu.PrefetchScalarGridSpec(
        num_scalar_prefetch=3,
        grid=(m // MOE_CHUNK, nb),
        in_specs=[pl.BlockSpec((MOE_CHUNK, k), lambda c, j, te, nq, nv: (last(c, nv), 0)),
                  pl.BlockSpec((1, k, tn), lambda c, j, te, nq, nv: (te[c], 0, col(j, c, nv))),
                  pl.BlockSpec((1, k, tn), lambda c, j, te, nq, nv: (te[c], 0, col(j, c, nv) + nb))],
        out_specs=pl.BlockSpec((MOE_CHUNK, tn), lambda c, j, te, nq, nv: (c, j)),
    )
    return pl.pallas_call(
        _gmm_swiglu_kernel,
        grid_spec=grid_spec,
        out_shape=jax.ShapeDtypeStruct((m, f), BF16),
        compiler_params=_cparams(("arbitrary", "arbitrary")),
        name="gmm_swiglu",
    )(chunk_expert, chunk_nq, n_valid, a, w_gu, w_gu)


def _gmm_down_kernel(te_ref, nq_ref, blk_ref, a_ref, w_ref, o_ref):
    u = pl.program_id(0)

    def compute(rows):
        w = w_ref[0].astype(BF16)
        o_ref[pl.ds(0, rows), :] = jnp.dot(a_ref[pl.ds(0, rows), :], w, preferred_element_type=F32)

    _for_valid_rows(nq_ref[u], o_ref, compute)


def _gmm_down(unit_expert, unit_nq, unit_blk, a, w_d, tn):
    m, k = a.shape
    n = w_d.shape[2]
    nj = n // tn
    col = lambda j, u, nq: jnp.where(nq[u] > 0, j, nj - 1)
    grid_spec = pltpu.PrefetchScalarGridSpec(
        num_scalar_prefetch=3,
        grid=(m // MOE_UNIT, nj),
        in_specs=[pl.BlockSpec((MOE_UNIT, k), lambda u, j, te, nq, blk: (blk[u], 0)),
                  pl.BlockSpec((1, k, tn), lambda u, j, te, nq, blk: (te[u], 0, col(j, u, nq)))],
        out_specs=pl.BlockSpec((MOE_UNIT, tn), lambda u, j, te, nq, blk: (u, j)),
    )
    return pl.pallas_call(
        _gmm_down_kernel,
        grid_spec=grid_spec,
        out_shape=jax.ShapeDtypeStruct((m, n), F32),
        compiler_params=pltpu.CompilerParams(dimension_semantics=("arbitrary", "arbitrary"),
                                             vmem_limit_bytes=VMEM_LIMIT_MAX),
        name="gmm_down",
    )(unit_expert, unit_nq, unit_blk, a, w_d)


def _combine_norm_kernel(pos_ref, x_ref, r_ref, ys_hbm, w_ref, o_ref, buf, sem, *, tm):
    i = pl.program_id(0)
    n = pl.num_programs(0)
    slot = i % 2

    def start(tile, s):
        for kk in range(TOP_K):
            _start_row_gather(pos_ref, kk * (n * tm) + tile * tm, ys_hbm, buf.at[s, kk], sem.at[s], tm)

    @pl.when(i == 0)
    def _():
        start(0, 0)

    @pl.when(i + 1 < n)
    def _():
        start(i + 1, 1 - slot)

    for kk in range(TOP_K):
        _wait_row_gather(ys_hbm, buf.at[slot, kk], sem.at[slot], tm)
    r = r_ref[...]
    d = x_ref.shape[1]
    moe = r[:, TOP_K:TOP_K + 1] * buf[slot, 0].reshape(tm, d)
    for kk in range(1, TOP_K):
        moe = moe + r[:, TOP_K + kk:TOP_K + kk + 1] * buf[slot, kk].reshape(tm, d)
    x = x_ref[...] + moe
    y = x * lax.rsqrt(jnp.mean(x * x, axis=-1, keepdims=True) + EPS)
    o_ref[...] = y * w_ref[...]


def _combine_norm(pos, x, route, ys, w, tm=256):
    m, d = x.shape
    grid_spec = pltpu.PrefetchScalarGridSpec(
        num_scalar_prefetch=1,
        grid=(m // tm,),
        in_specs=[pl.BlockSpec((tm, d), lambda i, p: (i, 0)),
                  pl.BlockSpec((tm, LANES), lambda i, p: (i, 0)),
                  pl.BlockSpec(memory_space=pl.ANY),
                  pl.BlockSpec((1, d), lambda i, p: (0, 0))],
        out_specs=pl.BlockSpec((tm, d), lambda i, p: (i, 0)),
        scratch_shapes=[pltpu.VMEM((2, TOP_K, tm // SUBLANES, SUBLANES, d), F32),
                        pltpu.SemaphoreType.DMA((2,))],
    )
    return pl.pallas_call(
        functools.partial(_combine_norm_kernel, tm=tm),
        grid_spec=grid_spec,
        out_shape=jax.ShapeDtypeStruct((m, d), F32),
        compiler_params=_cparams(("arbitrary",)),
        name="combine_norm",
    )(pos, x, route, ys.reshape(-1, SUBLANES, d), w.reshape(1, d).astype(F32))


def _moe_routing(route, n_experts):
    m = route.shape[0]
    n_sub = MOE_CHUNK // MOE_SUB
    ids = route[:, :TOP_K].astype(jnp.int32)
    flat_e = ids.reshape(-1)
    onehot = (flat_e[:, None] == jnp.arange(n_experts)[None, :]).astype(jnp.int32)
    rank = jnp.sum((jnp.cumsum(onehot, axis=0) - onehot) * onehot, axis=1)
    counts = jnp.sum(onehot, axis=0)
    subs = (counts + MOE_SUB - 1) // MOE_SUB
    n_chunks = (subs + n_sub - 1) // n_sub
    chunk_end = jnp.cumsum(n_chunks)
    chunk_start = chunk_end - n_chunks
    max_chunks = (m * TOP_K // MOE_SUB + n_experts + n_sub - 1) // n_sub + n_experts
    rows = max_chunks * MOE_CHUNK
    first_subs = jnp.where(subs % n_sub == 0, n_sub, subs % n_sub)
    first_rows = jnp.sum(onehot * first_subs[None, :], axis=1) * MOE_SUB
    rest = rank - first_rows
    chunk_off = jnp.where(rest < 0, 0, 1 + rest // MOE_CHUNK)
    row_off = jnp.where(rest < 0, rank, rest % MOE_CHUNK)
    pos = (jnp.sum(onehot * chunk_start[None, :], axis=1) + chunk_off) * MOE_CHUNK + row_off
    src_token = jnp.zeros((rows,), jnp.int32).at[pos].set(jnp.arange(m * TOP_K, dtype=jnp.int32) // TOP_K)
    n_valid = chunk_end[n_experts - 1].astype(jnp.int32)
    cidx = jnp.minimum(jnp.arange(max_chunks, dtype=jnp.int32), n_valid - 1)
    chunk_expert = jnp.sum((cidx[:, None] >= chunk_end[None, :]).astype(jnp.int32), axis=1).astype(jnp.int32)
    sel = (chunk_expert[:, None] == jnp.arange(n_experts)[None, :]).astype(jnp.int32)
    within = cidx - jnp.sum(sel * chunk_start[None, :], axis=1)
    chunk_nq = jnp.where(within == 0, jnp.sum(sel * first_subs[None, :], axis=1), n_sub)
    chunk_nq = jnp.where(jnp.arange(max_chunks) < n_valid, chunk_nq, 0).astype(jnp.int32)
    pos_kmajor = pos.reshape(m, TOP_K).T.reshape(-1).astype(jnp.int32)
    per = MOE_CHUNK // MOE_UNIT
    unit_sub = MOE_UNIT // MOE_SUB
    half = jnp.arange(max_chunks * per, dtype=jnp.int32) % per
    unit_nq = jnp.clip(jnp.repeat(chunk_nq, per) - unit_sub * half, 0, unit_sub).astype(jnp.int32)
    unit_expert = jnp.repeat(chunk_expert, per)
    uidx = jnp.arange(max_chunks * per, dtype=jnp.int32)
    unit_blk = lax.cummax(jnp.where(unit_nq > 0, uidx, 0), axis=0).astype(jnp.int32)
    return (src_token, chunk_expert, chunk_nq, n_valid.reshape(1), pos_kmajor,
            unit_expert, unit_nq, unit_blk)


def kernel(x, norm_w, final_norm_w, mix_in_w, gdn_conv_w, gdn_a_log, gdn_dt_bias, gdn_norm_w, hgrn_lb_logits, hgrn_norm_w, mix_out_w, ffn_gate_up_w, ffn_down_w, rg_in_w, rg_conv_w, rg_conv_b, rg_gate_a_w, rg_gate_a_b, rg_gate_x_w, rg_gate_x_b, rg_lambda, rg_out_w, moe_router_w, moe_gate_up_w, moe_down_w):
    bsz, s, d = x.shape
    m = bsz * s
    n_heads = gdn_a_log.shape[1]
    hw = n_heads * HEAD_DIM
    n_experts = moe_router_w.shape[2]
    xr = x.reshape(m, d)

    hgrn_lb = jnp.cumsum(jax.nn.softmax(hgrn_lb_logits.astype(F32), axis=0), axis=0)
    w_in_t = jnp.swapaxes(mix_in_w[0], 0, 1)
    h = _rmsnorm(xr, norm_w[0, 0], BF16)
    proj_a = _mm_nt(h, w_in_t, F32, 1024, 1024, 0, 4 * hw).reshape(bsz, s, 4 * hw)
    proj_b = _mm_nt(h, w_in_t, F32, 1024, 1024, 4 * hw + 2 * n_heads, 4 * hw).reshape(bsz, s, 4 * hw)
    ba = _mm_nt(h, w_in_t, F32, 1024, LANES, 4 * hw, LANES).reshape(bsz, s, LANES)
    o_a = _gdn(proj_a, ba, gdn_conv_w[0].astype(F32), gdn_a_log[0], gdn_dt_bias[0], gdn_norm_w[0], col0=0)
    o_b = _hgrn(proj_b, hgrn_lb[0], hgrn_norm_w[0], col0=0, n_heads=n_heads)
    xr = _mm_resid([o_a.reshape(m, hw), o_b.reshape(m, hw)], mix_out_w[0], xr, 1024, 1024)
    h = _rmsnorm(xr, norm_w[0, 1], BF16)
    act = _mm_swiglu(h, ffn_gate_up_w[0], 1024, 512)
    xr = _mm_resid([act], ffn_down_w[0], xr, 512, 512)

    h = _rmsnorm(xr, norm_w[1, 0], BF16)
    yx = _mm(h, rg_in_w[0], F32, 1024, 1024).reshape(bsz, s, -1)
    rec = _rglru(yx, rg_conv_w[0], rg_conv_b[0], rg_gate_a_w[0], rg_gate_a_b[0],
                 rg_gate_x_w[0], rg_gate_x_b[0], rg_lambda[0])
    xr = _mm_resid([rec.reshape(m, d)], rg_out_w[0], xr, 1024, 1024)
    route = _norm_router(xr, norm_w[1, 1], moe_router_w[0])
    (src_token, chunk_expert, chunk_nq, n_valid, pos,
     unit_expert, unit_nq, unit_blk) = _moe_routing(route, n_experts)
    hs = _gather_norm(src_token, chunk_nq, xr, norm_w[1, 1])
    act = _gmm_swiglu(chunk_expert, chunk_nq, n_valid, hs, moe_gate_up_w[0], 512)
    ys = _gmm_down(unit_expert, unit_nq, unit_blk, act, moe_down_w[0], 256)
    out = _combine_norm(pos, xr, route, ys, final_norm_w)
    return out.reshape(bsz, s, d)
```

```python
import functools

import jax
import jax.numpy as jnp
from jax import lax
from jax.experimental import pallas as pl
from jax.experimental.pallas import tpu as pltpu

F32 = jnp.float32
BF16 = jnp.bfloat16
HIGHEST = lax.Precision.HIGHEST

EPS = 1e-6
CHUNK = 64
CONV_W = 4
HEAD_DIM = 128
RG_BLOCK = 256
RG_C = 8.0
TOP_K = 2
LANES = 128
SUBLANES = 8
VMEM_LIMIT = 56 * 1024 * 1024
VMEM_LIMIT_MAX = 60 * 1024 * 1024
MOE_CHUNK = 1024
MOE_SUB = 128

_NT = (((1,), (1,)), ((), ()))
_TN = (((0,), (0,)), ((), ()))


def _cparams(sem):
    return pltpu.CompilerParams(dimension_semantics=sem, vmem_limit_bytes=VMEM_LIMIT)


def _sigmoid(x):
    return jax.nn.sigmoid(x)


def _silu(x):
    return x * _sigmoid(x)


def _softplus(x):
    return jnp.maximum(x, 0.0) + jnp.log1p(jnp.exp(-jnp.abs(x)))


def _bdot(a, b):
    return jnp.dot(a.astype(BF16), b.astype(BF16), preferred_element_type=F32)


def _bdot_g(a, b, dims):
    return lax.dot_general(a.astype(BF16), b.astype(BF16), dims, preferred_element_type=F32)


def _rmsnorm_kernel(x_ref, w_ref, o_ref):
    x = x_ref[...]
    y = x * lax.rsqrt(jnp.mean(x * x, axis=-1, keepdims=True) + EPS)
    o_ref[...] = (y * w_ref[...]).astype(o_ref.dtype)


def _rmsnorm(x, w, out_dtype, tm=512):
    m, d = x.shape
    return pl.pallas_call(
        _rmsnorm_kernel,
        grid=(m // tm,),
        in_specs=[pl.BlockSpec((tm, d), lambda i: (i, 0)),
                  pl.BlockSpec((1, d), lambda i: (0, 0))],
        out_specs=pl.BlockSpec((tm, d), lambda i: (i, 0)),
        out_shape=jax.ShapeDtypeStruct((m, d), out_dtype),
        compiler_params=_cparams(("parallel",)),
        name="rmsnorm",
    )(x, w.reshape(1, d).astype(F32))


def _bf16_weight(w_ref, w_s, fresh):
    if not fresh:
        return w_s[...]
    w = w_ref[...].astype(BF16)
    w_s[...] = w
    return w


def _first_row_tile_or_not(body):
    first = pl.program_id(1) == 0
    pl.when(first)(lambda: body(True))
    pl.when(jnp.logical_not(first))(lambda: body(False))


def _mm_kernel(a_ref, w_ref, o_ref, w_s):
    def body(fresh):
        w = _bf16_weight(w_ref, w_s, fresh)
        o_ref[...] = jnp.dot(a_ref[...], w, preferred_element_type=F32).astype(o_ref.dtype)

    _first_row_tile_or_not(body)


def _mm_nt_kernel(*refs, shift):
    a_ref, w_ref = refs[0], refs[1]
    o_ref, w_s = refs[-2], refs[-1]

    def body(fresh):
        if fresh:
            w = w_ref[...]
            if shift:
                w = jnp.concatenate([w, refs[2][...]], axis=0)[shift:shift + w_ref.shape[0], :]
            w = w.astype(BF16)
            w_s[...] = w
        else:
            w = w_s[...]
        o_ref[...] = lax.dot_general(a_ref[...], w, _NT, preferred_element_type=F32).astype(o_ref.dtype)

    _first_row_tile_or_not(body)


def _mm_nt(a, wt, out_dtype, tm, tn, row0, n_rows):
    m, k = a.shape
    base, shift = row0 // tn * tn, row0 % tn
    assert shift % SUBLANES == 0 and shift <= LANES and n_rows % tn == 0
    in_specs = [pl.BlockSpec((tm, k), lambda j, i: (i, 0)),
                pl.BlockSpec((tn, k), lambda j, i: (base // tn + j, 0))]
    args = [a, wt]
    if shift:
        in_specs.append(pl.BlockSpec((LANES, k), lambda j, i: ((base + (j + 1) * tn) // LANES, 0)))
        args.append(wt)
    return pl.pallas_call(
        functools.partial(_mm_nt_kernel, shift=shift),
        grid=(n_rows // tn, m // tm),
        in_specs=in_specs,
        out_specs=pl.BlockSpec((tm, tn), lambda j, i: (i, j)),
        out_shape=jax.ShapeDtypeStruct((m, n_rows), out_dtype),
        scratch_shapes=[pltpu.VMEM((tn, k), BF16)],
        compiler_params=_cparams(("arbitrary", "arbitrary")),
        name="mm_nt",
    )(*args)


def _mm(a, w, out_dtype, tm, tn, n_cols=None, col0=0):
    m, k = a.shape
    n = w.shape[1] if n_cols is None else n_cols
    cb = col0 // tn
    return pl.pallas_call(
        _mm_kernel,
        grid=(n // tn, m // tm),
        in_specs=[pl.BlockSpec((tm, k), lambda j, i: (i, 0)),
                  pl.BlockSpec((k, tn), lambda j, i: (0, cb + j))],
        out_specs=pl.BlockSpec((tm, tn), lambda j, i: (i, j)),
        out_shape=jax.ShapeDtypeStruct((m, n), out_dtype),
        scratch_shapes=[pltpu.VMEM((k, tn), BF16)],
        compiler_params=_cparams(("arbitrary", "arbitrary")),
        name="mm",
    )(a, w)


def _mm_swiglu_kernel(a_ref, wg_ref, wu_ref, o_ref, wg_s, wu_s):
    def body(fresh):
        a = a_ref[...]
        g = jnp.dot(a, _bf16_weight(wg_ref, wg_s, fresh), preferred_element_type=F32)
        u = jnp.dot(a, _bf16_weight(wu_ref, wu_s, fresh), preferred_element_type=F32)
        o_ref[...] = (_silu(g) * u).astype(o_ref.dtype)

    _first_row_tile_or_not(body)


def _mm_swiglu(a, w_gu, tm, tn):
    m, k = a.shape
    f = w_gu.shape[1] // 2
    nb = f // tn
    return pl.pallas_call(
        _mm_swiglu_kernel,
        grid=(nb, m // tm),
        in_specs=[pl.BlockSpec((tm, k), lambda j, i: (i, 0)),
                  pl.BlockSpec((k, tn), lambda j, i: (0, j)),
                  pl.BlockSpec((k, tn), lambda j, i: (0, j + nb))],
        out_specs=pl.BlockSpec((tm, tn), lambda j, i: (i, j)),
        out_shape=jax.ShapeDtypeStruct((m, f), BF16),
        scratch_shapes=[pltpu.VMEM((k, tn), BF16)] * 2,
        compiler_params=_cparams(("arbitrary", "arbitrary")),
        name="mm_swiglu",
    )(a, w_gu, w_gu)


def _mm_resid_kernel(*refs, n_pairs):
    r_ref, o_ref = refs[2 * n_pairs], refs[2 * n_pairs + 1]
    w_s = refs[2 * n_pairs + 2:]

    def body(fresh):
        acc = r_ref[...]
        for p in range(n_pairs):
            w = _bf16_weight(refs[2 * p + 1], w_s[p], fresh)
            acc = acc + jnp.dot(refs[2 * p][...], w, preferred_element_type=F32)
        o_ref[...] = acc

    _first_row_tile_or_not(body)


def _mm_resid(a_list, w, resid, tm, tn):
    m, n = resid.shape
    in_specs, args, scratch = [], [], []
    for p, a in enumerate(a_list):
        k = a.shape[1]
        in_specs += [pl.BlockSpec((tm, k), lambda j, i: (i, 0)),
                     pl.BlockSpec((k, tn), lambda j, i, p=p: (p, j))]
        args += [a, w]
        scratch.append(pltpu.VMEM((k, tn), BF16))
    in_specs.append(pl.BlockSpec((tm, tn), lambda j, i: (i, j)))
    args.append(resid)
    return pl.pallas_call(
        functools.partial(_mm_resid_kernel, n_pairs=len(a_list)),
        grid=(n // tn, m // tm),
        in_specs=in_specs,
        out_specs=pl.BlockSpec((tm, tn), lambda j, i: (i, j)),
        out_shape=jax.ShapeDtypeStruct((m, n), F32),
        scratch_shapes=scratch,
        compiler_params=_cparams(("arbitrary", "arbitrary")),
        name="mm_resid",
    )(*args)


def _causal_conv(buf_ref, x, cw, t_len):
    buf_ref[pl.ds(SUBLANES, t_len), :] = x
    acc = cw[CONV_W - 1:CONV_W, :] * x
    for j in range(CONV_W - 1):
        off = SUBLANES - (CONV_W - 1) + j
        acc = acc + cw[j:j + 1, :] * buf_ref[pl.ds(off, t_len), :]
    buf_ref[pl.ds(0, SUBLANES), :] = x[t_len - SUBLANES:, :]
    return acc


def _tri_masks():
    row = lax.broadcasted_iota(jnp.int32, (CHUNK, CHUNK), 0)
    col = lax.broadcasted_iota(jnp.int32, (CHUNK, CHUNK), 1)
    return row >= col, row > col, row == col


def _gated_head_norm(o, nw, z):
    o = o * lax.rsqrt(jnp.mean(o * o, axis=-1, keepdims=True) + EPS) * nw
    return o * _silu(z)


def _gdn_kernel(q_ref, k_ref, v_ref, z_ref, ba_ref, cwq_ref, cwk_ref, cwv_ref, alog_ref, dtb_ref,
                nw_ref, o_ref, qbuf, kbuf, vbuf, qs, ks, vs, gcs, gct, bs, u_s, w_s, qk_s, qd_s, kd_s, s_ref,
                *, n_heads, t_len):
    t = pl.program_id(1)
    n_chunks = t_len // CHUNK
    width = n_heads * HEAD_DIM

    @pl.when(t == 0)
    def _():
        zeros = jnp.zeros((SUBLANES, width), F32)
        qbuf[pl.ds(0, SUBLANES), :] = zeros
        kbuf[pl.ds(0, SUBLANES), :] = zeros
        vbuf[pl.ds(0, SUBLANES), :] = zeros
        s_ref[...] = jnp.zeros_like(s_ref)

    qc = _silu(_causal_conv(qbuf, q_ref[0], cwq_ref[...], t_len))
    kc = _silu(_causal_conv(kbuf, k_ref[0], cwk_ref[...], t_len))
    vs[...] = _silu(_causal_conv(vbuf, v_ref[0], cwv_ref[...], t_len))
    for h in range(n_heads):
        sl = slice(h * HEAD_DIM, (h + 1) * HEAD_DIM)
        qh = qc[:, sl]
        kh = kc[:, sl]
        qs[:, sl] = qh * (lax.rsqrt(jnp.sum(qh * qh, axis=-1, keepdims=True) + EPS) * (HEAD_DIM ** -0.5))
        ks[:, sl] = kh * lax.rsqrt(jnp.sum(kh * kh, axis=-1, keepdims=True) + EPS)

    ba = ba_ref[0]
    g_log = -jnp.exp(alog_ref[...]) * _softplus(ba + dtb_ref[...])
    bs[...] = _sigmoid(ba)
    row = lax.broadcasted_iota(jnp.int32, (t_len, t_len), 0)
    col = lax.broadcasted_iota(jnp.int32, (t_len, t_len), 1)
    block_tri = ((row // CHUNK == col // CHUNK) & (row >= col)).astype(F32)
    gc_blk = jnp.dot(block_tri, g_log, precision=HIGHEST, preferred_element_type=F32)
    gcs[...] = gc_blk
    for c in range(n_chunks):
        gct[c] = gc_blk[c * CHUNK:(c + 1) * CHUNK, :].T

    incl, strict, eye = _tri_masks()
    eye_f = eye.astype(F32)
    nw = nw_ref[...]

    heads = range(n_heads)
    hsl = [slice(h * HEAD_DIM, (h + 1) * HEAD_DIM) for h in heads]
    qsl = [slice(h * HEAD_DIM, h * HEAD_DIM + CHUNK) for h in heads]
    chunks_per_iter = 2
    assert n_chunks % chunks_per_iter == 0

    def pass_a(it, carry):
        items = []
        for cc in range(chunks_per_iter):
            c = it * chunks_per_iter + cc
            rows = pl.ds(pl.multiple_of(c * CHUNK, CHUNK), CHUNK)
            gc_all = gcs[rows, :]
            beta_all = bs[rows, :]
            gct_c = gct[c]
            for h in heads:
                gc = gc_all[:, n_heads + h:n_heads + h + 1]
                beta = beta_all[:, h:h + 1]
                gc_row = gct_c[n_heads + h:n_heads + h + 1, :]
                decay = jnp.where(incl, jnp.exp(jnp.where(incl, gc - gc_row, 0.0)), 0.0)
                items.append(dict(rows=rows, h=h, gc=gc, beta=beta, decay=decay,
                                  q=qs[rows, hsl[h]], k=ks[rows, hsl[h]]))
        for it_ in items:
            it_["kb"] = it_["k"].astype(BF16)
            it_["kk"] = lax.dot_general(it_["kb"], it_["kb"], _NT, preferred_element_type=F32)
        for it_ in items:
            it_["x"] = jnp.where(strict, -(it_["beta"] * it_["kk"] * it_["decay"]), 0.0)
            it_["p"] = eye_f + it_["x"]
        for _ in range(5):
            for it_ in items:
                xb = it_["x"].astype(BF16)
                it_["x"] = jnp.dot(xb, xb, preferred_element_type=F32)
            for it_ in items:
                it_["p"] = it_["p"] + _bdot(it_["p"], it_["x"])
        for it_ in items:
            egc = jnp.exp(it_["gc"])
            it_["egc"] = egc
            v = vs[it_["rows"], hsl[it_["h"]]]
            rhs = jnp.concatenate([v * it_["beta"], it_["k"] * (it_["beta"] * egc)], axis=-1)
            it_["sol"] = _bdot(it_["p"], rhs)
        for it_ in items:
            it_["qk"] = lax.dot_general(it_["q"].astype(BF16), it_["kb"], _NT,
                                        preferred_element_type=F32) * it_["decay"]
        for it_ in items:
            rows, h, gc = it_["rows"], it_["h"], it_["gc"]
            g_last = gc[CHUNK - 1:CHUNK, :]
            u_s[rows, hsl[h]] = it_["sol"][:, :HEAD_DIM]
            w_s[rows, hsl[h]] = it_["sol"][:, HEAD_DIM:].astype(BF16)
            qk_s[rows, qsl[h]] = it_["qk"].astype(BF16)
            qd_s[rows, hsl[h]] = (it_["q"] * it_["egc"]).astype(BF16)
            kd_s[rows, hsl[h]] = (it_["k"] * jnp.exp(g_last - gc)).astype(BF16)
        return carry

    lax.fori_loop(0, n_chunks // chunks_per_iter, pass_a, 0)

    def pass_b(c, carry):
        rows = pl.ds(pl.multiple_of(c * CHUNK, CHUNK), CHUNK)
        gc_all = gcs[rows, :]
        states = [s_ref[h] for h in heads]
        sbs = [s.astype(BF16) for s in states]
        v_new = [u_s[rows, hsl[h]] - jnp.dot(w_s[rows, hsl[h]], sbs[h], preferred_element_type=F32)
                 for h in heads]
        vbs = [v.astype(BF16) for v in v_new]
        for h in heads:
            g_last = gc_all[CHUNK - 1:CHUNK, n_heads + h:n_heads + h + 1]
            s_ref[h] = states[h] * jnp.exp(g_last) + lax.dot_general(kd_s[rows, hsl[h]], vbs[h], _TN,
                                                                      preferred_element_type=F32)
        outs = [jnp.dot(qd_s[rows, hsl[h]], sbs[h], preferred_element_type=F32)
                + jnp.dot(qk_s[rows, qsl[h]], vbs[h], preferred_element_type=F32) for h in heads]
        for h in heads:
            o = _gated_head_norm(outs[h], nw, z_ref[0, rows, hsl[h]])
            o_ref[0, rows, hsl[h]] = o.astype(o_ref.dtype)
        return carry

    lax.fori_loop(0, n_chunks, pass_b, 0)


def _gdn(proj, ba, conv_w, a_log, dt_bias, norm_w, *, col0, t_len=256):
    bsz, s, _ = proj.shape
    n_heads = a_log.shape[0]
    width = n_heads * HEAD_DIM
    cb = col0 // width

    def col_spec(group):
        return pl.BlockSpec((1, t_len, width), lambda b, t: (b, t, cb + group))

    def cw_spec(group):
        return pl.BlockSpec((CONV_W, width), lambda b, t: (0, group))

    row = jnp.zeros((1, LANES), F32)
    alog_row = row.at[0, n_heads:2 * n_heads].set(a_log.astype(F32))
    dtb_row = row.at[0, n_heads:2 * n_heads].set(dt_bias.astype(F32))
    small = pl.BlockSpec((1, LANES), lambda b, t: (0, 0))
    kern = functools.partial(_gdn_kernel, n_heads=n_heads, t_len=t_len)
    return pl.pallas_call(
        kern,
        grid=(bsz, s // t_len),
        in_specs=[col_spec(0), col_spec(1), col_spec(2), col_spec(3),
                  pl.BlockSpec((1, t_len, LANES), lambda b, t: (b, t, 0)),
                  cw_spec(0), cw_spec(1), cw_spec(2), small, small,
                  pl.BlockSpec((1, HEAD_DIM), lambda b, t: (0, 0))],
        out_specs=pl.BlockSpec((1, t_len, width), lambda b, t: (b, t, 0)),
        out_shape=jax.ShapeDtypeStruct((bsz, s, width), BF16),
        scratch_shapes=[pltpu.VMEM((SUBLANES + t_len, width), F32)] * 3
        + [pltpu.VMEM((t_len, width), F32)] * 3
        + [pltpu.VMEM((t_len, LANES), F32),
           pltpu.VMEM((t_len // CHUNK, LANES, CHUNK), F32),
           pltpu.VMEM((t_len, LANES), F32),
           pltpu.VMEM((t_len, width), F32)]
        + [pltpu.VMEM((t_len, width), BF16)] * 4
        + [pltpu.VMEM((n_heads, HEAD_DIM, HEAD_DIM), F32)],
        compiler_params=_cparams(("parallel", "arbitrary")),
        name="gdn",
    )(proj, proj, proj, proj, ba, conv_w, conv_w, conv_w, alog_row, dtb_row,
      norm_w.reshape(1, HEAD_DIM).astype(F32))


def _hgrn_kernel(q_ref, f_ref, i_ref, g_ref, lb_ref, nw_ref, o_ref, qs, ks, ls, st_ref, *, heads, t_len):
    t = pl.program_id(1)

    @pl.when(t == 0)
    def _():
        st_ref[...] = jnp.zeros_like(st_ref)

    lb = lb_ref[...]
    forget = lb + (1.0 - lb) * _sigmoid(f_ref[0])
    ks[...] = 1.0 - forget
    qs[...] = _silu(q_ref[0]) * (HEAD_DIM ** -0.5)
    row = lax.broadcasted_iota(jnp.int32, (t_len, t_len), 0)
    col = lax.broadcasted_iota(jnp.int32, (t_len, t_len), 1)
    block_tri = ((row // CHUNK == col // CHUNK) & (row >= col)).astype(BF16)
    log_f = jnp.log(forget)
    hi = log_f.astype(BF16)
    rest = log_f - hi.astype(F32)
    mid = rest.astype(BF16)
    lo = (rest - mid.astype(F32)).astype(BF16)
    ls[...] = (jnp.dot(block_tri, hi, preferred_element_type=F32)
               + (jnp.dot(block_tri, mid, preferred_element_type=F32)
                  + jnp.dot(block_tri, lo, preferred_element_type=F32)))

    incl, _, _ = _tri_masks()
    nw = nw_ref[...]
    mid = CHUNK // 2 - 1

    hsl = [slice(j * HEAD_DIM, (j + 1) * HEAD_DIM) for j in range(heads)]

    def chunk_body(c, carry):
        rows = pl.ds(pl.multiple_of(c * CHUNK, CHUNK), CHUNK)
        items = []
        for j in range(heads):
            b = ls[rows, hsl[j]]
            items.append(dict(b=b, b_mid=b[mid:mid + 1, :], b_last=b[CHUNK - 1:CHUNK, :],
                              q=qs[rows, hsl[j]], k=ks[rows, hsl[j]],
                              vb=i_ref[0, rows, hsl[j]].astype(BF16)))
        for it in items:
            it["att"] = jnp.where(incl, _bdot_g(it["q"] * jnp.exp(it["b"] - it["b_mid"]),
                                                it["k"] * jnp.exp(it["b_mid"] - it["b"]), _NT), 0.0)
        for j, it in enumerate(items):
            st = st_ref[j]
            it["o"] = _bdot_g(it["q"] * jnp.exp(it["b"]), st, _NT)
            st_ref[j] = st * jnp.exp(it["b_last"]) + lax.dot_general(
                it["vb"], (it["k"] * jnp.exp(it["b_last"] - it["b"])).astype(BF16), _TN,
                preferred_element_type=F32)
        for it in items:
            it["o"] = it["o"] + jnp.dot(it["att"].astype(BF16), it["vb"], preferred_element_type=F32)
        for j, it in enumerate(items):
            o = _gated_head_norm(it["o"], nw, g_ref[0, rows, hsl[j]])
            o_ref[0, rows, hsl[j]] = o.astype(o_ref.dtype)
        return carry

    lax.fori_loop(0, t_len // CHUNK, chunk_body, 0)


def _hgrn(proj, lb, norm_w, *, col0, n_heads, t_len=256):
    bsz, s, _ = proj.shape
    width = n_heads * HEAD_DIM
    cb = col0 // width

    def col_spec(group):
        return pl.BlockSpec((1, t_len, width), lambda b, t: (b, t, cb + group))

    kern = functools.partial(_hgrn_kernel, heads=n_heads, t_len=t_len)
    return pl.pallas_call(
        kern,
        grid=(bsz, s // t_len),
        in_specs=[col_spec(0), col_spec(1), col_spec(2), col_spec(3),
                  pl.BlockSpec((1, width), lambda b, t: (0, 0)),
                  pl.BlockSpec((1, HEAD_DIM), lambda b, t: (0, 0))],
        out_specs=pl.BlockSpec((1, t_len, width), lambda b, t: (b, t, 0)),
        out_shape=jax.ShapeDtypeStruct((bsz, s, width), BF16),
        scratch_shapes=[pltpu.VMEM((t_len, width), F32)] * 3
        + [pltpu.VMEM((n_heads, HEAD_DIM, HEAD_DIM), F32)],
        compiler_params=_cparams(("parallel", "arbitrary")),
        name="hgrn2",
    )(proj, proj, proj, proj, lb.reshape(1, width).astype(F32), norm_w.reshape(1, HEAD_DIM).astype(F32))


def _rglru_kernel(y_ref, x_ref, cw_ref, cb_ref, wa_ref, ba_ref, wx_ref, bx_ref, lam_ref, o_ref,
                  xbuf, a_s, b_s, h_s, hcar, *, blocks, t_len):
    t = pl.program_id(2)

    @pl.when(t == 0)
    def _():
        xbuf[pl.ds(0, SUBLANES), :] = jnp.zeros((SUBLANES, blocks * RG_BLOCK), F32)
        hcar[...] = jnp.zeros_like(hcar)

    xc = _causal_conv(xbuf, x_ref[0], cw_ref[...], t_len) + cb_ref[...]
    for n in range(blocks):
        sl = slice(n * RG_BLOCK, (n + 1) * RG_BLOCK)
        xb = xc[:, sl]
        r = _sigmoid(_bdot(xb, wa_ref[n]) + ba_ref[:, sl])
        gi = _sigmoid(_bdot(xb, wx_ref[n]) + bx_ref[:, sl])
        log_a = (-RG_C) * r * _softplus(-lam_ref[:, sl])
        a = jnp.exp(log_a)
        a_s[:, sl] = a
        one_minus_a2 = -jnp.tanh(log_a) * (a * a + 1.0)
        b_s[:, sl] = jnp.sqrt(jnp.maximum(one_minus_a2, 0.0)) * (gi * xb)

    def row_body(i, h):
        h = a_s[pl.ds(i, 1), :] * h + b_s[pl.ds(i, 1), :]
        h_s[pl.ds(i, 1), :] = h
        return h

    hcar[...] = lax.fori_loop(0, t_len, row_body, hcar[...])
    o_ref[0] = (jax.nn.gelu(y_ref[0], approximate=True) * h_s[...]).astype(o_ref.dtype)


def _rglru(yx, conv_w, conv_b, wa, ba, wx, bx, lam, *, blocks_per_step=8, t_len=256):
    bsz, s, w2 = yx.shape
    width = w2 // 2
    gw = blocks_per_step * RG_BLOCK
    n_g = width // gw
    vec = lambda a: a.reshape(1, width).astype(F32)
    vspec = pl.BlockSpec((1, gw), lambda b, g, t: (0, g))
    wspec = pl.BlockSpec((blocks_per_step, RG_BLOCK, RG_BLOCK), lambda b, g, t: (g, 0, 0))
    kern = functools.partial(_rglru_kernel, blocks=blocks_per_step, t_len=t_len)
    return pl.pallas_call(
        kern,
        grid=(bsz, n_g, s // t_len),
        in_specs=[pl.BlockSpec((1, t_len, gw), lambda b, g, t: (b, t, g)),
                  pl.BlockSpec((1, t_len, gw), lambda b, g, t: (b, t, n_g + g)),
                  pl.BlockSpec((CONV_W, gw), lambda b, g, t: (0, g)),
                  vspec, wspec, vspec, wspec, vspec, vspec],
        out_specs=pl.BlockSpec((1, t_len, gw), lambda b, g, t: (b, t, g)),
        out_shape=jax.ShapeDtypeStruct((bsz, s, width), BF16),
        scratch_shapes=[pltpu.VMEM((SUBLANES + t_len, gw), F32)]
        + [pltpu.VMEM((t_len, gw), F32)] * 3
        + [pltpu.VMEM((1, gw), F32)],
        compiler_params=_cparams(("parallel", "parallel", "arbitrary")),
        name="rglru",
    )(yx, yx, conv_w.astype(F32), vec(conv_b), wa.astype(BF16), vec(ba), wx.astype(BF16), vec(bx), vec(lam))


def _norm_router_kernel(x_ref, nw_ref, w_ref, o_ref, *, n_experts):
    x = x_ref[...]
    y = x * lax.rsqrt(jnp.mean(x * x, axis=-1, keepdims=True) + EPS) * nw_ref[...]
    w = w_ref[...]
    y_hi = y.astype(BF16)
    y_lo = (y - y_hi.astype(F32)).astype(BF16)
    w_hi = w.astype(BF16)
    w_lo = (w - w_hi.astype(F32)).astype(BF16)
    logits = (jnp.dot(y_hi, w_hi, preferred_element_type=F32)
              + (jnp.dot(y_lo, w_hi, preferred_element_type=F32)
                 + jnp.dot(y_hi, w_lo, preferred_element_type=F32)))
    lane = lax.broadcasted_iota(jnp.int32, logits.shape, 1).astype(F32)
    neg = jnp.float32(-jnp.inf)
    l1 = jnp.where(lane < n_experts, logits, neg)
    m1 = jnp.max(l1, axis=-1, keepdims=True)
    i1 = jnp.min(jnp.where(l1 == m1, lane, float(LANES)), axis=-1, keepdims=True)
    l2 = jnp.where(lane == i1, neg, l1)
    m2 = jnp.max(l2, axis=-1, keepdims=True)
    i2 = jnp.min(jnp.where(l2 == m2, lane, float(LANES)), axis=-1, keepdims=True)
    e2 = jnp.exp(m2 - m1)
    g1 = 1.0 / (1.0 + e2)
    g2 = e2 / (1.0 + e2)
    out = jnp.where(lane == 0, i1, 0.0)
    out = jnp.where(lane == 1, i2, out)
    out = jnp.where(lane == 2, g1, out)
    out = jnp.where(lane == 3, g2, out)
    o_ref[...] = out


def _norm_router(x, nw, router_w, tm=512):
    m, d = x.shape
    n_experts = router_w.shape[1]
    w = jnp.zeros((d, LANES), F32).at[:, :n_experts].set(router_w.astype(F32))
    return pl.pallas_call(
        functools.partial(_norm_router_kernel, n_experts=n_experts),
        grid=(m // tm,),
        in_specs=[pl.BlockSpec((tm, d), lambda i: (i, 0)),
                  pl.BlockSpec((1, d), lambda i: (0, 0)),
                  pl.BlockSpec((d, LANES), lambda i: (0, 0))],
        out_specs=pl.BlockSpec((tm, LANES), lambda i: (i, 0)),
        out_shape=jax.ShapeDtypeStruct((m, LANES), F32),
        compiler_params=_cparams(("parallel",)),
        name="norm_router",
    )(x, nw.reshape(1, d).astype(F32), w)


def _start_row_gather(idx_ref, base, src3, dst3, sem, n_rows):
    def body(g, carry):
        for u in range(SUBLANES):
            row = idx_ref[base + g * SUBLANES + u]
            src = src3.at[lax.shift_right_logical(row, 3), pl.ds(row & (SUBLANES - 1), 1), :]
            pltpu.make_async_copy(src, dst3.at[g, pl.ds(u, 1), :], sem).start(priority=u % 2)
        return carry
    lax.fori_loop(0, n_rows // SUBLANES, body, 0)


def _wait_row_gather(src3, dst3, sem, n_rows):
    n = n_rows // SUBLANES
    pltpu.make_async_copy(src3.at[pl.ds(0, n)], dst3.at[pl.ds(0, n)], sem).wait()


def _gather_norm_kernel(idx_ref, nq_ref, x_hbm, nw_ref, o_ref, buf, sem):
    c = pl.program_id(0)
    slot = c % 2

    def start(chunk, s):
        _start_row_gather(idx_ref, chunk * MOE_CHUNK, x_hbm, buf.at[s], sem.at[s], nq_ref[chunk] * MOE_SUB)

    @pl.when(c == 0)
    def _():
        start(0, 0)

    @pl.when(c + 1 < pl.num_programs(0))
    def _():
        start(c + 1, 1 - slot)

    for sb in range(MOE_CHUNK // MOE_SUB):
        @pl.when(sb < nq_ref[c])
        def _():
            _wait_row_gather(x_hbm, buf.at[slot], sem.at[slot], MOE_SUB)

    for sb in range(MOE_CHUNK // MOE_SUB):
        rows = pl.ds(sb * MOE_SUB, MOE_SUB)

        @pl.when(sb < nq_ref[c])
        def _(sb=sb, rows=rows):
            tiles = pl.ds(sb * (MOE_SUB // SUBLANES), MOE_SUB // SUBLANES)
            x = buf[slot, tiles].reshape(MOE_SUB, o_ref.shape[1])
            y = x * lax.rsqrt(jnp.mean(x * x, axis=-1, keepdims=True) + EPS) * nw_ref[...]
            o_ref[rows, :] = y.astype(o_ref.dtype)

        @pl.when(sb >= nq_ref[c])
        def _():
            o_ref[rows, :] = jnp.zeros((MOE_SUB, o_ref.shape[1]), o_ref.dtype)


def _gather_norm(src_token, chunk_nq, x, nw):
    rows = src_token.shape[0]
    d = x.shape[1]
    grid_spec = pltpu.PrefetchScalarGridSpec(
        num_scalar_prefetch=2,
        grid=(rows // MOE_CHUNK,),
        in_specs=[pl.BlockSpec(memory_space=pl.ANY),
                  pl.BlockSpec((1, d), lambda c, idx, nq: (0, 0))],
        out_specs=pl.BlockSpec((MOE_CHUNK, d), lambda c, idx, nq: (c, 0)),
        scratch_shapes=[pltpu.VMEM((2, MOE_CHUNK // SUBLANES, SUBLANES, d), F32),
                        pltpu.SemaphoreType.DMA((2,))],
    )
    return pl.pallas_call(
        _gather_norm_kernel,
        grid_spec=grid_spec,
        out_shape=jax.ShapeDtypeStruct((rows, d), BF16),
        compiler_params=_cparams(("arbitrary",)),
        name="gather_norm",
    )(src_token, chunk_nq, x.reshape(-1, SUBLANES, d), nw.reshape(1, d).astype(F32))


def _for_valid_rows(nq, o_ref, fn, pred=True):
    total = o_ref.shape[0]
    for v in range(total // MOE_SUB + 1):
        @pl.when(jnp.logical_and(nq == v, pred))
        def _(v=v):
            rows = v * MOE_SUB
            if rows:
                fn(rows)
            if rows < total:
                o_ref[pl.ds(rows, total - rows), :] = jnp.zeros((total - rows, o_ref.shape[1]), o_ref.dtype)


def _gmm_swiglu_kernel(te_ref, nq_ref, nv_ref, a_ref, wg_ref, wu_ref, o_ref):
    c = pl.program_id(0)

    def compute(rows):
        a = a_ref[pl.ds(0, rows), :]
        g = jnp.dot(a, wg_ref[0].astype(BF16), preferred_element_type=F32)
        u = jnp.dot(a, wu_ref[0].astype(BF16), preferred_element_type=F32)
        o_ref[pl.ds(0, rows), :] = (_silu(g) * u).astype(o_ref.dtype)

    _for_valid_rows(nq_ref[c], o_ref, compute)


def _gmm_swiglu(chunk_expert, chunk_nq, n_valid, a, w_gu, tn):
    m, k = a.shape
    f = w_gu.shape[2] // 2
    nb = f // tn
    last = lambda c, nv: jnp.minimum(c, nv[0] - 1)
    col = lambda j, c, nv: jnp.where(c < nv[0], j, nb - 1)
    grid_spec = pltpu.PrefetchScalarGridSpec(
        num_scalar_prefetch=3,
        grid=(m // MOE_CHUNK, nb),
        in_specs=[pl.BlockSpec((MOE_CHUNK, k), lambda c, j, te, nq, nv: (last(c, nv), 0)),
                  pl.BlockSpec((1, k, tn), lambda c, j, te, nq, nv: (te[c], 0, col(j, c, nv))),
                  pl.BlockSpec((1, k, tn), lambda c, j, te, nq, nv: (te[c], 0, col(j, c, nv) + nb))],
        out_specs=pl.BlockSpec((MOE_CHUNK, tn), lambda c, j, te, nq, nv: (c, j)),
    )
    return pl.pallas_call(
        _gmm_swiglu_kernel,
        grid_spec=grid_spec,
        out_shape=jax.ShapeDtypeStruct((m, f), BF16),
        compiler_params=_cparams(("arbitrary", "arbitrary")),
        name="gmm_swiglu",
    )(chunk_expert, chunk_nq, n_valid, a, w_gu, w_gu)


def _gmm_down_kernel(te_ref, nq_ref, nv_ref, a_ref, w_ref, o_ref):
    c = pl.program_id(0)

    def compute(rows):
        w = w_ref[0].astype(BF16)
        o_ref[pl.ds(0, rows), :] = jnp.dot(a_ref[pl.ds(0, rows), :], w, preferred_element_type=F32)

    _for_valid_rows(nq_ref[c], o_ref, compute)


def _gmm_down(chunk_expert, chunk_nq, n_valid, a, w_d, tn):
    m, k = a.shape
    n = w_d.shape[2]
    nj = n // tn
    last = lambda c, nv: jnp.minimum(c, nv[0] - 1)
    col = lambda j, c, nv: jnp.where(c < nv[0], j, nj - 1)
    grid_spec = pltpu.PrefetchScalarGridSpec(
        num_scalar_prefetch=3,
        grid=(m // MOE_CHUNK, nj),
        in_specs=[pl.BlockSpec((MOE_CHUNK, k), lambda c, j, te, nq, nv: (last(c, nv), 0)),
                  pl.BlockSpec((1, k, tn), lambda c, j, te, nq, nv: (te[c], 0, col(j, c, nv)))],
        out_specs=pl.BlockSpec((MOE_CHUNK, tn), lambda c, j, te, nq, nv: (c, j)),
    )
    return pl.pallas_call(
        _gmm_down_kernel,
        grid_spec=grid_spec,
        out_shape=jax.ShapeDtypeStruct((m, n), F32),
        compiler_params=pltpu.CompilerParams(dimension_semantics=("arbitrary", "arbitrary"),
                                             vmem_limit_bytes=VMEM_LIMIT_MAX),
        name="gmm_down",
    )(chunk_expert, chunk_nq, n_valid, a, w_d)


def _combine_norm_kernel(pos_ref, x_ref, r_ref, ys_hbm, w_ref, o_ref, buf, sem, *, tm):
    i = pl.program_id(0)
    n = pl.num_programs(0)
    slot = i % 2

    def start(tile, s):
        for kk in range(TOP_K):
            _start_row_gather(pos_ref, kk * (n * tm) + tile * tm, ys_hbm, buf.at[s, kk], sem.at[s], tm)

    @pl.when(i == 0)
    def _():
        start(0, 0)

    @pl.when(i + 1 < n)
    def _():
        start(i + 1, 1 - slot)

    for kk in range(TOP_K):
        _wait_row_gather(ys_hbm, buf.at[slot, kk], sem.at[slot], tm)
    r = r_ref[...]
    d = x_ref.shape[1]
    moe = r[:, TOP_K:TOP_K + 1] * buf[slot, 0].reshape(tm, d)
    for kk in range(1, TOP_K):
        moe = moe + r[:, TOP_K + kk:TOP_K + kk + 1] * buf[slot, kk].reshape(tm, d)
    x = x_ref[...] + moe
    y = x * lax.rsqrt(jnp.mean(x * x, axis=-1, keepdims=True) + EPS)
    o_ref[...] = y * w_ref[...]


def _combine_norm(pos, x, route, ys, w, tm=256):
    m, d = x.shape
    grid_spec = pltpu.PrefetchScalarGridSpec(
        num_scalar_prefetch=1,
        grid=(m // tm,),
        in_specs=[pl.BlockSpec((tm, d), lambda i, p: (i, 0)),
                  pl.BlockSpec((tm, LANES), lambda i, p: (i, 0)),
                  pl.BlockSpec(memory_space=pl.ANY),
                  pl.BlockSpec((1, d), lambda i, p: (0, 0))],
        out_specs=pl.BlockSpec((tm, d), lambda i, p: (i, 0)),
        scratch_shapes=[pltpu.VMEM((2, TOP_K, tm // SUBLANES, SUBLANES, d), F32),
                        pltpu.SemaphoreType.DMA((2,))],
    )
    return pl.pallas_call(
        functools.partial(_combine_norm_kernel, tm=tm),
        grid_spec=grid_spec,
        out_shape=jax.ShapeDtypeStruct((m, d), F32),
        compiler_params=_cparams(("arbitrary",)),
        name="combine_norm",
    )(pos, x, route, ys.reshape(-1, SUBLANES, d), w.reshape(1, d).astype(F32))


def _moe_routing(route, n_experts):
    m = route.shape[0]
    n_sub = MOE_CHUNK // MOE_SUB
    ids = route[:, :TOP_K].astype(jnp.int32)
    flat_e = ids.reshape(-1)
    onehot = (flat_e[:, None] == jnp.arange(n_experts)[None, :]).astype(jnp.int32)
    rank = jnp.sum((jnp.cumsum(onehot, axis=0) - onehot) * onehot, axis=1)
    counts = jnp.sum(onehot, axis=0)
    subs = (counts + MOE_SUB - 1) // MOE_SUB
    n_chunks = (subs + n_sub - 1) // n_sub
    chunk_end = jnp.cumsum(n_chunks)
    chunk_start = chunk_end - n_chunks
    max_chunks = (m * TOP_K // MOE_SUB + n_experts + n_sub - 1) // n_sub + n_experts
    rows = max_chunks * MOE_CHUNK
    base = subs // jnp.maximum(n_chunks, 1)
    extra = subs - base * n_chunks
    per_slot = lambda v: jnp.sum(onehot * v[None, :], axis=1)
    big_rows = (per_slot(base) + 1) * MOE_SUB
    small_rows = jnp.maximum(per_slot(base), 1) * MOE_SUB
    rest = rank - per_slot(extra) * big_rows
    chunk_off = jnp.where(rest < 0, rank // big_rows, per_slot(extra) + rest // small_rows)
    row_off = jnp.where(rest < 0, rank % big_rows, rest % small_rows)
    pos = (per_slot(chunk_start) + chunk_off) * MOE_CHUNK + row_off
    src_token = jnp.zeros((rows,), jnp.int32).at[pos].set(jnp.arange(m * TOP_K, dtype=jnp.int32) // TOP_K)
    n_valid = chunk_end[n_experts - 1].astype(jnp.int32)
    cidx = jnp.minimum(jnp.arange(max_chunks, dtype=jnp.int32), n_valid - 1)
    chunk_expert = jnp.sum((cidx[:, None] >= chunk_end[None, :]).astype(jnp.int32), axis=1).astype(jnp.int32)
    sel = (chunk_expert[:, None] == jnp.arange(n_experts)[None, :]).astype(jnp.int32)
    within = cidx - jnp.sum(sel * chunk_start[None, :], axis=1)
    chunk_nq = jnp.sum(sel * base[None, :], axis=1) + (within < jnp.sum(sel * extra[None, :], axis=1))
    chunk_nq = jnp.where(jnp.arange(max_chunks) < n_valid, chunk_nq, 0).astype(jnp.int32)
    pos_kmajor = pos.reshape(m, TOP_K).T.reshape(-1).astype(jnp.int32)
    return src_token, chunk_expert, chunk_nq, n_valid.reshape(1), pos_kmajor


def kernel(x, norm_w, final_norm_w, mix_in_w, gdn_conv_w, gdn_a_log, gdn_dt_bias, gdn_norm_w, hgrn_lb_logits, hgrn_norm_w, mix_out_w, ffn_gate_up_w, ffn_down_w, rg_in_w, rg_conv_w, rg_conv_b, rg_gate_a_w, rg_gate_a_b, rg_gate_x_w, rg_gate_x_b, rg_lambda, rg_out_w, moe_router_w, moe_gate_up_w, moe_down_w):
    bsz, s, d = x.shape
    m = bsz * s
    n_heads = gdn_a_log.shape[1]
    hw = n_heads * HEAD_DIM
    n_experts = moe_router_w.shape[2]
    xr = x.reshape(m, d)

    hgrn_lb = jnp.cumsum(jax.nn.softmax(hgrn_lb_logits.astype(F32), axis=0), axis=0)
    w_in_t = jnp.swapaxes(mix_in_w[0], 0, 1)
    h = _rmsnorm(xr, norm_w[0, 0], BF16)
    proj_a = _mm_nt(h, w_in_t, F32, 1024, 1024, 0, 4 * hw).reshape(bsz, s, 4 * hw)
    proj_b = _mm_nt(h, w_in_t, F32, 1024, 1024, 4 * hw + 2 * n_heads, 4 * hw).reshape(bsz, s, 4 * hw)
    ba = _mm_nt(h, w_in_t, F32, 1024, LANES, 4 * hw, LANES).reshape(bsz, s, LANES)
    o_a = _gdn(proj_a, ba, gdn_conv_w[0].astype(F32), gdn_a_log[0], gdn_dt_bias[0], gdn_norm_w[0], col0=0)
    o_b = _hgrn(proj_b, hgrn_lb[0], hgrn_norm_w[0], col0=0, n_heads=n_heads)
    xr = _mm_resid([o_a.reshape(m, hw), o_b.reshape(m, hw)], mix_out_w[0], xr, 1024, 1024)
    h = _rmsnorm(xr, norm_w[0, 1], BF16)
    act = _mm_swiglu(h, ffn_gate_up_w[0], 1024, 512)
    xr = _mm_resid([act], ffn_down_w[0], xr, 512, 512)

    h = _rmsnorm(xr, norm_w[1, 0], BF16)
    yx = _mm(h, rg_in_w[0], F32, 1024, 1024).reshape(bsz, s, -1)
    rec = _rglru(yx, rg_conv_w[0], rg_conv_b[0], rg_gate_a_w[0], rg_gate_a_b[0],
                 rg_gate_x_w[0], rg_gate_x_b[0], rg_lambda[0])
    xr = _mm_resid([rec.reshape(m, d)], rg_out_w[0], xr, 1024, 1024)
    route = _norm_router(xr, norm_w[1, 1], moe_router_w[0])
    src_token, chunk_expert, chunk_nq, n_valid, pos = _moe_routing(route, n_experts)
    hs = _gather_norm(src_token, chunk_nq, xr, norm_w[1, 1])
    act = _gmm_swiglu(chunk_expert, chunk_nq, n_valid, hs, moe_gate_up_w[0], 512)
    ys = _gmm_down(chunk_expert, chunk_nq, n_valid, act, moe_down_w[0], 256)
    out = _combine_norm(pos, xr, route, ys, final_norm_w)
    return out.reshape(bsz, s, d)
```

```python
import functools

import jax
import jax.numpy as jnp
from jax import lax
from jax.experimental import pallas as pl
from jax.experimental.pallas import tpu as pltpu

F32 = jnp.float32
BF16 = jnp.bfloat16
HIGHEST = lax.Precision.HIGHEST

EPS = 1e-6
CHUNK = 64
CONV_W = 4
HEAD_DIM = 128
RG_BLOCK = 256
RG_C = 8.0
TOP_K = 2
LANES = 128
SUBLANES = 8
VMEM_LIMIT = 56 * 1024 * 1024
VMEM_LIMIT_MAX = 60 * 1024 * 1024
MOE_CHUNK = 1024
MOE_SUB = 128

_NT = (((1,), (1,)), ((), ()))
_TN = (((0,), (0,)), ((), ()))


def _cparams(sem):
    return pltpu.CompilerParams(dimension_semantics=sem, vmem_limit_bytes=VMEM_LIMIT)


def _sigmoid(x):
    return jax.nn.sigmoid(x)


def _silu(x):
    return x * _sigmoid(x)


def _softplus(x):
    return jnp.maximum(x, 0.0) + jnp.log1p(jnp.exp(-jnp.abs(x)))


def _bdot(a, b):
    return jnp.dot(a.astype(BF16), b.astype(BF16), preferred_element_type=F32)


def _bdot_g(a, b, dims):
    return lax.dot_general(a.astype(BF16), b.astype(BF16), dims, preferred_element_type=F32)


def _rmsnorm_kernel(x_ref, w_ref, o_ref):
    x = x_ref[...]
    y = x * lax.rsqrt(jnp.mean(x * x, axis=-1, keepdims=True) + EPS)
    o_ref[...] = (y * w_ref[...]).astype(o_ref.dtype)


def _rmsnorm(x, w, out_dtype, tm=512):
    m, d = x.shape
    return pl.pallas_call(
        _rmsnorm_kernel,
        grid=(m // tm,),
        in_specs=[pl.BlockSpec((tm, d), lambda i: (i, 0)),
                  pl.BlockSpec((1, d), lambda i: (0, 0))],
        out_specs=pl.BlockSpec((tm, d), lambda i: (i, 0)),
        out_shape=jax.ShapeDtypeStruct((m, d), out_dtype),
        compiler_params=_cparams(("parallel",)),
        name="rmsnorm",
    )(x, w.reshape(1, d).astype(F32))


def _bf16_weight(w_ref, w_s, fresh):
    if not fresh:
        return w_s[...]
    w = w_ref[...].astype(BF16)
    w_s[...] = w
    return w


def _first_row_tile_or_not(body):
    first = pl.program_id(1) == 0
    pl.when(first)(lambda: body(True))
    pl.when(jnp.logical_not(first))(lambda: body(False))


def _mm_kernel(a_ref, w_ref, o_ref, w_s):
    def body(fresh):
        w = _bf16_weight(w_ref, w_s, fresh)
        o_ref[...] = jnp.dot(a_ref[...], w, preferred_element_type=F32).astype(o_ref.dtype)

    _first_row_tile_or_not(body)


def _mm_nt_kernel(*refs, shift):
    a_ref, w_ref = refs[0], refs[1]
    o_ref, w_s = refs[-2], refs[-1]

    def body(fresh):
        if fresh:
            w = w_ref[...]
            if shift:
                w = jnp.concatenate([w, refs[2][...]], axis=0)[shift:shift + w_ref.shape[0], :]
            w = w.astype(BF16)
            w_s[...] = w
        else:
            w = w_s[...]
        o_ref[...] = lax.dot_general(a_ref[...], w, _NT, preferred_element_type=F32).astype(o_ref.dtype)

    _first_row_tile_or_not(body)


def _mm_nt(a, wt, out_dtype, tm, tn, row0, n_rows):
    m, k = a.shape
    base, shift = row0 // tn * tn, row0 % tn
    assert shift % SUBLANES == 0 and shift <= LANES and n_rows % tn == 0
    in_specs = [pl.BlockSpec((tm, k), lambda j, i: (i, 0)),
                pl.BlockSpec((tn, k), lambda j, i: (base // tn + j, 0))]
    args = [a, wt]
    if shift:
        in_specs.append(pl.BlockSpec((LANES, k), lambda j, i: ((base + (j + 1) * tn) // LANES, 0)))
        args.append(wt)
    return pl.pallas_call(
        functools.partial(_mm_nt_kernel, shift=shift),
        grid=(n_rows // tn, m // tm),
        in_specs=in_specs,
        out_specs=pl.BlockSpec((tm, tn), lambda j, i: (i, j)),
        out_shape=jax.ShapeDtypeStruct((m, n_rows), out_dtype),
        scratch_shapes=[pltpu.VMEM((tn, k), BF16)],
        compiler_params=_cparams(("arbitrary", "arbitrary")),
        name="mm_nt",
    )(*args)


def _mm(a, w, out_dtype, tm, tn, n_cols=None, col0=0):
    m, k = a.shape
    n = w.shape[1] if n_cols is None else n_cols
    cb = col0 // tn
    return pl.pallas_call(
        _mm_kernel,
        grid=(n // tn, m // tm),
        in_specs=[pl.BlockSpec((tm, k), lambda j, i: (i, 0)),
                  pl.BlockSpec((k, tn), lambda j, i: (0, cb + j))],
        out_specs=pl.BlockSpec((tm, tn), lambda j, i: (i, j)),
        out_shape=jax.ShapeDtypeStruct((m, n), out_dtype),
        scratch_shapes=[pltpu.VMEM((k, tn), BF16)],
        compiler_params=_cparams(("arbitrary", "arbitrary")),
        name="mm",
    )(a, w)


def _mm_swiglu_kernel(a_ref, wg_ref, wu_ref, o_ref, wg_s, wu_s):
    def body(fresh):
        a = a_ref[...]
        g = jnp.dot(a, _bf16_weight(wg_ref, wg_s, fresh), preferred_element_type=F32)
        u = jnp.dot(a, _bf16_weight(wu_ref, wu_s, fresh), preferred_element_type=F32)
        o_ref[...] = (_silu(g) * u).astype(o_ref.dtype)

    _first_row_tile_or_not(body)


def _mm_swiglu(a, w_gu, tm, tn):
    m, k = a.shape
    f = w_gu.shape[1] // 2
    nb = f // tn
    return pl.pallas_call(
        _mm_swiglu_kernel,
        grid=(nb, m // tm),
        in_specs=[pl.BlockSpec((tm, k), lambda j, i: (i, 0)),
                  pl.BlockSpec((k, tn), lambda j, i: (0, j)),
                  pl.BlockSpec((k, tn), lambda j, i: (0, j + nb))],
        out_specs=pl.BlockSpec((tm, tn), lambda j, i: (i, j)),
        out_shape=jax.ShapeDtypeStruct((m, f), BF16),
        scratch_shapes=[pltpu.VMEM((k, tn), BF16)] * 2,
        compiler_params=_cparams(("arbitrary", "arbitrary")),
        name="mm_swiglu",
    )(a, w_gu, w_gu)


def _mm_resid_kernel(*refs, n_pairs):
    r_ref, o_ref = refs[2 * n_pairs], refs[2 * n_pairs + 1]
    w_s = refs[2 * n_pairs + 2:]

    def body(fresh):
        acc = r_ref[...]
        for p in range(n_pairs):
            w = _bf16_weight(refs[2 * p + 1], w_s[p], fresh)
            acc = acc + jnp.dot(refs[2 * p][...], w, preferred_element_type=F32)
        o_ref[...] = acc

    _first_row_tile_or_not(body)


def _mm_resid(a_list, w, resid, tm, tn):
    m, n = resid.shape
    in_specs, args, scratch = [], [], []
    for p, a in enumerate(a_list):
        k = a.shape[1]
        in_specs += [pl.BlockSpec((tm, k), lambda j, i: (i, 0)),
                     pl.BlockSpec((k, tn), lambda j, i, p=p: (p, j))]
        args += [a, w]
        scratch.append(pltpu.VMEM((k, tn), BF16))
    in_specs.append(pl.BlockSpec((tm, tn), lambda j, i: (i, j)))
    args.append(resid)
    return pl.pallas_call(
        functools.partial(_mm_resid_kernel, n_pairs=len(a_list)),
        grid=(n // tn, m // tm),
        in_specs=in_specs,
        out_specs=pl.BlockSpec((tm, tn), lambda j, i: (i, j)),
        out_shape=jax.ShapeDtypeStruct((m, n), F32),
        scratch_shapes=scratch,
        compiler_params=_cparams(("arbitrary", "arbitrary")),
        name="mm_resid",
    )(*args)


def _causal_conv(buf_ref, x, cw, t_len):
    buf_ref[pl.ds(SUBLANES, t_len), :] = x
    acc = cw[CONV_W - 1:CONV_W, :] * x
    for j in range(CONV_W - 1):
        off = SUBLANES - (CONV_W - 1) + j
        acc = acc + cw[j:j + 1, :] * buf_ref[pl.ds(off, t_len), :]
    buf_ref[pl.ds(0, SUBLANES), :] = x[t_len - SUBLANES:, :]
    return acc


def _tri_masks():
    row = lax.broadcasted_iota(jnp.int32, (CHUNK, CHUNK), 0)
    col = lax.broadcasted_iota(jnp.int32, (CHUNK, CHUNK), 1)
    return row >= col, row > col, row == col


def _gated_head_norm(o, nw, z):
    o = o * lax.rsqrt(jnp.mean(o * o, axis=-1, keepdims=True) + EPS) * nw
    return o * _silu(z)


def _gdn_kernel(q_ref, k_ref, v_ref, z_ref, ba_ref, cwq_ref, cwk_ref, cwv_ref, alog_ref, dtb_ref,
                nw_ref, o_ref, qbuf, kbuf, vbuf, qs, ks, vs, gcs, gct, bs, u_s, w_s, qk_s, qd_s, kd_s, s_ref,
                *, n_heads, t_len):
    t = pl.program_id(1)
    n_chunks = t_len // CHUNK
    width = n_heads * HEAD_DIM

    @pl.when(t == 0)
    def _():
        zeros = jnp.zeros((SUBLANES, width), F32)
        qbuf[pl.ds(0, SUBLANES), :] = zeros
        kbuf[pl.ds(0, SUBLANES), :] = zeros
        vbuf[pl.ds(0, SUBLANES), :] = zeros
        s_ref[...] = jnp.zeros_like(s_ref)

    qc = _silu(_causal_conv(qbuf, q_ref[0], cwq_ref[...], t_len))
    kc = _silu(_causal_conv(kbuf, k_ref[0], cwk_ref[...], t_len))
    vs[...] = _silu(_causal_conv(vbuf, v_ref[0], cwv_ref[...], t_len))
    for h in range(n_heads):
        sl = slice(h * HEAD_DIM, (h + 1) * HEAD_DIM)
        qh = qc[:, sl]
        kh = kc[:, sl]
        qs[:, sl] = qh * (lax.rsqrt(jnp.sum(qh * qh, axis=-1, keepdims=True) + EPS) * (HEAD_DIM ** -0.5))
        ks[:, sl] = kh * lax.rsqrt(jnp.sum(kh * kh, axis=-1, keepdims=True) + EPS)

    ba = ba_ref[0]
    g_log = -jnp.exp(alog_ref[...]) * _softplus(ba + dtb_ref[...])
    bs[...] = _sigmoid(ba)
    row = lax.broadcasted_iota(jnp.int32, (t_len, t_len), 0)
    col = lax.broadcasted_iota(jnp.int32, (t_len, t_len), 1)
    block_tri = ((row // CHUNK == col // CHUNK) & (row >= col)).astype(F32)
    gc_blk = jnp.dot(block_tri, g_log, precision=HIGHEST, preferred_element_type=F32)
    gcs[...] = gc_blk
    for c in range(n_chunks):
        gct[c] = gc_blk[c * CHUNK:(c + 1) * CHUNK, :].T

    incl, strict, eye = _tri_masks()
    eye_f = eye.astype(F32)
    nw = nw_ref[...]

    heads = range(n_heads)
    hsl = [slice(h * HEAD_DIM, (h + 1) * HEAD_DIM) for h in heads]
    qsl = [slice(h * HEAD_DIM, h * HEAD_DIM + CHUNK) for h in heads]
    chunks_per_iter = 2
    assert n_chunks % chunks_per_iter == 0

    def pass_a(it, carry):
        items = []
        for cc in range(chunks_per_iter):
            c = it * chunks_per_iter + cc
            rows = pl.ds(pl.multiple_of(c * CHUNK, CHUNK), CHUNK)
            gc_all = gcs[rows, :]
            beta_all = bs[rows, :]
            gct_c = gct[c]
            for h in heads:
                gc = gc_all[:, n_heads + h:n_heads + h + 1]
                beta = beta_all[:, h:h + 1]
                gc_row = gct_c[n_heads + h:n_heads + h + 1, :]
                decay = jnp.where(incl, jnp.exp(jnp.where(incl, gc - gc_row, 0.0)), 0.0)
                items.append(dict(rows=rows, h=h, gc=gc, beta=beta, decay=decay,
                                  q=qs[rows, hsl[h]], k=ks[rows, hsl[h]]))
        for it_ in items:
            it_["kb"] = it_["k"].astype(BF16)
            it_["kk"] = lax.dot_general(it_["kb"], it_["kb"], _NT, preferred_element_type=F32)
        for it_ in items:
            it_["x"] = jnp.where(strict, -(it_["beta"] * it_["kk"] * it_["decay"]), 0.0)
            it_["p"] = eye_f + it_["x"]
        for _ in range(5):
            for it_ in items:
                xb = it_["x"].astype(BF16)
                it_["x"] = jnp.dot(xb, xb, preferred_element_type=F32)
            for it_ in items:
                it_["p"] = it_["p"] + _bdot(it_["p"], it_["x"])
        for it_ in items:
            egc = jnp.exp(it_["gc"])
            it_["egc"] = egc
            v = vs[it_["rows"], hsl[it_["h"]]]
            rhs = jnp.concatenate([v * it_["beta"], it_["k"] * (it_["beta"] * egc)], axis=-1)
            it_["sol"] = _bdot(it_["p"], rhs)
        for it_ in items:
            it_["qk"] = lax.dot_general(it_["q"].astype(BF16), it_["kb"], _NT,
                                        preferred_element_type=F32) * it_["decay"]
        for it_ in items:
            rows, h, gc = it_["rows"], it_["h"], it_["gc"]
            g_last = gc[CHUNK - 1:CHUNK, :]
            u_s[rows, hsl[h]] = it_["sol"][:, :HEAD_DIM]
            w_s[rows, hsl[h]] = it_["sol"][:, HEAD_DIM:].astype(BF16)
            qk_s[rows, qsl[h]] = it_["qk"].astype(BF16)
            qd_s[rows, hsl[h]] = (it_["q"] * it_["egc"]).astype(BF16)
            kd_s[rows, hsl[h]] = (it_["k"] * jnp.exp(g_last - gc)).astype(BF16)
        return carry

    lax.fori_loop(0, n_chunks // chunks_per_iter, pass_a, 0)

    def pass_b(c, carry):
        rows = pl.ds(pl.multiple_of(c * CHUNK, CHUNK), CHUNK)
        gc_all = gcs[rows, :]
        states = [s_ref[h] for h in heads]
        sbs = [s.astype(BF16) for s in states]
        v_new = [u_s[rows, hsl[h]] - jnp.dot(w_s[rows, hsl[h]], sbs[h], preferred_element_type=F32)
                 for h in heads]
        vbs = [v.astype(BF16) for v in v_new]
        for h in heads:
            g_last = gc_all[CHUNK - 1:CHUNK, n_heads + h:n_heads + h + 1]
            s_ref[h] = states[h] * jnp.exp(g_last) + lax.dot_general(kd_s[rows, hsl[h]], vbs[h], _TN,
                                                                      preferred_element_type=F32)
        outs = [jnp.dot(qd_s[rows, hsl[h]], sbs[h], preferred_element_type=F32)
                + jnp.dot(qk_s[rows, qsl[h]], vbs[h], preferred_element_type=F32) for h in heads]
        for h in heads:
            o = _gated_head_norm(outs[h], nw, z_ref[0, rows, hsl[h]])
            o_ref[0, rows, hsl[h]] = o.astype(o_ref.dtype)
        return carry

    lax.fori_loop(0, n_chunks, pass_b, 0)


def _gdn(proj, ba, conv_w, a_log, dt_bias, norm_w, *, col0, t_len=256):
    bsz, s, _ = proj.shape
    n_heads = a_log.shape[0]
    width = n_heads * HEAD_DIM
    cb = col0 // width

    def col_spec(group):
        return pl.BlockSpec((1, t_len, width), lambda b, t: (b, t, cb + group))

    def cw_spec(group):
        return pl.BlockSpec((CONV_W, width), lambda b, t: (0, group))

    row = jnp.zeros((1, LANES), F32)
    alog_row = row.at[0, n_heads:2 * n_heads].set(a_log.astype(F32))
    dtb_row = row.at[0, n_heads:2 * n_heads].set(dt_bias.astype(F32))
    small = pl.BlockSpec((1, LANES), lambda b, t: (0, 0))
    kern = functools.partial(_gdn_kernel, n_heads=n_heads, t_len=t_len)
    return pl.pallas_call(
        kern,
        grid=(bsz, s // t_len),
        in_specs=[col_spec(0), col_spec(1), col_spec(2), col_spec(3),
                  pl.BlockSpec((1, t_len, LANES), lambda b, t: (b, t, 0)),
                  cw_spec(0), cw_spec(1), cw_spec(2), small, small,
                  pl.BlockSpec((1, HEAD_DIM), lambda b, t: (0, 0))],
        out_specs=pl.BlockSpec((1, t_len, width), lambda b, t: (b, t, 0)),
        out_shape=jax.ShapeDtypeStruct((bsz, s, width), BF16),
        scratch_shapes=[pltpu.VMEM((SUBLANES + t_len, width), F32)] * 3
        + [pltpu.VMEM((t_len, width), F32)] * 3
        + [pltpu.VMEM((t_len, LANES), F32),
           pltpu.VMEM((t_len // CHUNK, LANES, CHUNK), F32),
           pltpu.VMEM((t_len, LANES), F32),
           pltpu.VMEM((t_len, width), F32)]
        + [pltpu.VMEM((t_len, width), BF16)] * 4
        + [pltpu.VMEM((n_heads, HEAD_DIM, HEAD_DIM), F32)],
        compiler_params=_cparams(("parallel", "arbitrary")),
        name="gdn",
    )(proj, proj, proj, proj, ba, conv_w, conv_w, conv_w, alog_row, dtb_row,
      norm_w.reshape(1, HEAD_DIM).astype(F32))


def _hgrn_kernel(q_ref, f_ref, i_ref, g_ref, lb_ref, nw_ref, o_ref, qs, ks, ls, st_ref, *, heads, t_len):
    t = pl.program_id(1)

    @pl.when(t == 0)
    def _():
        st_ref[...] = jnp.zeros_like(st_ref)

    lb = lb_ref[...]
    forget = lb + (1.0 - lb) * _sigmoid(f_ref[0])
    ks[...] = 1.0 - forget
    qs[...] = _silu(q_ref[0]) * (HEAD_DIM ** -0.5)
    row = lax.broadcasted_iota(jnp.int32, (t_len, t_len), 0)
    col = lax.broadcasted_iota(jnp.int32, (t_len, t_len), 1)
    block_tri = ((row // CHUNK == col // CHUNK) & (row >= col)).astype(BF16)
    log_f = jnp.log(forget)
    hi = log_f.astype(BF16)
    rest = log_f - hi.astype(F32)
    mid = rest.astype(BF16)
    lo = (rest - mid.astype(F32)).astype(BF16)
    ls[...] = (jnp.dot(block_tri, hi, preferred_element_type=F32)
               + (jnp.dot(block_tri, mid, preferred_element_type=F32)
                  + jnp.dot(block_tri, lo, preferred_element_type=F32)))

    incl, _, _ = _tri_masks()
    nw = nw_ref[...]
    mid = CHUNK // 2 - 1

    hsl = [slice(j * HEAD_DIM, (j + 1) * HEAD_DIM) for j in range(heads)]

    def chunk_body(c, carry):
        rows = pl.ds(pl.multiple_of(c * CHUNK, CHUNK), CHUNK)
        items = []
        for j in range(heads):
            b = ls[rows, hsl[j]]
            items.append(dict(b=b, b_mid=b[mid:mid + 1, :], b_last=b[CHUNK - 1:CHUNK, :],
                              q=qs[rows, hsl[j]], k=ks[rows, hsl[j]],
                              vb=i_ref[0, rows, hsl[j]].astype(BF16)))
        for it in items:
            it["att"] = jnp.where(incl, _bdot_g(it["q"] * jnp.exp(it["b"] - it["b_mid"]),
                                                it["k"] * jnp.exp(it["b_mid"] - it["b"]), _NT), 0.0)
        for j, it in enumerate(items):
            st = st_ref[j]
            it["o"] = _bdot_g(it["q"] * jnp.exp(it["b"]), st, _NT)
            st_ref[j] = st * jnp.exp(it["b_last"]) + lax.dot_general(
                it["vb"], (it["k"] * jnp.exp(it["b_last"] - it["b"])).astype(BF16), _TN,
                preferred_element_type=F32)
        for it in items:
            it["o"] = it["o"] + jnp.dot(it["att"].astype(BF16), it["vb"], preferred_element_type=F32)
        for j, it in enumerate(items):
            o = _gated_head_norm(it["o"], nw, g_ref[0, rows, hsl[j]])
            o_ref[0, rows, hsl[j]] = o.astype(o_ref.dtype)
        return carry

    lax.fori_loop(0, t_len // CHUNK, chunk_body, 0)


def _hgrn(proj, lb, norm_w, *, col0, n_heads, t_len=256):
    bsz, s, _ = proj.shape
    width = n_heads * HEAD_DIM
    cb = col0 // width

    def col_spec(group):
        return pl.BlockSpec((1, t_len, width), lambda b, t: (b, t, cb + group))

    kern = functools.partial(_hgrn_kernel, heads=n_heads, t_len=t_len)
    return pl.pallas_call(
        kern,
        grid=(bsz, s // t_len),
        in_specs=[col_spec(0), col_spec(1), col_spec(2), col_spec(3),
                  pl.BlockSpec((1, width), lambda b, t: (0, 0)),
                  pl.BlockSpec((1, HEAD_DIM), lambda b, t: (0, 0))],
        out_specs=pl.BlockSpec((1, t_len, width), lambda b, t: (b, t, 0)),
        out_shape=jax.ShapeDtypeStruct((bsz, s, width), BF16),
        scratch_shapes=[pltpu.VMEM((t_len, width), F32)] * 3
        + [pltpu.VMEM((n_heads, HEAD_DIM, HEAD_DIM), F32)],
        compiler_params=_cparams(("parallel", "arbitrary")),
        name="hgrn2",
    )(proj, proj, proj, proj, lb.reshape(1, width).astype(F32), norm_w.reshape(1, HEAD_DIM).astype(F32))


def _rglru_kernel(y_ref, x_ref, cw_ref, cb_ref, wa_ref, ba_ref, wx_ref, bx_ref, lam_ref, o_ref,
                  xbuf, a_s, b_s, h_s, hcar, *, blocks, t_len):
    t = pl.program_id(2)

    @pl.when(t == 0)
    def _():
        xbuf[pl.ds(0, SUBLANES), :] = jnp.zeros((SUBLANES, blocks * RG_BLOCK), F32)
        hcar[...] = jnp.zeros_like(hcar)

    xc = _causal_conv(xbuf, x_ref[0], cw_ref[...], t_len) + cb_ref[...]
    for n in range(blocks):
        sl = slice(n * RG_BLOCK, (n + 1) * RG_BLOCK)
        xb = xc[:, sl]
        r = _sigmoid(_bdot(xb, wa_ref[n]) + ba_ref[:, sl])
        gi = _sigmoid(_bdot(xb, wx_ref[n]) + bx_ref[:, sl])
        log_a = (-RG_C) * r * _softplus(-lam_ref[:, sl])
        a = jnp.exp(log_a)
        a_s[:, sl] = a
        one_minus_a2 = -jnp.tanh(log_a) * (a * a + 1.0)
        b_s[:, sl] = jnp.sqrt(jnp.maximum(one_minus_a2, 0.0)) * (gi * xb)

    def row_body(i, h):
        h = a_s[pl.ds(i, 1), :] * h + b_s[pl.ds(i, 1), :]
        h_s[pl.ds(i, 1), :] = h
        return h

    hcar[...] = lax.fori_loop(0, t_len, row_body, hcar[...])
    o_ref[0] = (jax.nn.gelu(y_ref[0], approximate=True) * h_s[...]).astype(o_ref.dtype)


def _rglru(yx, conv_w, conv_b, wa, ba, wx, bx, lam, *, blocks_per_step=8, t_len=256):
    bsz, s, w2 = yx.shape
    width = w2 // 2
    gw = blocks_per_step * RG_BLOCK
    n_g = width // gw
    vec = lambda a: a.reshape(1, width).astype(F32)
    vspec = pl.BlockSpec((1, gw), lambda b, g, t: (0, g))
    wspec = pl.BlockSpec((blocks_per_step, RG_BLOCK, RG_BLOCK), lambda b, g, t: (g, 0, 0))
    kern = functools.partial(_rglru_kernel, blocks=blocks_per_step, t_len=t_len)
    return pl.pallas_call(
        kern,
        grid=(bsz, n_g, s // t_len),
        in_specs=[pl.BlockSpec((1, t_len, gw), lambda b, g, t: (b, t, g)),
                  pl.BlockSpec((1, t_len, gw), lambda b, g, t: (b, t, n_g + g)),
                  pl.BlockSpec((CONV_W, gw), lambda b, g, t: (0, g)),
                  vspec, wspec, vspec, wspec, vspec, vspec],
        out_specs=pl.BlockSpec((1, t_len, gw), lambda b, g, t: (b, t, g)),
        out_shape=jax.ShapeDtypeStruct((bsz, s, width), BF16),
        scratch_shapes=[pltpu.VMEM((SUBLANES + t_len, gw), F32)]
        + [pltpu.VMEM((t_len, gw), F32)] * 3
        + [pltpu.VMEM((1, gw), F32)],
        compiler_params=_cparams(("parallel", "parallel", "arbitrary")),
        name="rglru",
    )(yx, yx, conv_w.astype(F32), vec(conv_b), wa.astype(BF16), vec(ba), wx.astype(BF16), vec(bx), vec(lam))


def _norm_router_kernel(x_ref, nw_ref, w_ref, o_ref, *, n_experts):
    x = x_ref[...]
    y = x * lax.rsqrt(jnp.mean(x * x, axis=-1, keepdims=True) + EPS) * nw_ref[...]
    w = w_ref[...]
    y_hi = y.astype(BF16)
    y_lo = (y - y_hi.astype(F32)).astype(BF16)
    w_hi = w.astype(BF16)
    w_lo = (w - w_hi.astype(F32)).astype(BF16)
    logits = (jnp.dot(y_hi, w_hi, preferred_element_type=F32)
              + (jnp.dot(y_lo, w_hi, preferred_element_type=F32)
                 + jnp.dot(y_hi, w_lo, preferred_element_type=F32)))
    lane = lax.broadcasted_iota(jnp.int32, logits.shape, 1).astype(F32)
    neg = jnp.float32(-jnp.inf)
    l1 = jnp.where(lane < n_experts, logits, neg)
    m1 = jnp.max(l1, axis=-1, keepdims=True)
    i1 = jnp.min(jnp.where(l1 == m1, lane, float(LANES)), axis=-1, keepdims=True)
    l2 = jnp.where(lane == i1, neg, l1)
    m2 = jnp.max(l2, axis=-1, keepdims=True)
    i2 = jnp.min(jnp.where(l2 == m2, lane, float(LANES)), axis=-1, keepdims=True)
    e2 = jnp.exp(m2 - m1)
    g1 = 1.0 / (1.0 + e2)
    g2 = e2 / (1.0 + e2)
    out = jnp.where(lane == 0, i1, 0.0)
    out = jnp.where(lane == 1, i2, out)
    out = jnp.where(lane == 2, g1, out)
    out = jnp.where(lane == 3, g2, out)
    o_ref[...] = out


def _norm_router(x, nw, router_w, tm=512):
    m, d = x.shape
    n_experts = router_w.shape[1]
    w = jnp.zeros((d, LANES), F32).at[:, :n_experts].set(router_w.astype(F32))
    return pl.pallas_call(
        functools.partial(_norm_router_kernel, n_experts=n_experts),
        grid=(m // tm,),
        in_specs=[pl.BlockSpec((tm, d), lambda i: (i, 0)),
                  pl.BlockSpec((1, d), lambda i: (0, 0)),
                  pl.BlockSpec((d, LANES), lambda i: (0, 0))],
        out_specs=pl.BlockSpec((tm, LANES), lambda i: (i, 0)),
        out_shape=jax.ShapeDtypeStruct((m, LANES), F32),
        compiler_params=_cparams(("parallel",)),
        name="norm_router",
    )(x, nw.reshape(1, d).astype(F32), w)


def _start_row_gather(idx_ref, base, src3, dst3, sem, n_rows):
    def body(g, carry):
        for u in range(SUBLANES):
            row = idx_ref[base + g * SUBLANES + u]
            src = src3.at[lax.shift_right_logical(row, 3), pl.ds(row & (SUBLANES - 1), 1), :]
            pltpu.make_async_copy(src, dst3.at[g, pl.ds(u, 1), :], sem).start(priority=u % 2)
        return carry
    lax.fori_loop(0, n_rows // SUBLANES, body, 0)


def _wait_row_gather(src3, dst3, sem, n_rows):
    n = n_rows // SUBLANES
    pltpu.make_async_copy(src3.at[pl.ds(0, n)], dst3.at[pl.ds(0, n)], sem).wait()


def _gather_norm_kernel(idx_ref, nq_ref, x_hbm, nw_ref, o_ref, buf, sem):
    c = pl.program_id(0)
    slot = c % 2

    def start(chunk, s):
        _start_row_gather(idx_ref, chunk * MOE_CHUNK, x_hbm, buf.at[s], sem.at[s], nq_ref[chunk] * MOE_SUB)

    @pl.when(c == 0)
    def _():
        start(0, 0)

    @pl.when(c + 1 < pl.num_programs(0))
    def _():
        start(c + 1, 1 - slot)

    for sb in range(MOE_CHUNK // MOE_SUB):
        @pl.when(sb < nq_ref[c])
        def _():
            _wait_row_gather(x_hbm, buf.at[slot], sem.at[slot], MOE_SUB)

    for sb in range(MOE_CHUNK // MOE_SUB):
        rows = pl.ds(sb * MOE_SUB, MOE_SUB)

        @pl.when(sb < nq_ref[c])
        def _(sb=sb, rows=rows):
            tiles = pl.ds(sb * (MOE_SUB // SUBLANES), MOE_SUB // SUBLANES)
            x = buf[slot, tiles].reshape(MOE_SUB, o_ref.shape[1])
            y = x * lax.rsqrt(jnp.mean(x * x, axis=-1, keepdims=True) + EPS) * nw_ref[...]
            o_ref[rows, :] = y.astype(o_ref.dtype)

        @pl.when(sb >= nq_ref[c])
        def _():
            o_ref[rows, :] = jnp.zeros((MOE_SUB, o_ref.shape[1]), o_ref.dtype)


def _gather_norm(src_token, chunk_nq, x, nw):
    rows = src_token.shape[0]
    d = x.shape[1]
    grid_spec = pltpu.PrefetchScalarGridSpec(
        num_scalar_prefetch=2,
        grid=(rows // MOE_CHUNK,),
        in_specs=[pl.BlockSpec(memory_space=pl.ANY),
                  pl.BlockSpec((1, d), lambda c, idx, nq: (0, 0))],
        out_specs=pl.BlockSpec((MOE_CHUNK, d), lambda c, idx, nq: (c, 0)),
        scratch_shapes=[pltpu.VMEM((2, MOE_CHUNK // SUBLANES, SUBLANES, d), F32),
                        pltpu.SemaphoreType.DMA((2,))],
    )
    return pl.pallas_call(
        _gather_norm_kernel,
        grid_spec=grid_spec,
        out_shape=jax.ShapeDtypeStruct((rows, d), BF16),
        compiler_params=_cparams(("arbitrary",)),
        name="gather_norm",
    )(src_token, chunk_nq, x.reshape(-1, SUBLANES, d), nw.reshape(1, d).astype(F32))


def _for_valid_rows(nq, o_ref, fn, pred=True):
    total = o_ref.shape[0]
    for v in range(total // MOE_SUB + 1):
        @pl.when(jnp.logical_and(nq == v, pred))
        def _(v=v):
            rows = v * MOE_SUB
            if rows:
                fn(rows)
            if rows < total:
                o_ref[pl.ds(rows, total - rows), :] = jnp.zeros((total - rows, o_ref.shape[1]), o_ref.dtype)


def _gmm_swiglu_kernel(te_ref, nq_ref, nv_ref, a_ref, wg_ref, wu_ref, o_ref):
    c = pl.program_id(0)

    def compute(rows):
        a = a_ref[pl.ds(0, rows), :]
        g = jnp.dot(a, wg_ref[0].astype(BF16), preferred_element_type=F32)
        u = jnp.dot(a, wu_ref[0].astype(BF16), preferred_element_type=F32)
        o_ref[pl.ds(0, rows), :] = (_silu(g) * u).astype(o_ref.dtype)

    _for_valid_rows(nq_ref[c], o_ref, compute)


def _gmm_swiglu(chunk_expert, chunk_nq, n_valid, a, w_gu, tn):
    m, k = a.shape
    f = w_gu.shape[2] // 2
    nb = f // tn
    last = lambda c, nv: jnp.minimum(c, nv[0] - 1)
    col = lambda j, c, nv: jnp.where(c < nv[0], j, nb - 1)
    grid_spec = pltpu.PrefetchScalarGridSpec(
        num_scalar_prefetch=3,
        grid=(m // MOE_CHUNK, nb),
        in_specs=[pl.BlockSpec((MOE_CHUNK, k), lambda c, j, te, nq, nv: (last(c, nv), 0)),
                  pl.BlockSpec((1, k, tn), lambda c, j, te, nq, nv: (te[c], 0, col(j, c, nv))),
                  pl.BlockSpec((1, k, tn), lambda c, j, te, nq, nv: (te[c], 0, col(j, c, nv) + nb))],
        out_specs=pl.BlockSpec((MOE_CHUNK, tn), lambda c, j, te, nq, nv: (c, j)),
    )
    return pl.pallas_call(
        _gmm_swiglu_kernel,
        grid_spec=grid_spec,
        out_shape=jax.ShapeDtypeStruct((m, f), BF16),
        compiler_params=_cparams(("arbitrary", "arbitrary")),
        name="gmm_swiglu",
    )(chunk_expert, chunk_nq, n_valid, a, w_gu, w_gu)


def _gmm_down_kernel(te_ref, nq_ref, nv_ref, a_ref, w_ref, o_ref):
    c = pl.program_id(0)

    def compute(rows):
        w = w_ref[0].astype(BF16)
        o_ref[pl.ds(0, rows), :] = jnp.dot(a_ref[pl.ds(0, rows), :], w, preferred_element_type=F32)

    _for_valid_rows(nq_ref[c], o_ref, compute)


def _gmm_down(chunk_expert, chunk_nq, n_valid, a, w_d, tn):
    m, k = a.shape
    n = w_d.shape[2]
    nj = n // tn
    last = lambda c, nv: jnp.minimum(c, nv[0] - 1)
    col = lambda j, c, nv: jnp.where(c < nv[0], j, nj - 1)
    grid_spec = pltpu.PrefetchScalarGridSpec(
        num_scalar_prefetch=3,
        grid=(m // MOE_CHUNK, nj),
        in_specs=[pl.BlockSpec((MOE_CHUNK, k), lambda c, j, te, nq, nv: (last(c, nv), 0)),
                  pl.BlockSpec((1, k, tn), lambda c, j, te, nq, nv: (te[c], 0, col(j, c, nv)))],
        out_specs=pl.BlockSpec((MOE_CHUNK, tn), lambda c, j, te, nq, nv: (c, j)),
    )
    return pl.pallas_call(
        _gmm_down_kernel,
        grid_spec=grid_spec,
        out_shape=jax.ShapeDtypeStruct((m, n), F32),
        compiler_params=pltpu.CompilerParams(dimension_semantics=("arbitrary", "arbitrary"),
                                             vmem_limit_bytes=VMEM_LIMIT_MAX),
        name="gmm_down",
    )(chunk_expert, chunk_nq, n_valid, a, w_d)


def _combine_norm_kernel(pos_ref, x_ref, r_ref, ys_hbm, w_ref, o_ref, buf, sem, *, tm):
    i = pl.program_id(0)
    n = pl.num_programs(0)
    slot = i % 2

    def start(tile, s):
        for kk in range(TOP_K):
            _start_row_gather(pos_ref, kk * (n * tm) + tile * tm, ys_hbm, buf.at[s, kk], sem.at[s], tm)

    @pl.when(i == 0)
    def _():
        start(0, 0)

    @pl.when(i + 1 < n)
    def _():
        start(i + 1, 1 - slot)

    for kk in range(TOP_K):
        _wait_row_gather(ys_hbm, buf.at[slot, kk], sem.at[slot], tm)
    r = r_ref[...]
    d = x_ref.shape[1]
    moe = r[:, TOP_K:TOP_K + 1] * buf[slot, 0].reshape(tm, d)
    for kk in range(1, TOP_K):
        moe = moe + r[:, TOP_K + kk:TOP_K + kk + 1] * buf[slot, kk].reshape(tm, d)
    x = x_ref[...] + moe
    y = x * lax.rsqrt(jnp.mean(x * x, axis=-1, keepdims=True) + EPS)
    o_ref[...] = y * w_ref[...]


def _combine_norm(pos, x, route, ys, w, tm=256):
    m, d = x.shape
    grid_spec = pltpu.PrefetchScalarGridSpec(
        num_scalar_prefetch=1,
        grid=(m // tm,),
        in_specs=[pl.BlockSpec((tm, d), lambda i, p: (i, 0)),
                  pl.BlockSpec((tm, LANES), lambda i, p: (i, 0)),
                  pl.BlockSpec(memory_space=pl.ANY),
                  pl.BlockSpec((1, d), lambda i, p: (0, 0))],
        out_specs=pl.BlockSpec((tm, d), lambda i, p: (i, 0)),
        scratch_shapes=[pltpu.VMEM((2, TOP_K, tm // SUBLANES, SUBLANES, d), F32),
                        pltpu.SemaphoreType.DMA((2,))],
    )
    return pl.pallas_call(
        functools.partial(_combine_norm_kernel, tm=tm),
        grid_spec=grid_spec,
        out_shape=jax.ShapeDtypeStruct((m, d), F32),
        compiler_params=_cparams(("arbitrary",)),
        name="combine_norm",
    )(pos, x, route, ys.reshape(-1, SUBLANES, d), w.reshape(1, d).astype(F32))


def _moe_routing(route, n_experts):
    m = route.shape[0]
    n_sub = MOE_CHUNK // MOE_SUB
    ids = route[:, :TOP_K].astype(jnp.int32)
    flat_e = ids.reshape(-1)
    onehot = (flat_e[:, None] == jnp.arange(n_experts)[None, :]).astype(jnp.int32)
    rank = jnp.sum((jnp.cumsum(onehot, axis=0) - onehot) * onehot, axis=1)
    counts = jnp.sum(onehot, axis=0)
    subs = (counts + MOE_SUB - 1) // MOE_SUB
    n_chunks = (subs + n_sub - 1) // n_sub
    chunk_end = jnp.cumsum(n_chunks)
    chunk_start = chunk_end - n_chunks
    max_chunks = (m * TOP_K // MOE_SUB + n_experts + n_sub - 1) // n_sub + n_experts
    rows = max_chunks * MOE_CHUNK
    base = subs // jnp.maximum(n_chunks, 1)
    extra = subs - base * n_chunks
    n_valid = chunk_end[n_experts - 1].astype(jnp.int32)
    cidx = jnp.minimum(jnp.arange(max_chunks, dtype=jnp.int32), n_valid - 1)
    chunk_expert = jnp.sum((cidx[:, None] >= chunk_end[None, :]).astype(jnp.int32), axis=1).astype(jnp.int32)
    sel = (chunk_expert[:, None] == jnp.arange(n_experts)[None, :]).astype(jnp.int32)
    per_chunk = lambda v: jnp.sum(sel * v[None, :], axis=1)
    within = cidx - per_chunk(chunk_start)
    chunk_nq = per_chunk(base) + (within < per_chunk(extra))
    is_valid = jnp.arange(max_chunks) < n_valid
    chunk_nq = jnp.where(is_valid, chunk_nq, 0).astype(jnp.int32)
    first_rank = (jnp.minimum(within, per_chunk(extra)) * (per_chunk(base) + 1)
                  + jnp.maximum(within - per_chunk(extra), 0) * per_chunk(base)) * MOE_SUB
    mine = (flat_e[:, None] == chunk_expert[None, :]) & is_valid[None, :]
    chunk_of = (jnp.sum(onehot * chunk_start[None, :], axis=1)
                + jnp.sum((mine & (first_rank[None, :] <= rank[:, None])).astype(jnp.int32), axis=1) - 1)
    at_chunk = (chunk_of[:, None] == jnp.arange(max_chunks)[None, :]).astype(jnp.int32)
    pos = chunk_of * MOE_CHUNK + rank - jnp.sum(at_chunk * first_rank[None, :], axis=1)
    src_token = jnp.zeros((rows,), jnp.int32).at[pos].set(jnp.arange(m * TOP_K, dtype=jnp.int32) // TOP_K)
    pos_kmajor = pos.reshape(m, TOP_K).T.reshape(-1).astype(jnp.int32)
    return src_token, chunk_expert, chunk_nq, n_valid.reshape(1), pos_kmajor


def kernel(x, norm_w, final_norm_w, mix_in_w, gdn_conv_w, gdn_a_log, gdn_dt_bias, gdn_norm_w, hgrn_lb_logits, hgrn_norm_w, mix_out_w, ffn_gate_up_w, ffn_down_w, rg_in_w, rg_conv_w, rg_conv_b, rg_gate_a_w, rg_gate_a_b, rg_gate_x_w, rg_gate_x_b, rg_lambda, rg_out_w, moe_router_w, moe_gate_up_w, moe_down_w):
    bsz, s, d = x.shape
    m = bsz * s
    n_heads = gdn_a_log.shape[1]
    hw = n_heads * HEAD_DIM
    n_experts = moe_router_w.shape[2]
    xr = x.reshape(m, d)

    hgrn_lb = jnp.cumsum(jax.nn.softmax(hgrn_lb_logits.astype(F32), axis=0), axis=0)
    w_in_t = jnp.swapaxes(mix_in_w[0], 0, 1)
    h = _rmsnorm(xr, norm_w[0, 0], BF16)
    proj_a = _mm_nt(h, w_in_t, F32, 1024, 1024, 0, 4 * hw).reshape(bsz, s, 4 * hw)
    proj_b = _mm_nt(h, w_in_t, F32, 1024, 1024, 4 * hw + 2 * n_heads, 4 * hw).reshape(bsz, s, 4 * hw)
    ba = _mm_nt(h, w_in_t, F32, 1024, LANES, 4 * hw, LANES).reshape(bsz, s, LANES)
    o_a = _gdn(proj_a, ba, gdn_conv_w[0].astype(F32), gdn_a_log[0], gdn_dt_bias[0], gdn_norm_w[0], col0=0)
    o_b = _hgrn(proj_b, hgrn_lb[0], hgrn_norm_w[0], col0=0, n_heads=n_heads)
    xr = _mm_resid([o_a.reshape(m, hw), o_b.reshape(m, hw)], mix_out_w[0], xr, 1024, 1024)
    h = _rmsnorm(xr, norm_w[0, 1], BF16)
    act = _mm_swiglu(h, ffn_gate_up_w[0], 1024, 512)
    xr = _mm_resid([act], ffn_down_w[0], xr, 512, 512)

    h = _rmsnorm(xr, norm_w[1, 0], BF16)
    yx = _mm(h, rg_in_w[0], F32, 1024, 1024).reshape(bsz, s, -1)
    rec = _rglru(yx, rg_conv_w[0], rg_conv_b[0], rg_gate_a_w[0], rg_gate_a_b[0],
                 rg_gate_x_w[0], rg_gate_x_b[0], rg_lambda[0])
    xr = _mm_resid([rec.reshape(m, d)], rg_out_w[0], xr, 1024, 1024)
    route = _norm_router(xr, norm_w[1, 1], moe_router_w[0])
    src_token, chunk_expert, chunk_nq, n_valid, pos = _moe_routing(route, n_experts)
    hs = _gather_norm(src_token, chunk_nq, xr, norm_w[1, 1])
    act = _gmm_swiglu(chunk_expert, chunk_nq, n_valid, hs, moe_gate_up_w[0], 512)
    ys = _gmm_down(chunk_expert, chunk_nq, n_valid, act, moe_down_w[0], 256)
    out = _combine_norm(pos, xr, route, ys, final_norm_w)
    return out.reshape(bsz, s, d)
```

```python
import functools

import jax
import jax.numpy as jnp
from jax import lax
from jax.experimental import pallas as pl
from jax.experimental.pallas import tpu as pltpu

F32 = jnp.float32
BF16 = jnp.bfloat16
HIGHEST = lax.Precision.HIGHEST

EPS = 1e-6
CHUNK = 64
CONV_W = 4
HEAD_DIM = 128
RG_BLOCK = 256
RG_C = 8.0
TOP_K = 2
LANES = 128
SUBLANES = 8
VMEM_LIMIT = 56 * 1024 * 1024
VMEM_LIMIT_MAX = 60 * 1024 * 1024
MOE_CHUNK = 1024
MOE_SUB = 128

_NT = (((1,), (1,)), ((), ()))
_TN = (((0,), (0,)), ((), ()))


def _cparams(sem):
    return pltpu.CompilerParams(dimension_semantics=sem, vmem_limit_bytes=VMEM_LIMIT)


def _sigmoid(x):
    return jax.nn.sigmoid(x)


def _silu(x):
    return x * _sigmoid(x)


def _softplus(x):
    return jnp.maximum(x, 0.0) + jnp.log1p(jnp.exp(-jnp.abs(x)))


def _bdot(a, b):
    return jnp.dot(a.astype(BF16), b.astype(BF16), preferred_element_type=F32)


def _bdot_g(a, b, dims):
    return lax.dot_general(a.astype(BF16), b.astype(BF16), dims, preferred_element_type=F32)


def _rmsnorm_kernel(x_ref, w_ref, o_ref):
    x = x_ref[...]
    y = x * lax.rsqrt(jnp.mean(x * x, axis=-1, keepdims=True) + EPS)
    o_ref[...] = (y * w_ref[...]).astype(o_ref.dtype)


def _rmsnorm(x, w, out_dtype, tm=512):
    m, d = x.shape
    return pl.pallas_call(
        _rmsnorm_kernel,
        grid=(m // tm,),
        in_specs=[pl.BlockSpec((tm, d), lambda i: (i, 0)),
                  pl.BlockSpec((1, d), lambda i: (0, 0))],
        out_specs=pl.BlockSpec((tm, d), lambda i: (i, 0)),
        out_shape=jax.ShapeDtypeStruct((m, d), out_dtype),
        compiler_params=_cparams(("parallel",)),
        name="rmsnorm",
    )(x, w.reshape(1, d).astype(F32))


def _bf16_weight(w_ref, w_s, fresh):
    if not fresh:
        return w_s[...]
    w = w_ref[...].astype(BF16)
    w_s[...] = w
    return w


def _first_row_tile_or_not(body):
    first = pl.program_id(1) == 0
    pl.when(first)(lambda: body(True))
    pl.when(jnp.logical_not(first))(lambda: body(False))


def _mm_nt_kernel(*refs, shift):
    a_ref, w_ref = refs[0], refs[1]
    o_ref, w_s = refs[-2], refs[-1]

    def body(fresh):
        if fresh:
            w = w_ref[...]
            if shift:
                w = jnp.concatenate([w, refs[2][...]], axis=0)[shift:shift + w_ref.shape[0], :]
            w = w.astype(BF16)
            w_s[...] = w
        else:
            w = w_s[...]
        o_ref[...] = lax.dot_general(a_ref[...], w, _NT, preferred_element_type=F32).astype(o_ref.dtype)

    _first_row_tile_or_not(body)


def _mm_nt(a, wt, out_dtype, tm, tn, row0, n_rows):
    m, k = a.shape
    base, shift = row0 // tn * tn, row0 % tn
    assert shift % SUBLANES == 0 and shift <= LANES and n_rows % tn == 0
    in_specs = [pl.BlockSpec((tm, k), lambda j, i: (i, 0)),
                pl.BlockSpec((tn, k), lambda j, i: (base // tn + j, 0))]
    args = [a, wt]
    if shift:
        in_specs.append(pl.BlockSpec((LANES, k), lambda j, i: ((base + (j + 1) * tn) // LANES, 0)))
        args.append(wt)
    return pl.pallas_call(
        functools.partial(_mm_nt_kernel, shift=shift),
        grid=(n_rows // tn, m // tm),
        in_specs=in_specs,
        out_specs=pl.BlockSpec((tm, tn), lambda j, i: (i, j)),
        out_shape=jax.ShapeDtypeStruct((m, n_rows), out_dtype),
        scratch_shapes=[pltpu.VMEM((tn, k), BF16)],
        compiler_params=_cparams(("arbitrary", "arbitrary")),
        name="mm_nt",
    )(*args)


def _normed_rows(x_ref, nw_ref, h_s, fresh):
    if not fresh:
        return h_s[...]
    x = x_ref[...]
    h = (x * lax.rsqrt(jnp.mean(x * x, axis=-1, keepdims=True) + EPS) * nw_ref[...]).astype(BF16)
    h_s[...] = h
    return h


def _norm_mm_kernel(x_ref, nw_ref, w_ref, o_ref, h_s):
    def body(fresh):
        h = _normed_rows(x_ref, nw_ref, h_s, fresh)
        o_ref[...] = jnp.dot(h, w_ref[...].astype(BF16), preferred_element_type=F32).astype(o_ref.dtype)

    _first_row_tile_or_not(body)


def _norm_mm(x, nw, w, out_dtype, tm, tn):
    m, k = x.shape
    n = w.shape[1]
    return pl.pallas_call(
        _norm_mm_kernel,
        grid=(m // tm, n // tn),
        in_specs=[pl.BlockSpec((tm, k), lambda i, j: (i, 0)),
                  pl.BlockSpec((1, k), lambda i, j: (0, 0)),
                  pl.BlockSpec((k, tn), lambda i, j: (0, j))],
        out_specs=pl.BlockSpec((tm, tn), lambda i, j: (i, j)),
        out_shape=jax.ShapeDtypeStruct((m, n), out_dtype),
        scratch_shapes=[pltpu.VMEM((tm, k), BF16)],
        compiler_params=_cparams(("arbitrary", "arbitrary")),
        name="norm_mm",
    )(x, nw.reshape(1, k).astype(F32), w)


def _norm_mm_swiglu_kernel(x_ref, nw_ref, wg_ref, wu_ref, o_ref, h_s):
    def body(fresh):
        h = _normed_rows(x_ref, nw_ref, h_s, fresh)
        g = jnp.dot(h, wg_ref[...].astype(BF16), preferred_element_type=F32)
        u = jnp.dot(h, wu_ref[...].astype(BF16), preferred_element_type=F32)
        o_ref[...] = (_silu(g) * u).astype(o_ref.dtype)

    _first_row_tile_or_not(body)


def _norm_mm_swiglu(x, nw, w_gu, tm, tn):
    m, k = x.shape
    f = w_gu.shape[1] // 2
    nb = f // tn
    return pl.pallas_call(
        _norm_mm_swiglu_kernel,
        grid=(m // tm, nb),
        in_specs=[pl.BlockSpec((tm, k), lambda i, j: (i, 0)),
                  pl.BlockSpec((1, k), lambda i, j: (0, 0)),
                  pl.BlockSpec((k, tn), lambda i, j: (0, j)),
                  pl.BlockSpec((k, tn), lambda i, j: (0, j + nb))],
        out_specs=pl.BlockSpec((tm, tn), lambda i, j: (i, j)),
        out_shape=jax.ShapeDtypeStruct((m, f), BF16),
        scratch_shapes=[pltpu.VMEM((tm, k), BF16)],
        compiler_params=_cparams(("arbitrary", "arbitrary")),
        name="norm_mm_swiglu",
    )(x, nw.reshape(1, k).astype(F32), w_gu, w_gu)


def _mm_resid_kernel(*refs, n_pairs):
    r_ref, o_ref = refs[2 * n_pairs], refs[2 * n_pairs + 1]
    w_s = refs[2 * n_pairs + 2:]

    def body(fresh):
        acc = r_ref[...]
        for p in range(n_pairs):
            w = _bf16_weight(refs[2 * p + 1], w_s[p], fresh)
            acc = acc + jnp.dot(refs[2 * p][...], w, preferred_element_type=F32)
        o_ref[...] = acc

    _first_row_tile_or_not(body)


def _mm_resid(a_list, w, resid, tm, tn):
    m, n = resid.shape
    in_specs, args, scratch = [], [], []
    for p, a in enumerate(a_list):
        k = a.shape[1]
        in_specs += [pl.BlockSpec((tm, k), lambda j, i: (i, 0)),
                     pl.BlockSpec((k, tn), lambda j, i, p=p: (p, j))]
        args += [a, w]
        scratch.append(pltpu.VMEM((k, tn), BF16))
    in_specs.append(pl.BlockSpec((tm, tn), lambda j, i: (i, j)))
    args.append(resid)
    return pl.pallas_call(
        functools.partial(_mm_resid_kernel, n_pairs=len(a_list)),
        grid=(n // tn, m // tm),
        in_specs=in_specs,
        out_specs=pl.BlockSpec((tm, tn), lambda j, i: (i, j)),
        out_shape=jax.ShapeDtypeStruct((m, n), F32),
        scratch_shapes=scratch,
        compiler_params=_cparams(("arbitrary", "arbitrary")),
        name="mm_resid",
    )(*args)


def _causal_conv(buf_ref, x, cw, t_len):
    buf_ref[pl.ds(SUBLANES, t_len), :] = x
    acc = cw[CONV_W - 1:CONV_W, :] * x
    for j in range(CONV_W - 1):
        off = SUBLANES - (CONV_W - 1) + j
        acc = acc + cw[j:j + 1, :] * buf_ref[pl.ds(off, t_len), :]
    buf_ref[pl.ds(0, SUBLANES), :] = x[t_len - SUBLANES:, :]
    return acc


def _tri_masks():
    row = lax.broadcasted_iota(jnp.int32, (CHUNK, CHUNK), 0)
    col = lax.broadcasted_iota(jnp.int32, (CHUNK, CHUNK), 1)
    return row >= col, row > col, row == col


def _gated_head_norm(o, nw, z):
    o = o * lax.rsqrt(jnp.mean(o * o, axis=-1, keepdims=True) + EPS) * nw
    return o * _silu(z)


def _gdn_kernel(q_ref, k_ref, v_ref, z_ref, ba_ref, cwq_ref, cwk_ref, cwv_ref, alog_ref, dtb_ref,
                nw_ref, o_ref, qbuf, kbuf, vbuf, qs, ks, vs, gcs, gct, bs, u_s, w_s, qk_s, qd_s, kd_s, s_ref,
                *, n_heads, t_len):
    t = pl.program_id(1)
    n_chunks = t_len // CHUNK
    width = n_heads * HEAD_DIM

    @pl.when(t == 0)
    def _():
        zeros = jnp.zeros((SUBLANES, width), F32)
        qbuf[pl.ds(0, SUBLANES), :] = zeros
        kbuf[pl.ds(0, SUBLANES), :] = zeros
        vbuf[pl.ds(0, SUBLANES), :] = zeros
        s_ref[...] = jnp.zeros_like(s_ref)

    qc = _silu(_causal_conv(qbuf, q_ref[0], cwq_ref[...], t_len))
    kc = _silu(_causal_conv(kbuf, k_ref[0], cwk_ref[...], t_len))
    vs[...] = _silu(_causal_conv(vbuf, v_ref[0], cwv_ref[...], t_len))
    for h in range(n_heads):
        sl = slice(h * HEAD_DIM, (h + 1) * HEAD_DIM)
        qh = qc[:, sl]
        kh = kc[:, sl]
        qs[:, sl] = qh * (lax.rsqrt(jnp.sum(qh * qh, axis=-1, keepdims=True) + EPS) * (HEAD_DIM ** -0.5))
        ks[:, sl] = kh * lax.rsqrt(jnp.sum(kh * kh, axis=-1, keepdims=True) + EPS)

    ba = ba_ref[0]
    g_log = -jnp.exp(alog_ref[...]) * _softplus(ba + dtb_ref[...])
    bs[...] = _sigmoid(ba)
    row = lax.broadcasted_iota(jnp.int32, (t_len, t_len), 0)
    col = lax.broadcasted_iota(jnp.int32, (t_len, t_len), 1)
    block_tri = ((row // CHUNK == col // CHUNK) & (row >= col)).astype(F32)
    gc_blk = jnp.dot(block_tri, g_log, precision=HIGHEST, preferred_element_type=F32)
    gcs[...] = gc_blk
    for c in range(n_chunks):
        gct[c] = gc_blk[c * CHUNK:(c + 1) * CHUNK, :].T

    incl, strict, eye = _tri_masks()
    eye_f = eye.astype(F32)
    nw = nw_ref[...]

    heads = range(n_heads)
    hsl = [slice(h * HEAD_DIM, (h + 1) * HEAD_DIM) for h in heads]
    qsl = [slice(h * HEAD_DIM, h * HEAD_DIM + CHUNK) for h in heads]
    chunks_per_iter = 2
    assert n_chunks % chunks_per_iter == 0

    def pass_a(it, carry):
        items = []
        for cc in range(chunks_per_iter):
            c = it * chunks_per_iter + cc
            rows = pl.ds(pl.multiple_of(c * CHUNK, CHUNK), CHUNK)
            gc_all = gcs[rows, :]
            beta_all = bs[rows, :]
            gct_c = gct[c]
            for h in heads:
                gc = gc_all[:, n_heads + h:n_heads + h + 1]
                beta = beta_all[:, h:h + 1]
                gc_row = gct_c[n_heads + h:n_heads + h + 1, :]
                decay = jnp.where(incl, jnp.exp(jnp.where(incl, gc - gc_row, 0.0)), 0.0)
                items.append(dict(rows=rows, h=h, gc=gc, beta=beta, decay=decay,
                                  q=qs[rows, hsl[h]], k=ks[rows, hsl[h]]))
        for it_ in items:
            it_["kb"] = it_["k"].astype(BF16)
            it_["kk"] = lax.dot_general(it_["kb"], it_["kb"], _NT, preferred_element_type=F32)
        for it_ in items:
            it_["x"] = jnp.where(strict, -(it_["beta"] * it_["kk"] * it_["decay"]), 0.0)
            it_["p"] = eye_f + it_["x"]
        for _ in range(5):
            for it_ in items:
                xb = it_["x"].astype(BF16)
                it_["x"] = jnp.dot(xb, xb, preferred_element_type=F32)
            for it_ in items:
                it_["p"] = it_["p"] + _bdot(it_["p"], it_["x"])
        for it_ in items:
            egc = jnp.exp(it_["gc"])
            it_["egc"] = egc
            v = vs[it_["rows"], hsl[it_["h"]]]
            rhs = jnp.concatenate([v * it_["beta"], it_["k"] * (it_["beta"] * egc)], axis=-1)
            it_["sol"] = _bdot(it_["p"], rhs)
        for it_ in items:
            it_["qk"] = lax.dot_general(it_["q"].astype(BF16), it_["kb"], _NT,
                                        preferred_element_type=F32) * it_["decay"]
        for it_ in items:
            rows, h, gc = it_["rows"], it_["h"], it_["gc"]
            g_last = gc[CHUNK - 1:CHUNK, :]
            u_s[rows, hsl[h]] = it_["sol"][:, :HEAD_DIM]
            w_s[rows, hsl[h]] = it_["sol"][:, HEAD_DIM:].astype(BF16)
            qk_s[rows, qsl[h]] = it_["qk"].astype(BF16)
            qd_s[rows, hsl[h]] = (it_["q"] * it_["egc"]).astype(BF16)
            kd_s[rows, hsl[h]] = (it_["k"] * jnp.exp(g_last - gc)).astype(BF16)
        return carry

    lax.fori_loop(0, n_chunks // chunks_per_iter, pass_a, 0)

    def pass_b(c, carry):
        rows = pl.ds(pl.multiple_of(c * CHUNK, CHUNK), CHUNK)
        gc_all = gcs[rows, :]
        states = [s_ref[h] for h in heads]
        sbs = [s.astype(BF16) for s in states]
        v_new = [u_s[rows, hsl[h]] - jnp.dot(w_s[rows, hsl[h]], sbs[h], preferred_element_type=F32)
                 for h in heads]
        vbs = [v.astype(BF16) for v in v_new]
        for h in heads:
            g_last = gc_all[CHUNK - 1:CHUNK, n_heads + h:n_heads + h + 1]
            s_ref[h] = states[h] * jnp.exp(g_last) + lax.dot_general(kd_s[rows, hsl[h]], vbs[h], _TN,
                                                                      preferred_element_type=F32)
        outs = [jnp.dot(qd_s[rows, hsl[h]], sbs[h], preferred_element_type=F32)
                + jnp.dot(qk_s[rows, qsl[h]], vbs[h], preferred_element_type=F32) for h in heads]
        for h in heads:
            o = _gated_head_norm(outs[h], nw, z_ref[0, rows, hsl[h]])
            o_ref[0, rows, hsl[h]] = o.astype(o_ref.dtype)
        return carry

    lax.fori_loop(0, n_chunks, pass_b, 0)


def _gdn(proj, ba, conv_w, a_log, dt_bias, norm_w, *, col0, t_len=256):
    bsz, s, _ = proj.shape
    n_heads = a_log.shape[0]
    width = n_heads * HEAD_DIM
    cb = col0 // width

    def col_spec(group):
        return pl.BlockSpec((1, t_len, width), lambda b, t: (b, t, cb + group))

    def cw_spec(group):
        return pl.BlockSpec((CONV_W, width), lambda b, t: (0, group))

    row = jnp.zeros((1, LANES), F32)
    alog_row = row.at[0, n_heads:2 * n_heads].set(a_log.astype(F32))
    dtb_row = row.at[0, n_heads:2 * n_heads].set(dt_bias.astype(F32))
    small = pl.BlockSpec((1, LANES), lambda b, t: (0, 0))
    kern = functools.partial(_gdn_kernel, n_heads=n_heads, t_len=t_len)
    return pl.pallas_call(
        kern,
        grid=(bsz, s // t_len),
        in_specs=[col_spec(0), col_spec(1), col_spec(2), col_spec(3),
                  pl.BlockSpec((1, t_len, LANES), lambda b, t: (b, t, 0)),
                  cw_spec(0), cw_spec(1), cw_spec(2), small, small,
                  pl.BlockSpec((1, HEAD_DIM), lambda b, t: (0, 0))],
        out_specs=pl.BlockSpec((1, t_len, width), lambda b, t: (b, t, 0)),
        out_shape=jax.ShapeDtypeStruct((bsz, s, width), BF16),
        scratch_shapes=[pltpu.VMEM((SUBLANES + t_len, width), F32)] * 3
        + [pltpu.VMEM((t_len, width), F32)] * 3
        + [pltpu.VMEM((t_len, LANES), F32),
           pltpu.VMEM((t_len // CHUNK, LANES, CHUNK), F32),
           pltpu.VMEM((t_len, LANES), F32),
           pltpu.VMEM((t_len, width), F32)]
        + [pltpu.VMEM((t_len, width), BF16)] * 4
        + [pltpu.VMEM((n_heads, HEAD_DIM, HEAD_DIM), F32)],
        compiler_params=_cparams(("parallel", "arbitrary")),
        name="gdn",
    )(proj, proj, proj, proj, ba, conv_w, conv_w, conv_w, alog_row, dtb_row,
      norm_w.reshape(1, HEAD_DIM).astype(F32))


def _hgrn_kernel(q_ref, f_ref, i_ref, g_ref, lb_ref, nw_ref, o_ref, qs, ks, ls, st_ref, *, heads, t_len):
    t = pl.program_id(1)

    @pl.when(t == 0)
    def _():
        st_ref[...] = jnp.zeros_like(st_ref)

    lb = lb_ref[...]
    forget = lb + (1.0 - lb) * _sigmoid(f_ref[0])
    ks[...] = 1.0 - forget
    qs[...] = _silu(q_ref[0]) * (HEAD_DIM ** -0.5)
    row = lax.broadcasted_iota(jnp.int32, (t_len, t_len), 0)
    col = lax.broadcasted_iota(jnp.int32, (t_len, t_len), 1)
    block_tri = ((row // CHUNK == col // CHUNK) & (row >= col)).astype(BF16)
    log_f = jnp.log(forget)
    hi = log_f.astype(BF16)
    rest = log_f - hi.astype(F32)
    mid = rest.astype(BF16)
    lo = (rest - mid.astype(F32)).astype(BF16)
    ls[...] = (jnp.dot(block_tri, hi, preferred_element_type=F32)
               + (jnp.dot(block_tri, mid, preferred_element_type=F32)
                  + jnp.dot(block_tri, lo, preferred_element_type=F32)))

    incl, _, _ = _tri_masks()
    nw = nw_ref[...]
    mid = CHUNK // 2 - 1

    hsl = [slice(j * HEAD_DIM, (j + 1) * HEAD_DIM) for j in range(heads)]

    def chunk_body(c, carry):
        rows = pl.ds(pl.multiple_of(c * CHUNK, CHUNK), CHUNK)
        items = []
        for j in range(heads):
            b = ls[rows, hsl[j]]
            items.append(dict(b=b, b_mid=b[mid:mid + 1, :], b_last=b[CHUNK - 1:CHUNK, :],
                              q=qs[rows, hsl[j]], k=ks[rows, hsl[j]],
                              vb=i_ref[0, rows, hsl[j]].astype(BF16)))
        for it in items:
            it["att"] = jnp.where(incl, _bdot_g(it["q"] * jnp.exp(it["b"] - it["b_mid"]),
                                                it["k"] * jnp.exp(it["b_mid"] - it["b"]), _NT), 0.0)
        for j, it in enumerate(items):
            st = st_ref[j]
            it["o"] = _bdot_g(it["q"] * jnp.exp(it["b"]), st, _NT)
            st_ref[j] = st * jnp.exp(it["b_last"]) + lax.dot_general(
                it["vb"], (it["k"] * jnp.exp(it["b_last"] - it["b"])).astype(BF16), _TN,
                preferred_element_type=F32)
        for it in items:
            it["o"] = it["o"] + jnp.dot(it["att"].astype(BF16), it["vb"], preferred_element_type=F32)
        for j, it in enumerate(items):
            o = _gated_head_norm(it["o"], nw, g_ref[0, rows, hsl[j]])
            o_ref[0, rows, hsl[j]] = o.astype(o_ref.dtype)
        return carry

    lax.fori_loop(0, t_len // CHUNK, chunk_body, 0)


def _hgrn(proj, lb, norm_w, *, col0, n_heads, t_len=256):
    bsz, s, _ = proj.shape
    width = n_heads * HEAD_DIM
    cb = col0 // width

    def col_spec(group):
        return pl.BlockSpec((1, t_len, width), lambda b, t: (b, t, cb + group))

    kern = functools.partial(_hgrn_kernel, heads=n_heads, t_len=t_len)
    return pl.pallas_call(
        kern,
        grid=(bsz, s // t_len),
        in_specs=[col_spec(0), col_spec(1), col_spec(2), col_spec(3),
                  pl.BlockSpec((1, width), lambda b, t: (0, 0)),
                  pl.BlockSpec((1, HEAD_DIM), lambda b, t: (0, 0))],
        out_specs=pl.BlockSpec((1, t_len, width), lambda b, t: (b, t, 0)),
        out_shape=jax.ShapeDtypeStruct((bsz, s, width), BF16),
        scratch_shapes=[pltpu.VMEM((t_len, width), F32)] * 3
        + [pltpu.VMEM((n_heads, HEAD_DIM, HEAD_DIM), F32)],
        compiler_params=_cparams(("parallel", "arbitrary")),
        name="hgrn2",
    )(proj, proj, proj, proj, lb.reshape(1, width).astype(F32), norm_w.reshape(1, HEAD_DIM).astype(F32))


def _rglru_kernel(y_ref, x_ref, cw_ref, cb_ref, wa_ref, ba_ref, wx_ref, bx_ref, lam_ref, o_ref,
                  xbuf, a_s, b_s, h_s, hcar, *, blocks, t_len):
    t = pl.program_id(2)

    @pl.when(t == 0)
    def _():
        xbuf[pl.ds(0, SUBLANES), :] = jnp.zeros((SUBLANES, blocks * RG_BLOCK), F32)
        hcar[...] = jnp.zeros_like(hcar)

    xc = _causal_conv(xbuf, x_ref[0], cw_ref[...], t_len) + cb_ref[...]
    for n in range(blocks):
        sl = slice(n * RG_BLOCK, (n + 1) * RG_BLOCK)
        xb = xc[:, sl]
        r = _sigmoid(_bdot(xb, wa_ref[n]) + ba_ref[:, sl])
        gi = _sigmoid(_bdot(xb, wx_ref[n]) + bx_ref[:, sl])
        log_a = (-RG_C) * r * _softplus(-lam_ref[:, sl])
        a = jnp.exp(log_a)
        a_s[:, sl] = a
        one_minus_a2 = -jnp.tanh(log_a) * (a * a + 1.0)
        b_s[:, sl] = jnp.sqrt(jnp.maximum(one_minus_a2, 0.0)) * (gi * xb)

    def row_body(i, h):
        h = a_s[pl.ds(i, 1), :] * h + b_s[pl.ds(i, 1), :]
        h_s[pl.ds(i, 1), :] = h
        return h

    hcar[...] = lax.fori_loop(0, t_len, row_body, hcar[...])
    o_ref[0] = (jax.nn.gelu(y_ref[0], approximate=True) * h_s[...]).astype(o_ref.dtype)


def _rglru(yx, conv_w, conv_b, wa, ba, wx, bx, lam, *, blocks_per_step=8, t_len=256):
    bsz, s, w2 = yx.shape
    width = w2 // 2
    gw = blocks_per_step * RG_BLOCK
    n_g = width // gw
    vec = lambda a: a.reshape(1, width).astype(F32)
    vspec = pl.BlockSpec((1, gw), lambda b, g, t: (0, g))
    wspec = pl.BlockSpec((blocks_per_step, RG_BLOCK, RG_BLOCK), lambda b, g, t: (g, 0, 0))
    kern = functools.partial(_rglru_kernel, blocks=blocks_per_step, t_len=t_len)
    return pl.pallas_call(
        kern,
        grid=(bsz, n_g, s // t_len),
        in_specs=[pl.BlockSpec((1, t_len, gw), lambda b, g, t: (b, t, g)),
                  pl.BlockSpec((1, t_len, gw), lambda b, g, t: (b, t, n_g + g)),
                  pl.BlockSpec((CONV_W, gw), lambda b, g, t: (0, g)),
                  vspec, wspec, vspec, wspec, vspec, vspec],
        out_specs=pl.BlockSpec((1, t_len, gw), lambda b, g, t: (b, t, g)),
        out_shape=jax.ShapeDtypeStruct((bsz, s, width), BF16),
        scratch_shapes=[pltpu.VMEM((SUBLANES + t_len, gw), F32)]
        + [pltpu.VMEM((t_len, gw), F32)] * 3
        + [pltpu.VMEM((1, gw), F32)],
        compiler_params=_cparams(("parallel", "parallel", "arbitrary")),
        name="rglru",
    )(yx, yx, conv_w.astype(F32), vec(conv_b), wa.astype(BF16), vec(ba), wx.astype(BF16), vec(bx), vec(lam))


def _norm_router_kernel(x_ref, nw_ref, w_ref, o_ref, *, n_experts):
    x = x_ref[...]
    y = x * lax.rsqrt(jnp.mean(x * x, axis=-1, keepdims=True) + EPS) * nw_ref[...]
    w = w_ref[...]
    y_hi = y.astype(BF16)
    y_lo = (y - y_hi.astype(F32)).astype(BF16)
    w_hi = w.astype(BF16)
    w_lo = (w - w_hi.astype(F32)).astype(BF16)
    logits = (jnp.dot(y_hi, w_hi, preferred_element_type=F32)
              + (jnp.dot(y_lo, w_hi, preferred_element_type=F32)
                 + jnp.dot(y_hi, w_lo, preferred_element_type=F32)))
    lane = lax.broadcasted_iota(jnp.int32, logits.shape, 1).astype(F32)
    neg = jnp.float32(-jnp.inf)
    l1 = jnp.where(lane < n_experts, logits, neg)
    m1 = jnp.max(l1, axis=-1, keepdims=True)
    i1 = jnp.min(jnp.where(l1 == m1, lane, float(LANES)), axis=-1, keepdims=True)
    l2 = jnp.where(lane == i1, neg, l1)
    m2 = jnp.max(l2, axis=-1, keepdims=True)
    i2 = jnp.min(jnp.where(l2 == m2, lane, float(LANES)), axis=-1, keepdims=True)
    e2 = jnp.exp(m2 - m1)
    g1 = 1.0 / (1.0 + e2)
    g2 = e2 / (1.0 + e2)
    out = jnp.where(lane == 0, i1, 0.0)
    out = jnp.where(lane == 1, i2, out)
    out = jnp.where(lane == 2, g1, out)
    out = jnp.where(lane == 3, g2, out)
    o_ref[...] = out


def _norm_router(x, nw, router_w, tm=512):
    m, d = x.shape
    n_experts = router_w.shape[1]
    w = jnp.zeros((d, LANES), F32).at[:, :n_experts].set(router_w.astype(F32))
    return pl.pallas_call(
        functools.partial(_norm_router_kernel, n_experts=n_experts),
        grid=(m // tm,),
        in_specs=[pl.BlockSpec((tm, d), lambda i: (i, 0)),
                  pl.BlockSpec((1, d), lambda i: (0, 0)),
                  pl.BlockSpec((d, LANES), lambda i: (0, 0))],
        out_specs=pl.BlockSpec((tm, LANES), lambda i: (i, 0)),
        out_shape=jax.ShapeDtypeStruct((m, LANES), F32),
        compiler_params=_cparams(("parallel",)),
        name="norm_router",
    )(x, nw.reshape(1, d).astype(F32), w)


def _start_row_gather(idx_ref, base, src3, dst3, sem, n_rows):
    def body(g, carry):
        for u in range(SUBLANES):
            row = idx_ref[base + g * SUBLANES + u]
            src = src3.at[lax.shift_right_logical(row, 3), pl.ds(row & (SUBLANES - 1), 1), :]
            pltpu.make_async_copy(src, dst3.at[g, pl.ds(u, 1), :], sem).start(priority=u % 2)
        return carry
    lax.fori_loop(0, n_rows // SUBLANES, body, 0)


def _wait_row_gather(src3, dst3, sem, n_rows):
    n = n_rows // SUBLANES
    pltpu.make_async_copy(src3.at[pl.ds(0, n)], dst3.at[pl.ds(0, n)], sem).wait()


def _gather_norm_kernel(idx_ref, nq_ref, x_hbm, nw_ref, o_ref, buf, sem):
    c = pl.program_id(0)
    slot = c % 2

    def start(chunk, s):
        _start_row_gather(idx_ref, chunk * MOE_CHUNK, x_hbm, buf.at[s], sem.at[s], nq_ref[chunk] * MOE_SUB)

    @pl.when(c == 0)
    def _():
        start(0, 0)

    @pl.when(c + 1 < pl.num_programs(0))
    def _():
        start(c + 1, 1 - slot)

    for sb in range(MOE_CHUNK // MOE_SUB):
        @pl.when(sb < nq_ref[c])
        def _():
            _wait_row_gather(x_hbm, buf.at[slot], sem.at[slot], MOE_SUB)

    for sb in range(MOE_CHUNK // MOE_SUB):
        rows = pl.ds(sb * MOE_SUB, MOE_SUB)

        @pl.when(sb < nq_ref[c])
        def _(sb=sb, rows=rows):
            tiles = pl.ds(sb * (MOE_SUB // SUBLANES), MOE_SUB // SUBLANES)
            x = buf[slot, tiles].reshape(MOE_SUB, o_ref.shape[1])
            y = x * lax.rsqrt(jnp.mean(x * x, axis=-1, keepdims=True) + EPS) * nw_ref[...]
            o_ref[rows, :] = y.astype(o_ref.dtype)

        @pl.when(sb >= nq_ref[c])
        def _():
            o_ref[rows, :] = jnp.zeros((MOE_SUB, o_ref.shape[1]), o_ref.dtype)


def _gather_norm(src_token, chunk_nq, x, nw):
    rows = src_token.shape[0]
    d = x.shape[1]
    grid_spec = pltpu.PrefetchScalarGridSpec(
        num_scalar_prefetch=2,
        grid=(rows // MOE_CHUNK,),
        in_specs=[pl.BlockSpec(memory_space=pl.ANY),
                  pl.BlockSpec((1, d), lambda c, idx, nq: (0, 0))],
        out_specs=pl.BlockSpec((MOE_CHUNK, d), lambda c, idx, nq: (c, 0)),
        scratch_shapes=[pltpu.VMEM((2, MOE_CHUNK // SUBLANES, SUBLANES, d), F32),
                        pltpu.SemaphoreType.DMA((2,))],
    )
    return pl.pallas_call(
        _gather_norm_kernel,
        grid_spec=grid_spec,
        out_shape=jax.ShapeDtypeStruct((rows, d), BF16),
        compiler_params=_cparams(("arbitrary",)),
        name="gather_norm",
    )(src_token, chunk_nq, x.reshape(-1, SUBLANES, d), nw.reshape(1, d).astype(F32))


def _for_valid_rows(nq, o_ref, fn, pred=True):
    total = o_ref.shape[0]
    for v in range(total // MOE_SUB + 1):
        @pl.when(jnp.logical_and(nq == v, pred))
        def _(v=v):
            rows = v * MOE_SUB
            if rows:
                fn(rows)
            if rows < total:
                o_ref[pl.ds(rows, total - rows), :] = jnp.zeros((total - rows, o_ref.shape[1]), o_ref.dtype)


def _gmm_swiglu_kernel(te_ref, nq_ref, nv_ref, a_ref, wg_ref, wu_ref, o_ref):
    c = pl.program_id(0)

    def compute(rows):
        a = a_ref[pl.ds(0, rows), :]
        g = jnp.dot(a, wg_ref[0].astype(BF16), preferred_element_type=F32)
        u = jnp.dot(a, wu_ref[0].astype(BF16), preferred_element_type=F32)
        o_ref[pl.ds(0, rows), :] = (_silu(g) * u).astype(o_ref.dtype)

    _for_valid_rows(nq_ref[c], o_ref, compute)


def _gmm_swiglu(chunk_expert, chunk_nq, n_valid, a, w_gu, tn):
    m, k = a.shape
    f = w_gu.shape[2] // 2
    nb = f // tn
    last = lambda c, nv: jnp.minimum(c, nv[0] - 1)
    col = lambda j, c, nv: jnp.where(c < nv[0], j, nb - 1)
    grid_spec = pltpu.PrefetchScalarGridSpec(
        num_scalar_prefetch=3,
        grid=(m // MOE_CHUNK, nb),
        in_specs=[pl.BlockSpec((MOE_CHUNK, k), lambda c, j, te, nq, nv: (last(c, nv), 0)),
                  pl.BlockSpec((1, k, tn), lambda c, j, te, nq, nv: (te[c], 0, col(j, c, nv))),
                  pl.BlockSpec((1, k, tn), lambda c, j, te, nq, nv: (te[c], 0, col(j, c, nv) + nb))],
        out_specs=pl.BlockSpec((MOE_CHUNK, tn), lambda c, j, te, nq, nv: (c, j)),
    )
    return pl.pallas_call(
        _gmm_swiglu_kernel,
        grid_spec=grid_spec,
        out_shape=jax.ShapeDtypeStruct((m, f), BF16),
        compiler_params=_cparams(("arbitrary", "arbitrary")),
        name="gmm_swiglu",
    )(chunk_expert, chunk_nq, n_valid, a, w_gu, w_gu)


def _gmm_down_kernel(te_ref, nq_ref, nv_ref, a_ref, w_ref, o_ref):
    c = pl.program_id(0)

    def compute(rows):
        w = w_ref[0].astype(BF16)
        o_ref[pl.ds(0, rows), :] = jnp.dot(a_ref[pl.ds(0, rows), :], w, preferred_element_type=F32)

    _for_valid_rows(nq_ref[c], o_ref, compute)


def _gmm_down(chunk_expert, chunk_nq, n_valid, a, w_d, tn):
    m, k = a.shape
    n = w_d.shape[2]
    nj = n // tn
    last = lambda c, nv: jnp.minimum(c, nv[0] - 1)
    col = lambda j, c, nv: jnp.where(c < nv[0], j, nj - 1)
    grid_spec = pltpu.PrefetchScalarGridSpec(
        num_scalar_prefetch=3,
        grid=(m // MOE_CHUNK, nj),
        in_specs=[pl.BlockSpec((MOE_CHUNK, k), lambda c, j, te, nq, nv: (last(c, nv), 0)),
                  pl.BlockSpec((1, k, tn), lambda c, j, te, nq, nv: (te[c], 0, col(j, c, nv)))],
        out_specs=pl.BlockSpec((MOE_CHUNK, tn), lambda c, j, te, nq, nv: (c, j)),
    )
    return pl.pallas_call(
        _gmm_down_kernel,
        grid_spec=grid_spec,
        out_shape=jax.ShapeDtypeStruct((m, n), F32),
        compiler_params=pltpu.CompilerParams(dimension_semantics=("arbitrary", "arbitrary"),
                                             vmem_limit_bytes=VMEM_LIMIT_MAX),
        name="gmm_down",
    )(chunk_expert, chunk_nq, n_valid, a, w_d)


def _combine_norm_kernel(pos_ref, x_ref, r_ref, ys_hbm, w_ref, o_ref, buf, sem, *, tm):
    i = pl.program_id(0)
    n = pl.num_programs(0)
    slot = i % 2

    def start(tile, s):
        for kk in range(TOP_K):
            _start_row_gather(pos_ref, kk * (n * tm) + tile * tm, ys_hbm, buf.at[s, kk], sem.at[s], tm)

    @pl.when(i == 0)
    def _():
        start(0, 0)

    @pl.when(i + 1 < n)
    def _():
        start(i + 1, 1 - slot)

    for kk in range(TOP_K):
        _wait_row_gather(ys_hbm, buf.at[slot, kk], sem.at[slot], tm)
    r = r_ref[...]
    d = x_ref.shape[1]
    moe = r[:, TOP_K:TOP_K + 1] * buf[slot, 0].reshape(tm, d)
    for kk in range(1, TOP_K):
        moe = moe + r[:, TOP_K + kk:TOP_K + kk + 1] * buf[slot, kk].reshape(tm, d)
    x = x_ref[...] + moe
    y = x * lax.rsqrt(jnp.mean(x * x, axis=-1, keepdims=True) + EPS)
    o_ref[...] = y * w_ref[...]


def _combine_norm(pos, x, route, ys, w, tm=256):
    m, d = x.shape
    grid_spec = pltpu.PrefetchScalarGridSpec(
        num_scalar_prefetch=1,
        grid=(m // tm,),
        in_specs=[pl.BlockSpec((tm, d), lambda i, p: (i, 0)),
                  pl.BlockSpec((tm, LANES), lambda i, p: (i, 0)),
                  pl.BlockSpec(memory_space=pl.ANY),
                  pl.BlockSpec((1, d), lambda i, p: (0, 0))],
        out_specs=pl.BlockSpec((tm, d), lambda i, p: (i, 0)),
        scratch_shapes=[pltpu.VMEM((2, TOP_K, tm // SUBLANES, SUBLANES, d), F32),
                        pltpu.SemaphoreType.DMA((2,))],
    )
    return pl.pallas_call(
        functools.partial(_combine_norm_kernel, tm=tm),
        grid_spec=grid_spec,
        out_shape=jax.ShapeDtypeStruct((m, d), F32),
        compiler_params=_cparams(("arbitrary",)),
        name="combine_norm",
    )(pos, x, route, ys.reshape(-1, SUBLANES, d), w.reshape(1, d).astype(F32))


def _moe_routing(route, n_experts):
    m = route.shape[0]
    n_sub = MOE_CHUNK // MOE_SUB
    ids = route[:, :TOP_K].astype(jnp.int32)
    flat_e = ids.reshape(-1)
    onehot = (flat_e[:, None] == jnp.arange(n_experts)[None, :]).astype(jnp.int32)
    rank = jnp.sum((jnp.cumsum(onehot, axis=0) - onehot) * onehot, axis=1)
    counts = jnp.sum(onehot, axis=0)
    subs = (counts + MOE_SUB - 1) // MOE_SUB
    n_chunks = (subs + n_sub - 1) // n_sub
    chunk_end = jnp.cumsum(n_chunks)
    chunk_start = chunk_end - n_chunks
    max_chunks = (m * TOP_K // MOE_SUB + n_experts + n_sub - 1) // n_sub + n_experts
    rows = max_chunks * MOE_CHUNK
    base = subs // jnp.maximum(n_chunks, 1)
    extra = subs - base * n_chunks
    n_valid = chunk_end[n_experts - 1].astype(jnp.int32)
    cidx = jnp.minimum(jnp.arange(max_chunks, dtype=jnp.int32), n_valid - 1)
    chunk_expert = jnp.sum((cidx[:, None] >= chunk_end[None, :]).astype(jnp.int32), axis=1).astype(jnp.int32)
    sel = (chunk_expert[:, None] == jnp.arange(n_experts)[None, :]).astype(jnp.int32)
    per_chunk = lambda v: jnp.sum(sel * v[None, :], axis=1)
    within = cidx - per_chunk(chunk_start)
    chunk_nq = per_chunk(base) + (within < per_chunk(extra))
    is_valid = jnp.arange(max_chunks) < n_valid
    chunk_nq = jnp.where(is_valid, chunk_nq, 0).astype(jnp.int32)
    first_rank = (jnp.minimum(within, per_chunk(extra)) * (per_chunk(base) + 1)
                  + jnp.maximum(within - per_chunk(extra), 0) * per_chunk(base)) * MOE_SUB
    mine = (flat_e[:, None] == chunk_expert[None, :]) & is_valid[None, :]
    chunk_of = (jnp.sum(onehot * chunk_start[None, :], axis=1)
                + jnp.sum((mine & (first_rank[None, :] <= rank[:, None])).astype(jnp.int32), axis=1) - 1)
    at_chunk = (chunk_of[:, None] == jnp.arange(max_chunks)[None, :]).astype(jnp.int32)
    pos = chunk_of * MOE_CHUNK + rank - jnp.sum(at_chunk * first_rank[None, :], axis=1)
    src_token = jnp.zeros((rows,), jnp.int32).at[pos].set(jnp.arange(m * TOP_K, dtype=jnp.int32) // TOP_K)
    pos_kmajor = pos.reshape(m, TOP_K).T.reshape(-1).astype(jnp.int32)
    return src_token, chunk_expert, chunk_nq, n_valid.reshape(1), pos_kmajor


def kernel(x, norm_w, final_norm_w, mix_in_w, gdn_conv_w, gdn_a_log, gdn_dt_bias, gdn_norm_w, hgrn_lb_logits, hgrn_norm_w, mix_out_w, ffn_gate_up_w, ffn_down_w, rg_in_w, rg_conv_w, rg_conv_b, rg_gate_a_w, rg_gate_a_b, rg_gate_x_w, rg_gate_x_b, rg_lambda, rg_out_w, moe_router_w, moe_gate_up_w, moe_down_w):
    bsz, s, d = x.shape
    m = bsz * s
    n_heads = gdn_a_log.shape[1]
    hw = n_heads * HEAD_DIM
    n_experts = moe_router_w.shape[2]
    xr = x.reshape(m, d)

    hgrn_lb = jnp.cumsum(jax.nn.softmax(hgrn_lb_logits.astype(F32), axis=0), axis=0)
    w_in_t = jnp.swapaxes(mix_in_w[0], 0, 1)
    h = _rmsnorm(xr, norm_w[0, 0], BF16)
    proj_a = _mm_nt(h, w_in_t, F32, 1024, 1024, 0, 4 * hw).reshape(bsz, s, 4 * hw)
    proj_b = _mm_nt(h, w_in_t, F32, 1024, 1024, 4 * hw + 2 * n_heads, 4 * hw).reshape(bsz, s, 4 * hw)
    ba = _mm_nt(h, w_in_t, F32, 1024, LANES, 4 * hw, LANES).reshape(bsz, s, LANES)
    o_a = _gdn(proj_a, ba, gdn_conv_w[0].astype(F32), gdn_a_log[0], gdn_dt_bias[0], gdn_norm_w[0], col0=0)
    o_b = _hgrn(proj_b, hgrn_lb[0], hgrn_norm_w[0], col0=0, n_heads=n_heads)
    xr = _mm_resid([o_a.reshape(m, hw), o_b.reshape(m, hw)], mix_out_w[0], xr, 1024, 1024)
    act = _norm_mm_swiglu(xr, norm_w[0, 1], ffn_gate_up_w[0], 1024, 512)
    xr = _mm_resid([act], ffn_down_w[0], xr, 512, 512)

    yx = _norm_mm(xr, norm_w[1, 0], rg_in_w[0], F32, 1024, 1024).reshape(bsz, s, -1)
    rec = _rglru(yx, rg_conv_w[0], rg_conv_b[0], rg_gate_a_w[0], rg_gate_a_b[0],
                 rg_gate_x_w[0], rg_gate_x_b[0], rg_lambda[0])
    xr = _mm_resid([rec.reshape(m, d)], rg_out_w[0], xr, 1024, 1024)
    route = _norm_router(xr, norm_w[1, 1], moe_router_w[0])
    src_token, chunk_expert, chunk_nq, n_valid, pos = _moe_routing(route, n_experts)
    hs = _gather_norm(src_token, chunk_nq, xr, norm_w[1, 1])
    act = _gmm_swiglu(chunk_expert, chunk_nq, n_valid, hs, moe_gate_up_w[0], 512)
    ys = _gmm_down(chunk_expert, chunk_nq, n_valid, act, moe_down_w[0], 256)
    out = _combine_norm(pos, xr, route, ys, final_norm_w)
    return out.reshape(bsz, s, d)
```

```python
import functools

import jax
import jax.numpy as jnp
from jax import lax
from jax.experimental import pallas as pl
from jax.experimental.pallas import tpu as pltpu

F32 = jnp.float32
BF16 = jnp.bfloat16
HIGHEST = lax.Precision.HIGHEST

EPS = 1e-6
CHUNK = 64
CONV_W = 4
HEAD_DIM = 128
RG_BLOCK = 256
RG_C = 8.0
TOP_K = 2
LANES = 128
SUBLANES = 8
VMEM_LIMIT = 56 * 1024 * 1024
VMEM_LIMIT_MAX = 60 * 1024 * 1024
MOE_CHUNK = 1024
MOE_SUB = 128

_NT = (((1,), (1,)), ((), ()))
_TN = (((0,), (0,)), ((), ()))


def _cparams(sem):
    return pltpu.CompilerParams(dimension_semantics=sem, vmem_limit_bytes=VMEM_LIMIT)


def _sigmoid(x):
    return jax.nn.sigmoid(x)


def _silu(x):
    return x * _sigmoid(x)


def _softplus(x):
    return jnp.maximum(x, 0.0) + jnp.log1p(jnp.exp(-jnp.abs(x)))


def _bdot(a, b):
    return jnp.dot(a.astype(BF16), b.astype(BF16), preferred_element_type=F32)


def _bdot_g(a, b, dims):
    return lax.dot_general(a.astype(BF16), b.astype(BF16), dims, preferred_element_type=F32)


def _rmsnorm_kernel(x_ref, w_ref, o_ref):
    x = x_ref[...]
    y = x * lax.rsqrt(jnp.mean(x * x, axis=-1, keepdims=True) + EPS)
    o_ref[...] = (y * w_ref[...]).astype(o_ref.dtype)


def _rmsnorm(x, w, out_dtype, tm=512):
    m, d = x.shape
    return pl.pallas_call(
        _rmsnorm_kernel,
        grid=(m // tm,),
        in_specs=[pl.BlockSpec((tm, d), lambda i: (i, 0)),
                  pl.BlockSpec((1, d), lambda i: (0, 0))],
        out_specs=pl.BlockSpec((tm, d), lambda i: (i, 0)),
        out_shape=jax.ShapeDtypeStruct((m, d), out_dtype),
        compiler_params=_cparams(("parallel",)),
        name="rmsnorm",
    )(x, w.reshape(1, d).astype(F32))


def _bf16_weight(w_ref, w_s, fresh):
    if not fresh:
        return w_s[...]
    w = w_ref[...].astype(BF16)
    w_s[...] = w
    return w


def _first_row_tile_or_not(body):
    first = pl.program_id(1) == 0
    pl.when(first)(lambda: body(True))
    pl.when(jnp.logical_not(first))(lambda: body(False))


def _mm_nt_kernel(*refs, shift):
    a_ref, w_ref = refs[0], refs[1]
    o_ref, w_s = refs[-2], refs[-1]

    def body(fresh):
        if fresh:
            w = w_ref[...]
            if shift:
                w = jnp.concatenate([w, refs[2][...]], axis=0)[shift:shift + w_ref.shape[0], :]
            w = w.astype(BF16)
            w_s[...] = w
        else:
            w = w_s[...]
        o_ref[...] = lax.dot_general(a_ref[...], w, _NT, preferred_element_type=F32).astype(o_ref.dtype)

    _first_row_tile_or_not(body)


def _mm_nt(a, wt, out_dtype, tm, tn, row0, n_rows):
    m, k = a.shape
    base, shift = row0 // tn * tn, row0 % tn
    assert shift % SUBLANES == 0 and shift <= LANES and n_rows % tn == 0
    in_specs = [pl.BlockSpec((tm, k), lambda j, i: (i, 0)),
                pl.BlockSpec((tn, k), lambda j, i: (base // tn + j, 0))]
    args = [a, wt]
    if shift:
        in_specs.append(pl.BlockSpec((LANES, k), lambda j, i: ((base + (j + 1) * tn) // LANES, 0)))
        args.append(wt)
    return pl.pallas_call(
        functools.partial(_mm_nt_kernel, shift=shift),
        grid=(n_rows // tn, m // tm),
        in_specs=in_specs,
        out_specs=pl.BlockSpec((tm, tn), lambda j, i: (i, j)),
        out_shape=jax.ShapeDtypeStruct((m, n_rows), out_dtype),
        scratch_shapes=[pltpu.VMEM((tn, k), BF16)],
        compiler_params=_cparams(("arbitrary", "arbitrary")),
        name="mm_nt",
    )(*args)


def _normed_rows(x_ref, nw_ref, h_s, fresh):
    if not fresh:
        return h_s[...]
    x = x_ref[...]
    h = (x * lax.rsqrt(jnp.mean(x * x, axis=-1, keepdims=True) + EPS) * nw_ref[...]).astype(BF16)
    h_s[...] = h
    return h


def _norm_mm_kernel(x_ref, nw_ref, w_ref, o_ref, h_s):
    def body(fresh):
        h = _normed_rows(x_ref, nw_ref, h_s, fresh)
        o_ref[...] = jnp.dot(h, w_ref[...].astype(BF16), preferred_element_type=F32).astype(o_ref.dtype)

    _first_row_tile_or_not(body)


def _norm_mm(x, nw, w, out_dtype, tm, tn):
    m, k = x.shape
    n = w.shape[1]
    return pl.pallas_call(
        _norm_mm_kernel,
        grid=(m // tm, n // tn),
        in_specs=[pl.BlockSpec((tm, k), lambda i, j: (i, 0)),
                  pl.BlockSpec((1, k), lambda i, j: (0, 0)),
                  pl.BlockSpec((k, tn), lambda i, j: (0, j))],
        out_specs=pl.BlockSpec((tm, tn), lambda i, j: (i, j)),
        out_shape=jax.ShapeDtypeStruct((m, n), out_dtype),
        scratch_shapes=[pltpu.VMEM((tm, k), BF16)],
        compiler_params=_cparams(("arbitrary", "arbitrary")),
        name="norm_mm",
    )(x, nw.reshape(1, k).astype(F32), w)


def _norm_mm_swiglu_kernel(x_ref, nw_ref, wg_ref, wu_ref, o_ref, h_s):
    def body(fresh):
        h = _normed_rows(x_ref, nw_ref, h_s, fresh)
        g = jnp.dot(h, wg_ref[...].astype(BF16), preferred_element_type=F32)
        u = jnp.dot(h, wu_ref[...].astype(BF16), preferred_element_type=F32)
        o_ref[...] = (_silu(g) * u).astype(o_ref.dtype)

    _first_row_tile_or_not(body)


def _norm_mm_swiglu(x, nw, w_gu, tm, tn):
    m, k = x.shape
    f = w_gu.shape[1] // 2
    nb = f // tn
    return pl.pallas_call(
        _norm_mm_swiglu_kernel,
        grid=(m // tm, nb),
        in_specs=[pl.BlockSpec((tm, k), lambda i, j: (i, 0)),
                  pl.BlockSpec((1, k), lambda i, j: (0, 0)),
                  pl.BlockSpec((k, tn), lambda i, j: (0, j)),
                  pl.BlockSpec((k, tn), lambda i, j: (0, j + nb))],
        out_specs=pl.BlockSpec((tm, tn), lambda i, j: (i, j)),
        out_shape=jax.ShapeDtypeStruct((m, f), BF16),
        scratch_shapes=[pltpu.VMEM((tm, k), BF16)],
        compiler_params=_cparams(("arbitrary", "arbitrary")),
        name="norm_mm_swiglu",
    )(x, nw.reshape(1, k).astype(F32), w_gu, w_gu)


def _mm_resid_kernel(*refs, n_pairs):
    r_ref, o_ref = refs[2 * n_pairs], refs[2 * n_pairs + 1]
    w_s = refs[2 * n_pairs + 2:]

    def body(fresh):
        acc = r_ref[...]
        for p in range(n_pairs):
            w = _bf16_weight(refs[2 * p + 1], w_s[p], fresh)
            acc = acc + jnp.dot(refs[2 * p][...], w, preferred_element_type=F32)
        o_ref[...] = acc

    _first_row_tile_or_not(body)


def _mm_resid(a_list, w, resid, tm, tn):
    m, n = resid.shape
    in_specs, args, scratch = [], [], []
    for p, a in enumerate(a_list):
        k = a.shape[1]
        in_specs += [pl.BlockSpec((tm, k), lambda j, i: (i, 0)),
                     pl.BlockSpec((k, tn), lambda j, i, p=p: (p, j))]
        args += [a, w]
        scratch.append(pltpu.VMEM((k, tn), BF16))
    in_specs.append(pl.BlockSpec((tm, tn), lambda j, i: (i, j)))
    args.append(resid)
    return pl.pallas_call(
        functools.partial(_mm_resid_kernel, n_pairs=len(a_list)),
        grid=(n // tn, m // tm),
        in_specs=in_specs,
        out_specs=pl.BlockSpec((tm, tn), lambda j, i: (i, j)),
        out_shape=jax.ShapeDtypeStruct((m, n), F32),
        scratch_shapes=scratch,
        compiler_params=_cparams(("arbitrary", "arbitrary")),
        name="mm_resid",
    )(*args)


def _causal_conv(buf_ref, x, cw, t_len):
    buf_ref[pl.ds(SUBLANES, t_len), :] = x
    acc = cw[CONV_W - 1:CONV_W, :] * x
    for j in range(CONV_W - 1):
        off = SUBLANES - (CONV_W - 1) + j
        acc = acc + cw[j:j + 1, :] * buf_ref[pl.ds(off, t_len), :]
    buf_ref[pl.ds(0, SUBLANES), :] = x[t_len - SUBLANES:, :]
    return acc


def _tri_masks():
    row = lax.broadcasted_iota(jnp.int32, (CHUNK, CHUNK), 0)
    col = lax.broadcasted_iota(jnp.int32, (CHUNK, CHUNK), 1)
    return row >= col, row > col, row == col


def _gated_head_norm(o, nw, z):
    o = o * lax.rsqrt(jnp.mean(o * o, axis=-1, keepdims=True) + EPS) * nw
    return o * _silu(z)


def _gdn_kernel(q_ref, k_ref, v_ref, z_ref, ba_ref, cwq_ref, cwk_ref, cwv_ref, alog_ref, dtb_ref,
                nw_ref, o_ref, qbuf, kbuf, vbuf, qs, ks, vs, gcs, gct, bs, u_s, w_s, qk_s, qd_s, kd_s, s_ref,
                *, n_heads, t_len):
    t = pl.program_id(1)
    n_chunks = t_len // CHUNK
    width = n_heads * HEAD_DIM

    @pl.when(t == 0)
    def _():
        zeros = jnp.zeros((SUBLANES, width), F32)
        qbuf[pl.ds(0, SUBLANES), :] = zeros
        kbuf[pl.ds(0, SUBLANES), :] = zeros
        vbuf[pl.ds(0, SUBLANES), :] = zeros
        s_ref[...] = jnp.zeros_like(s_ref)

    qc = _silu(_causal_conv(qbuf, q_ref[0], cwq_ref[...], t_len))
    kc = _silu(_causal_conv(kbuf, k_ref[0], cwk_ref[...], t_len))
    vs[...] = _silu(_causal_conv(vbuf, v_ref[0], cwv_ref[...], t_len))
    for h in range(n_heads):
        sl = slice(h * HEAD_DIM, (h + 1) * HEAD_DIM)
        qh = qc[:, sl]
        kh = kc[:, sl]
        qs[:, sl] = qh * (lax.rsqrt(jnp.sum(qh * qh, axis=-1, keepdims=True) + EPS) * (HEAD_DIM ** -0.5))
        ks[:, sl] = kh * lax.rsqrt(jnp.sum(kh * kh, axis=-1, keepdims=True) + EPS)

    ba = ba_ref[0]
    g_log = -jnp.exp(alog_ref[...]) * _softplus(ba + dtb_ref[...])
    bs[...] = _sigmoid(ba)
    row = lax.broadcasted_iota(jnp.int32, (t_len, t_len), 0)
    col = lax.broadcasted_iota(jnp.int32, (t_len, t_len), 1)
    block_tri = ((row // CHUNK == col // CHUNK) & (row >= col)).astype(F32)
    gc_blk = jnp.dot(block_tri, g_log, precision=HIGHEST, preferred_element_type=F32)
    gcs[...] = gc_blk
    for c in range(n_chunks):
        gct[c] = gc_blk[c * CHUNK:(c + 1) * CHUNK, :].T

    incl, strict, eye = _tri_masks()
    eye_f = eye.astype(F32)
    nw = nw_ref[...]

    heads = range(n_heads)
    hsl = [slice(h * HEAD_DIM, (h + 1) * HEAD_DIM) for h in heads]
    qsl = [slice(h * HEAD_DIM, h * HEAD_DIM + CHUNK) for h in heads]
    chunks_per_iter = 4
    assert n_chunks % chunks_per_iter == 0

    def pass_a(it, carry):
        items = []
        for cc in range(chunks_per_iter):
            c = it * chunks_per_iter + cc
            rows = pl.ds(pl.multiple_of(c * CHUNK, CHUNK), CHUNK)
            gc_all = gcs[rows, :]
            beta_all = bs[rows, :]
            gct_c = gct[c]
            for h in heads:
                gc = gc_all[:, n_heads + h:n_heads + h + 1]
                beta = beta_all[:, h:h + 1]
                gc_row = gct_c[n_heads + h:n_heads + h + 1, :]
                decay = jnp.where(incl, jnp.exp(jnp.where(incl, gc - gc_row, 0.0)), 0.0)
                items.append(dict(rows=rows, h=h, gc=gc, beta=beta, decay=decay,
                                  q=qs[rows, hsl[h]], k=ks[rows, hsl[h]]))
        for it_ in items:
            it_["kb"] = it_["k"].astype(BF16)
            it_["kk"] = lax.dot_general(it_["kb"], it_["kb"], _NT, preferred_element_type=F32)
        for it_ in items:
            it_["x"] = jnp.where(strict, -(it_["beta"] * it_["kk"] * it_["decay"]), 0.0)
            it_["p"] = eye_f + it_["x"]
        for _ in range(5):
            for it_ in items:
                xb = it_["x"].astype(BF16)
                it_["x"] = jnp.dot(xb, xb, preferred_element_type=F32)
            for it_ in items:
                it_["p"] = it_["p"] + _bdot(it_["p"], it_["x"])
        for it_ in items:
            egc = jnp.exp(it_["gc"])
            it_["egc"] = egc
            v = vs[it_["rows"], hsl[it_["h"]]]
            rhs = jnp.concatenate([v * it_["beta"], it_["k"] * (it_["beta"] * egc)], axis=-1)
            it_["sol"] = _bdot(it_["p"], rhs)
        for it_ in items:
            it_["qk"] = lax.dot_general(it_["q"].astype(BF16), it_["kb"], _NT,
                                        preferred_element_type=F32) * it_["decay"]
        for it_ in items:
            rows, h, gc = it_["rows"], it_["h"], it_["gc"]
            g_last = gc[CHUNK - 1:CHUNK, :]
            u_s[rows, hsl[h]] = it_["sol"][:, :HEAD_DIM]
            w_s[rows, hsl[h]] = it_["sol"][:, HEAD_DIM:].astype(BF16)
            qk_s[rows, qsl[h]] = it_["qk"].astype(BF16)
            qd_s[rows, hsl[h]] = (it_["q"] * it_["egc"]).astype(BF16)
            kd_s[rows, hsl[h]] = (it_["k"] * jnp.exp(g_last - gc)).astype(BF16)
        return carry

    lax.fori_loop(0, n_chunks // chunks_per_iter, pass_a, 0)

    def pass_b(c, carry):
        rows = pl.ds(pl.multiple_of(c * CHUNK, CHUNK), CHUNK)
        gc_all = gcs[rows, :]
        states = [s_ref[h] for h in heads]
        sbs = [s.astype(BF16) for s in states]
        v_new = [u_s[rows, hsl[h]] - jnp.dot(w_s[rows, hsl[h]], sbs[h], preferred_element_type=F32)
                 for h in heads]
        vbs = [v.astype(BF16) for v in v_new]
        for h in heads:
            g_last = gc_all[CHUNK - 1:CHUNK, n_heads + h:n_heads + h + 1]
            s_ref[h] = states[h] * jnp.exp(g_last) + lax.dot_general(kd_s[rows, hsl[h]], vbs[h], _TN,
                                                                      preferred_element_type=F32)
        outs = [jnp.dot(qd_s[rows, hsl[h]], sbs[h], preferred_element_type=F32)
                + jnp.dot(qk_s[rows, qsl[h]], vbs[h], preferred_element_type=F32) for h in heads]
        for h in heads:
            o = _gated_head_norm(outs[h], nw, z_ref[0, rows, hsl[h]])
            o_ref[0, rows, hsl[h]] = o.astype(o_ref.dtype)
        return carry

    lax.fori_loop(0, n_chunks, pass_b, 0, unroll=True)


def _gdn(proj, ba, conv_w, a_log, dt_bias, norm_w, *, col0, t_len=256):
    bsz, s, _ = proj.shape
    n_heads = a_log.shape[0]
    width = n_heads * HEAD_DIM
    cb = col0 // width

    def col_spec(group):
        return pl.BlockSpec((1, t_len, width), lambda b, t: (b, t, cb + group))

    def cw_spec(group):
        return pl.BlockSpec((CONV_W, width), lambda b, t: (0, group))

    row = jnp.zeros((1, LANES), F32)
    alog_row = row.at[0, n_heads:2 * n_heads].set(a_log.astype(F32))
    dtb_row = row.at[0, n_heads:2 * n_heads].set(dt_bias.astype(F32))
    small = pl.BlockSpec((1, LANES), lambda b, t: (0, 0))
    kern = functools.partial(_gdn_kernel, n_heads=n_heads, t_len=t_len)
    return pl.pallas_call(
        kern,
        grid=(bsz, s // t_len),
        in_specs=[col_spec(0), col_spec(1), col_spec(2), col_spec(3),
                  pl.BlockSpec((1, t_len, LANES), lambda b, t: (b, t, 0)),
                  cw_spec(0), cw_spec(1), cw_spec(2), small, small,
                  pl.BlockSpec((1, HEAD_DIM), lambda b, t: (0, 0))],
        out_specs=pl.BlockSpec((1, t_len, width), lambda b, t: (b, t, 0)),
        out_shape=jax.ShapeDtypeStruct((bsz, s, width), BF16),
        scratch_shapes=[pltpu.VMEM((SUBLANES + t_len, width), F32)] * 3
        + [pltpu.VMEM((t_len, width), F32)] * 3
        + [pltpu.VMEM((t_len, LANES), F32),
           pltpu.VMEM((t_len // CHUNK, LANES, CHUNK), F32),
           pltpu.VMEM((t_len, LANES), F32),
           pltpu.VMEM((t_len, width), F32)]
        + [pltpu.VMEM((t_len, width), BF16)] * 4
        + [pltpu.VMEM((n_heads, HEAD_DIM, HEAD_DIM), F32)],
        compiler_params=_cparams(("parallel", "arbitrary")),
        name="gdn",
    )(proj, proj, proj, proj, ba, conv_w, conv_w, conv_w, alog_row, dtb_row,
      norm_w.reshape(1, HEAD_DIM).astype(F32))


def _hgrn_kernel(q_ref, f_ref, i_ref, g_ref, lb_ref, nw_ref, o_ref, qs, ks, ls, st_ref, *, heads, t_len):
    t = pl.program_id(1)

    @pl.when(t == 0)
    def _():
        st_ref[...] = jnp.zeros_like(st_ref)

    lb = lb_ref[...]
    forget = lb + (1.0 - lb) * _sigmoid(f_ref[0])
    ks[...] = 1.0 - forget
    qs[...] = _silu(q_ref[0]) * (HEAD_DIM ** -0.5)
    row = lax.broadcasted_iota(jnp.int32, (t_len, t_len), 0)
    col = lax.broadcasted_iota(jnp.int32, (t_len, t_len), 1)
    block_tri = ((row // CHUNK == col // CHUNK) & (row >= col)).astype(BF16)
    log_f = jnp.log(forget)
    hi = log_f.astype(BF16)
    rest = log_f - hi.astype(F32)
    mid = rest.astype(BF16)
    lo = (rest - mid.astype(F32)).astype(BF16)
    ls[...] = (jnp.dot(block_tri, hi, preferred_element_type=F32)
               + (jnp.dot(block_tri, mid, preferred_element_type=F32)
                  + jnp.dot(block_tri, lo, preferred_element_type=F32)))

    incl, _, _ = _tri_masks()
    nw = nw_ref[...]
    mid = CHUNK // 2 - 1

    hsl = [slice(j * HEAD_DIM, (j + 1) * HEAD_DIM) for j in range(heads)]

    def chunk_body(c, carry):
        rows = pl.ds(pl.multiple_of(c * CHUNK, CHUNK), CHUNK)
        items = []
        for j in range(heads):
            b = ls[rows, hsl[j]]
            items.append(dict(b=b, b_mid=b[mid:mid + 1, :], b_last=b[CHUNK - 1:CHUNK, :],
                              q=qs[rows, hsl[j]], k=ks[rows, hsl[j]],
                              vb=i_ref[0, rows, hsl[j]].astype(BF16)))
        for it in items:
            it["att"] = jnp.where(incl, _bdot_g(it["q"] * jnp.exp(it["b"] - it["b_mid"]),
                                                it["k"] * jnp.exp(it["b_mid"] - it["b"]), _NT), 0.0)
        for j, it in enumerate(items):
            st = st_ref[j]
            it["o"] = _bdot_g(it["q"] * jnp.exp(it["b"]), st, _NT)
            st_ref[j] = st * jnp.exp(it["b_last"]) + lax.dot_general(
                it["vb"], (it["k"] * jnp.exp(it["b_last"] - it["b"])).astype(BF16), _TN,
                preferred_element_type=F32)
        for it in items:
            it["o"] = it["o"] + jnp.dot(it["att"].astype(BF16), it["vb"], preferred_element_type=F32)
        for j, it in enumerate(items):
            o = _gated_head_norm(it["o"], nw, g_ref[0, rows, hsl[j]])
            o_ref[0, rows, hsl[j]] = o.astype(o_ref.dtype)
        return carry

    lax.fori_loop(0, t_len // CHUNK, chunk_body, 0, unroll=True)


def _hgrn(proj, lb, norm_w, *, col0, n_heads, t_len=256):
    bsz, s, _ = proj.shape
    width = n_heads * HEAD_DIM
    cb = col0 // width

    def col_spec(group):
        return pl.BlockSpec((1, t_len, width), lambda b, t: (b, t, cb + group))

    kern = functools.partial(_hgrn_kernel, heads=n_heads, t_len=t_len)
    return pl.pallas_call(
        kern,
        grid=(bsz, s // t_len),
        in_specs=[col_spec(0), col_spec(1), col_spec(2), col_spec(3),
                  pl.BlockSpec((1, width), lambda b, t: (0, 0)),
                  pl.BlockSpec((1, HEAD_DIM), lambda b, t: (0, 0))],
        out_specs=pl.BlockSpec((1, t_len, width), lambda b, t: (b, t, 0)),
        out_shape=jax.ShapeDtypeStruct((bsz, s, width), BF16),
        scratch_shapes=[pltpu.VMEM((t_len, width), F32)] * 3
        + [pltpu.VMEM((n_heads, HEAD_DIM, HEAD_DIM), F32)],
        compiler_params=_cparams(("parallel", "arbitrary")),
        name="hgrn2",
    )(proj, proj, proj, proj, lb.reshape(1, width).astype(F32), norm_w.reshape(1, HEAD_DIM).astype(F32))


def _rglru_kernel(y_ref, x_ref, cw_ref, cb_ref, wa_ref, ba_ref, wx_ref, bx_ref, lam_ref, o_ref,
                  xbuf, a_s, b_s, h_s, hcar, *, blocks, t_len):
    t = pl.program_id(2)

    @pl.when(t == 0)
    def _():
        xbuf[pl.ds(0, SUBLANES), :] = jnp.zeros((SUBLANES, blocks * RG_BLOCK), F32)
        hcar[...] = jnp.zeros_like(hcar)

    xc = _causal_conv(xbuf, x_ref[0], cw_ref[...], t_len) + cb_ref[...]
    for n in range(blocks):
        sl = slice(n * RG_BLOCK, (n + 1) * RG_BLOCK)
        xb = xc[:, sl]
        r = _sigmoid(_bdot(xb, wa_ref[n]) + ba_ref[:, sl])
        gi = _sigmoid(_bdot(xb, wx_ref[n]) + bx_ref[:, sl])
        log_a = (-RG_C) * r * _softplus(-lam_ref[:, sl])
        a = jnp.exp(log_a)
        a_s[:, sl] = a
        one_minus_a2 = -jnp.tanh(log_a) * (a * a + 1.0)
        b_s[:, sl] = jnp.sqrt(jnp.maximum(one_minus_a2, 0.0)) * (gi * xb)

    def row_body(i, h):
        h = a_s[pl.ds(i, 1), :] * h + b_s[pl.ds(i, 1), :]
        h_s[pl.ds(i, 1), :] = h
        return h

    hcar[...] = lax.fori_loop(0, t_len, row_body, hcar[...])
    o_ref[0] = (jax.nn.gelu(y_ref[0], approximate=True) * h_s[...]).astype(o_ref.dtype)


def _rglru(yx, conv_w, conv_b, wa, ba, wx, bx, lam, *, blocks_per_step=8, t_len=256):
    bsz, s, w2 = yx.shape
    width = w2 // 2
    gw = blocks_per_step * RG_BLOCK
    n_g = width // gw
    vec = lambda a: a.reshape(1, width).astype(F32)
    vspec = pl.BlockSpec((1, gw), lambda b, g, t: (0, g))
    wspec = pl.BlockSpec((blocks_per_step, RG_BLOCK, RG_BLOCK), lambda b, g, t: (g, 0, 0))
    kern = functools.partial(_rglru_kernel, blocks=blocks_per_step, t_len=t_len)
    return pl.pallas_call(
        kern,
        grid=(bsz, n_g, s // t_len),
        in_specs=[pl.BlockSpec((1, t_len, gw), lambda b, g, t: (b, t, g)),
                  pl.BlockSpec((1, t_len, gw), lambda b, g, t: (b, t, n_g + g)),
                  pl.BlockSpec((CONV_W, gw), lambda b, g, t: (0, g)),
                  vspec, wspec, vspec, wspec, vspec, vspec],
        out_specs=pl.BlockSpec((1, t_len, gw), lambda b, g, t: (b, t, g)),
        out_shape=jax.ShapeDtypeStruct((bsz, s, width), BF16),
        scratch_shapes=[pltpu.VMEM((SUBLANES + t_len, gw), F32)]
        + [pltpu.VMEM((t_len, gw), F32)] * 3
        + [pltpu.VMEM((1, gw), F32)],
        compiler_params=_cparams(("parallel", "parallel", "arbitrary")),
        name="rglru",
    )(yx, yx, conv_w.astype(F32), vec(conv_b), wa.astype(BF16), vec(ba), wx.astype(BF16), vec(bx), vec(lam))


def _norm_router_kernel(x_ref, nw_ref, w_ref, o_ref, *, n_experts):
    x = x_ref[...]
    y = x * lax.rsqrt(jnp.mean(x * x, axis=-1, keepdims=True) + EPS) * nw_ref[...]
    w = w_ref[...]
    y_hi = y.astype(BF16)
    y_lo = (y - y_hi.astype(F32)).astype(BF16)
    w_hi = w.astype(BF16)
    w_lo = (w - w_hi.astype(F32)).astype(BF16)
    logits = (jnp.dot(y_hi, w_hi, preferred_element_type=F32)
              + (jnp.dot(y_lo, w_hi, preferred_element_type=F32)
                 + jnp.dot(y_hi, w_lo, preferred_element_type=F32)))
    lane = lax.broadcasted_iota(jnp.int32, logits.shape, 1).astype(F32)
    neg = jnp.float32(-jnp.inf)
    l1 = jnp.where(lane < n_experts, logits, neg)
    m1 = jnp.max(l1, axis=-1, keepdims=True)
    i1 = jnp.min(jnp.where(l1 == m1, lane, float(LANES)), axis=-1, keepdims=True)
    l2 = jnp.where(lane == i1, neg, l1)
    m2 = jnp.max(l2, axis=-1, keepdims=True)
    i2 = jnp.min(jnp.where(l2 == m2, lane, float(LANES)), axis=-1, keepdims=True)
    e2 = jnp.exp(m2 - m1)
    g1 = 1.0 / (1.0 + e2)
    g2 = e2 / (1.0 + e2)
    out = jnp.where(lane == 0, i1, 0.0)
    out = jnp.where(lane == 1, i2, out)
    out = jnp.where(lane == 2, g1, out)
    out = jnp.where(lane == 3, g2, out)
    o_ref[...] = out


def _norm_router(x, nw, router_w, tm=512):
    m, d = x.shape
    n_experts = router_w.shape[1]
    w = jnp.zeros((d, LANES), F32).at[:, :n_experts].set(router_w.astype(F32))
    return pl.pallas_call(
        functools.partial(_norm_router_kernel, n_experts=n_experts),
        grid=(m // tm,),
        in_specs=[pl.BlockSpec((tm, d), lambda i: (i, 0)),
                  pl.BlockSpec((1, d), lambda i: (0, 0)),
                  pl.BlockSpec((d, LANES), lambda i: (0, 0))],
        out_specs=pl.BlockSpec((tm, LANES), lambda i: (i, 0)),
        out_shape=jax.ShapeDtypeStruct((m, LANES), F32),
        compiler_params=_cparams(("parallel",)),
        name="norm_router",
    )(x, nw.reshape(1, d).astype(F32), w)


def _start_row_gather(idx_ref, base, src3, dst3, sem, n_rows):
    def body(g, carry):
        for u in range(SUBLANES):
            row = idx_ref[base + g * SUBLANES + u]
            src = src3.at[lax.shift_right_logical(row, 3), pl.ds(row & (SUBLANES - 1), 1), :]
            pltpu.make_async_copy(src, dst3.at[g, pl.ds(u, 1), :], sem).start(priority=u % 2)
        return carry
    lax.fori_loop(0, n_rows // SUBLANES, body, 0)


def _wait_row_gather(src3, dst3, sem, n_rows):
    n = n_rows // SUBLANES
    pltpu.make_async_copy(src3.at[pl.ds(0, n)], dst3.at[pl.ds(0, n)], sem).wait()


def _gather_norm_kernel(idx_ref, nq_ref, x_hbm, nw_ref, o_ref, buf, sem):
    c = pl.program_id(0)
    slot = c % 2

    def start(chunk, s):
        _start_row_gather(idx_ref, chunk * MOE_CHUNK, x_hbm, buf.at[s], sem.at[s], nq_ref[chunk] * MOE_SUB)

    @pl.when(c == 0)
    def _():
        start(0, 0)

    @pl.when(c + 1 < pl.num_programs(0))
    def _():
        start(c + 1, 1 - slot)

    for sb in range(MOE_CHUNK // MOE_SUB):
        @pl.when(sb < nq_ref[c])
        def _():
            _wait_row_gather(x_hbm, buf.at[slot], sem.at[slot], MOE_SUB)

    for sb in range(MOE_CHUNK // MOE_SUB):
        rows = pl.ds(sb * MOE_SUB, MOE_SUB)

        @pl.when(sb < nq_ref[c])
        def _(sb=sb, rows=rows):
            tiles = pl.ds(sb * (MOE_SUB // SUBLANES), MOE_SUB // SUBLANES)
            x = buf[slot, tiles].reshape(MOE_SUB, o_ref.shape[1])
            y = x * lax.rsqrt(jnp.mean(x * x, axis=-1, keepdims=True) + EPS) * nw_ref[...]
            o_ref[rows, :] = y.astype(o_ref.dtype)

        @pl.when(sb >= nq_ref[c])
        def _():
            o_ref[rows, :] = jnp.zeros((MOE_SUB, o_ref.shape[1]), o_ref.dtype)


def _gather_norm(src_token, chunk_nq, x, nw):
    rows = src_token.shape[0]
    d = x.shape[1]
    grid_spec = pltpu.PrefetchScalarGridSpec(
        num_scalar_prefetch=2,
        grid=(rows // MOE_CHUNK,),
        in_specs=[pl.BlockSpec(memory_space=pl.ANY),
                  pl.BlockSpec((1, d), lambda c, idx, nq: (0, 0))],
        out_specs=pl.BlockSpec((MOE_CHUNK, d), lambda c, idx, nq: (c, 0)),
        scratch_shapes=[pltpu.VMEM((2, MOE_CHUNK // SUBLANES, SUBLANES, d), F32),
                        pltpu.SemaphoreType.DMA((2,))],
    )
    return pl.pallas_call(
        _gather_norm_kernel,
        grid_spec=grid_spec,
        out_shape=jax.ShapeDtypeStruct((rows, d), BF16),
        compiler_params=_cparams(("arbitrary",)),
        name="gather_norm",
    )(src_token, chunk_nq, x.reshape(-1, SUBLANES, d), nw.reshape(1, d).astype(F32))


def _for_valid_rows(nq, o_ref, fn, pred=True):
    total = o_ref.shape[0]
    for v in range(total // MOE_SUB + 1):
        @pl.when(jnp.logical_and(nq == v, pred))
        def _(v=v):
            rows = v * MOE_SUB
            if rows:
                fn(rows)
            if rows < total:
                o_ref[pl.ds(rows, total - rows), :] = jnp.zeros((total - rows, o_ref.shape[1]), o_ref.dtype)


def _gmm_swiglu_kernel(te_ref, nq_ref, nv_ref, a_ref, wg_ref, wu_ref, o_ref):
    c = pl.program_id(0)

    def compute(rows):
        a = a_ref[pl.ds(0, rows), :]
        g = jnp.dot(a, wg_ref[0].astype(BF16), preferred_element_type=F32)
        u = jnp.dot(a, wu_ref[0].astype(BF16), preferred_element_type=F32)
        o_ref[pl.ds(0, rows), :] = (_silu(g) * u).astype(o_ref.dtype)

    _for_valid_rows(nq_ref[c], o_ref, compute)


def _gmm_swiglu(chunk_expert, chunk_nq, n_valid, a, w_gu, tn):
    m, k = a.shape
    f = w_gu.shape[2] // 2
    nb = f // tn
    last = lambda c, nv: jnp.minimum(c, nv[0] - 1)
    col = lambda j, c, nv: jnp.where(c < nv[0], j, nb - 1)
    grid_spec = pltpu.PrefetchScalarGridSpec(
        num_scalar_prefetch=3,
        grid=(m // MOE_CHUNK, nb),
        in_specs=[pl.BlockSpec((MOE_CHUNK, k), lambda c, j, te, nq, nv: (last(c, nv), 0)),
                  pl.BlockSpec((1, k, tn), lambda c, j, te, nq, nv: (te[c], 0, col(j, c, nv))),
                  pl.BlockSpec((1, k, tn), lambda c, j, te, nq, nv: (te[c], 0, col(j, c, nv) + nb))],
        out_specs=pl.BlockSpec((MOE_CHUNK, tn), lambda c, j, te, nq, nv: (c, j)),
    )
    return pl.pallas_call(
        _gmm_swiglu_kernel,
        grid_spec=grid_spec,
        out_shape=jax.ShapeDtypeStruct((m, f), BF16),
        compiler_params=_cparams(("arbitrary", "arbitrary")),
        name="gmm_swiglu",
    )(chunk_expert, chunk_nq, n_valid, a, w_gu, w_gu)


def _gmm_down_kernel(te_ref, nq_ref, nv_ref, a_ref, w_ref, o_ref):
    c = pl.program_id(0)

    def compute(rows):
        w = w_ref[0].astype(BF16)
        o_ref[pl.ds(0, rows), :] = jnp.dot(a_ref[pl.ds(0, rows), :], w, preferred_element_type=F32)

    _for_valid_rows(nq_ref[c], o_ref, compute)


def _gmm_down(chunk_expert, chunk_nq, n_valid, a, w_d, tn):
    m, k = a.shape
    n = w_d.shape[2]
    nj = n // tn
    last = lambda c, nv: jnp.minimum(c, nv[0] - 1)
    col = lambda j, c, nv: jnp.where(c < nv[0], j, nj - 1)
    grid_spec = pltpu.PrefetchScalarGridSpec(
        num_scalar_prefetch=3,
        grid=(m // MOE_CHUNK, nj),
        in_specs=[pl.BlockSpec((MOE_CHUNK, k), lambda c, j, te, nq, nv: (last(c, nv), 0)),
                  pl.BlockSpec((1, k, tn), lambda c, j, te, nq, nv: (te[c], 0, col(j, c, nv)))],
        out_specs=pl.BlockSpec((MOE_CHUNK, tn), lambda c, j, te, nq, nv: (c, j)),
    )
    return pl.pallas_call(
        _gmm_down_kernel,
        grid_spec=grid_spec,
        out_shape=jax.ShapeDtypeStruct((m, n), F32),
        compiler_params=pltpu.CompilerParams(dimension_semantics=("arbitrary", "arbitrary"),
                                             vmem_limit_bytes=VMEM_LIMIT_MAX),
        name="gmm_down",
    )(chunk_expert, chunk_nq, n_valid, a, w_d)


def _combine_norm_kernel(pos_ref, x_ref, r_ref, ys_hbm, w_ref, o_ref, buf, sem, *, tm):
    i = pl.program_id(0)
    n = pl.num_programs(0)
    slot = i % 2

    def start(tile, s):
        for kk in range(TOP_K):
            _start_row_gather(pos_ref, kk * (n * tm) + tile * tm, ys_hbm, buf.at[s, kk], sem.at[s], tm)

    @pl.when(i == 0)
    def _():
        start(0, 0)

    @pl.when(i + 1 < n)
    def _():
        start(i + 1, 1 - slot)

    for kk in range(TOP_K):
        _wait_row_gather(ys_hbm, buf.at[slot, kk], sem.at[slot], tm)
    r = r_ref[...]
    d = x_ref.shape[1]
    moe = r[:, TOP_K:TOP_K + 1] * buf[slot, 0].reshape(tm, d)
    for kk in range(1, TOP_K):
        moe = moe + r[:, TOP_K + kk:TOP_K + kk + 1] * buf[slot, kk].reshape(tm, d)
    x = x_ref[...] + moe
    y = x * lax.rsqrt(jnp.mean(x * x, axis=-1, keepdims=True) + EPS)
    o_ref[...] = y * w_ref[...]


def _combine_norm(pos, x, route, ys, w, tm=256):
    m, d = x.shape
    grid_spec = pltpu.PrefetchScalarGridSpec(
        num_scalar_prefetch=1,
        grid=(m // tm,),
        in_specs=[pl.BlockSpec((tm, d), lambda i, p: (i, 0)),
                  pl.BlockSpec((tm, LANES), lambda i, p: (i, 0)),
                  pl.BlockSpec(memory_space=pl.ANY),
                  pl.BlockSpec((1, d), lambda i, p: (0, 0))],
        out_specs=pl.BlockSpec((tm, d), lambda i, p: (i, 0)),
        scratch_shapes=[pltpu.VMEM((2, TOP_K, tm // SUBLANES, SUBLANES, d), F32),
                        pltpu.SemaphoreType.DMA((2,))],
    )
    return pl.pallas_call(
        functools.partial(_combine_norm_kernel, tm=tm),
        grid_spec=grid_spec,
        out_shape=jax.ShapeDtypeStruct((m, d), F32),
        compiler_params=_cparams(("arbitrary",)),
        name="combine_norm",
    )(pos, x, route, ys.reshape(-1, SUBLANES, d), w.reshape(1, d).astype(F32))


def _moe_routing(route, n_experts):
    m = route.shape[0]
    n_sub = MOE_CHUNK // MOE_SUB
    ids = route[:, :TOP_K].astype(jnp.int32)
    flat_e = ids.reshape(-1)
    onehot = (flat_e[:, None] == jnp.arange(n_experts)[None, :]).astype(jnp.int32)
    rank = jnp.sum((jnp.cumsum(onehot, axis=0) - onehot) * onehot, axis=1)
    counts = jnp.sum(onehot, axis=0)
    subs = (counts + MOE_SUB - 1) // MOE_SUB
    n_chunks = (subs + n_sub - 1) // n_sub
    chunk_end = jnp.cumsum(n_chunks)
    chunk_start = chunk_end - n_chunks
    max_chunks = (m * TOP_K // MOE_SUB + n_experts + n_sub - 1) // n_sub + n_experts
    rows = max_chunks * MOE_CHUNK
    base = subs // jnp.maximum(n_chunks, 1)
    extra = subs - base * n_chunks
    n_valid = chunk_end[n_experts - 1].astype(jnp.int32)
    cidx = jnp.minimum(jnp.arange(max_chunks, dtype=jnp.int32), n_valid - 1)
    chunk_expert = jnp.sum((cidx[:, None] >= chunk_end[None, :]).astype(jnp.int32), axis=1).astype(jnp.int32)
    sel = (chunk_expert[:, None] == jnp.arange(n_experts)[None, :]).astype(jnp.int32)
    per_chunk = lambda v: jnp.sum(sel * v[None, :], axis=1)
    within = cidx - per_chunk(chunk_start)
    chunk_nq = per_chunk(base) + (within < per_chunk(extra))
    is_valid = jnp.arange(max_chunks) < n_valid
    chunk_nq = jnp.where(is_valid, chunk_nq, 0).astype(jnp.int32)
    first_rank = (jnp.minimum(within, per_chunk(extra)) * (per_chunk(base) + 1)
                  + jnp.maximum(within - per_chunk(extra), 0) * per_chunk(base)) * MOE_SUB
    mine = (flat_e[:, None] == chunk_expert[None, :]) & is_valid[None, :]
    chunk_of = (jnp.sum(onehot * chunk_start[None, :], axis=1)
                + jnp.sum((mine & (first_rank[None, :] <= rank[:, None])).astype(jnp.int32), axis=1) - 1)
    at_chunk = (chunk_of[:, None] == jnp.arange(max_chunks)[None, :]).astype(jnp.int32)
    pos = chunk_of * MOE_CHUNK + rank - jnp.sum(at_chunk * first_rank[None, :], axis=1)
    src_token = jnp.zeros((rows,), jnp.int32).at[pos].set(jnp.arange(m * TOP_K, dtype=jnp.int32) // TOP_K)
    pos_kmajor = pos.reshape(m, TOP_K).T.reshape(-1).astype(jnp.int32)
    return src_token, chunk_expert, chunk_nq, n_valid.reshape(1), pos_kmajor


def kernel(x, norm_w, final_norm_w, mix_in_w, gdn_conv_w, gdn_a_log, gdn_dt_bias, gdn_norm_w, hgrn_lb_logits, hgrn_norm_w, mix_out_w, ffn_gate_up_w, ffn_down_w, rg_in_w, rg_conv_w, rg_conv_b, rg_gate_a_w, rg_gate_a_b, rg_gate_x_w, rg_gate_x_b, rg_lambda, rg_out_w, moe_router_w, moe_gate_up_w, moe_down_w):
    bsz, s, d = x.shape
    m = bsz * s
    n_heads = gdn_a_log.shape[1]
    hw = n_heads * HEAD_DIM
    n_experts = moe_router_w.shape[2]
    xr = x.reshape(m, d)

    hgrn_lb = jnp.cumsum(jax.nn.softmax(hgrn_lb_logits.astype(F32), axis=0), axis=0)
    w_in_t = jnp.swapaxes(mix_in_w[0], 0, 1)
    h = _rmsnorm(xr, norm_w[0, 0], BF16)
    proj_a = _mm_nt(h, w_in_t, F32, 1024, 1024, 0, 4 * hw).reshape(bsz, s, 4 * hw)
    proj_b = _mm_nt(h, w_in_t, F32, 1024, 1024, 4 * hw + 2 * n_heads, 4 * hw).reshape(bsz, s, 4 * hw)
    ba = _mm_nt(h, w_in_t, F32, 1024, LANES, 4 * hw, LANES).reshape(bsz, s, LANES)
    o_a = _gdn(proj_a, ba, gdn_conv_w[0].astype(F32), gdn_a_log[0], gdn_dt_bias[0], gdn_norm_w[0], col0=0)
    o_b = _hgrn(proj_b, hgrn_lb[0], hgrn_norm_w[0], col0=0, n_heads=n_heads)
    xr = _mm_resid([o_a.reshape(m, hw), o_b.reshape(m, hw)], mix_out_w[0], xr, 1024, 1024)
    act = _norm_mm_swiglu(xr, norm_w[0, 1], ffn_gate_up_w[0], 1024, 512)
    xr = _mm_resid([act], ffn_down_w[0], xr, 512, 512)

    yx = _norm_mm(xr, norm_w[1, 0], rg_in_w[0], F32, 1024, 1024).reshape(bsz, s, -1)
    rec = _rglru(yx, rg_conv_w[0], rg_conv_b[0], rg_gate_a_w[0], rg_gate_a_b[0],
                 rg_gate_x_w[0], rg_gate_x_b[0], rg_lambda[0])
    xr = _mm_resid([rec.reshape(m, d)], rg_out_w[0], xr, 1024, 1024)
    route = _norm_router(xr, norm_w[1, 1], moe_router_w[0])
    src_token, chunk_expert, chunk_nq, n_valid, pos = _moe_routing(route, n_experts)
    hs = _gather_norm(src_token, chunk_nq, xr, norm_w[1, 1])
    act = _gmm_swiglu(chunk_expert, chunk_nq, n_valid, hs, moe_gate_up_w[0], 512)
    ys = _gmm_down(chunk_expert, chunk_nq, n_valid, act, moe_down_w[0], 256)
    out = _combine_norm(pos, xr, route, ys, final_norm_w)
    return out.reshape(bsz, s, d)
```

```python
import functools

import jax
import jax.numpy as jnp
from jax import lax
from jax.experimental import pallas as pl
from jax.experimental.pallas import tpu as pltpu

F32 = jnp.float32
BF16 = jnp.bfloat16
HIGHEST = lax.Precision.HIGHEST

EPS = 1e-6
CHUNK = 64
CONV_W = 4
HEAD_DIM = 128
RG_BLOCK = 256
RG_C = 8.0
TOP_K = 2
LANES = 128
SUBLANES = 8
VMEM_LIMIT = 56 * 1024 * 1024
VMEM_LIMIT_MAX = 60 * 1024 * 1024
MOE_CHUNK = 1024
MOE_SUB = 128

_NT = (((1,), (1,)), ((), ()))
_TN = (((0,), (0,)), ((), ()))


def _cparams(sem):
    return pltpu.CompilerParams(dimension_semantics=sem, vmem_limit_bytes=VMEM_LIMIT)


def _sigmoid(x):
    return jax.nn.sigmoid(x)


def _silu(x):
    return x * _sigmoid(x)


def _softplus(x):
    return jnp.maximum(x, 0.0) + jnp.log1p(jnp.exp(-jnp.abs(x)))


def _bdot(a, b):
    return jnp.dot(a.astype(BF16), b.astype(BF16), preferred_element_type=F32)


def _bdot_g(a, b, dims):
    return lax.dot_general(a.astype(BF16), b.astype(BF16), dims, preferred_element_type=F32)


def _rmsnorm_kernel(x_ref, w_ref, o_ref):
    x = x_ref[...]
    y = x * lax.rsqrt(jnp.mean(x * x, axis=-1, keepdims=True) + EPS)
    o_ref[...] = (y * w_ref[...]).astype(o_ref.dtype)


def _rmsnorm(x, w, out_dtype, tm=512):
    m, d = x.shape
    return pl.pallas_call(
        _rmsnorm_kernel,
        grid=(m // tm,),
        in_specs=[pl.BlockSpec((tm, d), lambda i: (i, 0)),
                  pl.BlockSpec((1, d), lambda i: (0, 0))],
        out_specs=pl.BlockSpec((tm, d), lambda i: (i, 0)),
        out_shape=jax.ShapeDtypeStruct((m, d), out_dtype),
        compiler_params=_cparams(("parallel",)),
        name="rmsnorm",
    )(x, w.reshape(1, d).astype(F32))


def _bf16_weight(w_ref, w_s, fresh):
    if not fresh:
        return w_s[...]
    w = w_ref[...].astype(BF16)
    w_s[...] = w
    return w


def _first_row_tile_or_not(body):
    first = pl.program_id(1) == 0
    pl.when(first)(lambda: body(True))
    pl.when(jnp.logical_not(first))(lambda: body(False))


def _mm_nt_kernel(*refs, shift):
    a_ref, w_ref = refs[0], refs[1]
    o_ref, w_s = refs[-2], refs[-1]

    def body(fresh):
        if fresh:
            w = w_ref[...]
            if shift:
                w = jnp.concatenate([w, refs[2][...]], axis=0)[shift:shift + w_ref.shape[0], :]
            w = w.astype(BF16)
            w_s[...] = w
        else:
            w = w_s[...]
        o_ref[...] = lax.dot_general(a_ref[...], w, _NT, preferred_element_type=F32).astype(o_ref.dtype)

    _first_row_tile_or_not(body)


def _mm_nt(a, wt, out_dtype, tm, tn, row0, n_rows):
    m, k = a.shape
    base, shift = row0 // tn * tn, row0 % tn
    assert shift % SUBLANES == 0 and shift <= LANES and n_rows % tn == 0
    in_specs = [pl.BlockSpec((tm, k), lambda j, i: (i, 0)),
                pl.BlockSpec((tn, k), lambda j, i: (base // tn + j, 0))]
    args = [a, wt]
    if shift:
        in_specs.append(pl.BlockSpec((LANES, k), lambda j, i: ((base + (j + 1) * tn) // LANES, 0)))
        args.append(wt)
    return pl.pallas_call(
        functools.partial(_mm_nt_kernel, shift=shift),
        grid=(n_rows // tn, m // tm),
        in_specs=in_specs,
        out_specs=pl.BlockSpec((tm, tn), lambda j, i: (i, j)),
        out_shape=jax.ShapeDtypeStruct((m, n_rows), out_dtype),
        scratch_shapes=[pltpu.VMEM((tn, k), BF16)],
        compiler_params=_cparams(("arbitrary", "arbitrary")),
        name="mm_nt",
    )(*args)


def _normed_rows(x_ref, nw_ref, h_s, fresh):
    if not fresh:
        return h_s[...]
    x = x_ref[...]
    h = (x * lax.rsqrt(jnp.mean(x * x, axis=-1, keepdims=True) + EPS) * nw_ref[...]).astype(BF16)
    h_s[...] = h
    return h


def _norm_mm_kernel(x_ref, nw_ref, w_ref, o_ref, h_s):
    def body(fresh):
        h = _normed_rows(x_ref, nw_ref, h_s, fresh)
        o_ref[...] = jnp.dot(h, w_ref[...].astype(BF16), preferred_element_type=F32).astype(o_ref.dtype)

    _first_row_tile_or_not(body)


def _norm_mm(x, nw, w, out_dtype, tm, tn):
    m, k = x.shape
    n = w.shape[1]
    return pl.pallas_call(
        _norm_mm_kernel,
        grid=(m // tm, n // tn),
        in_specs=[pl.BlockSpec((tm, k), lambda i, j: (i, 0)),
                  pl.BlockSpec((1, k), lambda i, j: (0, 0)),
                  pl.BlockSpec((k, tn), lambda i, j: (0, j))],
        out_specs=pl.BlockSpec((tm, tn), lambda i, j: (i, j)),
        out_shape=jax.ShapeDtypeStruct((m, n), out_dtype),
        scratch_shapes=[pltpu.VMEM((tm, k), BF16)],
        compiler_params=_cparams(("arbitrary", "arbitrary")),
        name="norm_mm",
    )(x, nw.reshape(1, k).astype(F32), w)


def _norm_mm_swiglu_kernel(x_ref, nw_ref, wg_ref, wu_ref, o_ref, h_s):
    def body(fresh):
        h = _normed_rows(x_ref, nw_ref, h_s, fresh)
        g = jnp.dot(h, wg_ref[...].astype(BF16), preferred_element_type=F32)
        u = jnp.dot(h, wu_ref[...].astype(BF16), preferred_element_type=F32)
        o_ref[...] = (_silu(g) * u).astype(o_ref.dtype)

    _first_row_tile_or_not(body)


def _norm_mm_swiglu(x, nw, w_gu, tm, tn):
    m, k = x.shape
    f = w_gu.shape[1] // 2
    nb = f // tn
    return pl.pallas_call(
        _norm_mm_swiglu_kernel,
        grid=(m // tm, nb),
        in_specs=[pl.BlockSpec((tm, k), lambda i, j: (i, 0)),
                  pl.BlockSpec((1, k), lambda i, j: (0, 0)),
                  pl.BlockSpec((k, tn), lambda i, j: (0, j)),
                  pl.BlockSpec((k, tn), lambda i, j: (0, j + nb))],
        out_specs=pl.BlockSpec((tm, tn), lambda i, j: (i, j)),
        out_shape=jax.ShapeDtypeStruct((m, f), BF16),
        scratch_shapes=[pltpu.VMEM((tm, k), BF16)],
        compiler_params=_cparams(("arbitrary", "arbitrary")),
        name="norm_mm_swiglu",
    )(x, nw.reshape(1, k).astype(F32), w_gu, w_gu)


def _mm_resid_kernel(*refs, n_pairs):
    r_ref, o_ref = refs[2 * n_pairs], refs[2 * n_pairs + 1]
    w_s = refs[2 * n_pairs + 2:]

    def body(fresh):
        acc = r_ref[...]
        for p in range(n_pairs):
            w = _bf16_weight(refs[2 * p + 1], w_s[p], fresh)
            acc = acc + jnp.dot(refs[2 * p][...], w, preferred_element_type=F32)
        o_ref[...] = acc

    _first_row_tile_or_not(body)


def _mm_resid(a_list, w, resid, tm, tn):
    m, n = resid.shape
    in_specs, args, scratch = [], [], []
    for p, a in enumerate(a_list):
        k = a.shape[1]
        in_specs += [pl.BlockSpec((tm, k), lambda j, i: (i, 0)),
                     pl.BlockSpec((k, tn), lambda j, i, p=p: (p, j))]
        args += [a, w]
        scratch.append(pltpu.VMEM((k, tn), BF16))
    in_specs.append(pl.BlockSpec((tm, tn), lambda j, i: (i, j)))
    args.append(resid)
    return pl.pallas_call(
        functools.partial(_mm_resid_kernel, n_pairs=len(a_list)),
        grid=(n // tn, m // tm),
        in_specs=in_specs,
        out_specs=pl.BlockSpec((tm, tn), lambda j, i: (i, j)),
        out_shape=jax.ShapeDtypeStruct((m, n), F32),
        scratch_shapes=scratch,
        compiler_params=_cparams(("arbitrary", "arbitrary")),
        name="mm_resid",
    )(*args)


def _causal_conv(buf_ref, x, cw, t_len):
    buf_ref[pl.ds(SUBLANES, t_len), :] = x
    acc = cw[CONV_W - 1:CONV_W, :] * x
    for j in range(CONV_W - 1):
        off = SUBLANES - (CONV_W - 1) + j
        acc = acc + cw[j:j + 1, :] * buf_ref[pl.ds(off, t_len), :]
    buf_ref[pl.ds(0, SUBLANES), :] = x[t_len - SUBLANES:, :]
    return acc


def _tri_masks():
    row = lax.broadcasted_iota(jnp.int32, (CHUNK, CHUNK), 0)
    col = lax.broadcasted_iota(jnp.int32, (CHUNK, CHUNK), 1)
    return row >= col, row > col, row == col


def _gated_head_norm(o, nw, z):
    o = o * lax.rsqrt(jnp.mean(o * o, axis=-1, keepdims=True) + EPS) * nw
    return o * _silu(z)


def _gdn_kernel(q_ref, k_ref, v_ref, z_ref, ba_ref, cwq_ref, cwk_ref, cwv_ref, alog_ref, dtb_ref,
                nw_ref, o_ref, qbuf, kbuf, vbuf, qs, ks, vs, gcs, gct, bs, u_s, w_s, qk_s, qd_s, kd_s, s_ref,
                *, n_heads, t_len):
    t = pl.program_id(1)
    n_chunks = t_len // CHUNK
    width = n_heads * HEAD_DIM

    @pl.when(t == 0)
    def _():
        zeros = jnp.zeros((SUBLANES, width), F32)
        qbuf[pl.ds(0, SUBLANES), :] = zeros
        kbuf[pl.ds(0, SUBLANES), :] = zeros
        vbuf[pl.ds(0, SUBLANES), :] = zeros
        s_ref[...] = jnp.zeros_like(s_ref)

    qc = _silu(_causal_conv(qbuf, q_ref[0], cwq_ref[...], t_len))
    kc = _silu(_causal_conv(kbuf, k_ref[0], cwk_ref[...], t_len))
    vs[...] = _silu(_causal_conv(vbuf, v_ref[0], cwv_ref[...], t_len))
    for h in range(n_heads):
        sl = slice(h * HEAD_DIM, (h + 1) * HEAD_DIM)
        qh = qc[:, sl]
        kh = kc[:, sl]
        qs[:, sl] = qh * (lax.rsqrt(jnp.sum(qh * qh, axis=-1, keepdims=True) + EPS) * (HEAD_DIM ** -0.5))
        ks[:, sl] = kh * lax.rsqrt(jnp.sum(kh * kh, axis=-1, keepdims=True) + EPS)

    ba = ba_ref[0]
    g_log = -jnp.exp(alog_ref[...]) * _softplus(ba + dtb_ref[...])
    bs[...] = _sigmoid(ba)
    row = lax.broadcasted_iota(jnp.int32, (t_len, t_len), 0)
    col = lax.broadcasted_iota(jnp.int32, (t_len, t_len), 1)
    block_tri = ((row // CHUNK == col // CHUNK) & (row >= col)).astype(F32)
    gc_blk = jnp.dot(block_tri, g_log, precision=HIGHEST, preferred_element_type=F32)
    gcs[...] = gc_blk
    for c in range(n_chunks):
        gct[c] = gc_blk[c * CHUNK:(c + 1) * CHUNK, :].T

    incl, strict, eye = _tri_masks()
    eye_f = eye.astype(F32)
    nw = nw_ref[...]

    heads = range(n_heads)
    hsl = [slice(h * HEAD_DIM, (h + 1) * HEAD_DIM) for h in heads]
    qsl = [slice(h * HEAD_DIM, h * HEAD_DIM + CHUNK) for h in heads]
    chunks_per_iter = 4
    assert n_chunks % chunks_per_iter == 0

    def pass_a(it, carry):
        items = []
        for cc in range(chunks_per_iter):
            c = it * chunks_per_iter + cc
            rows = pl.ds(pl.multiple_of(c * CHUNK, CHUNK), CHUNK)
            gc_all = gcs[rows, :]
            beta_all = bs[rows, :]
            gct_c = gct[c]
            for h in heads:
                gc = gc_all[:, n_heads + h:n_heads + h + 1]
                beta = beta_all[:, h:h + 1]
                gc_row = gct_c[n_heads + h:n_heads + h + 1, :]
                decay = jnp.where(incl, jnp.exp(jnp.where(incl, gc - gc_row, 0.0)), 0.0)
                items.append(dict(rows=rows, h=h, gc=gc, beta=beta, decay=decay,
                                  q=qs[rows, hsl[h]], k=ks[rows, hsl[h]]))
        for it_ in items:
            it_["kb"] = it_["k"].astype(BF16)
            it_["kk"] = lax.dot_general(it_["kb"], it_["kb"], _NT, preferred_element_type=F32)
        for it_ in items:
            it_["x"] = jnp.where(strict, -(it_["beta"] * it_["kk"] * it_["decay"]), 0.0)
            it_["p"] = eye_f + it_["x"]
        for _ in range(5):
            for it_ in items:
                xb = it_["x"].astype(BF16)
                it_["x"] = jnp.dot(xb, xb, preferred_element_type=F32)
            for it_ in items:
                it_["p"] = it_["p"] + _bdot(it_["p"], it_["x"])
        for it_ in items:
            egc = jnp.exp(it_["gc"])
            it_["egc"] = egc
            v = vs[it_["rows"], hsl[it_["h"]]]
            rhs = jnp.concatenate([v * it_["beta"], it_["k"] * (it_["beta"] * egc)], axis=-1)
            it_["sol"] = _bdot(it_["p"], rhs)
        for it_ in items:
            it_["qk"] = lax.dot_general(it_["q"].astype(BF16), it_["kb"], _NT,
                                        preferred_element_type=F32) * it_["decay"]
        for it_ in items:
            rows, h, gc = it_["rows"], it_["h"], it_["gc"]
            g_last = gc[CHUNK - 1:CHUNK, :]
            u_s[rows, hsl[h]] = it_["sol"][:, :HEAD_DIM]
            w_s[rows, hsl[h]] = it_["sol"][:, HEAD_DIM:].astype(BF16)
            qk_s[rows, qsl[h]] = it_["qk"].astype(BF16)
            qd_s[rows, hsl[h]] = (it_["q"] * it_["egc"]).astype(BF16)
            kd_s[rows, hsl[h]] = (it_["k"] * jnp.exp(g_last - gc)).astype(BF16)
        return carry

    lax.fori_loop(0, n_chunks // chunks_per_iter, pass_a, 0)

    def pass_b(c, carry):
        rows = pl.ds(pl.multiple_of(c * CHUNK, CHUNK), CHUNK)
        gc_all = gcs[rows, :]
        states = [s_ref[h] for h in heads]
        sbs = [s.astype(BF16) for s in states]
        v_new = [u_s[rows, hsl[h]] - jnp.dot(w_s[rows, hsl[h]], sbs[h], preferred_element_type=F32)
                 for h in heads]
        vbs = [v.astype(BF16) for v in v_new]
        for h in heads:
            g_last = gc_all[CHUNK - 1:CHUNK, n_heads + h:n_heads + h + 1]
            s_ref[h] = states[h] * jnp.exp(g_last) + lax.dot_general(kd_s[rows, hsl[h]], vbs[h], _TN,
                                                                      preferred_element_type=F32)
        outs = [jnp.dot(qd_s[rows, hsl[h]], sbs[h], preferred_element_type=F32)
                + jnp.dot(qk_s[rows, qsl[h]], vbs[h], preferred_element_type=F32) for h in heads]
        for h in heads:
            o = _gated_head_norm(outs[h], nw, z_ref[0, rows, hsl[h]])
            o_ref[0, rows, hsl[h]] = o.astype(o_ref.dtype)
        return carry

    lax.fori_loop(0, n_chunks, pass_b, 0, unroll=True)


def _gdn(proj, ba, conv_w, a_log, dt_bias, norm_w, *, col0, t_len=256):
    bsz, s, _ = proj.shape
    n_heads = a_log.shape[0]
    width = n_heads * HEAD_DIM
    cb = col0 // width

    def col_spec(group):
        return pl.BlockSpec((1, t_len, width), lambda b, t: (b, t, cb + group))

    def cw_spec(group):
        return pl.BlockSpec((CONV_W, width), lambda b, t: (0, group))

    row = jnp.zeros((1, LANES), F32)
    alog_row = row.at[0, n_heads:2 * n_heads].set(a_log.astype(F32))
    dtb_row = row.at[0, n_heads:2 * n_heads].set(dt_bias.astype(F32))
    small = pl.BlockSpec((1, LANES), lambda b, t: (0, 0))
    kern = functools.partial(_gdn_kernel, n_heads=n_heads, t_len=t_len)
    return pl.pallas_call(
        kern,
        grid=(bsz, s // t_len),
        in_specs=[col_spec(0), col_spec(1), col_spec(2), col_spec(3),
                  pl.BlockSpec((1, t_len, LANES), lambda b, t: (b, t, 0)),
                  cw_spec(0), cw_spec(1), cw_spec(2), small, small,
                  pl.BlockSpec((1, HEAD_DIM), lambda b, t: (0, 0))],
        out_specs=pl.BlockSpec((1, t_len, width), lambda b, t: (b, t, 0)),
        out_shape=jax.ShapeDtypeStruct((bsz, s, width), BF16),
        scratch_shapes=[pltpu.VMEM((SUBLANES + t_len, width), F32)] * 3
        + [pltpu.VMEM((t_len, width), F32)] * 3
        + [pltpu.VMEM((t_len, LANES), F32),
           pltpu.VMEM((t_len // CHUNK, LANES, CHUNK), F32),
           pltpu.VMEM((t_len, LANES), F32),
           pltpu.VMEM((t_len, width), F32)]
        + [pltpu.VMEM((t_len, width), BF16)] * 4
        + [pltpu.VMEM((n_heads, HEAD_DIM, HEAD_DIM), F32)],
        compiler_params=_cparams(("parallel", "arbitrary")),
        name="gdn",
    )(proj, proj, proj, proj, ba, conv_w, conv_w, conv_w, alog_row, dtb_row,
      norm_w.reshape(1, HEAD_DIM).astype(F32))


def _hgrn_kernel(q_ref, f_ref, i_ref, g_ref, lb_ref, nw_ref, o_ref, qs, ks, ls, st_ref, *, heads, t_len):
    t = pl.program_id(1)

    @pl.when(t == 0)
    def _():
        st_ref[...] = jnp.zeros_like(st_ref)

    lb = lb_ref[...]
    forget = lb + (1.0 - lb) * _sigmoid(f_ref[0])
    ks[...] = 1.0 - forget
    qs[...] = _silu(q_ref[0]) * (HEAD_DIM ** -0.5)
    row = lax.broadcasted_iota(jnp.int32, (t_len, t_len), 0)
    col = lax.broadcasted_iota(jnp.int32, (t_len, t_len), 1)
    block_tri = ((row // CHUNK == col // CHUNK) & (row >= col)).astype(BF16)
    log_f = jnp.log(forget)
    hi = log_f.astype(BF16)
    rest = log_f - hi.astype(F32)
    mid = rest.astype(BF16)
    lo = (rest - mid.astype(F32)).astype(BF16)
    ls[...] = (jnp.dot(block_tri, hi, preferred_element_type=F32)
               + (jnp.dot(block_tri, mid, preferred_element_type=F32)
                  + jnp.dot(block_tri, lo, preferred_element_type=F32)))

    incl, _, _ = _tri_masks()
    nw = nw_ref[...]
    mid = CHUNK // 2 - 1

    hsl = [slice(j * HEAD_DIM, (j + 1) * HEAD_DIM) for j in range(heads)]

    def chunk_body(c, carry):
        rows = pl.ds(pl.multiple_of(c * CHUNK, CHUNK), CHUNK)
        items = []
        for j in range(heads):
            b = ls[rows, hsl[j]]
            items.append(dict(b=b, b_mid=b[mid:mid + 1, :], b_last=b[CHUNK - 1:CHUNK, :],
                              q=qs[rows, hsl[j]], k=ks[rows, hsl[j]],
                              vb=i_ref[0, rows, hsl[j]].astype(BF16)))
        for it in items:
            it["att"] = jnp.where(incl, _bdot_g(it["q"] * jnp.exp(it["b"] - it["b_mid"]),
                                                it["k"] * jnp.exp(it["b_mid"] - it["b"]), _NT), 0.0)
        for j, it in enumerate(items):
            st = st_ref[j]
            it["o"] = _bdot_g(it["q"] * jnp.exp(it["b"]), st, _NT)
            st_ref[j] = st * jnp.exp(it["b_last"]) + lax.dot_general(
                it["vb"], (it["k"] * jnp.exp(it["b_last"] - it["b"])).astype(BF16), _TN,
                preferred_element_type=F32)
        for it in items:
            it["o"] = it["o"] + jnp.dot(it["att"].astype(BF16), it["vb"], preferred_element_type=F32)
        for j, it in enumerate(items):
            o = _gated_head_norm(it["o"], nw, g_ref[0, rows, hsl[j]])
            o_ref[0, rows, hsl[j]] = o.astype(o_ref.dtype)
        return carry

    lax.fori_loop(0, t_len // CHUNK, chunk_body, 0, unroll=True)


def _hgrn(proj, lb, norm_w, *, col0, n_heads, t_len=256):
    bsz, s, _ = proj.shape
    width = n_heads * HEAD_DIM
    cb = col0 // width

    def col_spec(group):
        return pl.BlockSpec((1, t_len, width), lambda b, t: (b, t, cb + group))

    kern = functools.partial(_hgrn_kernel, heads=n_heads, t_len=t_len)
    return pl.pallas_call(
        kern,
        grid=(bsz, s // t_len),
        in_specs=[col_spec(0), col_spec(1), col_spec(2), col_spec(3),
                  pl.BlockSpec((1, width), lambda b, t: (0, 0)),
                  pl.BlockSpec((1, HEAD_DIM), lambda b, t: (0, 0))],
        out_specs=pl.BlockSpec((1, t_len, width), lambda b, t: (b, t, 0)),
        out_shape=jax.ShapeDtypeStruct((bsz, s, width), BF16),
        scratch_shapes=[pltpu.VMEM((t_len, width), F32)] * 3
        + [pltpu.VMEM((n_heads, HEAD_DIM, HEAD_DIM), F32)],
        compiler_params=_cparams(("parallel", "arbitrary")),
        name="hgrn2",
    )(proj, proj, proj, proj, lb.reshape(1, width).astype(F32), norm_w.reshape(1, HEAD_DIM).astype(F32))


def _rglru_kernel(y_ref, x_ref, cw_ref, cb_ref, wa_ref, ba_ref, wx_ref, bx_ref, lam_ref, o_ref,
                  xbuf, a_s, b_s, h_s, hcar, *, blocks, t_len):
    t = pl.program_id(2)

    @pl.when(t == 0)
    def _():
        xbuf[pl.ds(0, SUBLANES), :] = jnp.zeros((SUBLANES, blocks * RG_BLOCK), F32)
        hcar[...] = jnp.zeros_like(hcar)

    xc = _causal_conv(xbuf, x_ref[0], cw_ref[...], t_len) + cb_ref[...]
    for n in range(blocks):
        sl = slice(n * RG_BLOCK, (n + 1) * RG_BLOCK)
        xb = xc[:, sl]
        xb16 = xb.astype(BF16)
        r = _sigmoid(jnp.dot(xb16, wa_ref[n], preferred_element_type=F32) + ba_ref[:, sl])
        gi = _sigmoid(jnp.dot(xb16, wx_ref[n], preferred_element_type=F32) + bx_ref[:, sl])
        log_a = (-RG_C) * r * _softplus(-lam_ref[:, sl])
        a = jnp.exp(log_a)
        a_s[:, sl] = a
        one_minus_a2 = -jnp.tanh(log_a) * (a * a + 1.0)
        b_s[:, sl] = jnp.sqrt(jnp.maximum(one_minus_a2, 0.0)) * (gi * xb)

    def row_body(i, h):
        h = a_s[pl.ds(i, 1), :] * h + b_s[pl.ds(i, 1), :]
        h_s[pl.ds(i, 1), :] = h
        return h

    hcar[...] = lax.fori_loop(0, t_len, row_body, hcar[...], unroll=SUBLANES)
    o_ref[0] = (jax.nn.gelu(y_ref[0], approximate=True) * h_s[...]).astype(o_ref.dtype)


def _rglru(yx, conv_w, conv_b, wa, ba, wx, bx, lam, *, blocks_per_step=8, t_len=256):
    bsz, s, w2 = yx.shape
    width = w2 // 2
    gw = blocks_per_step * RG_BLOCK
    n_g = width // gw
    vec = lambda a: a.reshape(1, width).astype(F32)
    vspec = pl.BlockSpec((1, gw), lambda b, g, t: (0, g))
    wspec = pl.BlockSpec((blocks_per_step, RG_BLOCK, RG_BLOCK), lambda b, g, t: (g, 0, 0))
    kern = functools.partial(_rglru_kernel, blocks=blocks_per_step, t_len=t_len)
    return pl.pallas_call(
        kern,
        grid=(bsz, n_g, s // t_len),
        in_specs=[pl.BlockSpec((1, t_len, gw), lambda b, g, t: (b, t, g)),
                  pl.BlockSpec((1, t_len, gw), lambda b, g, t: (b, t, n_g + g)),
                  pl.BlockSpec((CONV_W, gw), lambda b, g, t: (0, g)),
                  vspec, wspec, vspec, wspec, vspec, vspec],
        out_specs=pl.BlockSpec((1, t_len, gw), lambda b, g, t: (b, t, g)),
        out_shape=jax.ShapeDtypeStruct((bsz, s, width), BF16),
        scratch_shapes=[pltpu.VMEM((SUBLANES + t_len, gw), F32)]
        + [pltpu.VMEM((t_len, gw), F32)] * 3
        + [pltpu.VMEM((1, gw), F32)],
        compiler_params=_cparams(("parallel", "parallel", "arbitrary")),
        name="rglru",
    )(yx, yx, conv_w.astype(F32), vec(conv_b), wa.astype(BF16), vec(ba), wx.astype(BF16), vec(bx), vec(lam))


def _norm_router_kernel(x_ref, nw_ref, w_ref, o_ref, *, n_experts):
    x = x_ref[...]
    y = x * lax.rsqrt(jnp.mean(x * x, axis=-1, keepdims=True) + EPS) * nw_ref[...]
    w = w_ref[...]
    y_hi = y.astype(BF16)
    y_lo = (y - y_hi.astype(F32)).astype(BF16)
    w_hi = w.astype(BF16)
    w_lo = (w - w_hi.astype(F32)).astype(BF16)
    logits = (jnp.dot(y_hi, w_hi, preferred_element_type=F32)
              + (jnp.dot(y_lo, w_hi, preferred_element_type=F32)
                 + jnp.dot(y_hi, w_lo, preferred_element_type=F32)))
    lane = lax.broadcasted_iota(jnp.int32, logits.shape, 1).astype(F32)
    neg = jnp.float32(-jnp.inf)
    l1 = jnp.where(lane < n_experts, logits, neg)
    m1 = jnp.max(l1, axis=-1, keepdims=True)
    i1 = jnp.min(jnp.where(l1 == m1, lane, float(LANES)), axis=-1, keepdims=True)
    l2 = jnp.where(lane == i1, neg, l1)
    m2 = jnp.max(l2, axis=-1, keepdims=True)
    i2 = jnp.min(jnp.where(l2 == m2, lane, float(LANES)), axis=-1, keepdims=True)
    e2 = jnp.exp(m2 - m1)
    g1 = 1.0 / (1.0 + e2)
    g2 = e2 / (1.0 + e2)
    out = jnp.where(lane == 0, i1, 0.0)
    out = jnp.where(lane == 1, i2, out)
    out = jnp.where(lane == 2, g1, out)
    out = jnp.where(lane == 3, g2, out)
    o_ref[...] = out


def _norm_router(x, nw, router_w, tm=512):
    m, d = x.shape
    n_experts = router_w.shape[1]
    w = jnp.zeros((d, LANES), F32).at[:, :n_experts].set(router_w.astype(F32))
    return pl.pallas_call(
        functools.partial(_norm_router_kernel, n_experts=n_experts),
        grid=(m // tm,),
        in_specs=[pl.BlockSpec((tm, d), lambda i: (i, 0)),
                  pl.BlockSpec((1, d), lambda i: (0, 0)),
                  pl.BlockSpec((d, LANES), lambda i: (0, 0))],
        out_specs=pl.BlockSpec((tm, LANES), lambda i: (i, 0)),
        out_shape=jax.ShapeDtypeStruct((m, LANES), F32),
        compiler_params=_cparams(("parallel",)),
        name="norm_router",
    )(x, nw.reshape(1, d).astype(F32), w)


def _start_row_gather(idx_ref, base, src3, dst3, sem, n_rows):
    def body(g, carry):
        for u in range(SUBLANES):
            row = idx_ref[base + g * SUBLANES + u]
            src = src3.at[lax.shift_right_logical(row, 3), pl.ds(row & (SUBLANES - 1), 1), :]
            pltpu.make_async_copy(src, dst3.at[g, pl.ds(u, 1), :], sem).start(priority=u % 2)
        return carry
    lax.fori_loop(0, n_rows // SUBLANES, body, 0)


def _wait_row_gather(src3, dst3, sem, n_rows):
    n = n_rows // SUBLANES
    pltpu.make_async_copy(src3.at[pl.ds(0, n)], dst3.at[pl.ds(0, n)], sem).wait()


def _gather_norm_kernel(idx_ref, nq_ref, x_hbm, nw_ref, o_ref, buf, sem):
    c = pl.program_id(0)
    slot = c % 2

    def start(chunk, s):
        _start_row_gather(idx_ref, chunk * MOE_CHUNK, x_hbm, buf.at[s], sem.at[s], nq_ref[chunk] * MOE_SUB)

    @pl.when(c == 0)
    def _():
        start(0, 0)

    @pl.when(c + 1 < pl.num_programs(0))
    def _():
        start(c + 1, 1 - slot)

    for sb in range(MOE_CHUNK // MOE_SUB):
        @pl.when(sb < nq_ref[c])
        def _():
            _wait_row_gather(x_hbm, buf.at[slot], sem.at[slot], MOE_SUB)

    for sb in range(MOE_CHUNK // MOE_SUB):
        rows = pl.ds(sb * MOE_SUB, MOE_SUB)

        @pl.when(sb < nq_ref[c])
        def _(sb=sb, rows=rows):
            tiles = pl.ds(sb * (MOE_SUB // SUBLANES), MOE_SUB // SUBLANES)
            x = buf[slot, tiles].reshape(MOE_SUB, o_ref.shape[1])
            y = x * lax.rsqrt(jnp.mean(x * x, axis=-1, keepdims=True) + EPS) * nw_ref[...]
            o_ref[rows, :] = y.astype(o_ref.dtype)

        @pl.when(sb >= nq_ref[c])
        def _():
            o_ref[rows, :] = jnp.zeros((MOE_SUB, o_ref.shape[1]), o_ref.dtype)


def _gather_norm(src_token, chunk_nq, x, nw):
    rows = src_token.shape[0]
    d = x.shape[1]
    grid_spec = pltpu.PrefetchScalarGridSpec(
        num_scalar_prefetch=2,
        grid=(rows // MOE_CHUNK,),
        in_specs=[pl.BlockSpec(memory_space=pl.ANY),
                  pl.BlockSpec((1, d), lambda c, idx, nq: (0, 0))],
        out_specs=pl.BlockSpec((MOE_CHUNK, d), lambda c, idx, nq: (c, 0)),
        scratch_shapes=[pltpu.VMEM((2, MOE_CHUNK // SUBLANES, SUBLANES, d), F32),
                        pltpu.SemaphoreType.DMA((2,))],
    )
    return pl.pallas_call(
        _gather_norm_kernel,
        grid_spec=grid_spec,
        out_shape=jax.ShapeDtypeStruct((rows, d), BF16),
        compiler_params=_cparams(("arbitrary",)),
        name="gather_norm",
    )(src_token, chunk_nq, x.reshape(-1, SUBLANES, d), nw.reshape(1, d).astype(F32))


def _for_valid_rows(nq, o_ref, fn, pred=True):
    total = o_ref.shape[0]
    for v in range(total // MOE_SUB + 1):
        @pl.when(jnp.logical_and(nq == v, pred))
        def _(v=v):
            rows = v * MOE_SUB
            if rows:
                fn(rows)
            if rows < total:
                o_ref[pl.ds(rows, total - rows), :] = jnp.zeros((total - rows, o_ref.shape[1]), o_ref.dtype)


def _gmm_swiglu_kernel(te_ref, nq_ref, nv_ref, a_ref, wg_ref, wu_ref, o_ref):
    c = pl.program_id(0)

    def compute(rows):
        a = a_ref[pl.ds(0, rows), :]
        g = jnp.dot(a, wg_ref[0].astype(BF16), preferred_element_type=F32)
        u = jnp.dot(a, wu_ref[0].astype(BF16), preferred_element_type=F32)
        o_ref[pl.ds(0, rows), :] = (_silu(g) * u).astype(o_ref.dtype)

    _for_valid_rows(nq_ref[c], o_ref, compute)


def _gmm_swiglu(chunk_expert, chunk_nq, n_valid, a, w_gu, tn):
    m, k = a.shape
    f = w_gu.shape[2] // 2
    nb = f // tn
    last = lambda c, nv: jnp.minimum(c, nv[0] - 1)
    col = lambda j, c, nv: jnp.where(c < nv[0], j, nb - 1)
    grid_spec = pltpu.PrefetchScalarGridSpec(
        num_scalar_prefetch=3,
        grid=(m // MOE_CHUNK, nb),
        in_specs=[pl.BlockSpec((MOE_CHUNK, k), lambda c, j, te, nq, nv: (last(c, nv), 0)),
                  pl.BlockSpec((1, k, tn), lambda c, j, te, nq, nv: (te[c], 0, col(j, c, nv))),
                  pl.BlockSpec((1, k, tn), lambda c, j, te, nq, nv: (te[c], 0, col(j, c, nv) + nb))],
        out_specs=pl.BlockSpec((MOE_CHUNK, tn), lambda c, j, te, nq, nv: (c, j)),
    )
    return pl.pallas_call(
        _gmm_swiglu_kernel,
        grid_spec=grid_spec,
        out_shape=jax.ShapeDtypeStruct((m, f), BF16),
        compiler_params=_cparams(("arbitrary", "arbitrary")),
        name="gmm_swiglu",
    )(chunk_expert, chunk_nq, n_valid, a, w_gu, w_gu)


def _gmm_down_kernel(te_ref, nq_ref, nv_ref, a_ref, w_ref, o_ref):
    c = pl.program_id(0)

    def compute(rows):
        w = w_ref[0].astype(BF16)
        o_ref[pl.ds(0, rows), :] = jnp.dot(a_ref[pl.ds(0, rows), :], w, preferred_element_type=F32)

    _for_valid_rows(nq_ref[c], o_ref, compute)


def _gmm_down(chunk_expert, chunk_nq, n_valid, a, w_d, tn):
    m, k = a.shape
    n = w_d.shape[2]
    nj = n // tn
    last = lambda c, nv: jnp.minimum(c, nv[0] - 1)
    col = lambda j, c, nv: jnp.where(c < nv[0], j, nj - 1)
    grid_spec = pltpu.PrefetchScalarGridSpec(
        num_scalar_prefetch=3,
        grid=(m // MOE_CHUNK, nj),
        in_specs=[pl.BlockSpec((MOE_CHUNK, k), lambda c, j, te, nq, nv: (last(c, nv), 0)),
                  pl.BlockSpec((1, k, tn), lambda c, j, te, nq, nv: (te[c], 0, col(j, c, nv)))],
        out_specs=pl.BlockSpec((MOE_CHUNK, tn), lambda c, j, te, nq, nv: (c, j)),
    )
    return pl.pallas_call(
        _gmm_down_kernel,
        grid_spec=grid_spec,
        out_shape=jax.ShapeDtypeStruct((m, n), F32),
        compiler_params=pltpu.CompilerParams(dimension_semantics=("arbitrary", "arbitrary"),
                                             vmem_limit_bytes=VMEM_LIMIT_MAX),
        name="gmm_down",
    )(chunk_expert, chunk_nq, n_valid, a, w_d)


def _combine_norm_kernel(pos_ref, x_ref, r_ref, ys_hbm, w_ref, o_ref, buf, sem, *, tm):
    i = pl.program_id(0)
    n = pl.num_programs(0)
    slot = i % 2

    def start(tile, s):
        for kk in range(TOP_K):
            _start_row_gather(pos_ref, kk * (n * tm) + tile * tm, ys_hbm, buf.at[s, kk], sem.at[s], tm)

    @pl.when(i == 0)
    def _():
        start(0, 0)

    @pl.when(i + 1 < n)
    def _():
        start(i + 1, 1 - slot)

    for kk in range(TOP_K):
        _wait_row_gather(ys_hbm, buf.at[slot, kk], sem.at[slot], tm)
    r = r_ref[...]
    d = x_ref.shape[1]
    moe = r[:, TOP_K:TOP_K + 1] * buf[slot, 0].reshape(tm, d)
    for kk in range(1, TOP_K):
        moe = moe + r[:, TOP_K + kk:TOP_K + kk + 1] * buf[slot, kk].reshape(tm, d)
    x = x_ref[...] + moe
    y = x * lax.rsqrt(jnp.mean(x * x, axis=-1, keepdims=True) + EPS)
    o_ref[...] = y * w_ref[...]


def _combine_norm(pos, x, route, ys, w, tm=256):
    m, d = x.shape
    grid_spec = pltpu.PrefetchScalarGridSpec(
        num_scalar_prefetch=1,
        grid=(m // tm,),
        in_specs=[pl.BlockSpec((tm, d), lambda i, p: (i, 0)),
                  pl.BlockSpec((tm, LANES), lambda i, p: (i, 0)),
                  pl.BlockSpec(memory_space=pl.ANY),
                  pl.BlockSpec((1, d), lambda i, p: (0, 0))],
        out_specs=pl.BlockSpec((tm, d), lambda i, p: (i, 0)),
        scratch_shapes=[pltpu.VMEM((2, TOP_K, tm // SUBLANES, SUBLANES, d), F32),
                        pltpu.SemaphoreType.DMA((2,))],
    )
    return pl.pallas_call(
        functools.partial(_combine_norm_kernel, tm=tm),
        grid_spec=grid_spec,
        out_shape=jax.ShapeDtypeStruct((m, d), F32),
        compiler_params=_cparams(("arbitrary",)),
        name="combine_norm",
    )(pos, x, route, ys.reshape(-1, SUBLANES, d), w.reshape(1, d).astype(F32))


def _moe_routing(route, n_experts):
    m = route.shape[0]
    n_sub = MOE_CHUNK // MOE_SUB
    ids = route[:, :TOP_K].astype(jnp.int32)
    flat_e = ids.reshape(-1)
    onehot = (flat_e[:, None] == jnp.arange(n_experts)[None, :]).astype(jnp.int32)
    rank = jnp.sum((jnp.cumsum(onehot, axis=0) - onehot) * onehot, axis=1)
    counts = jnp.sum(onehot, axis=0)
    subs = (counts + MOE_SUB - 1) // MOE_SUB
    n_chunks = (subs + n_sub - 1) // n_sub
    chunk_end = jnp.cumsum(n_chunks)
    chunk_start = chunk_end - n_chunks
    max_chunks = (m * TOP_K // MOE_SUB + n_experts + n_sub - 1) // n_sub + n_experts
    rows = max_chunks * MOE_CHUNK
    base = subs // jnp.maximum(n_chunks, 1)
    extra = subs - base * n_chunks
    n_valid = chunk_end[n_experts - 1].astype(jnp.int32)
    cidx = jnp.minimum(jnp.arange(max_chunks, dtype=jnp.int32), n_valid - 1)
    chunk_expert = jnp.sum((cidx[:, None] >= chunk_end[None, :]).astype(jnp.int32), axis=1).astype(jnp.int32)
    sel = (chunk_expert[:, None] == jnp.arange(n_experts)[None, :]).astype(jnp.int32)
    per_chunk = lambda v: jnp.sum(sel * v[None, :], axis=1)
    within = cidx - per_chunk(chunk_start)
    chunk_nq = per_chunk(base) + (within < per_chunk(extra))
    is_valid = jnp.arange(max_chunks) < n_valid
    chunk_nq = jnp.where(is_valid, chunk_nq, 0).astype(jnp.int32)
    first_rank = (jnp.minimum(within, per_chunk(extra)) * (per_chunk(base) + 1)
                  + jnp.maximum(within - per_chunk(extra), 0) * per_chunk(base)) * MOE_SUB
    mine = (flat_e[:, None] == chunk_expert[None, :]) & is_valid[None, :]
    chunk_of = (jnp.sum(onehot * chunk_start[None, :], axis=1)
                + jnp.sum((mine & (first_rank[None, :] <= rank[:, None])).astype(jnp.int32), axis=1) - 1)
    at_chunk = (chunk_of[:, None] == jnp.arange(max_chunks)[None, :]).astype(jnp.int32)
    pos = chunk_of * MOE_CHUNK + rank - jnp.sum(at_chunk * first_rank[None, :], axis=1)
    src_token = jnp.zeros((rows,), jnp.int32).at[pos].set(jnp.arange(m * TOP_K, dtype=jnp.int32) // TOP_K)
    pos_kmajor = pos.reshape(m, TOP_K).T.reshape(-1).astype(jnp.int32)
    return src_token, chunk_expert, chunk_nq, n_valid.reshape(1), pos_kmajor


def kernel(x, norm_w, final_norm_w, mix_in_w, gdn_conv_w, gdn_a_log, gdn_dt_bias, gdn_norm_w, hgrn_lb_logits, hgrn_norm_w, mix_out_w, ffn_gate_up_w, ffn_down_w, rg_in_w, rg_conv_w, rg_conv_b, rg_gate_a_w, rg_gate_a_b, rg_gate_x_w, rg_gate_x_b, rg_lambda, rg_out_w, moe_router_w, moe_gate_up_w, moe_down_w):
    bsz, s, d = x.shape
    m = bsz * s
    n_heads = gdn_a_log.shape[1]
    hw = n_heads * HEAD_DIM
    n_experts = moe_router_w.shape[2]
    xr = x.reshape(m, d)

    hgrn_lb = jnp.cumsum(jax.nn.softmax(hgrn_lb_logits.astype(F32), axis=0), axis=0)
    w_in_t = jnp.swapaxes(mix_in_w[0], 0, 1)
    h = _rmsnorm(xr, norm_w[0, 0], BF16)
    proj_a = _mm_nt(h, w_in_t, F32, 1024, 1024, 0, 4 * hw).reshape(bsz, s, 4 * hw)
    proj_b = _mm_nt(h, w_in_t, F32, 1024, 1024, 4 * hw + 2 * n_heads, 4 * hw).reshape(bsz, s, 4 * hw)
    ba = _mm_nt(h, w_in_t, F32, 1024, LANES, 4 * hw, LANES).reshape(bsz, s, LANES)
    o_a = _gdn(proj_a, ba, gdn_conv_w[0].astype(F32), gdn_a_log[0], gdn_dt_bias[0], gdn_norm_w[0], col0=0)
    o_b = _hgrn(proj_b, hgrn_lb[0], hgrn_norm_w[0], col0=0, n_heads=n_heads)
    xr = _mm_resid([o_a.reshape(m, hw), o_b.reshape(m, hw)], mix_out_w[0], xr, 1024, 1024)
    act = _norm_mm_swiglu(xr, norm_w[0, 1], ffn_gate_up_w[0], 1024, 512)
    xr = _mm_resid([act], ffn_down_w[0], xr, 512, 512)

    yx = _norm_mm(xr, norm_w[1, 0], rg_in_w[0], F32, 1024, 1024).reshape(bsz, s, -1)
    rec = _rglru(yx, rg_conv_w[0], rg_conv_b[0], rg_gate_a_w[0], rg_gate_a_b[0],
                 rg_gate_x_w[0], rg_gate_x_b[0], rg_lambda[0])
    xr = _mm_resid([rec.reshape(m, d)], rg_out_w[0], xr, 1024, 1024)
    route = _norm_router(xr, norm_w[1, 1], moe_router_w[0])
    src_token, chunk_expert, chunk_nq, n_valid, pos = _moe_routing(route, n_experts)
    hs = _gather_norm(src_token, chunk_nq, xr, norm_w[1, 1])
    act = _gmm_swiglu(chunk_expert, chunk_nq, n_valid, hs, moe_gate_up_w[0], 512)
    ys = _gmm_down(chunk_expert, chunk_nq, n_valid, act, moe_down_w[0], 256)
    out = _combine_norm(pos, xr, route, ys, final_norm_w)
    return out.reshape(bsz, s, d)
```

```python
import functools

import jax
import jax.numpy as jnp
from jax import lax
from jax.experimental import pallas as pl
from jax.experimental.pallas import tpu as pltpu

F32 = jnp.float32
BF16 = jnp.bfloat16
HIGHEST = lax.Precision.HIGHEST

EPS = 1e-6
CHUNK = 64
CONV_W = 4
HEAD_DIM = 128
RG_BLOCK = 256
RG_C = 8.0
TOP_K = 2
LANES = 128
SUBLANES = 8
VMEM_LIMIT = 56 * 1024 * 1024
VMEM_LIMIT_MAX = 60 * 1024 * 1024
MOE_CHUNK = 1024
MOE_SUB = 128

_NT = (((1,), (1,)), ((), ()))
_TN = (((0,), (0,)), ((), ()))


def _cparams(sem):
    return pltpu.CompilerParams(dimension_semantics=sem, vmem_limit_bytes=VMEM_LIMIT)


def _sigmoid(x):
    return jax.nn.sigmoid(x)


def _silu(x):
    return x * _sigmoid(x)


def _softplus(x):
    return jnp.maximum(x, 0.0) + jnp.log1p(jnp.exp(-jnp.abs(x)))


def _bdot(a, b):
    return jnp.dot(a.astype(BF16), b.astype(BF16), preferred_element_type=F32)


def _bdot_g(a, b, dims):
    return lax.dot_general(a.astype(BF16), b.astype(BF16), dims, preferred_element_type=F32)


def _rmsnorm_kernel(x_ref, w_ref, o_ref):
    x = x_ref[...]
    y = x * lax.rsqrt(jnp.mean(x * x, axis=-1, keepdims=True) + EPS)
    o_ref[...] = (y * w_ref[...]).astype(o_ref.dtype)


def _rmsnorm(x, w, out_dtype, tm=512):
    m, d = x.shape
    return pl.pallas_call(
        _rmsnorm_kernel,
        grid=(m // tm,),
        in_specs=[pl.BlockSpec((tm, d), lambda i: (i, 0)),
                  pl.BlockSpec((1, d), lambda i: (0, 0))],
        out_specs=pl.BlockSpec((tm, d), lambda i: (i, 0)),
        out_shape=jax.ShapeDtypeStruct((m, d), out_dtype),
        compiler_params=_cparams(("parallel",)),
        name="rmsnorm",
    )(x, w.reshape(1, d).astype(F32))


def _bf16_weight(w_ref, w_s, fresh):
    if not fresh:
        return w_s[...]
    w = w_ref[...].astype(BF16)
    w_s[...] = w
    return w


def _first_row_tile_or_not(body):
    first = pl.program_id(1) == 0
    pl.when(first)(lambda: body(True))
    pl.when(jnp.logical_not(first))(lambda: body(False))


def _mm_nt_kernel(*refs, shift):
    a_ref, w_ref = refs[0], refs[1]
    o_ref, w_s = refs[-2], refs[-1]

    def body(fresh):
        if fresh:
            w = w_ref[...]
            if shift:
                w = jnp.concatenate([w, refs[2][...]], axis=0)[shift:shift + w_ref.shape[0], :]
            w = w.astype(BF16)
            w_s[...] = w
        else:
            w = w_s[...]
        o_ref[...] = lax.dot_general(a_ref[...], w, _NT, preferred_element_type=F32).astype(o_ref.dtype)

    _first_row_tile_or_not(body)


def _mm_nt(a, wt, out_dtype, tm, tn, row0, n_rows):
    m, k = a.shape
    base, shift = row0 // tn * tn, row0 % tn
    assert shift % SUBLANES == 0 and shift <= LANES and n_rows % tn == 0
    in_specs = [pl.BlockSpec((tm, k), lambda j, i: (i, 0)),
                pl.BlockSpec((tn, k), lambda j, i: (base // tn + j, 0))]
    args = [a, wt]
    if shift:
        in_specs.append(pl.BlockSpec((LANES, k), lambda j, i: ((base + (j + 1) * tn) // LANES, 0)))
        args.append(wt)
    return pl.pallas_call(
        functools.partial(_mm_nt_kernel, shift=shift),
        grid=(n_rows // tn, m // tm),
        in_specs=in_specs,
        out_specs=pl.BlockSpec((tm, tn), lambda j, i: (i, j)),
        out_shape=jax.ShapeDtypeStruct((m, n_rows), out_dtype),
        scratch_shapes=[pltpu.VMEM((tn, k), BF16)],
        compiler_params=_cparams(("arbitrary", "arbitrary")),
        name="mm_nt",
    )(*args)


def _normed_rows(x_ref, nw_ref, h_s, fresh):
    if not fresh:
        return h_s[...]
    x = x_ref[...]
    h = (x * lax.rsqrt(jnp.mean(x * x, axis=-1, keepdims=True) + EPS) * nw_ref[...]).astype(BF16)
    h_s[...] = h
    return h


def _norm_mm_kernel(x_ref, nw_ref, w_ref, o_ref, h_s):
    def body(fresh):
        h = _normed_rows(x_ref, nw_ref, h_s, fresh)
        o_ref[...] = jnp.dot(h, w_ref[...].astype(BF16), preferred_element_type=F32).astype(o_ref.dtype)

    _first_row_tile_or_not(body)


def _norm_mm(x, nw, w, out_dtype, tm, tn):
    m, k = x.shape
    n = w.shape[1]
    return pl.pallas_call(
        _norm_mm_kernel,
        grid=(m // tm, n // tn),
        in_specs=[pl.BlockSpec((tm, k), lambda i, j: (i, 0)),
                  pl.BlockSpec((1, k), lambda i, j: (0, 0)),
                  pl.BlockSpec((k, tn), lambda i, j: (0, j))],
        out_specs=pl.BlockSpec((tm, tn), lambda i, j: (i, j)),
        out_shape=jax.ShapeDtypeStruct((m, n), out_dtype),
        scratch_shapes=[pltpu.VMEM((tm, k), BF16)],
        compiler_params=_cparams(("arbitrary", "arbitrary")),
        name="norm_mm",
    )(x, nw.reshape(1, k).astype(F32), w)


def _norm_mm_swiglu_kernel(x_ref, nw_ref, wg_ref, wu_ref, o_ref, h_s):
    def body(fresh):
        h = _normed_rows(x_ref, nw_ref, h_s, fresh)
        g = jnp.dot(h, wg_ref[...].astype(BF16), preferred_element_type=F32)
        u = jnp.dot(h, wu_ref[...].astype(BF16), preferred_element_type=F32)
        o_ref[...] = (_silu(g) * u).astype(o_ref.dtype)

    _first_row_tile_or_not(body)


def _norm_mm_swiglu(x, nw, w_gu, tm, tn):
    m, k = x.shape
    f = w_gu.shape[1] // 2
    nb = f // tn
    return pl.pallas_call(
        _norm_mm_swiglu_kernel,
        grid=(m // tm, nb),
        in_specs=[pl.BlockSpec((tm, k), lambda i, j: (i, 0)),
                  pl.BlockSpec((1, k), lambda i, j: (0, 0)),
                  pl.BlockSpec((k, tn), lambda i, j: (0, j)),
                  pl.BlockSpec((k, tn), lambda i, j: (0, j + nb))],
        out_specs=pl.BlockSpec((tm, tn), lambda i, j: (i, j)),
        out_shape=jax.ShapeDtypeStruct((m, f), BF16),
        scratch_shapes=[pltpu.VMEM((tm, k), BF16)],
        compiler_params=_cparams(("arbitrary", "arbitrary")),
        name="norm_mm_swiglu",
    )(x, nw.reshape(1, k).astype(F32), w_gu, w_gu)


def _mm_resid_kernel(*refs, n_pairs):
    r_ref, o_ref = refs[2 * n_pairs], refs[2 * n_pairs + 1]
    w_s = refs[2 * n_pairs + 2:]

    def body(fresh):
        acc = r_ref[...]
        for p in range(n_pairs):
            w = _bf16_weight(refs[2 * p + 1], w_s[p], fresh)
            acc = acc + jnp.dot(refs[2 * p][...], w, preferred_element_type=F32)
        o_ref[...] = acc

    _first_row_tile_or_not(body)


def _mm_resid(a_list, w, resid, tm, tn):
    m, n = resid.shape
    in_specs, args, scratch = [], [], []
    for p, a in enumerate(a_list):
        k = a.shape[1]
        in_specs += [pl.BlockSpec((tm, k), lambda j, i: (i, 0)),
                     pl.BlockSpec((k, tn), lambda j, i, p=p: (p, j))]
        args += [a, w]
        scratch.append(pltpu.VMEM((k, tn), BF16))
    in_specs.append(pl.BlockSpec((tm, tn), lambda j, i: (i, j)))
    args.append(resid)
    return pl.pallas_call(
        functools.partial(_mm_resid_kernel, n_pairs=len(a_list)),
        grid=(n // tn, m // tm),
        in_specs=in_specs,
        out_specs=pl.BlockSpec((tm, tn), lambda j, i: (i, j)),
        out_shape=jax.ShapeDtypeStruct((m, n), F32),
        scratch_shapes=scratch,
        compiler_params=_cparams(("arbitrary", "arbitrary")),
        name="mm_resid",
    )(*args)


def _causal_conv(buf_ref, x, cw, t_len):
    buf_ref[pl.ds(SUBLANES, t_len), :] = x
    acc = cw[CONV_W - 1:CONV_W, :] * x
    for j in range(CONV_W - 1):
        off = SUBLANES - (CONV_W - 1) + j
        acc = acc + cw[j:j + 1, :] * buf_ref[pl.ds(off, t_len), :]
    buf_ref[pl.ds(0, SUBLANES), :] = x[t_len - SUBLANES:, :]
    return acc


def _tri_masks():
    row = lax.broadcasted_iota(jnp.int32, (CHUNK, CHUNK), 0)
    col = lax.broadcasted_iota(jnp.int32, (CHUNK, CHUNK), 1)
    return row >= col, row > col, row == col


def _gated_head_norm(o, nw, z):
    o = o * lax.rsqrt(jnp.mean(o * o, axis=-1, keepdims=True) + EPS) * nw
    return o * _silu(z)


def _gdn_kernel(q_ref, k_ref, v_ref, z_ref, ba_ref, cwq_ref, cwk_ref, cwv_ref, alog_ref, dtb_ref,
                nw_ref, o_ref, qbuf, kbuf, vbuf, qs, ks, vs, gcs, gct, bs, u_s, w_s, qk_s, qd_s, kd_s, s_ref,
                *, n_heads, t_len):
    t = pl.program_id(1)
    n_chunks = t_len // CHUNK
    width = n_heads * HEAD_DIM

    @pl.when(t == 0)
    def _():
        zeros = jnp.zeros((SUBLANES, width), F32)
        qbuf[pl.ds(0, SUBLANES), :] = zeros
        kbuf[pl.ds(0, SUBLANES), :] = zeros
        vbuf[pl.ds(0, SUBLANES), :] = zeros
        s_ref[...] = jnp.zeros_like(s_ref)

    qc = _silu(_causal_conv(qbuf, q_ref[0], cwq_ref[...], t_len))
    kc = _silu(_causal_conv(kbuf, k_ref[0], cwk_ref[...], t_len))
    vs[...] = _silu(_causal_conv(vbuf, v_ref[0], cwv_ref[...], t_len))
    for h in range(n_heads):
        sl = slice(h * HEAD_DIM, (h + 1) * HEAD_DIM)
        qh = qc[:, sl]
        kh = kc[:, sl]
        qs[:, sl] = qh * (lax.rsqrt(jnp.sum(qh * qh, axis=-1, keepdims=True) + EPS) * (HEAD_DIM ** -0.5))
        ks[:, sl] = kh * lax.rsqrt(jnp.sum(kh * kh, axis=-1, keepdims=True) + EPS)

    ba = ba_ref[0]
    g_log = -jnp.exp(alog_ref[...]) * _softplus(ba + dtb_ref[...])
    bs[...] = _sigmoid(ba)
    row = lax.broadcasted_iota(jnp.int32, (t_len, t_len), 0)
    col = lax.broadcasted_iota(jnp.int32, (t_len, t_len), 1)
    block_tri = ((row // CHUNK == col // CHUNK) & (row >= col)).astype(F32)
    gc_blk = jnp.dot(block_tri, g_log, precision=HIGHEST, preferred_element_type=F32)
    gcs[...] = gc_blk
    for c in range(n_chunks):
        gct[c] = gc_blk[c * CHUNK:(c + 1) * CHUNK, :].T

    incl, strict, eye = _tri_masks()
    eye_f = eye.astype(F32)
    nw = nw_ref[...]

    heads = range(n_heads)
    hsl = [slice(h * HEAD_DIM, (h + 1) * HEAD_DIM) for h in heads]
    qsl = [slice(h * HEAD_DIM, h * HEAD_DIM + CHUNK) for h in heads]
    chunks_per_iter = 4
    assert n_chunks % chunks_per_iter == 0

    def pass_a(it, carry):
        items = []
        for cc in range(chunks_per_iter):
            c = it * chunks_per_iter + cc
            rows = pl.ds(pl.multiple_of(c * CHUNK, CHUNK), CHUNK)
            gc_all = gcs[rows, :]
            beta_all = bs[rows, :]
            gct_c = gct[c]
            for h in heads:
                gc = gc_all[:, n_heads + h:n_heads + h + 1]
                beta = beta_all[:, h:h + 1]
                gc_row = gct_c[n_heads + h:n_heads + h + 1, :]
                decay = jnp.where(incl, jnp.exp(jnp.where(incl, gc - gc_row, 0.0)), 0.0)
                items.append(dict(rows=rows, h=h, gc=gc, beta=beta, decay=decay,
                                  q=qs[rows, hsl[h]], k=ks[rows, hsl[h]]))
        for it_ in items:
            it_["kb"] = it_["k"].astype(BF16)
            it_["kk"] = lax.dot_general(it_["kb"], it_["kb"], _NT, preferred_element_type=F32)
        for it_ in items:
            it_["x"] = jnp.where(strict, -(it_["beta"] * it_["kk"] * it_["decay"]), 0.0)
            it_["p"] = eye_f + it_["x"]
        for _ in range(5):
            for it_ in items:
                xb = it_["x"].astype(BF16)
                it_["x"] = jnp.dot(xb, xb, preferred_element_type=F32)
            for it_ in items:
                it_["p"] = it_["p"] + _bdot(it_["p"], it_["x"])
        for it_ in items:
            egc = jnp.exp(it_["gc"])
            it_["egc"] = egc
            v = vs[it_["rows"], hsl[it_["h"]]]
            rhs = jnp.concatenate([v * it_["beta"], it_["k"] * (it_["beta"] * egc)], axis=-1)
            it_["sol"] = _bdot(it_["p"], rhs)
        for it_ in items:
            it_["qk"] = lax.dot_general(it_["q"].astype(BF16), it_["kb"], _NT,
                                        preferred_element_type=F32) * it_["decay"]
        for it_ in items:
            rows, h, gc = it_["rows"], it_["h"], it_["gc"]
            g_last = gc[CHUNK - 1:CHUNK, :]
            u_s[rows, hsl[h]] = it_["sol"][:, :HEAD_DIM]
            w_s[rows, hsl[h]] = it_["sol"][:, HEAD_DIM:].astype(BF16)
            qk_s[rows, qsl[h]] = it_["qk"].astype(BF16)
            qd_s[rows, hsl[h]] = (it_["q"] * it_["egc"]).astype(BF16)
            kd_s[rows, hsl[h]] = (it_["k"] * jnp.exp(g_last - gc)).astype(BF16)
        return carry

    lax.fori_loop(0, n_chunks // chunks_per_iter, pass_a, 0)

    def pass_b(c, carry):
        rows = pl.ds(pl.multiple_of(c * CHUNK, CHUNK), CHUNK)
        gc_all = gcs[rows, :]
        states = [s_ref[h] for h in heads]
        sbs = [s.astype(BF16) for s in states]
        v_new = [u_s[rows, hsl[h]] - jnp.dot(w_s[rows, hsl[h]], sbs[h], preferred_element_type=F32)
                 for h in heads]
        vbs = [v.astype(BF16) for v in v_new]
        for h in heads:
            g_last = gc_all[CHUNK - 1:CHUNK, n_heads + h:n_heads + h + 1]
            s_ref[h] = states[h] * jnp.exp(g_last) + lax.dot_general(kd_s[rows, hsl[h]], vbs[h], _TN,
                                                                      preferred_element_type=F32)
        outs = [jnp.dot(qd_s[rows, hsl[h]], sbs[h], preferred_element_type=F32)
                + jnp.dot(qk_s[rows, qsl[h]], vbs[h], preferred_element_type=F32) for h in heads]
        for h in heads:
            o = _gated_head_norm(outs[h], nw, z_ref[0, rows, hsl[h]])
            o_ref[0, rows, hsl[h]] = o.astype(o_ref.dtype)
        return carry

    lax.fori_loop(0, n_chunks, pass_b, 0, unroll=True)


def _gdn(proj, ba, conv_w, a_log, dt_bias, norm_w, *, col0, t_len=256):
    bsz, s, _ = proj.shape
    n_heads = a_log.shape[0]
    width = n_heads * HEAD_DIM
    cb = col0 // width

    def col_spec(group):
        return pl.BlockSpec((1, t_len, width), lambda b, t: (b, t, cb + group))

    def cw_spec(group):
        return pl.BlockSpec((CONV_W, width), lambda b, t: (0, group))

    row = jnp.zeros((1, LANES), F32)
    alog_row = row.at[0, n_heads:2 * n_heads].set(a_log.astype(F32))
    dtb_row = row.at[0, n_heads:2 * n_heads].set(dt_bias.astype(F32))
    small = pl.BlockSpec((1, LANES), lambda b, t: (0, 0))
    kern = functools.partial(_gdn_kernel, n_heads=n_heads, t_len=t_len)
    return pl.pallas_call(
        kern,
        grid=(bsz, s // t_len),
        in_specs=[col_spec(0), col_spec(1), col_spec(2), col_spec(3),
                  pl.BlockSpec((1, t_len, LANES), lambda b, t: (b, t, 0)),
                  cw_spec(0), cw_spec(1), cw_spec(2), small, small,
                  pl.BlockSpec((1, HEAD_DIM), lambda b, t: (0, 0))],
        out_specs=pl.BlockSpec((1, t_len, width), lambda b, t: (b, t, 0)),
        out_shape=jax.ShapeDtypeStruct((bsz, s, width), BF16),
        scratch_shapes=[pltpu.VMEM((SUBLANES + t_len, width), F32)] * 3
        + [pltpu.VMEM((t_len, width), F32)] * 3
        + [pltpu.VMEM((t_len, LANES), F32),
           pltpu.VMEM((t_len // CHUNK, LANES, CHUNK), F32),
           pltpu.VMEM((t_len, LANES), F32),
           pltpu.VMEM((t_len, width), F32)]
        + [pltpu.VMEM((t_len, width), BF16)] * 4
        + [pltpu.VMEM((n_heads, HEAD_DIM, HEAD_DIM), F32)],
        compiler_params=_cparams(("parallel", "arbitrary")),
        name="gdn",
    )(proj, proj, proj, proj, ba, conv_w, conv_w, conv_w, alog_row, dtb_row,
      norm_w.reshape(1, HEAD_DIM).astype(F32))


def _hgrn_kernel(q_ref, f_ref, i_ref, g_ref, lb_ref, nw_ref, o_ref, qs, ks, ls, st_ref, *, heads, t_len):
    t = pl.program_id(1)

    @pl.when(t == 0)
    def _():
        st_ref[...] = jnp.zeros_like(st_ref)

    lb = lb_ref[...]
    forget = lb + (1.0 - lb) * _sigmoid(f_ref[0])
    ks[...] = 1.0 - forget
    qs[...] = _silu(q_ref[0]) * (HEAD_DIM ** -0.5)
    row = lax.broadcasted_iota(jnp.int32, (t_len, t_len), 0)
    col = lax.broadcasted_iota(jnp.int32, (t_len, t_len), 1)
    block_tri = ((row // CHUNK == col // CHUNK) & (row >= col)).astype(BF16)
    log_f = jnp.log(forget)
    hi = log_f.astype(BF16)
    rest = log_f - hi.astype(F32)
    mid = rest.astype(BF16)
    lo = (rest - mid.astype(F32)).astype(BF16)
    ls[...] = (jnp.dot(block_tri, hi, preferred_element_type=F32)
               + (jnp.dot(block_tri, mid, preferred_element_type=F32)
                  + jnp.dot(block_tri, lo, preferred_element_type=F32)))

    incl, _, _ = _tri_masks()
    nw = nw_ref[...]
    mid = CHUNK // 2 - 1

    hsl = [slice(j * HEAD_DIM, (j + 1) * HEAD_DIM) for j in range(heads)]

    def chunk_body(c, carry):
        rows = pl.ds(pl.multiple_of(c * CHUNK, CHUNK), CHUNK)
        items = []
        for j in range(heads):
            b = ls[rows, hsl[j]]
            items.append(dict(b=b, b_mid=b[mid:mid + 1, :], b_last=b[CHUNK - 1:CHUNK, :],
                              q=qs[rows, hsl[j]], k=ks[rows, hsl[j]],
                              vb=i_ref[0, rows, hsl[j]].astype(BF16)))
        for it in items:
            it["att"] = jnp.where(incl, _bdot_g(it["q"] * jnp.exp(it["b"] - it["b_mid"]),
                                                it["k"] * jnp.exp(it["b_mid"] - it["b"]), _NT), 0.0)
        for j, it in enumerate(items):
            st = st_ref[j]
            it["o"] = _bdot_g(it["q"] * jnp.exp(it["b"]), st, _NT)
            st_ref[j] = st * jnp.exp(it["b_last"]) + lax.dot_general(
                it["vb"], (it["k"] * jnp.exp(it["b_last"] - it["b"])).astype(BF16), _TN,
                preferred_element_type=F32)
        for it in items:
            it["o"] = it["o"] + jnp.dot(it["att"].astype(BF16), it["vb"], preferred_element_type=F32)
        for j, it in enumerate(items):
            o = _gated_head_norm(it["o"], nw, g_ref[0, rows, hsl[j]])
            o_ref[0, rows, hsl[j]] = o.astype(o_ref.dtype)
        return carry

    lax.fori_loop(0, t_len // CHUNK, chunk_body, 0, unroll=True)


def _hgrn(proj, lb, norm_w, *, col0, n_heads, t_len=256):
    bsz, s, _ = proj.shape
    width = n_heads * HEAD_DIM
    cb = col0 // width

    def col_spec(group):
        return pl.BlockSpec((1, t_len, width), lambda b, t: (b, t, cb + group))

    kern = functools.partial(_hgrn_kernel, heads=n_heads, t_len=t_len)
    return pl.pallas_call(
        kern,
        grid=(bsz, s // t_len),
        in_specs=[col_spec(0), col_spec(1), col_spec(2), col_spec(3),
                  pl.BlockSpec((1, width), lambda b, t: (0, 0)),
                  pl.BlockSpec((1, HEAD_DIM), lambda b, t: (0, 0))],
        out_specs=pl.BlockSpec((1, t_len, width), lambda b, t: (b, t, 0)),
        out_shape=jax.ShapeDtypeStruct((bsz, s, width), BF16),
        scratch_shapes=[pltpu.VMEM((t_len, width), F32)] * 3
        + [pltpu.VMEM((n_heads, HEAD_DIM, HEAD_DIM), F32)],
        compiler_params=_cparams(("parallel", "arbitrary")),
        name="hgrn2",
    )(proj, proj, proj, proj, lb.reshape(1, width).astype(F32), norm_w.reshape(1, HEAD_DIM).astype(F32))


def _rglru_kernel(y_ref, x_ref, cw_ref, cb_ref, wa_ref, ba_ref, wx_ref, bx_ref, lam_ref, o_ref,
                  xbuf, a_s, b_s, h_s, hcar, *, blocks, t_len):
    t = pl.program_id(2)

    @pl.when(t == 0)
    def _():
        xbuf[pl.ds(0, SUBLANES), :] = jnp.zeros((SUBLANES, blocks * RG_BLOCK), F32)
        hcar[...] = jnp.zeros_like(hcar)

    xc = _causal_conv(xbuf, x_ref[0], cw_ref[...], t_len) + cb_ref[...]
    for n in range(blocks):
        sl = slice(n * RG_BLOCK, (n + 1) * RG_BLOCK)
        xb = xc[:, sl]
        xb16 = xb.astype(BF16)
        r = _sigmoid(jnp.dot(xb16, wa_ref[n], preferred_element_type=F32) + ba_ref[:, sl])
        gi = _sigmoid(jnp.dot(xb16, wx_ref[n], preferred_element_type=F32) + bx_ref[:, sl])
        log_a = (-RG_C) * r * _softplus(-lam_ref[:, sl])
        a = jnp.exp(log_a)
        a_s[:, sl] = a
        one_minus_a2 = -jnp.tanh(log_a) * (a * a + 1.0)
        b_s[:, sl] = jnp.sqrt(jnp.maximum(one_minus_a2, 0.0)) * (gi * xb)

    def row_body(i, h):
        h = a_s[pl.ds(i, 1), :] * h + b_s[pl.ds(i, 1), :]
        h_s[pl.ds(i, 1), :] = h
        return h

    hcar[...] = lax.fori_loop(0, t_len, row_body, hcar[...], unroll=SUBLANES)
    o_ref[0] = (jax.nn.gelu(y_ref[0], approximate=True) * h_s[...]).astype(o_ref.dtype)


def _rglru(yx, conv_w, conv_b, wa, ba, wx, bx, lam, *, blocks_per_step=8, t_len=256):
    bsz, s, w2 = yx.shape
    width = w2 // 2
    gw = blocks_per_step * RG_BLOCK
    n_g = width // gw
    vec = lambda a: a.reshape(1, width).astype(F32)
    vspec = pl.BlockSpec((1, gw), lambda b, g, t: (0, g))
    wspec = pl.BlockSpec((blocks_per_step, RG_BLOCK, RG_BLOCK), lambda b, g, t: (g, 0, 0))
    kern = functools.partial(_rglru_kernel, blocks=blocks_per_step, t_len=t_len)
    return pl.pallas_call(
        kern,
        grid=(bsz, n_g, s // t_len),
        in_specs=[pl.BlockSpec((1, t_len, gw), lambda b, g, t: (b, t, g)),
                  pl.BlockSpec((1, t_len, gw), lambda b, g, t: (b, t, n_g + g)),
                  pl.BlockSpec((CONV_W, gw), lambda b, g, t: (0, g)),
                  vspec, wspec, vspec, wspec, vspec, vspec],
        out_specs=pl.BlockSpec((1, t_len, gw), lambda b, g, t: (b, t, g)),
        out_shape=jax.ShapeDtypeStruct((bsz, s, width), BF16),
        scratch_shapes=[pltpu.VMEM((SUBLANES + t_len, gw), F32)]
        + [pltpu.VMEM((t_len, gw), F32)] * 3
        + [pltpu.VMEM((1, gw), F32)],
        compiler_params=_cparams(("parallel", "parallel", "arbitrary")),
        name="rglru",
    )(yx, yx, conv_w.astype(F32), vec(conv_b), wa.astype(BF16), vec(ba), wx.astype(BF16), vec(bx), vec(lam))


def _norm_router_kernel(x_ref, nw_ref, w_ref, o_ref, *, n_experts):
    x = x_ref[...]
    y = x * lax.rsqrt(jnp.mean(x * x, axis=-1, keepdims=True) + EPS) * nw_ref[...]
    w = w_ref[...]
    y_hi = y.astype(BF16)
    y_lo = (y - y_hi.astype(F32)).astype(BF16)
    w_hi = w.astype(BF16)
    w_lo = (w - w_hi.astype(F32)).astype(BF16)
    logits = (jnp.dot(y_hi, w_hi, preferred_element_type=F32)
              + (jnp.dot(y_lo, w_hi, preferred_element_type=F32)
                 + jnp.dot(y_hi, w_lo, preferred_element_type=F32)))
    lane = lax.broadcasted_iota(jnp.int32, logits.shape, 1).astype(F32)
    neg = jnp.float32(-jnp.inf)
    l1 = jnp.where(lane < n_experts, logits, neg)
    m1 = jnp.max(l1, axis=-1, keepdims=True)
    i1 = jnp.min(jnp.where(l1 == m1, lane, float(LANES)), axis=-1, keepdims=True)
    l2 = jnp.where(lane == i1, neg, l1)
    m2 = jnp.max(l2, axis=-1, keepdims=True)
    i2 = jnp.min(jnp.where(l2 == m2, lane, float(LANES)), axis=-1, keepdims=True)
    e2 = jnp.exp(m2 - m1)
    g1 = 1.0 / (1.0 + e2)
    g2 = e2 / (1.0 + e2)
    out = jnp.where(lane == 0, i1, 0.0)
    out = jnp.where(lane == 1, i2, out)
    out = jnp.where(lane == 2, g1, out)
    out = jnp.where(lane == 3, g2, out)
    o_ref[...] = out


def _norm_router(x, nw, router_w, tm=512):
    m, d = x.shape
    n_experts = router_w.shape[1]
    w = jnp.zeros((d, LANES), F32).at[:, :n_experts].set(router_w.astype(F32))
    return pl.pallas_call(
        functools.partial(_norm_router_kernel, n_experts=n_experts),
        grid=(m // tm,),
        in_specs=[pl.BlockSpec((tm, d), lambda i: (i, 0)),
                  pl.BlockSpec((1, d), lambda i: (0, 0)),
                  pl.BlockSpec((d, LANES), lambda i: (0, 0))],
        out_specs=pl.BlockSpec((tm, LANES), lambda i: (i, 0)),
        out_shape=jax.ShapeDtypeStruct((m, LANES), F32),
        compiler_params=_cparams(("parallel",)),
        name="norm_router",
    )(x, nw.reshape(1, d).astype(F32), w)


def _start_row_gather(idx_ref, base, src3, dst3, sem, n_rows):
    def body(g, carry):
        for u in range(SUBLANES):
            row = idx_ref[base + g * SUBLANES + u]
            src = src3.at[lax.shift_right_logical(row, 3), pl.ds(row & (SUBLANES - 1), 1), :]
            pltpu.make_async_copy(src, dst3.at[g, pl.ds(u, 1), :], sem).start(priority=u % 2)
        return carry
    lax.fori_loop(0, n_rows // SUBLANES, body, 0)


def _wait_row_gather(src3, dst3, sem, n_rows):
    n = n_rows // SUBLANES
    pltpu.make_async_copy(src3.at[pl.ds(0, n)], dst3.at[pl.ds(0, n)], sem).wait()


def _gather_norm_kernel(idx_ref, nq_ref, x_hbm, nw_ref, o_ref, buf, sem):
    c = pl.program_id(0)
    slot = c % 2

    def start(chunk, s):
        _start_row_gather(idx_ref, chunk * MOE_CHUNK, x_hbm, buf.at[s], sem.at[s], nq_ref[chunk] * MOE_SUB)

    @pl.when(c == 0)
    def _():
        start(0, 0)

    @pl.when(c + 1 < pl.num_programs(0))
    def _():
        start(c + 1, 1 - slot)

    for sb in range(MOE_CHUNK // MOE_SUB):
        @pl.when(sb < nq_ref[c])
        def _():
            _wait_row_gather(x_hbm, buf.at[slot], sem.at[slot], MOE_SUB)

    for sb in range(MOE_CHUNK // MOE_SUB):
        rows = pl.ds(sb * MOE_SUB, MOE_SUB)

        @pl.when(sb < nq_ref[c])
        def _(sb=sb, rows=rows):
            tiles = pl.ds(sb * (MOE_SUB // SUBLANES), MOE_SUB // SUBLANES)
            x = buf[slot, tiles].reshape(MOE_SUB, o_ref.shape[1])
            y = x * lax.rsqrt(jnp.mean(x * x, axis=-1, keepdims=True) + EPS) * nw_ref[...]
            o_ref[rows, :] = y.astype(o_ref.dtype)

        @pl.when(sb >= nq_ref[c])
        def _():
            o_ref[rows, :] = jnp.zeros((MOE_SUB, o_ref.shape[1]), o_ref.dtype)


def _gather_norm(src_token, chunk_nq, x, nw):
    rows = src_token.shape[0]
    d = x.shape[1]
    grid_spec = pltpu.PrefetchScalarGridSpec(
        num_scalar_prefetch=2,
        grid=(rows // MOE_CHUNK,),
        in_specs=[pl.BlockSpec(memory_space=pl.ANY),
                  pl.BlockSpec((1, d), lambda c, idx, nq: (0, 0))],
        out_specs=pl.BlockSpec((MOE_CHUNK, d), lambda c, idx, nq: (c, 0)),
        scratch_shapes=[pltpu.VMEM((2, MOE_CHUNK // SUBLANES, SUBLANES, d), F32),
                        pltpu.SemaphoreType.DMA((2,))],
    )
    return pl.pallas_call(
        _gather_norm_kernel,
        grid_spec=grid_spec,
        out_shape=jax.ShapeDtypeStruct((rows, d), BF16),
        compiler_params=_cparams(("arbitrary",)),
        name="gather_norm",
    )(src_token, chunk_nq, x.reshape(-1, SUBLANES, d), nw.reshape(1, d).astype(F32))


def _for_valid_rows(nq, o_ref, fn, pred=True):
    total = o_ref.shape[0]
    for v in range(total // MOE_SUB + 1):
        @pl.when(jnp.logical_and(nq == v, pred))
        def _(v=v):
            rows = v * MOE_SUB
            if rows:
                fn(rows)
            if rows < total:
                o_ref[pl.ds(rows, total - rows), :] = jnp.zeros((total - rows, o_ref.shape[1]), o_ref.dtype)


def _gmm_swiglu_kernel(te_ref, nq_ref, nv_ref, a_ref, wg_ref, wu_ref, o_ref):
    c = pl.program_id(0)

    def compute(rows):
        a = a_ref[pl.ds(0, rows), :]
        g = jnp.dot(a, wg_ref[0].astype(BF16), preferred_element_type=F32)
        u = jnp.dot(a, wu_ref[0].astype(BF16), preferred_element_type=F32)
        o_ref[pl.ds(0, rows), :] = (_silu(g) * u).astype(o_ref.dtype)

    _for_valid_rows(nq_ref[c], o_ref, compute)


def _gmm_swiglu(chunk_expert, chunk_nq, n_valid, a, w_gu, tn):
    m, k = a.shape
    f = w_gu.shape[2] // 2
    nb = f // tn
    last = lambda c, nv: jnp.minimum(c, nv[0] - 1)
    col = lambda j, c, nv: jnp.where(c < nv[0], j, nb - 1)
    grid_spec = pltpu.PrefetchScalarGridSpec(
        num_scalar_prefetch=3,
        grid=(m // MOE_CHUNK, nb),
        in_specs=[pl.BlockSpec((MOE_CHUNK, k), lambda c, j, te, nq, nv: (last(c, nv), 0)),
                  pl.BlockSpec((1, k, tn), lambda c, j, te, nq, nv: (te[c], 0, col(j, c, nv))),
                  pl.BlockSpec((1, k, tn), lambda c, j, te, nq, nv: (te[c], 0, col(j, c, nv) + nb))],
        out_specs=pl.BlockSpec((MOE_CHUNK, tn), lambda c, j, te, nq, nv: (c, j)),
    )
    return pl.pallas_call(
        _gmm_swiglu_kernel,
        grid_spec=grid_spec,
        out_shape=jax.ShapeDtypeStruct((m, f), BF16),
        compiler_params=_cparams(("arbitrary", "arbitrary")),
        name="gmm_swiglu",
    )(chunk_expert, chunk_nq, n_valid, a, w_gu, w_gu)


def _gmm_down_kernel(te_ref, nq_ref, nv_ref, a_ref, w_ref, o_ref):
    c = pl.program_id(0)

    def compute(rows):
        w = w_ref[0].astype(BF16)
        o_ref[pl.ds(0, rows), :] = jnp.dot(a_ref[pl.ds(0, rows), :], w, preferred_element_type=F32)

    _for_valid_rows(nq_ref[c], o_ref, compute)


def _gmm_down(chunk_expert, chunk_nq, n_valid, a, w_d, tn):
    m, k = a.shape
    n = w_d.shape[2]
    nj = n // tn
    last = lambda c, nv: jnp.minimum(c, nv[0] - 1)
    col = lambda j, c, nv: jnp.where(c < nv[0], j, nj - 1)
    grid_spec = pltpu.PrefetchScalarGridSpec(
        num_scalar_prefetch=3,
        grid=(m // MOE_CHUNK, nj),
        in_specs=[pl.BlockSpec((MOE_CHUNK, k), lambda c, j, te, nq, nv: (last(c, nv), 0)),
                  pl.BlockSpec((1, k, tn), lambda c, j, te, nq, nv: (te[c], 0, col(j, c, nv)))],
        out_specs=pl.BlockSpec((MOE_CHUNK, tn), lambda c, j, te, nq, nv: (c, j)),
    )
    return pl.pallas_call(
        _gmm_down_kernel,
        grid_spec=grid_spec,
        out_shape=jax.ShapeDtypeStruct((m, n), F32),
        compiler_params=pltpu.CompilerParams(dimension_semantics=("arbitrary", "arbitrary"),
                                             vmem_limit_bytes=VMEM_LIMIT_MAX),
        name="gmm_down",
    )(chunk_expert, chunk_nq, n_valid, a, w_d)


def _combine_norm_kernel(pos_ref, x_ref, r_ref, ys_hbm, w_ref, o_ref, buf, sem, *, tm):
    i = pl.program_id(0)
    n = pl.num_programs(0)
    slot = i % 2

    def start(tile, s):
        for kk in range(TOP_K):
            _start_row_gather(pos_ref, kk * (n * tm) + tile * tm, ys_hbm, buf.at[s, kk], sem.at[s], tm)

    @pl.when(i == 0)
    def _():
        start(0, 0)

    for kk in range(TOP_K):
        _wait_row_gather(ys_hbm, buf.at[slot, kk], sem.at[slot], tm)

    group = 4 * SUBLANES
    w = w_ref[...]

    def combine_rows(g):
        rows = pl.ds(pl.multiple_of(g * group, group), group)
        tiles = pl.ds(pl.multiple_of(g * (group // SUBLANES), group // SUBLANES), group // SUBLANES)
        r = r_ref[rows, :]
        x = x_ref[rows, :]
        for kk in range(TOP_K):
            x = x + r[:, TOP_K + kk:TOP_K + kk + 1] * buf[slot, kk, tiles].reshape(group, x.shape[1])
        o_ref[rows, :] = x * lax.rsqrt(jnp.mean(x * x, axis=-1, keepdims=True) + EPS) * w

    def issue_rows(g):
        for kk in range(TOP_K):
            base = kk * (n * tm) + (i + 1) * tm
            for u in range(group):
                row = pos_ref[base + g * group + u]
                src = ys_hbm.at[lax.shift_right_logical(row, 3), pl.ds(row & (SUBLANES - 1), 1), :]
                dst = buf.at[1 - slot, kk, g * (group // SUBLANES) + u // SUBLANES, pl.ds(u % SUBLANES, 1), :]
                pltpu.make_async_copy(src, dst, sem.at[1 - slot]).start(priority=u % 2)

    @pl.when(i + 1 < n)
    def _():
        def body(g, carry):
            issue_rows(g)
            combine_rows(g)
            return carry
        lax.fori_loop(0, tm // group, body, 0)

    @pl.when(i + 1 >= n)
    def _():
        def body(g, carry):
            combine_rows(g)
            return carry
        lax.fori_loop(0, tm // group, body, 0)


def _combine_norm(pos, x, route, ys, w, tm=256):
    m, d = x.shape
    grid_spec = pltpu.PrefetchScalarGridSpec(
        num_scalar_prefetch=1,
        grid=(m // tm,),
        in_specs=[pl.BlockSpec((tm, d), lambda i, p: (i, 0)),
                  pl.BlockSpec((tm, LANES), lambda i, p: (i, 0)),
                  pl.BlockSpec(memory_space=pl.ANY),
                  pl.BlockSpec((1, d), lambda i, p: (0, 0))],
        out_specs=pl.BlockSpec((tm, d), lambda i, p: (i, 0)),
        scratch_shapes=[pltpu.VMEM((2, TOP_K, tm // SUBLANES, SUBLANES, d), F32),
                        pltpu.SemaphoreType.DMA((2,))],
    )
    return pl.pallas_call(
        functools.partial(_combine_norm_kernel, tm=tm),
        grid_spec=grid_spec,
        out_shape=jax.ShapeDtypeStruct((m, d), F32),
        compiler_params=_cparams(("arbitrary",)),
        name="combine_norm",
    )(pos, x, route, ys.reshape(-1, SUBLANES, d), w.reshape(1, d).astype(F32))


def _moe_routing(route, n_experts):
    m = route.shape[0]
    n_sub = MOE_CHUNK // MOE_SUB
    ids = route[:, :TOP_K].astype(jnp.int32)
    flat_e = ids.reshape(-1)
    onehot = (flat_e[:, None] == jnp.arange(n_experts)[None, :]).astype(jnp.int32)
    rank = jnp.sum((jnp.cumsum(onehot, axis=0) - onehot) * onehot, axis=1)
    counts = jnp.sum(onehot, axis=0)
    subs = (counts + MOE_SUB - 1) // MOE_SUB
    n_chunks = (subs + n_sub - 1) // n_sub
    chunk_end = jnp.cumsum(n_chunks)
    chunk_start = chunk_end - n_chunks
    max_chunks = (m * TOP_K // MOE_SUB + n_experts + n_sub - 1) // n_sub + n_experts
    rows = max_chunks * MOE_CHUNK
    base = subs // jnp.maximum(n_chunks, 1)
    extra = subs - base * n_chunks
    n_valid = chunk_end[n_experts - 1].astype(jnp.int32)
    cidx = jnp.minimum(jnp.arange(max_chunks, dtype=jnp.int32), n_valid - 1)
    chunk_expert = jnp.sum((cidx[:, None] >= chunk_end[None, :]).astype(jnp.int32), axis=1).astype(jnp.int32)
    sel = (chunk_expert[:, None] == jnp.arange(n_experts)[None, :]).astype(jnp.int32)
    per_chunk = lambda v: jnp.sum(sel * v[None, :], axis=1)
    within = cidx - per_chunk(chunk_start)
    chunk_nq = per_chunk(base) + (within < per_chunk(extra))
    is_valid = jnp.arange(max_chunks) < n_valid
    chunk_nq = jnp.where(is_valid, chunk_nq, 0).astype(jnp.int32)
    first_rank = (jnp.minimum(within, per_chunk(extra)) * (per_chunk(base) + 1)
                  + jnp.maximum(within - per_chunk(extra), 0) * per_chunk(base)) * MOE_SUB
    mine = (flat_e[:, None] == chunk_expert[None, :]) & is_valid[None, :]
    chunk_of = (jnp.sum(onehot * chunk_start[None, :], axis=1)
                + jnp.sum((mine & (first_rank[None, :] <= rank[:, None])).astype(jnp.int32), axis=1) - 1)
    at_chunk = (chunk_of[:, None] == jnp.arange(max_chunks)[None, :]).astype(jnp.int32)
    pos = chunk_of * MOE_CHUNK + rank - jnp.sum(at_chunk * first_rank[None, :], axis=1)
    src_token = jnp.zeros((rows,), jnp.int32).at[pos].set(jnp.arange(m * TOP_K, dtype=jnp.int32) // TOP_K)
    pos_kmajor = pos.reshape(m, TOP_K).T.reshape(-1).astype(jnp.int32)
    return src_token, chunk_expert, chunk_nq, n_valid.reshape(1), pos_kmajor


def kernel(x, norm_w, final_norm_w, mix_in_w, gdn_conv_w, gdn_a_log, gdn_dt_bias, gdn_norm_w, hgrn_lb_logits, hgrn_norm_w, mix_out_w, ffn_gate_up_w, ffn_down_w, rg_in_w, rg_conv_w, rg_conv_b, rg_gate_a_w, rg_gate_a_b, rg_gate_x_w, rg_gate_x_b, rg_lambda, rg_out_w, moe_router_w, moe_gate_up_w, moe_down_w):
    bsz, s, d = x.shape
    m = bsz * s
    n_heads = gdn_a_log.shape[1]
    hw = n_heads * HEAD_DIM
    n_experts = moe_router_w.shape[2]
    xr = x.reshape(m, d)

    hgrn_lb = jnp.cumsum(jax.nn.softmax(hgrn_lb_logits.astype(F32), axis=0), axis=0)
    w_in_t = jnp.swapaxes(mix_in_w[0], 0, 1)
    h = _rmsnorm(xr, norm_w[0, 0], BF16)
    proj_a = _mm_nt(h, w_in_t, F32, 1024, 1024, 0, 4 * hw).reshape(bsz, s, 4 * hw)
    proj_b = _mm_nt(h, w_in_t, F32, 1024, 1024, 4 * hw + 2 * n_heads, 4 * hw).reshape(bsz, s, 4 * hw)
    ba = _mm_nt(h, w_in_t, F32, 1024, LANES, 4 * hw, LANES).reshape(bsz, s, LANES)
    o_a = _gdn(proj_a, ba, gdn_conv_w[0].astype(F32), gdn_a_log[0], gdn_dt_bias[0], gdn_norm_w[0], col0=0)
    o_b = _hgrn(proj_b, hgrn_lb[0], hgrn_norm_w[0], col0=0, n_heads=n_heads)
    xr = _mm_resid([o_a.reshape(m, hw), o_b.reshape(m, hw)], mix_out_w[0], xr, 1024, 1024)
    act = _norm_mm_swiglu(xr, norm_w[0, 1], ffn_gate_up_w[0], 1024, 512)
    xr = _mm_resid([act], ffn_down_w[0], xr, 512, 512)

    yx = _norm_mm(xr, norm_w[1, 0], rg_in_w[0], F32, 1024, 1024).reshape(bsz, s, -1)
    rec = _rglru(yx, rg_conv_w[0], rg_conv_b[0], rg_gate_a_w[0], rg_gate_a_b[0],
                 rg_gate_x_w[0], rg_gate_x_b[0], rg_lambda[0])
    xr = _mm_resid([rec.reshape(m, d)], rg_out_w[0], xr, 1024, 1024)
    route = _norm_router(xr, norm_w[1, 1], moe_router_w[0])
    src_token, chunk_expert, chunk_nq, n_valid, pos = _moe_routing(route, n_experts)
    hs = _gather_norm(src_token, chunk_nq, xr, norm_w[1, 1])
    act = _gmm_swiglu(chunk_expert, chunk_nq, n_valid, hs, moe_gate_up_w[0], 512)
    ys = _gmm_down(chunk_expert, chunk_nq, n_valid, act, moe_down_w[0], 256)
    out = _combine_norm(pos, xr, route, ys, final_norm_w)
    return out.reshape(bsz, s, d)
```
